```python
import math
import jax
import jax.numpy as jnp
from jax import lax
import numpy as np

D_MODEL = 2048
BATCH = 2
SEQ = 8192
DEPTH = 4
DEC_BATCH = 8
DEC_SEQ = 16
PAST_LEN = 2048

CHUNK = 64
Q_BLOCK = 128
N_MEM = 256
ML_DH = 128
ML_W = 3 * D_MODEL // 8
ML_H = ML_W // ML_DH
CONV_W = 4
DF_DH = 64
DF_DV = 2 * DF_DH
DF_W = D_MODEL // 4
DF_H = DF_W // DF_DV
HG_DK = 128
HG_DV = 128
HG_W = D_MODEL - ML_W - DF_W
HG_H = HG_W // HG_DV
MIX_W = ML_W + DF_W + HG_W
CA_H = 4
CA_DH = 128
CA_W = CA_H * CA_DH
FF_HIDDEN = -(-8 * D_MODEL // (3 * 256)) * 256
EPS = 1e-6
NEG_BIG = -1e30
LB_FLOOR = 1e-30
IN_SPLITS = (2 * ML_W, ML_W, ML_W, ML_H, ML_H, DF_W, DF_W, DF_W, HG_W, HG_W, HG_W, HG_W)
IN_COLS = sum(IN_SPLITS)
IN_SPLIT_POINTS = tuple(int(s) for s in np.cumsum(IN_SPLITS)[:-1])
F32 = jnp.float32

kernel_name = 'hybrid_mlstm_diffattn_hgrn2_stream_step'


def rms_norm(x, g):
    xf = x.astype(F32)
    y = xf * lax.rsqrt(jnp.mean(xf * xf, axis=-1, keepdims=True) + EPS)
    return (y * g.astype(F32)).astype(x.dtype)


def to_chunks(a, L):
    B, H, T = a.shape[:3]
    return jnp.moveaxis(a.reshape(B, H, T // L, L, *a.shape[3:]), 2, 0)


def from_chunks(a):
    nc, B, H, L = a.shape[:4]
    return jnp.moveaxis(a, 0, 2).reshape(B, H, nc * L, *a.shape[4:])


def causal_conv(u, buf, w, b):
    T = u.shape[1]
    up = jnp.concatenate([buf.astype(u.dtype), u], axis=1)
    y = b.astype(u.dtype)
    for j in range(CONV_W):
        y = y + w[j] * up[:, j:j + T]
    return y, up[:, T:]


def mlstm_chunk(carry, xs):
    C, n, m = carry
    q, k, v, ig, lf = xs
    L = q.shape[2]
    causal = jnp.tril(jnp.ones((L, L), dtype=bool))
    b = jnp.cumsum(lf, axis=-1)
    logD = jnp.where(causal, b[..., :, None] - b[..., None, :] + ig[..., None, :], NEG_BIG)
    inter = b + m[..., None]
    m_t = jnp.maximum(inter, jnp.max(logD, axis=-1))
    D = jnp.where(causal, jnp.exp(logD - m_t[..., None]), 0.0)
    w_inter = jnp.exp(inter - m_t)
    s = jnp.einsum('bhtd,bhsd->bhts', q, k) * D
    num = jnp.einsum('bhts,bhse->bhte', s, v) + w_inter[..., None] * jnp.einsum('bhtd,bhde->bhte', q, C)
    den = jnp.sum(s, axis=-1) + w_inter * jnp.einsum('bhtd,bhd->bht', q, n)
    h = num / jnp.maximum(jnp.abs(den), jnp.exp(-m_t))[..., None]
    m_new = m_t[..., -1]
    w_s = jnp.exp(b[..., -1:] - b + ig - m_new[..., None])
    decay = jnp.exp(b[..., -1] + m - m_new)
    C_new = decay[..., None, None] * C + jnp.einsum('bhs,bhsd,bhse->bhde', w_s, k, v)
    n_new = decay[..., None] * n + jnp.einsum('bhs,bhsd->bhd', w_s, k)
    return (C_new, n_new, m_new), h


def mlstm_mixer(qk_pre, v, o_pre, i_pre, f_pre, conv_buf, C0, n0, m0, conv_w, conv_b, b_i, b_f, out_g):
    B, T = v.shape[:2]
    qk, conv_new = causal_conv(qk_pre, conv_buf, conv_w, conv_b)
    qk = jax.nn.silu(qk.astype(F32))
    q, k = jnp.split(qk, 2, axis=-1)
    heads = lambda a: a.reshape(B, T, ML_H, ML_DH).transpose(0, 2, 1, 3)
    q = heads(q)
    k = heads(k) * (ML_DH ** -0.5)
    vh = heads(v.astype(F32))
    ig = (i_pre.astype(F32) + b_i.astype(F32)).transpose(0, 2, 1)
    lf = jax.nn.log_sigmoid(f_pre.astype(F32) + b_f.astype(F32)).transpose(0, 2, 1)
    L = min(CHUNK, T)
    (C, n, m), h = lax.scan(mlstm_chunk, (C0.astype(F32), n0.astype(F32), m0.astype(F32)),
                            (to_chunks(q, L), to_chunks(k, L), to_chunks(vh, L), to_chunks(ig, L), to_chunks(lf, L)))
    h = from_chunks(h).transpose(0, 2, 1, 3)
    o = jax.nn.sigmoid(o_pre.astype(F32)).reshape(B, T, ML_H, ML_DH)
    h = rms_norm(h, out_g) * o
    return h.reshape(B, T, ML_W), conv_new, C, n, m


def diff_attention(q, k_all, v_all, q_pos, k_pos, lam):
    B, Tq = q.shape[:2]
    blk = min(Q_BLOCK, Tq)
    nb = Tq // blk
    slopes = jnp.exp2(-8.0 * (jnp.arange(DF_H, dtype=F32) + 1.0) / DF_H)
    k_chunk = k_pos // CHUNK
    kf = k_pos.astype(F32)

    def block(args):
        qi, pi = args
        s = jnp.einsum('bqhcd,bkhcd->bchqk', qi, k_all).astype(F32) * (DF_DH ** -0.5)
        dist = jnp.abs(pi.astype(F32)[:, None] - kf[None, :])
        s = s - slopes[:, None, None] * dist
        visible = k_chunk[None, :] <= (pi // CHUNK)[:, None]
        p = jax.nn.softmax(jnp.where(visible, s, NEG_BIG), axis=-1)
        a = p[:, 0] - lam * p[:, 1]
        return jnp.einsum('bhqk,bkhd->bqhd', a.astype(v_all.dtype), v_all)

    qb = jnp.moveaxis(q.reshape(B, nb, blk, *q.shape[2:]), 1, 0)
    out = lax.map(block, (qb, q_pos.reshape(nb, blk)))
    return jnp.moveaxis(out, 0, 1).reshape(B, Tq, DF_H, DF_DV)


def diff_attn_mixer(q, k, v, k_past, v_past, layer_idx, q_g, k_g, lam_p, out_g):
    B, T = q.shape[:2]
    P = k_past.shape[1]
    q = rms_norm(q.reshape(B, T, DF_H, 2, DF_DH), q_g)
    k = rms_norm(k.reshape(B, T, DF_H, 2, DF_DH), k_g)
    v = v.reshape(B, T, DF_H, DF_DV)
    k_all = jnp.concatenate([k_past.reshape(B, P, DF_H, 2, DF_DH).astype(k.dtype), k], axis=1)
    v_all = jnp.concatenate([v_past.astype(v.dtype), v], axis=1)
    q_pos = P + jnp.arange(T, dtype=jnp.int32)
    k_pos = jnp.arange(P + T, dtype=jnp.int32)
    lam_p = lam_p.astype(F32)
    lam_init = 0.8 - 0.6 * math.exp(-0.3 * layer_idx)
    lam = jnp.exp(jnp.sum(lam_p[0] * lam_p[1])) - jnp.exp(jnp.sum(lam_p[2] * lam_p[3])) + lam_init
    o = diff_attention(q, k_all, v_all, q_pos, k_pos, lam)
    o = rms_norm(o, out_g) * (1.0 - lam_init)
    return o.reshape(B, T, DF_W), k.reshape(B, T, DF_H, 2 * DF_DH), v


def hgrn_chunk(S, xs):
    q, k, i, lf = xs
    L = q.shape[2]
    A = jnp.cumsum(lf, axis=2)
    causal = jnp.tril(jnp.ones((L, L), dtype=bool))[:, :, None]
    diff = A[:, :, :, None, :] - A[:, :, None, :, :]
    dec = jnp.where(causal, jnp.exp(jnp.where(causal, diff, 0.0)), 0.0)
    att = jnp.einsum('bhtd,bhtsd,bhsd->bhts', q, dec, k)
    o = jnp.einsum('bhts,bhse->bhte', att, i) + jnp.einsum('bhtd,bhde->bhte', q * jnp.exp(A), S)
    A_end = A[:, :, -1:, :]
    S_new = jnp.exp(A_end[:, :, 0, :])[..., None] * S + jnp.einsum('bhsd,bhse->bhde', k * jnp.exp(A_end - A), i)
    return S_new, o


def hgrn_mixer(q, f_pre, i, g, S0, lb, out_g):
    B, T = q.shape[:2]
    lbf = lb.astype(F32)
    lf = jnp.logaddexp(jnp.log(jnp.maximum(lbf, LB_FLOOR)),
                       jnp.log1p(-lbf) + jax.nn.log_sigmoid(f_pre.astype(F32)))
    k = -jnp.expm1(lf)
    heads = lambda a, d: a.reshape(B, T, HG_H, d).transpose(0, 2, 1, 3)
    qh = heads(jax.nn.silu(q.astype(F32)) * (HG_DK ** -0.5), HG_DK)
    L = min(CHUNK, T)
    S, o = lax.scan(hgrn_chunk, S0.astype(F32),
                    (to_chunks(qh, L), to_chunks(heads(k, HG_DK), L),
                     to_chunks(heads(i.astype(F32), HG_DV), L), to_chunks(heads(lf, HG_DK), L)))
    o = from_chunks(o).transpose(0, 2, 1, 3)
    o = rms_norm(o, out_g) * jax.nn.silu(g.astype(F32)).reshape(B, T, HG_H, HG_DV)
    return o.reshape(B, T, HG_W), S


def memory_kv(mem, norm_g, wk, wv, k_g):
    B, N = mem.shape[:2]
    mn = rms_norm(mem, norm_g)
    k = rms_norm((mn @ wk).reshape(B, N, CA_H, CA_DH), k_g)
    v = (mn @ wv).reshape(B, N, CA_H, CA_DH)
    return k, v


def cross_attention(h, mk, mv, wq, q_g, wo):
    B, T = h.shape[:2]
    q = rms_norm((h @ wq).reshape(B, T, CA_H, CA_DH), q_g)
    s = jnp.einsum('bthd,bnhd->bhtn', q, mk.astype(q.dtype)).astype(F32) * (CA_DH ** -0.5)
    p = jax.nn.softmax(s, axis=-1)
    o = jnp.einsum('bhtn,bnhd->bthd', p.astype(mv.dtype), mv)
    return o.reshape(B, T, CA_W).astype(h.dtype) @ wo


def swiglu(h, w_gu, w_down):
    g, u = jnp.split(h @ w_gu, 2, axis=-1)
    return (jax.nn.silu(g) * u) @ w_down


def trunk_layer(x, mk, mv, k_past, v_past, conv_buf, C0, n0, m0, S0, lb, l, lp):
    hn = rms_norm(x, lp['norm_mix'])
    z = hn @ lp['w_in']
    (ml_qk, ml_v, ml_o, ml_i, ml_f, df_q, df_k, df_v, hg_q, hg_f, hg_i, hg_g) = jnp.split(z, IN_SPLIT_POINTS, axis=-1)
    y_ml, conv_new, C, n, m = mlstm_mixer(ml_qk, ml_v, ml_o, ml_i, ml_f, conv_buf, C0, n0, m0,
                                          lp['mlstm_conv_w'], lp['mlstm_conv_b'], lp['mlstm_b_i'],
                                          lp['mlstm_b_f'], lp['mlstm_out_norm'])
    y_df, k_new, v_new = diff_attn_mixer(df_q, df_k, df_v, k_past, v_past, l, lp['diff_q_norm'],
                                         lp['diff_k_norm'], lp['diff_lambda'], lp['diff_out_norm'])
    y_hg, S = hgrn_mixer(hg_q, hg_f, hg_i, hg_g, S0, lb, lp['hgrn_out_norm'])
    mix = jnp.concatenate([y_ml.astype(x.dtype), y_df.astype(x.dtype), y_hg.astype(x.dtype)], axis=-1)
    x = x + mix @ lp['w_out']
    x = x + cross_attention(rms_norm(x, lp['norm_cross']), mk, mv, lp['cross_wq'], lp['cross_q_norm'], lp['cross_wo'])
    x = x + swiglu(rms_norm(x, lp['norm_ffn']), lp['ffn_w_gate_up'], lp['ffn_w_down'])
    return x, k_new, v_new, conv_new, C, n, m, S


def setup_inputs(seed: int = 0) -> dict:
    key = jax.random.key(seed)
    ks = iter(jax.random.split(key, 64))
    nrm = lambda shape, scale: scale * jax.random.normal(next(ks), shape, jnp.float32)
    gain = lambda shape: 1.0 + 0.05 * jax.random.normal(next(ks), shape, jnp.float32)
    return {
        'x_prompt': nrm((BATCH, SEQ, D_MODEL), 1.0),
        'x_sample': nrm((DEC_BATCH, DEC_SEQ, D_MODEL), 1.0),
        'mem_prompt': nrm((BATCH, N_MEM, D_MODEL), 1.0),
        'cache_attn_k': nrm((DEPTH, DEC_BATCH, PAST_LEN, DF_H, 2 * DF_DH), 1.0),
        'cache_attn_v': nrm((DEPTH, DEC_BATCH, PAST_LEN, DF_H, DF_DV), 1.0),
        'cache_mem_k': nrm((DEPTH, DEC_BATCH, N_MEM, CA_H, CA_DH), 1.0),
        'cache_mem_v': nrm((DEPTH, DEC_BATCH, N_MEM, CA_H, CA_DH), 1.0),
        'state_mlstm_conv': nrm((DEPTH, DEC_BATCH, CONV_W - 1, 2 * ML_W), 1.0),
        'state_mlstm_C': nrm((DEPTH, DEC_BATCH, ML_H, ML_DH, ML_DH), 0.1),
        'state_mlstm_n': nrm((DEPTH, DEC_BATCH, ML_H, ML_DH), 0.1),
        'state_mlstm_m': nrm((DEPTH, DEC_BATCH, ML_H), 1.0),
        'state_hgrn_S': nrm((DEPTH, DEC_BATCH, HG_H, HG_DK, HG_DV), 0.3),
        'norm_mix': gain((DEPTH, D_MODEL)),
        'w_in': nrm((DEPTH, D_MODEL, IN_COLS), D_MODEL ** -0.5),
        'mlstm_conv_w': nrm((DEPTH, CONV_W, 2 * ML_W), CONV_W ** -0.5),
        'mlstm_conv_b': nrm((DEPTH, 2 * ML_W), 0.02),
        'mlstm_b_i': nrm((DEPTH, ML_H), 0.1),
        'mlstm_b_f': jnp.linspace(3.0, 6.0, ML_H)[None, :] + nrm((DEPTH, ML_H), 0.1),
        'mlstm_out_norm': gain((DEPTH, ML_DH)),
        'diff_q_norm': gain((DEPTH, DF_DH)),
        'diff_k_norm': gain((DEPTH, DF_DH)),
        'diff_lambda': nrm((DEPTH, 4, DF_DH), 0.1),
        'diff_out_norm': gain((DEPTH, DF_DV)),
        'hgrn_lb_logits': nrm((DEPTH, HG_W), 0.5),
        'hgrn_out_norm': gain((DEPTH, HG_DV)),
        'w_out': nrm((DEPTH, MIX_W, D_MODEL), MIX_W ** -0.5),
        'norm_cross': gain((DEPTH, D_MODEL)),
        'norm_mem': gain((DEPTH, D_MODEL)),
        'cross_wq': nrm((DEPTH, D_MODEL, CA_W), D_MODEL ** -0.5),
        'cross_wk': nrm((DEPTH, D_MODEL, CA_W), D_MODEL ** -0.5),
        'cross_wv': nrm((DEPTH, D_MODEL, CA_W), D_MODEL ** -0.5),
        'cross_q_norm': gain((DEPTH, CA_DH)),
        'cross_k_norm': gain((DEPTH, CA_DH)),
        'cross_wo': nrm((DEPTH, CA_W, D_MODEL), CA_W ** -0.5),
        'norm_ffn': gain((DEPTH, D_MODEL)),
        'ffn_w_gate_up': nrm((DEPTH, D_MODEL, 2 * FF_HIDDEN), D_MODEL ** -0.5),
        'ffn_w_down': nrm((DEPTH, FF_HIDDEN, D_MODEL), FF_HIDDEN ** -0.5),
        'final_norm': gain((D_MODEL,)),
    }


def reference(x_prompt, x_sample, mem_prompt, cache_attn_k, cache_attn_v, cache_mem_k, cache_mem_v,
              state_mlstm_conv, state_mlstm_C, state_mlstm_n, state_mlstm_m, state_hgrn_S,
              norm_mix, w_in, mlstm_conv_w, mlstm_conv_b, mlstm_b_i, mlstm_b_f, mlstm_out_norm,
              diff_q_norm, diff_k_norm, diff_lambda, diff_out_norm, hgrn_lb_logits, hgrn_out_norm,
              w_out, norm_cross, norm_mem, cross_wq, cross_wk, cross_wv, cross_q_norm, cross_k_norm,
              cross_wo, norm_ffn, ffn_w_gate_up, ffn_w_down, final_norm):
    lb_p = jax.nn.softmax(hgrn_lb_logits.astype(F32), axis=0)
    lower_bounds = jnp.cumsum(lb_p, axis=0) - lb_p[0]
    Bp = x_prompt.shape[0]
    dt = x_prompt.dtype
    zk = jnp.zeros((Bp, 0, DF_H, 2 * DF_DH), dt)
    zv = jnp.zeros((Bp, 0, DF_H, DF_DV), dt)
    zconv = jnp.zeros((Bp, CONV_W - 1, 2 * ML_W), dt)
    zC = jnp.zeros((Bp, ML_H, ML_DH, ML_DH), F32)
    zn = jnp.zeros((Bp, ML_H, ML_DH), F32)
    zm = jnp.zeros((Bp, ML_H), F32)
    zS = jnp.zeros((Bp, HG_H, HG_DK, HG_DV), F32)
    xp, xs = x_prompt, x_sample
    p_out = [[] for _ in range(9)]
    s_out = [[] for _ in range(7)]
    for l in range(DEPTH):
        lp = {'norm_mix': norm_mix[l], 'w_in': w_in[l], 'mlstm_conv_w': mlstm_conv_w[l],
              'mlstm_conv_b': mlstm_conv_b[l], 'mlstm_b_i': mlstm_b_i[l], 'mlstm_b_f': mlstm_b_f[l],
              'mlstm_out_norm': mlstm_out_norm[l], 'diff_q_norm': diff_q_norm[l],
              'diff_k_norm': diff_k_norm[l], 'diff_lambda': diff_lambda[l],
              'diff_out_norm': diff_out_norm[l], 'hgrn_out_norm': hgrn_out_norm[l], 'w_out': w_out[l],
              'norm_cross': norm_cross[l], 'cross_wq': cross_wq[l], 'cross_q_norm': cross_q_norm[l],
              'cross_wo': cross_wo[l], 'norm_ffn': norm_ffn[l], 'ffn_w_gate_up': ffn_w_gate_up[l],
              'ffn_w_down': ffn_w_down[l]}
        mk, mv = memory_kv(mem_prompt, norm_mem[l], cross_wk[l], cross_wv[l], cross_k_norm[l])
        xp, *st_p = trunk_layer(xp, mk, mv, zk, zv, zconv, zC, zn, zm, zS, lower_bounds[l], l, lp)
        xs, *st_s = trunk_layer(xs, cache_mem_k[l], cache_mem_v[l], cache_attn_k[l], cache_attn_v[l],
                                state_mlstm_conv[l], state_mlstm_C[l], state_mlstm_n[l], state_mlstm_m[l],
                                state_hgrn_S[l], lower_bounds[l], l, lp)
        for lst, a in zip(p_out, [st_p[0], st_p[1], mk, mv] + st_p[2:]):
            lst.append(a)
        for lst, a in zip(s_out, st_s):
            lst.append(a)
    p_attn_k, p_attn_v, p_mem_k, p_mem_v, p_mlstm_conv, p_mlstm_C, p_mlstm_n, p_mlstm_m, p_hgrn_S = [jnp.stack(a) for a in p_out]
    s_attn_k, s_attn_v, s_mlstm_conv, s_mlstm_C, s_mlstm_n, s_mlstm_m, s_hgrn_S = [jnp.stack(a) for a in s_out]
    y_prompt = rms_norm(xp, final_norm)
    y_sample = rms_norm(xs, final_norm)
    return (y_prompt, y_sample, p_attn_k, p_attn_v, p_mem_k, p_mem_v, p_mlstm_conv, p_mlstm_C, p_mlstm_n,
            p_mlstm_m, p_hgrn_S, s_attn_k, s_attn_v, s_mlstm_conv, s_mlstm_C, s_mlstm_n, s_mlstm_m, s_hgrn_S)
```

```python
import functools
import math

import numpy as np
import jax
import jax.numpy as jnp
from jax import lax
from jax.experimental import pallas as pl
from jax.experimental.pallas import tpu as pltpu

F32 = jnp.float32
BF16 = jnp.bfloat16

D_MODEL = 2048
DEPTH = 4
CHUNK = 64
CHUNK_SHIFT = 6
ML_DH = 128
ML_W = 768
ML_H = 6
CONV_W = 4
DF_DH = 64
DF_DV = 128
DF_W = 512
DF_H = 4
HG_DK = 128
HG_DV = 128
HG_W = 768
HG_H = 6
CA_H = 4
CA_DH = 128
CA_W = 512
N_MEM = 256
FF_HIDDEN = 5632
EPS = 1e-6
NEG_BIG = -1e30
LB_FLOOR = 1e-30

Z_W = 2 * ML_W + 2 * ML_W + 3 * DF_W + 4 * HG_W
GATE_W = 128
ZB_MLQ, ZB_MLK, ZB_MLV, ZB_MLO = 0, 1, 2, 3
ZB_HGQ, ZB_HGF, ZB_HGI, ZB_HGG = 6, 7, 8, 9
ZB_DFQ, ZB_DFK, ZB_DFV = 6, 7, 8

V7X_VMEM_BYTES = 64 * 1024 * 1024
V7X_VMEM_CAP = 60000 * 1024
HIST = 8

NT_DIMS = (((1,), (1,)), ((), ()))
TN_DIMS = (((0,), (0,)), ((), ()))


def _vmem_limit(nbytes):
    return int(min(V7X_VMEM_CAP, max(32 * 1024 * 1024, nbytes)))


def _cparams(sem, vmem_bytes):
    return pltpu.CompilerParams(dimension_semantics=sem, vmem_limit_bytes=_vmem_limit(vmem_bytes))


def _rms(x, g):
    ms = jnp.mean(x * x, axis=-1, keepdims=True)
    return x * lax.rsqrt(ms + EPS) * g


def _dot(a, b):
    return jnp.dot(a, b, preferred_element_type=F32)


def _dot_nt(a, b):
    return lax.dot_general(a, b, NT_DIMS, preferred_element_type=F32)


def _dot_tn(a, b):
    return lax.dot_general(a, b, TN_DIMS, preferred_element_type=F32)


def _log_sigmoid(x):
    return jnp.minimum(x, 0.0) - jnp.log1p(jnp.exp(-jnp.abs(x)))


def _split3(x):
    hi = x.astype(BF16)
    r1 = x - hi.astype(F32)
    mid = r1.astype(BF16)
    lo = (r1 - mid.astype(F32)).astype(BF16)
    return hi, mid, lo


def _dot_exact01(m01, x):
    hi, mid, lo = _split3(x)
    return _dot(m01, hi) + _dot(m01, mid) + _dot(m01, lo)


def _in_proj_kernel(l_ref, x_ref, g_ref, w_ref, wg_ref, z_ref, gate_ref, xn_ref):
    j = pl.program_id(1)

    @pl.when(j == 0)
    def _():
        xn = _rms(x_ref[...], g_ref[0]).astype(BF16)
        xn_ref[...] = xn
        gate_ref[...] = _dot(xn, wg_ref[0])

    z_ref[...] = _dot(xn_ref[...], w_ref[0])


def in_proj(x, lidx, g, w_main, w_gate, *, tm, tn):
    n_rows = x.shape[0]
    grid = (n_rows // tm, Z_W // tn)
    vmem = 2 * tm * D_MODEL * 4 + 2 * D_MODEL * tn * 2 + 2 * tm * tn * 4 + tm * D_MODEL * 2 \
        + 2 * tm * GATE_W * 4 + 2 * D_MODEL * GATE_W * 2 + (4 << 20)
    return pl.pallas_call(
        _in_proj_kernel,
        grid_spec=pltpu.PrefetchScalarGridSpec(
            num_scalar_prefetch=1,
            grid=grid,
            in_specs=[
                pl.BlockSpec((tm, D_MODEL), lambda i, j, l: (i, 0)),
                pl.BlockSpec((1, 1, D_MODEL), lambda i, j, l: (l[0], 0, 0)),
                pl.BlockSpec((1, D_MODEL, tn), lambda i, j, l: (l[0], 0, j)),
                pl.BlockSpec((1, D_MODEL, GATE_W), lambda i, j, l: (l[0], 0, 0)),
            ],
            out_specs=[
                pl.BlockSpec((tm, tn), lambda i, j, l: (i, j)),
                pl.BlockSpec((tm, GATE_W), lambda i, j, l: (i, 0)),
            ],
            scratch_shapes=[pltpu.VMEM((tm, D_MODEL), BF16)],
        ),
        out_shape=[jax.ShapeDtypeStruct((n_rows, Z_W), F32),
                   jax.ShapeDtypeStruct((n_rows, GATE_W), F32)],
        compiler_params=_cparams(("parallel", "arbitrary"), vmem),
        name="in_proj",
    )(lidx, x, g, w_main, w_gate)


def _ffn_kernel(l_ref, x_ref, g_ref, wg_ref, wu_ref, wd_ref, fg_ref, o_ref, xn_ref):
    j = pl.program_id(1)
    nj = pl.num_programs(1)

    @pl.when(j == 0)
    def _():
        xn_ref[...] = _rms(x_ref[...], g_ref[0]).astype(BF16)

    xn = xn_ref[...]
    gt = _dot(xn, wg_ref[0])
    up = _dot(xn, wu_ref[0])
    act = (gt * jax.nn.sigmoid(gt) * up).astype(BF16)
    down = _dot(act, wd_ref[0])

    @pl.when(j == 0)
    def _():
        o_ref[...] = x_ref[...] + down

    @pl.when(j > 0)
    def _():
        o_ref[...] += down

    @pl.when(jnp.logical_and(j == nj - 1, l_ref[0] == DEPTH - 1))
    def _():
        o_ref[...] = _rms(o_ref[...], fg_ref[...])


def ffn(x, lidx, g, w_gu, w_down, final_g, *, tm, th):
    n_rows = x.shape[0]
    nh = FF_HIDDEN // th
    grid = (n_rows // tm, nh)
    vmem = 4 * tm * D_MODEL * 4 + tm * D_MODEL * 2 + 6 * D_MODEL * th * 2 + 3 * tm * th * 4 \
        + tm * D_MODEL * 4 + (4 << 20)
    return pl.pallas_call(
        _ffn_kernel,
        grid_spec=pltpu.PrefetchScalarGridSpec(
            num_scalar_prefetch=1,
            grid=grid,
            in_specs=[
                pl.BlockSpec((tm, D_MODEL), lambda i, j, l: (i, 0)),
                pl.BlockSpec((1, 1, D_MODEL), lambda i, j, l: (l[0], 0, 0)),
                pl.BlockSpec((1, D_MODEL, th), lambda i, j, l: (l[0], 0, j)),
                pl.BlockSpec((1, D_MODEL, th), lambda i, j, l: (l[0], 0, j + nh)),
                pl.BlockSpec((1, th, D_MODEL), lambda i, j, l: (l[0], j, 0)),
                pl.BlockSpec((1, D_MODEL), lambda i, j, l: (0, 0)),
            ],
            out_specs=pl.BlockSpec((tm, D_MODEL), lambda i, j, l: (i, 0)),
            scratch_shapes=[pltpu.VMEM((tm, D_MODEL), BF16)],
        ),
        out_shape=jax.ShapeDtypeStruct((n_rows, D_MODEL), F32),
        compiler_params=_cparams(("parallel", "arbitrary"), vmem),
        name="ffn",
    )(lidx, x, g, w_gu, w_gu, w_down, final_g)


def _mem_kv_kernel(mem_ref, g_ref, wk_ref, wv_ref, kg_ref, k_ref, v_ref):
    mn = _rms(mem_ref[0], g_ref[0]).astype(BF16)
    k = _dot(mn, wk_ref[0])
    for h in range(CA_H):
        sl = slice(h * CA_DH, (h + 1) * CA_DH)
        k_ref[0, :, sl] = _rms(k[:, sl], kg_ref[0])
    v_ref[0] = _dot(mn, wv_ref[0])


def mem_kv(mem, g, wk, wv, kg):
    nb = mem.shape[0]
    out = jax.ShapeDtypeStruct((DEPTH * nb, N_MEM, CA_W), F32)
    return pl.pallas_call(
        _mem_kv_kernel,
        grid=(DEPTH, nb),
        in_specs=[
            pl.BlockSpec((1, N_MEM, D_MODEL), lambda l, b: (b, 0, 0)),
            pl.BlockSpec((1, 1, D_MODEL), lambda l, b: (l, 0, 0)),
            pl.BlockSpec((1, D_MODEL, CA_W), lambda l, b: (l, 0, 0)),
            pl.BlockSpec((1, D_MODEL, CA_W), lambda l, b: (l, 0, 0)),
            pl.BlockSpec((1, 1, CA_DH), lambda l, b: (l, 0, 0)),
        ],
        out_specs=[pl.BlockSpec((1, N_MEM, CA_W), lambda l, b: (l * nb + b, 0, 0)),
                   pl.BlockSpec((1, N_MEM, CA_W), lambda l, b: (l * nb + b, 0, 0))],
        out_shape=[out, out],
        compiler_params=_cparams(("arbitrary", "arbitrary"), 24 << 20),
        name="mem_kv",
    )(mem, g, wk, wv, kg)


def _post_kernel(l_ref, x_ref, yml_ref, ydf_ref, yhg_ref, wout_ref, gx_ref, wq_ref, qg_ref,
                 mk_ref, mv_ref, wo_ref, o_ref):
    x1 = x_ref[...]
    x1 = x1 + _dot(yml_ref[...], wout_ref[0, 0:ML_W, :])
    x1 = x1 + _dot(ydf_ref[...], wout_ref[0, ML_W:ML_W + DF_W, :])
    x1 = x1 + _dot(yhg_ref[...], wout_ref[0, ML_W + DF_W:D_MODEL, :])
    hn = _rms(x1, gx_ref[0]).astype(BF16)
    q = _dot(hn, wq_ref[0])
    heads = []
    for h in range(CA_H):
        sl = slice(h * CA_DH, (h + 1) * CA_DH)
        qh = _rms(q[:, sl], qg_ref[0]).astype(BF16)
        s = _dot_nt(qh, mk_ref[0, :, sl].astype(BF16)) * (CA_DH ** -0.5)
        s = s - jnp.max(s, axis=-1, keepdims=True)
        p = jnp.exp(s)
        p = p / jnp.sum(p, axis=-1, keepdims=True)
        heads.append(_dot(p.astype(BF16), mv_ref[0, :, sl].astype(BF16)))
    o = jnp.concatenate(heads, axis=-1).astype(BF16)
    o_ref[...] = x1 + _dot(o, wo_ref[0])


def post_mix(x, y_ml, y_df, y_hg, lidx, w_out, gx, wq, qg, mk, mv, wo, *, tm, rows_per_batch):
    n_rows = x.shape[0]
    nb = n_rows // rows_per_batch
    tiles_per_batch = rows_per_batch // tm
    grid = (n_rows // tm,)
    w_bytes = (D_MODEL * D_MODEL + 2 * D_MODEL * CA_W) * 2
    vmem = 2 * w_bytes + 6 * tm * D_MODEL * 4 + 2 * tm * D_MODEL * 2 + 4 * N_MEM * CA_W * 4 + (6 << 20)
    row = lambda i, l: (i, 0)
    mem_map = lambda i, l: (l[0] * nb + i // tiles_per_batch, 0, 0)
    return pl.pallas_call(
        _post_kernel,
        grid_spec=pltpu.PrefetchScalarGridSpec(
            num_scalar_prefetch=1,
            grid=grid,
            in_specs=[
                pl.BlockSpec((tm, D_MODEL), row),
                pl.BlockSpec((tm, ML_W), row),
                pl.BlockSpec((tm, DF_W), row),
                pl.BlockSpec((tm, HG_W), row),
                pl.BlockSpec((1, D_MODEL, D_MODEL), lambda i, l: (l[0], 0, 0)),
                pl.BlockSpec((1, 1, D_MODEL), lambda i, l: (l[0], 0, 0)),
                pl.BlockSpec((1, D_MODEL, CA_W), lambda i, l: (l[0], 0, 0)),
                pl.BlockSpec((1, 1, CA_DH), lambda i, l: (l[0], 0, 0)),
                pl.BlockSpec((1, N_MEM, CA_W), mem_map),
                pl.BlockSpec((1, N_MEM, CA_W), mem_map),
                pl.BlockSpec((1, CA_W, D_MODEL), lambda i, l: (l[0], 0, 0)),
            ],
            out_specs=pl.BlockSpec((tm, D_MODEL), row),
        ),
        out_shape=jax.ShapeDtypeStruct((n_rows, D_MODEL), F32),
        compiler_params=_cparams(("parallel",), vmem),
        name="post_mix",
    )(lidx, x, y_ml, y_df, y_hg, w_out, gx, wq, qg, mk, mv, wo)


def _mlstm_kernel(l_ref, zq_ref, zk_ref, zv_ref, zo_ref, gate_ref, hist_ref, cw_ref, cb_ref, bif_ref,
                  og_ref, c0_ref, n0_ref, m0_ref, tri_ref,
                  y_ref, c_ref, n_ref, m_ref, ext_ref, *, tc, lc):
    c = pl.program_id(1)

    @pl.when(c == 0)
    def _():
        c_ref[...] = c0_ref[...]
        n_ref[...] = n0_ref[...]
        m_ref[...] = m0_ref[...]
        ext_ref[0:HIST, :] = hist_ref[0]

    ext_ref[HIST:HIST + tc, 0:ML_W] = zq_ref[...]
    ext_ref[HIST:HIST + tc, ML_W:2 * ML_W] = zk_ref[...]
    acc = cb_ref[0]
    for j in range(CONV_W):
        acc = acc + cw_ref[0, j:j + 1, :] * ext_ref[pl.ds(HIST - (CONV_W - 1) + j, tc), :]
    tail = ext_ref[tc:tc + HIST, :]
    ext_ref[0:HIST, :] = tail
    qk = acc * jax.nn.sigmoid(acc)

    gz = gate_ref[...] + bif_ref[0]
    lf_all = _log_sigmoid(gz)
    tri = tri_ref[...]
    row_i = lax.broadcasted_iota(jnp.int32, (lc, lc), 0)
    col_i = lax.broadcasted_iota(jnp.int32, (lc, lc), 1)
    causal = row_i >= col_i
    diag = row_i == col_i
    og = og_ref[0]

    for ci in range(tc // lc):
        r0 = ci * lc
        ig_c = gz[r0:r0 + lc, :]
        b_c = _dot_exact01(tri, lf_all[r0:r0 + lc, :])
        for h in range(ML_H):
            sl = slice(h * ML_DH, (h + 1) * ML_DH)
            q = qk[r0:r0 + lc, h * ML_DH:(h + 1) * ML_DH]
            k = qk[r0:r0 + lc, ML_W + h * ML_DH:ML_W + (h + 1) * ML_DH] * (ML_DH ** -0.5)
            v = zv_ref[r0:r0 + lc, sl]
            qb = q.astype(BF16)
            b_col = b_c[:, ML_H + h:ML_H + h + 1]
            ig_col = ig_c[:, h:h + 1]
            r_row = jnp.sum(jnp.where(diag, ig_col - b_col, 0.0), axis=0, keepdims=True)
            m_prev = m_ref[0, :, h:h + 1]
            log_d = jnp.where(causal, b_col + r_row, NEG_BIG)
            inter = b_col + m_prev
            m_t = jnp.maximum(inter, jnp.max(log_d, axis=-1, keepdims=True))
            d_mat = jnp.exp(log_d - m_t)
            w_inter = jnp.exp(inter - m_t)
            s = _dot_nt(qb, k.astype(BF16)) * d_mat
            c_old = c_ref[0, h]
            n_old = n_ref[0, h:h + 1, :]
            num = _dot(s.astype(BF16), v.astype(BF16)) + w_inter * _dot(qb, c_old.astype(BF16))
            den = jnp.sum(s, axis=-1, keepdims=True) + w_inter * jnp.sum(q * n_old, axis=-1, keepdims=True)
            hh = num / jnp.maximum(jnp.abs(den), jnp.exp(-m_t))
            m_new = m_t[lc - 1:lc, :]
            b_last = b_col[lc - 1:lc, :]
            w_s = jnp.exp(b_last - b_col + ig_col - m_new)
            decay = jnp.exp(b_last + m_prev - m_new)
            kw = k * w_s
            c_ref[0, h] = decay * c_old + _dot_tn(kw.astype(BF16), v.astype(BF16))
            n_ref[0, h:h + 1, :] = decay * n_old + jnp.sum(kw, axis=0, keepdims=True)
            m_ref[0, :, h:h + 1] = m_new
            o_gate = jax.nn.sigmoid(zo_ref[r0:r0 + lc, sl])
            y_ref[r0:r0 + lc, sl] = (_rms(hh, og) * o_gate).astype(BF16)


def mlstm_mixer(z, gates, lidx, hist0, conv_w, conv_b, b_if, out_g, c0, n0, m0, *, nb, seq, tc, lc):
    n_rows = nb * seq
    nc = seq // tc
    tri = jnp.asarray(np.tril(np.ones((lc, lc), np.float32)), BF16)
    zmap = lambda blk: (lambda b, c, l: (b * nc + c, blk))
    lmap3 = lambda b, c, l: (l[0], 0, 0)
    vmem = 2 * 5 * tc * ML_W * 4 + (HIST + tc) * 2 * ML_W * 4 + 4 * ML_H * ML_DH * ML_DH * 4 \
        + 6 * tc * 2 * ML_W * 4 + (8 << 20)
    kern = functools.partial(_mlstm_kernel, tc=tc, lc=lc)
    return pl.pallas_call(
        kern,
        grid_spec=pltpu.PrefetchScalarGridSpec(
            num_scalar_prefetch=1,
            grid=(nb, nc),
            in_specs=[
                pl.BlockSpec((tc, ML_W), zmap(ZB_MLQ)),
                pl.BlockSpec((tc, ML_W), zmap(ZB_MLK)),
                pl.BlockSpec((tc, ML_W), zmap(ZB_MLV)),
                pl.BlockSpec((tc, ML_W), zmap(ZB_MLO)),
                pl.BlockSpec((tc, GATE_W), lambda b, c, l: (b * nc + c, 0)),
                pl.BlockSpec((1, HIST, 2 * ML_W), lambda b, c, l: (b, 0, 0)),
                pl.BlockSpec((1, CONV_W, 2 * ML_W), lmap3),
                pl.BlockSpec((1, 1, 2 * ML_W), lmap3),
                pl.BlockSpec((1, 1, GATE_W), lmap3),
                pl.BlockSpec((1, 1, ML_DH), lmap3),
                pl.BlockSpec((1, ML_H, ML_DH, ML_DH), lambda b, c, l: (b, 0, 0, 0)),
                pl.BlockSpec((1, 8, ML_DH), lambda b, c, l: (b, 0, 0)),
                pl.BlockSpec((1, 1, GATE_W), lambda b, c, l: (b, 0, 0)),
                pl.BlockSpec((lc, lc), lambda b, c, l: (0, 0)),
            ],
            out_specs=[
                pl.BlockSpec((tc, ML_W), lambda b, c, l: (b * nc + c, 0)),
                pl.BlockSpec((1, ML_H, ML_DH, ML_DH), lambda b, c, l: (b, 0, 0, 0)),
                pl.BlockSpec((1, 8, ML_DH), lambda b, c, l: (b, 0, 0)),
                pl.BlockSpec((1, 1, GATE_W), lambda b, c, l: (b, 0, 0)),
            ],
            scratch_shapes=[pltpu.VMEM((HIST + tc, 2 * ML_W), F32)],
        ),
        out_shape=[
            jax.ShapeDtypeStruct((n_rows, ML_W), BF16),
            jax.ShapeDtypeStruct((nb, ML_H, ML_DH, ML_DH), F32),
            jax.ShapeDtypeStruct((nb, 8, ML_DH), F32),
            jax.ShapeDtypeStruct((nb, 1, GATE_W), F32),
        ],
        compiler_params=_cparams(("parallel", "arbitrary"), vmem),
        name="mlstm",
    )(lidx, z, z, z, z, gates, hist0, conv_w, conv_b, b_if, out_g, c0, n0, m0, tri)


def _hgrn_level_mats(lc):
    t = np.arange(lc)[:, None]
    u = np.arange(lc)[None, :]
    mats = [(u <= t).astype(np.float32)]
    masks = []
    h = lc // 2
    while h >= 1:
        e = (t // (2 * h)) * (2 * h) + h - 1
        odd = (t % (2 * h)) >= h
        m = np.where(odd, (u > e) & (u <= t), (u > t) & (u <= e))
        mats.append(m.astype(np.float32))
        s = np.arange(lc)[None, :]
        same = (t // (2 * h)) == (s // (2 * h))
        masks.append((same & odd & ((s % (2 * h)) < h)).astype(np.float32))
        h //= 2
    masks.append((t == u).astype(np.float32))
    return np.concatenate(mats, axis=0), np.stack(masks, axis=0)


def _hgrn_kernel(l_ref, zq_ref, zf_ref, zi_ref, zg_ref, lbl_ref, og_ref, s0_ref, lvl_ref, msk_ref,
                 y_ref, s_ref, st_ref, *, tc, lc, nlev):
    c = pl.program_id(1)
    nc = pl.num_programs(1)

    @pl.when(c == 0)
    def _():
        for h in range(HG_H):
            st_ref[h] = s0_ref[0, h].T

    logits = lbl_ref[...]
    e = jnp.exp(logits - jnp.max(logits, axis=0, keepdims=True))
    p = e / jnp.sum(e, axis=0, keepdims=True)
    drow = lax.broadcasted_iota(jnp.int32, p.shape, 0)
    lb = jnp.sum(jnp.where(jnp.logical_and(drow >= 1, drow <= l_ref[0]), p, 0.0), axis=0, keepdims=True)
    log_lb = jnp.log(jnp.maximum(lb, LB_FLOOR))
    log_1m = jnp.log1p(-lb)

    zf = zf_ref[...]
    a2 = log_1m + _log_sigmoid(zf)
    hi = jnp.maximum(log_lb, a2)
    lf = hi + jnp.log1p(jnp.exp(-jnp.abs(log_lb - a2)))
    kk = (1.0 - lb) * jax.nn.sigmoid(-zf) + (lb - jnp.maximum(lb, LB_FLOOR))
    zq = zq_ref[...]
    qq = zq * jax.nn.sigmoid(zq) * (HG_DK ** -0.5)
    lvl = lvl_ref[...]
    og = og_ref[0]

    for ci in range(tc // lc):
        r0 = ci * lc
        ex = _dot_exact01(lvl, lf[r0:r0 + lc, :])
        for h in range(HG_H):
            sl = slice(h * HG_DK, (h + 1) * HG_DK)
            q = qq[r0:r0 + lc, sl]
            k = kk[r0:r0 + lc, sl]
            iv = zi_ref[r0:r0 + lc, sl].astype(BF16)
            a_in = ex[0:lc, sl]
            a_end = a_in[lc - 1:lc, :]
            st = st_ref[h]
            o = _dot_nt((q * jnp.exp(a_in)).astype(BF16), st.astype(BF16))
            att = jnp.where(msk_ref[nlev] > 0, _dot_nt(q.astype(BF16), k.astype(BF16)), 0.0)
            for lv in range(nlev):
                xf = jnp.exp(ex[(1 + lv) * lc:(2 + lv) * lc, sl])
                pm = _dot_nt((q * xf).astype(BF16), (k * xf).astype(BF16))
                att = att + jnp.where(msk_ref[lv] > 0, pm, 0.0)
            o = o + _dot(att.astype(BF16), iv)
            k_end = (k * jnp.exp(a_end - a_in)).astype(BF16)
            st_ref[h] = st * jnp.exp(a_end) + _dot_tn(iv, k_end)
            gsl = zg_ref[r0:r0 + lc, sl]
            y_ref[r0:r0 + lc, sl] = (_rms(o, og) * (gsl * jax.nn.sigmoid(gsl))).astype(BF16)

    @pl.when(c == nc - 1)
    def _():
        for h in range(HG_H):
            s_ref[0, h] = st_ref[h].T


def hgrn_mixer(z, lidx, lb_logits, out_g, s0, *, nb, seq, tc, lc):
    n_rows = nb * seq
    nc = seq // tc
    lvl_np, msk_np = _hgrn_level_mats(lc)
    nlev = msk_np.shape[0] - 1
    lvl = jnp.asarray(lvl_np, BF16)
    msk = jnp.asarray(msk_np, F32)
    zmap = lambda blk: (lambda b, c, l: (b * nc + c, blk))
    vmem = 2 * 5 * tc * HG_W * 4 + 5 * HG_H * HG_DK * HG_DV * 4 + 8 * tc * HG_W * 4 + (8 << 20)
    kern = functools.partial(_hgrn_kernel, tc=tc, lc=lc, nlev=nlev)
    return pl.pallas_call(
        kern,
        grid_spec=pltpu.PrefetchScalarGridSpec(
            num_scalar_prefetch=1,
            grid=(nb, nc),
            in_specs=[
                pl.BlockSpec((tc, HG_W), zmap(ZB_HGQ)),
                pl.BlockSpec((tc, HG_W), zmap(ZB_HGF)),
                pl.BlockSpec((tc, HG_W), zmap(ZB_HGI)),
                pl.BlockSpec((tc, HG_W), zmap(ZB_HGG)),
                pl.BlockSpec((DEPTH, HG_W), lambda b, c, l: (0, 0)),
                pl.BlockSpec((1, 1, HG_DV), lambda b, c, l: (l[0], 0, 0)),
                pl.BlockSpec((1, HG_H, HG_DK, HG_DV), lambda b, c, l: (b, 0, 0, 0)),
                pl.BlockSpec(lvl_np.shape, lambda b, c, l: (0, 0)),
                pl.BlockSpec(msk_np.shape, lambda b, c, l: (0, 0, 0)),
            ],
            out_specs=[
                pl.BlockSpec((tc, HG_W), lambda b, c, l: (b * nc + c, 0)),
                pl.BlockSpec((1, HG_H, HG_DK, HG_DV), lambda b, c, l: (b, 0, 0, 0)),
            ],
            scratch_shapes=[pltpu.VMEM((HG_H, HG_DV, HG_DK), F32)],
        ),
        out_shape=[
            jax.ShapeDtypeStruct((n_rows, HG_W), BF16),
            jax.ShapeDtypeStruct((nb, HG_H, HG_DK, HG_DV), F32),
        ],
        compiler_params=_cparams(("parallel", "arbitrary"), vmem),
        name="hgrn",
    )(lidx, z, z, z, z, lb_logits, out_g, s0, lvl, msk)


def _qk_prep_kernel(l_ref, zq_ref, zk_ref, zv_ref, qg_ref, kg_ref, grp_ref, qn_ref, kn_ref, knb_ref, vb_ref):
    grp = grp_ref[...]

    def norm(x, g):
        x2 = x * x
        hi = x2.astype(BF16)
        lo = (x2 - hi.astype(F32)).astype(BF16)
        ms = _dot(hi, grp) + _dot(lo, grp)
        return x * lax.rsqrt(ms + EPS) * g

    qn_ref[...] = (norm(zq_ref[...], qg_ref[0]) * (DF_DH ** -0.5)).astype(BF16)
    kn = norm(zk_ref[...], kg_ref[0])
    kn_ref[...] = kn
    knb_ref[...] = kn.astype(BF16)
    vb_ref[...] = zv_ref[...].astype(BF16)


def qk_prep(z, lidx, qg, kg, *, tm):
    n_rows = z.shape[0]
    lane = np.arange(DF_W)
    grp = jnp.asarray(((lane[:, None] // DF_DH) == (lane[None, :] // DF_DH)).astype(np.float32) / DF_DH, BF16)
    zmap = lambda blk: (lambda i, l: (i, blk))
    row = lambda i, l: (i, 0)
    return pl.pallas_call(
        _qk_prep_kernel,
        grid_spec=pltpu.PrefetchScalarGridSpec(
            num_scalar_prefetch=1,
            grid=(n_rows // tm,),
            in_specs=[
                pl.BlockSpec((tm, DF_W), zmap(ZB_DFQ)),
                pl.BlockSpec((tm, DF_W), zmap(ZB_DFK)),
                pl.BlockSpec((tm, DF_W), zmap(ZB_DFV)),
                pl.BlockSpec((1, 1, DF_W), lambda i, l: (l[0], 0, 0)),
                pl.BlockSpec((1, 1, DF_W), lambda i, l: (l[0], 0, 0)),
                pl.BlockSpec((DF_W, DF_W), lambda i, l: (0, 0)),
            ],
            out_specs=[pl.BlockSpec((tm, DF_W), row)] * 4,
        ),
        out_shape=[
            jax.ShapeDtypeStruct((n_rows, DF_W), BF16),
            jax.ShapeDtypeStruct((n_rows, DF_W), F32),
            jax.ShapeDtypeStruct((n_rows, DF_W), BF16),
            jax.ShapeDtypeStruct((n_rows, DF_W), BF16),
        ],
        compiler_params=_cparams(("parallel",), 32 << 20),
        name="qk_prep",
    )(lidx, z, z, z, qg, kg, grp)


def _flash_kernel(l_ref, q_ref, km_ref, vm_ref, kx_ref, vx_ref, lam_ref, og_ref, y_ref,
                  m_ref, l_sum_ref, acc_ref, *, tq, tk, past, prompt):
    h = pl.program_id(1)
    qi = pl.program_id(2)
    slope = jnp.exp2(jnp.full((1, 1), -8.0 / DF_H, F32) * (h + 1).astype(F32))
    q = q_ref[...]
    lane = lax.broadcasted_iota(jnp.int32, (tq, DF_DV), 1)
    zero = jnp.zeros_like(q)
    q2 = jnp.concatenate([jnp.where(lane < DF_DH, q, zero), jnp.where(lane >= DF_DH, q, zero)], axis=0)
    q0 = past + qi * tq

    m_ref[...] = jnp.full(m_ref.shape, NEG_BIG, F32)
    l_sum_ref[...] = jnp.zeros(l_sum_ref.shape, F32)
    acc_ref[...] = jnp.zeros(acc_ref.shape, F32)

    def update(s, v):
        m_old = m_ref[...]
        m_new = jnp.maximum(m_old, jnp.max(s, axis=-1, keepdims=True))
        alpha = jnp.exp(m_old - m_new)
        p = jnp.exp(s - m_new)
        l_sum_ref[...] = alpha * l_sum_ref[...] + jnp.sum(p, axis=-1, keepdims=True)
        acc_ref[...] = alpha * acc_ref[...] + _dot(p.astype(BF16), v)
        m_ref[...] = m_new

    def fast_block(kj, carry):
        r0 = pl.multiple_of(kj * tk, tk)
        k = km_ref[pl.ds(r0, tk), :].astype(BF16)
        v = vm_ref[pl.ds(r0, tk), :].astype(BF16)
        kpos = r0 + lax.broadcasted_iota(jnp.int32, (1, tk), 1)
        s = _dot_nt(q2, k) + slope * (kpos - q0).astype(F32)
        update(s, v)
        return carry

    n_fast = qi if prompt else past // tk
    lax.fori_loop(0, n_fast, fast_block, 0)

    if prompt:
        r0 = pl.multiple_of(qi * tq, tq)
        k = km_ref[pl.ds(r0, tq), :].astype(BF16)
        v = vm_ref[pl.ds(r0, tq), :].astype(BF16)
    else:
        k = kx_ref[...].astype(BF16)
        v = vx_ref[...].astype(BF16)
    rel_k = lax.broadcasted_iota(jnp.int32, (2 * tq, tq), 1)
    rel_q = lax.broadcasted_iota(jnp.int32, (2 * tq, tq), 0)
    rel_q = jnp.where(rel_q >= tq, rel_q - tq, rel_q)
    dist = jnp.abs(rel_q - rel_k).astype(F32)
    s = _dot_nt(q2, k) + slope * (rel_q.astype(F32) - dist)
    visible = ((q0 + rel_k) >> CHUNK_SHIFT) <= ((q0 + rel_q) >> CHUNK_SHIFT)
    update(jnp.where(visible, s, NEG_BIG), v)

    lam_p = lam_ref[0]
    lam_init = 0.8 - 0.6 * jnp.exp(jnp.full((1, 1), -0.3, F32) * l_ref[0].astype(F32))
    lam = (jnp.exp(jnp.sum(lam_p[0:1] * lam_p[1:2], axis=-1, keepdims=True))
           - jnp.exp(jnp.sum(lam_p[2:3] * lam_p[3:4], axis=-1, keepdims=True)) + lam_init)
    o_all = acc_ref[...] / l_sum_ref[...]
    o = o_all[0:tq] - lam * o_all[tq:2 * tq]
    y_ref[...] = (_rms(o, og_ref[0]) * (1.0 - lam_init)).astype(BF16)


def diff_flash(qn, k_main, v_main, k_new, v_new, lidx, lam_p, out_g, *, nb, seq, past, tq, tk, prompt,
               main_rows, main_block):
    nq = seq // tq
    n_rows = nb * seq
    kern = functools.partial(_flash_kernel, tq=tq, tk=tk, past=past, prompt=prompt)
    vmem = 4 * main_rows * DF_DV * k_main.dtype.itemsize + 8 * (2 * tq) * max(tk, tq) * 4 + (8 << 20)
    qmap = lambda b, h, i, l: (b * nq + i, h)
    return pl.pallas_call(
        kern,
        grid_spec=pltpu.PrefetchScalarGridSpec(
            num_scalar_prefetch=1,
            grid=(nb, DF_H, nq),
            in_specs=[
                pl.BlockSpec((tq, DF_DV), qmap),
                pl.BlockSpec((main_rows, DF_DV), lambda b, h, i, l: (main_block(l, b), h)),
                pl.BlockSpec((main_rows, DF_DV), lambda b, h, i, l: (main_block(l, b), h)),
                pl.BlockSpec((tq, DF_DV), qmap),
                pl.BlockSpec((tq, DF_DV), qmap),
                pl.BlockSpec((1, 4, DF_DH), lambda b, h, i, l: (l[0], 0, 0)),
                pl.BlockSpec((1, 1, DF_DV), lambda b, h, i, l: (l[0], 0, 0)),
            ],
            out_specs=pl.BlockSpec((tq, DF_DV), qmap),
            scratch_shapes=[pltpu.VMEM((2 * tq, 1), F32), pltpu.VMEM((2 * tq, 1), F32),
                            pltpu.VMEM((2 * tq, DF_DV), F32)],
        ),
        out_shape=jax.ShapeDtypeStruct((n_rows, DF_W), BF16),
        compiler_params=_cparams(("parallel", "parallel", "arbitrary"), vmem),
        name="diff_flash_prompt" if prompt else "diff_flash_sample",
    )(lidx, qn, k_main, v_main, k_new, v_new, lam_p, out_g)


def _trunk_layer(x, lidx, w, st, cfg):
    nb, seq, past = cfg["nb"], cfg["seq"], cfg["past"]
    z, gates = in_proj(x, lidx, w["norm_mix"], w["w_main"], w["w_gate"], tm=cfg["tm_in"], tn=cfg["tn_in"])
    y_ml, c_new, n_new, m_new = mlstm_mixer(
        z, gates, lidx, st["hist0"], w["conv_w"], w["conv_b"], w["b_if"], w["ml_og"],
        st["c0"], st["n0"], st["m0"], nb=nb, seq=seq, tc=cfg["tc"], lc=cfg["lc_ml"])
    qn, kn, kn_b, v_b = qk_prep(z, lidx, w["df_qg"], w["df_kg"], tm=cfg["tm_prep"])
    if cfg["prompt"]:
        y_df = diff_flash(qn, kn_b, v_b, kn_b, v_b, lidx, w["df_lam"], w["df_og"], nb=nb, seq=seq, past=0,
                          tq=cfg["tq"], tk=cfg["tq"], prompt=True, main_rows=seq,
                          main_block=lambda l, b: b)
    else:
        y_df = diff_flash(qn, st["past_k"], st["past_v"], kn_b, v_b, lidx, w["df_lam"], w["df_og"], nb=nb,
                          seq=seq, past=past, tq=seq, tk=cfg["tk_past"], prompt=False, main_rows=past,
                          main_block=lambda l, b: l[0] * nb + b)
    y_hg, s_new = hgrn_mixer(z, lidx, w["hg_lbl"], w["hg_og"], st["s0"], nb=nb, seq=seq, tc=cfg["tc"],
                             lc=cfg["lc_hg"])
    x = post_mix(x, y_ml, y_df, y_hg, lidx, w["w_out"], w["norm_cross"], w["wq"], w["ca_qg"],
                 st["mk"], st["mv"], w["wo"], tm=cfg["tm_post"], rows_per_batch=seq)
    x = ffn(x, lidx, w["norm_ffn"], w["w_gu"], w["w_down"], w["final_norm"], tm=cfg["tm_ffn"], th=cfg["th"])
    z3 = z.reshape(nb, seq, Z_W)
    new = {
        "attn_k": kn.reshape(nb, seq, DF_H, 2 * DF_DH),
        "attn_v": z3[:, :, ZB_DFV * DF_W:(ZB_DFV + 1) * DF_W].reshape(nb, seq, DF_H, DF_DV),
        "conv": z3[:, seq - (CONV_W - 1):, 0:2 * ML_W],
        "C": c_new,
        "n": n_new[:, :ML_H, :],
        "m": m_new[:, 0, :ML_H],
        "S": s_new,
    }
    return x, new


PROMPT_CFG = dict(prompt=True, tm_in=1024, tn_in=768, tc=256, lc_ml=64, lc_hg=64, tm_prep=512, tq=512,
                  tm_post=512, tm_ffn=512, th=512)
SAMPLE_CFG = dict(prompt=False, tm_in=128, tn_in=768, tc=16, lc_ml=16, lc_hg=16, tm_prep=128, tk_past=512,
                  tm_post=16, tm_ffn=128, th=512)


def kernel(x_prompt, x_sample, mem_prompt, cache_attn_k, cache_attn_v, cache_mem_k, cache_mem_v, state_mlstm_conv, state_mlstm_C, state_mlstm_n, state_mlstm_m, state_hgrn_S, norm_mix, w_in, mlstm_conv_w, mlstm_conv_b, mlstm_b_i, mlstm_b_f, mlstm_out_norm, diff_q_norm, diff_k_norm, diff_lambda, diff_out_norm, hgrn_lb_logits, hgrn_out_norm, w_out, norm_cross, norm_mem, cross_wq, cross_wk, cross_wv, cross_q_norm, cross_k_norm, cross_wo, norm_ffn, ffn_w_gate_up, ffn_w_down, final_norm):
    bp, tp = x_prompt.shape[:2]
    bs, ts = x_sample.shape[:2]
    past = cache_attn_k.shape[2]
    depth = w_in.shape[0]
    assert depth == DEPTH and x_prompt.shape[2] == D_MODEL

    g_off = 4 * ML_W
    r3 = lambda a: a.reshape(depth, 1, a.shape[-1])
    w = {
        "norm_mix": r3(norm_mix),
        "w_main": jnp.concatenate([w_in[:, :, :g_off], w_in[:, :, g_off + 2 * ML_H:]], axis=-1).astype(BF16),
        "w_gate": jnp.pad(w_in[:, :, g_off:g_off + 2 * ML_H], ((0, 0), (0, 0), (0, GATE_W - 2 * ML_H))).astype(BF16),
        "conv_w": mlstm_conv_w,
        "conv_b": r3(mlstm_conv_b),
        "b_if": r3(jnp.pad(jnp.concatenate([mlstm_b_i, mlstm_b_f], axis=-1), ((0, 0), (0, GATE_W - 2 * ML_H)))),
        "ml_og": r3(mlstm_out_norm),
        "df_qg": r3(jnp.tile(diff_q_norm, (1, DF_W // DF_DH))),
        "df_kg": r3(jnp.tile(diff_k_norm, (1, DF_W // DF_DH))),
        "df_lam": diff_lambda,
        "df_og": r3(diff_out_norm),
        "hg_lbl": hgrn_lb_logits,
        "hg_og": r3(hgrn_out_norm),
        "w_out": w_out.astype(BF16),
        "norm_cross": r3(norm_cross),
        "wq": cross_wq.astype(BF16),
        "ca_qg": r3(cross_q_norm),
        "wo": cross_wo.astype(BF16),
        "norm_ffn": r3(norm_ffn),
        "w_gu": ffn_w_gate_up.astype(BF16),
        "w_down": ffn_w_down.astype(BF16),
        "final_norm": final_norm.reshape(1, D_MODEL),
    }

    mk_p, mv_p = mem_kv(mem_prompt, r3(norm_mem), cross_wk.astype(BF16), cross_wv.astype(BF16), r3(cross_k_norm))

    def pad_hist(conv):
        pad = [(0, 0)] * (conv.ndim - 2) + [(HIST - (CONV_W - 1), 0), (0, 0)]
        return jnp.pad(conv, pad)

    st_p = {
        "hist0": jnp.zeros((bp, HIST, 2 * ML_W), F32),
        "c0": jnp.zeros((bp, ML_H, ML_DH, ML_DH), F32),
        "n0": jnp.zeros((bp, 8, ML_DH), F32),
        "m0": jnp.zeros((bp, 1, GATE_W), F32),
        "s0": jnp.zeros((bp, HG_H, HG_DK, HG_DV), F32),
        "mk": mk_p, "mv": mv_p,
    }
    hist_s = pad_hist(state_mlstm_conv)
    n_s = jnp.pad(state_mlstm_n, ((0, 0), (0, 0), (0, 8 - ML_H), (0, 0)))
    m_s = jnp.pad(state_mlstm_m, ((0, 0), (0, 0), (0, GATE_W - ML_H))).reshape(depth, bs, 1, GATE_W)
    past_k = cache_attn_k.reshape(depth * bs * past, DF_W)
    past_v = cache_attn_v.reshape(depth * bs * past, DF_W)
    mk_s = cache_mem_k.reshape(depth * bs, N_MEM, CA_W)
    mv_s = cache_mem_v.reshape(depth * bs, N_MEM, CA_W)

    cfg_p = dict(PROMPT_CFG, nb=bp, seq=tp, past=0)
    cfg_s = dict(SAMPLE_CFG, nb=bs, seq=ts, past=past)

    def layer(carry, xs):
        xp, xsm = carry
        l, hist_l, c_l, n_l, m_l, s_l = xs
        lidx = l.reshape(1).astype(jnp.int32)
        xp, new_p = _trunk_layer(xp, lidx, w, st_p, cfg_p)
        st_s = {"hist0": hist_l, "c0": c_l, "n0": n_l, "m0": m_l, "s0": s_l,
                "past_k": past_k, "past_v": past_v, "mk": mk_s, "mv": mv_s}
        xsm, new_s = _trunk_layer(xsm, lidx, w, st_s, cfg_s)
        return (xp, xsm), (new_p, new_s)

    xs = (jnp.arange(depth, dtype=jnp.int32), hist_s, state_mlstm_C, n_s, m_s, state_hgrn_S)
    (xp, xsm), (new_p, new_s) = lax.scan(
        layer, (x_prompt.reshape(bp * tp, D_MODEL), x_sample.reshape(bs * ts, D_MODEL)), xs)

    y_prompt = xp.reshape(bp, tp, D_MODEL)
    y_sample = xsm.reshape(bs, ts, D_MODEL)
    p_mem_k = mk_p.reshape(depth, bp, N_MEM, CA_H, CA_DH)
    p_mem_v = mv_p.reshape(depth, bp, N_MEM, CA_H, CA_DH)
    return (y_prompt, y_sample,
            new_p["attn_k"], new_p["attn_v"], p_mem_k, p_mem_v, new_p["conv"], new_p["C"], new_p["n"],
            new_p["m"], new_p["S"],
            new_s["attn_k"], new_s["attn_v"], new_s["conv"], new_s["C"], new_s["n"], new_s["m"], new_s["S"])
```

```python
import functools
import math

import numpy as np
import jax
import jax.numpy as jnp
from jax import lax
from jax.experimental import pallas as pl
from jax.experimental.pallas import tpu as pltpu

F32 = jnp.float32
BF16 = jnp.bfloat16

D_MODEL = 2048
DEPTH = 4
CHUNK = 64
CHUNK_SHIFT = 6
ML_DH = 128
ML_W = 768
ML_H = 6
CONV_W = 4
DF_DH = 64
DF_DV = 128
DF_W = 512
DF_H = 4
HG_DK = 128
HG_DV = 128
HG_W = 768
HG_H = 6
CA_H = 4
CA_DH = 128
CA_W = 512
N_MEM = 256
FF_HIDDEN = 5632
EPS = 1e-6
NEG_BIG = -1e30
LB_FLOOR = 1e-30

Z_W = 2 * ML_W + 2 * ML_W + 3 * DF_W + 4 * HG_W
GATE_W = 128
ZB_MLQ, ZB_MLK, ZB_MLV, ZB_MLO = 0, 1, 2, 3
ZB_HGQ, ZB_HGF, ZB_HGI, ZB_HGG = 6, 7, 8, 9
ZB_DFQ, ZB_DFK, ZB_DFV = 6, 7, 8

V7X_VMEM_BYTES = 64 * 1024 * 1024
V7X_VMEM_CAP = 60000 * 1024
HIST = 8

NT_DIMS = (((1,), (1,)), ((), ()))
TN_DIMS = (((0,), (0,)), ((), ()))


def _vmem_limit(nbytes):
    return int(min(V7X_VMEM_CAP, max(32 * 1024 * 1024, nbytes)))


def _cparams(sem, vmem_bytes):
    return pltpu.CompilerParams(dimension_semantics=sem, vmem_limit_bytes=_vmem_limit(vmem_bytes))


def _rms(x, g):
    ms = jnp.mean(x * x, axis=-1, keepdims=True)
    return x * lax.rsqrt(ms + EPS) * g


def _dot(a, b):
    return jnp.dot(a, b, preferred_element_type=F32)


def _dot_nt(a, b):
    return lax.dot_general(a, b, NT_DIMS, preferred_element_type=F32)


def _dot_tn(a, b):
    return lax.dot_general(a, b, TN_DIMS, preferred_element_type=F32)


def _log_sigmoid(x):
    return jnp.minimum(x, 0.0) - jnp.log1p(jnp.exp(-jnp.abs(x)))


def _split3(x):
    hi = x.astype(BF16)
    r1 = x - hi.astype(F32)
    mid = r1.astype(BF16)
    lo = (r1 - mid.astype(F32)).astype(BF16)
    return hi, mid, lo


def _dot_exact01(m01, x):
    hi, mid, lo = _split3(x)
    return _dot(m01, hi) + _dot(m01, mid) + _dot(m01, lo)


def _in_proj_kernel(l_ref, x_ref, g_ref, w_ref, wg_ref, z_ref, gate_ref, xn_ref):
    j = pl.program_id(1)

    @pl.when(j == 0)
    def _():
        xn = _rms(x_ref[...], g_ref[0]).astype(BF16)
        xn_ref[...] = xn
        gate_ref[...] = _dot(xn, wg_ref[0])

    z_ref[...] = _dot(xn_ref[...], w_ref[0])


def in_proj(x, lidx, g, w_main, w_gate, *, tm, tn):
    n_rows = x.shape[0]
    grid = (n_rows // tm, Z_W // tn)
    vmem = 2 * tm * D_MODEL * 4 + 2 * D_MODEL * tn * 2 + 2 * tm * tn * 4 + tm * D_MODEL * 2 \
        + 2 * tm * GATE_W * 4 + 2 * D_MODEL * GATE_W * 2 + (4 << 20)
    return pl.pallas_call(
        _in_proj_kernel,
        grid_spec=pltpu.PrefetchScalarGridSpec(
            num_scalar_prefetch=1,
            grid=grid,
            in_specs=[
                pl.BlockSpec((tm, D_MODEL), lambda i, j, l: (i, 0)),
                pl.BlockSpec((1, 1, D_MODEL), lambda i, j, l: (l[0], 0, 0)),
                pl.BlockSpec((1, D_MODEL, tn), lambda i, j, l: (l[0], 0, j)),
                pl.BlockSpec((1, D_MODEL, GATE_W), lambda i, j, l: (l[0], 0, 0)),
            ],
            out_specs=[
                pl.BlockSpec((tm, tn), lambda i, j, l: (i, j)),
                pl.BlockSpec((tm, GATE_W), lambda i, j, l: (i, 0)),
            ],
            scratch_shapes=[pltpu.VMEM((tm, D_MODEL), BF16)],
        ),
        out_shape=[jax.ShapeDtypeStruct((n_rows, Z_W), F32),
                   jax.ShapeDtypeStruct((n_rows, GATE_W), F32)],
        compiler_params=_cparams(("parallel", "arbitrary"), vmem),
        name="in_proj",
    )(lidx, x, g, w_main, w_gate)


def _ffn_kernel(l_ref, x_ref, g_ref, wg_ref, wu_ref, wd_ref, fg_ref, o_ref, xn_ref):
    j = pl.program_id(1)
    nj = pl.num_programs(1)

    @pl.when(j == 0)
    def _():
        xn_ref[...] = _rms(x_ref[...], g_ref[0]).astype(BF16)

    xn = xn_ref[...]
    gt = _dot(xn, wg_ref[0])
    up = _dot(xn, wu_ref[0])
    act = (gt * jax.nn.sigmoid(gt) * up).astype(BF16)
    down = _dot(act, wd_ref[0])

    @pl.when(j == 0)
    def _():
        o_ref[...] = x_ref[...] + down

    @pl.when(j > 0)
    def _():
        o_ref[...] += down

    @pl.when(jnp.logical_and(j == nj - 1, l_ref[0] == DEPTH - 1))
    def _():
        o_ref[...] = _rms(o_ref[...], fg_ref[...])


def ffn(x, lidx, g, w_gu, w_down, final_g, *, tm, th):
    n_rows = x.shape[0]
    nh = FF_HIDDEN // th
    grid = (n_rows // tm, nh)
    vmem = 4 * tm * D_MODEL * 4 + tm * D_MODEL * 2 + 6 * D_MODEL * th * 2 + 3 * tm * th * 4 \
        + tm * D_MODEL * 4 + (4 << 20)
    return pl.pallas_call(
        _ffn_kernel,
        grid_spec=pltpu.PrefetchScalarGridSpec(
            num_scalar_prefetch=1,
            grid=grid,
            in_specs=[
                pl.BlockSpec((tm, D_MODEL), lambda i, j, l: (i, 0)),
                pl.BlockSpec((1, 1, D_MODEL), lambda i, j, l: (l[0], 0, 0)),
                pl.BlockSpec((1, D_MODEL, th), lambda i, j, l: (l[0], 0, j)),
                pl.BlockSpec((1, D_MODEL, th), lambda i, j, l: (l[0], 0, j + nh)),
                pl.BlockSpec((1, th, D_MODEL), lambda i, j, l: (l[0], j, 0)),
                pl.BlockSpec((1, D_MODEL), lambda i, j, l: (0, 0)),
            ],
            out_specs=pl.BlockSpec((tm, D_MODEL), lambda i, j, l: (i, 0)),
            scratch_shapes=[pltpu.VMEM((tm, D_MODEL), BF16)],
        ),
        out_shape=jax.ShapeDtypeStruct((n_rows, D_MODEL), F32),
        compiler_params=_cparams(("parallel", "arbitrary"), vmem),
        name="ffn",
    )(lidx, x, g, w_gu, w_gu, w_down, final_g)


def _mem_kv_kernel(mem_ref, g_ref, wk_ref, wv_ref, kg_ref, k_ref, v_ref):
    mn = _rms(mem_ref[0], g_ref[0]).astype(BF16)
    k = _dot(mn, wk_ref[0])
    for h in range(CA_H):
        sl = slice(h * CA_DH, (h + 1) * CA_DH)
        k_ref[0, :, sl] = _rms(k[:, sl], kg_ref[0])
    v_ref[0] = _dot(mn, wv_ref[0])


def mem_kv(mem, g, wk, wv, kg):
    nb = mem.shape[0]
    out = jax.ShapeDtypeStruct((DEPTH * nb, N_MEM, CA_W), F32)
    return pl.pallas_call(
        _mem_kv_kernel,
        grid=(DEPTH, nb),
        in_specs=[
            pl.BlockSpec((1, N_MEM, D_MODEL), lambda l, b: (b, 0, 0)),
            pl.BlockSpec((1, 1, D_MODEL), lambda l, b: (l, 0, 0)),
            pl.BlockSpec((1, D_MODEL, CA_W), lambda l, b: (l, 0, 0)),
            pl.BlockSpec((1, D_MODEL, CA_W), lambda l, b: (l, 0, 0)),
            pl.BlockSpec((1, 1, CA_DH), lambda l, b: (l, 0, 0)),
        ],
        out_specs=[pl.BlockSpec((1, N_MEM, CA_W), lambda l, b: (l * nb + b, 0, 0)),
                   pl.BlockSpec((1, N_MEM, CA_W), lambda l, b: (l * nb + b, 0, 0))],
        out_shape=[out, out],
        compiler_params=_cparams(("arbitrary", "arbitrary"), 24 << 20),
        name="mem_kv",
    )(mem, g, wk, wv, kg)


def _post_kernel(l_ref, x_ref, yml_ref, ydf_ref, yhg_ref, wout_ref, gx_ref, wq_ref, qg_ref,
                 mk_ref, mv_ref, wo_ref, o_ref):
    x1 = x_ref[...]
    x1 = x1 + _dot(yml_ref[...], wout_ref[0, 0:ML_W, :])
    x1 = x1 + _dot(ydf_ref[...], wout_ref[0, ML_W:ML_W + DF_W, :])
    x1 = x1 + _dot(yhg_ref[...], wout_ref[0, ML_W + DF_W:D_MODEL, :])
    hn = _rms(x1, gx_ref[0]).astype(BF16)
    q = _dot(hn, wq_ref[0])
    heads = []
    for h in range(CA_H):
        sl = slice(h * CA_DH, (h + 1) * CA_DH)
        qh = _rms(q[:, sl], qg_ref[0]).astype(BF16)
        s = _dot_nt(qh, mk_ref[0, :, sl].astype(BF16)) * (CA_DH ** -0.5)
        s = s - jnp.max(s, axis=-1, keepdims=True)
        p = jnp.exp(s)
        p = p / jnp.sum(p, axis=-1, keepdims=True)
        heads.append(_dot(p.astype(BF16), mv_ref[0, :, sl].astype(BF16)))
    o = jnp.concatenate(heads, axis=-1).astype(BF16)
    o_ref[...] = x1 + _dot(o, wo_ref[0])


def post_mix(x, y_ml, y_df, y_hg, lidx, w_out, gx, wq, qg, mk, mv, wo, *, tm, rows_per_batch):
    n_rows = x.shape[0]
    nb = n_rows // rows_per_batch
    tiles_per_batch = rows_per_batch // tm
    grid = (n_rows // tm,)
    w_bytes = (D_MODEL * D_MODEL + 2 * D_MODEL * CA_W) * 2
    vmem = 2 * w_bytes + 6 * tm * D_MODEL * 4 + 2 * tm * D_MODEL * 2 + 4 * N_MEM * CA_W * 4 + (6 << 20)
    row = lambda i, l: (i, 0)
    mem_map = lambda i, l: (l[0] * nb + i // tiles_per_batch, 0, 0)
    return pl.pallas_call(
        _post_kernel,
        grid_spec=pltpu.PrefetchScalarGridSpec(
            num_scalar_prefetch=1,
            grid=grid,
            in_specs=[
                pl.BlockSpec((tm, D_MODEL), row),
                pl.BlockSpec((tm, ML_W), row),
                pl.BlockSpec((tm, DF_W), row),
                pl.BlockSpec((tm, HG_W), row),
                pl.BlockSpec((1, D_MODEL, D_MODEL), lambda i, l: (l[0], 0, 0)),
                pl.BlockSpec((1, 1, D_MODEL), lambda i, l: (l[0], 0, 0)),
                pl.BlockSpec((1, D_MODEL, CA_W), lambda i, l: (l[0], 0, 0)),
                pl.BlockSpec((1, 1, CA_DH), lambda i, l: (l[0], 0, 0)),
                pl.BlockSpec((1, N_MEM, CA_W), mem_map),
                pl.BlockSpec((1, N_MEM, CA_W), mem_map),
                pl.BlockSpec((1, CA_W, D_MODEL), lambda i, l: (l[0], 0, 0)),
            ],
            out_specs=pl.BlockSpec((tm, D_MODEL), row),
        ),
        out_shape=jax.ShapeDtypeStruct((n_rows, D_MODEL), F32),
        compiler_params=_cparams(("parallel",), vmem),
        name="post_mix",
    )(lidx, x, y_ml, y_df, y_hg, w_out, gx, wq, qg, mk, mv, wo)


def _mlstm_kernel(l_ref, zq_ref, zk_ref, zv_ref, zo_ref, gate_ref, hist_ref, cw_ref, cb_ref, bif_ref,
                  og_ref, c0_ref, n0_ref, m0_ref, tri_ref,
                  y_ref, c_ref, n_ref, m_ref, ext_ref, *, tc, lc):
    c = pl.program_id(1)

    @pl.when(c == 0)
    def _():
        c_ref[...] = c0_ref[...]
        n_ref[...] = n0_ref[...]
        m_ref[...] = m0_ref[...]
        ext_ref[0:HIST, :] = hist_ref[0]

    ext_ref[HIST:HIST + tc, 0:ML_W] = zq_ref[...]
    ext_ref[HIST:HIST + tc, ML_W:2 * ML_W] = zk_ref[...]
    acc = cb_ref[0]
    for j in range(CONV_W):
        acc = acc + cw_ref[0, j:j + 1, :] * ext_ref[pl.ds(HIST - (CONV_W - 1) + j, tc), :]
    tail = ext_ref[tc:tc + HIST, :]
    ext_ref[0:HIST, :] = tail
    qk = acc * jax.nn.sigmoid(acc)

    gz = gate_ref[...] + bif_ref[0]
    lf_all = _log_sigmoid(gz)
    tri = tri_ref[...]
    row_i = lax.broadcasted_iota(jnp.int32, (lc, lc), 0)
    col_i = lax.broadcasted_iota(jnp.int32, (lc, lc), 1)
    causal = row_i >= col_i
    diag = row_i == col_i
    og = og_ref[0]

    for ci in range(tc // lc):
        r0 = ci * lc
        ig_c = gz[r0:r0 + lc, :]
        b_c = _dot_exact01(tri, lf_all[r0:r0 + lc, :])
        for h in range(ML_H):
            sl = slice(h * ML_DH, (h + 1) * ML_DH)
            q = qk[r0:r0 + lc, h * ML_DH:(h + 1) * ML_DH]
            k = qk[r0:r0 + lc, ML_W + h * ML_DH:ML_W + (h + 1) * ML_DH] * (ML_DH ** -0.5)
            v = zv_ref[r0:r0 + lc, sl]
            qb = q.astype(BF16)
            b_col = b_c[:, ML_H + h:ML_H + h + 1]
            ig_col = ig_c[:, h:h + 1]
            r_row = jnp.sum(jnp.where(diag, ig_col - b_col, 0.0), axis=0, keepdims=True)
            m_prev = m_ref[0, :, h:h + 1]
            log_d = jnp.where(causal, b_col + r_row, NEG_BIG)
            inter = b_col + m_prev
            m_t = jnp.maximum(inter, jnp.max(log_d, axis=-1, keepdims=True))
            d_mat = jnp.exp(log_d - m_t)
            w_inter = jnp.exp(inter - m_t)
            s = _dot_nt(qb, k.astype(BF16)) * d_mat
            c_old = c_ref[0, h]
            n_old = n_ref[0, h:h + 1, :]
            num = _dot(s.astype(BF16), v.astype(BF16)) + w_inter * _dot(qb, c_old.astype(BF16))
            den = jnp.sum(s, axis=-1, keepdims=True) + w_inter * jnp.sum(q * n_old, axis=-1, keepdims=True)
            hh = num / jnp.maximum(jnp.abs(den), jnp.exp(-m_t))
            m_new = m_t[lc - 1:lc, :]
            b_last = b_col[lc - 1:lc, :]
            w_s = jnp.exp(b_last - b_col + ig_col - m_new)
            decay = jnp.exp(b_last + m_prev - m_new)
            kw = k * w_s
            c_ref[0, h] = decay * c_old + _dot_tn(kw.astype(BF16), v.astype(BF16))
            n_ref[0, h:h + 1, :] = decay * n_old + jnp.sum(kw, axis=0, keepdims=True)
            m_ref[0, :, h:h + 1] = m_new
            o_gate = jax.nn.sigmoid(zo_ref[r0:r0 + lc, sl])
            y_ref[r0:r0 + lc, sl] = (_rms(hh, og) * o_gate).astype(BF16)


def mlstm_mixer(z, gates, lidx, hist0, conv_w, conv_b, b_if, out_g, c0, n0, m0, *, nb, seq, tc, lc):
    n_rows = nb * seq
    nc = seq // tc
    tri = jnp.asarray(np.tril(np.ones((lc, lc), np.float32)), BF16)
    zmap = lambda blk: (lambda b, c, l: (b * nc + c, blk))
    lmap3 = lambda b, c, l: (l[0], 0, 0)
    vmem = 2 * 5 * tc * ML_W * 4 + (HIST + tc) * 2 * ML_W * 4 + 4 * ML_H * ML_DH * ML_DH * 4 \
        + 6 * tc * 2 * ML_W * 4 + (8 << 20)
    kern = functools.partial(_mlstm_kernel, tc=tc, lc=lc)
    return pl.pallas_call(
        kern,
        grid_spec=pltpu.PrefetchScalarGridSpec(
            num_scalar_prefetch=1,
            grid=(nb, nc),
            in_specs=[
                pl.BlockSpec((tc, ML_W), zmap(ZB_MLQ)),
                pl.BlockSpec((tc, ML_W), zmap(ZB_MLK)),
                pl.BlockSpec((tc, ML_W), zmap(ZB_MLV)),
                pl.BlockSpec((tc, ML_W), zmap(ZB_MLO)),
                pl.BlockSpec((tc, GATE_W), lambda b, c, l: (b * nc + c, 0)),
                pl.BlockSpec((1, HIST, 2 * ML_W), lambda b, c, l: (b, 0, 0)),
                pl.BlockSpec((1, CONV_W, 2 * ML_W), lmap3),
                pl.BlockSpec((1, 1, 2 * ML_W), lmap3),
                pl.BlockSpec((1, 1, GATE_W), lmap3),
                pl.BlockSpec((1, 1, ML_DH), lmap3),
                pl.BlockSpec((1, ML_H, ML_DH, ML_DH), lambda b, c, l: (b, 0, 0, 0)),
                pl.BlockSpec((1, 8, ML_DH), lambda b, c, l: (b, 0, 0)),
                pl.BlockSpec((1, 1, GATE_W), lambda b, c, l: (b, 0, 0)),
                pl.BlockSpec((lc, lc), lambda b, c, l: (0, 0)),
            ],
            out_specs=[
                pl.BlockSpec((tc, ML_W), lambda b, c, l: (b * nc + c, 0)),
                pl.BlockSpec((1, ML_H, ML_DH, ML_DH), lambda b, c, l: (b, 0, 0, 0)),
                pl.BlockSpec((1, 8, ML_DH), lambda b, c, l: (b, 0, 0)),
                pl.BlockSpec((1, 1, GATE_W), lambda b, c, l: (b, 0, 0)),
            ],
            scratch_shapes=[pltpu.VMEM((HIST + tc, 2 * ML_W), F32)],
        ),
        out_shape=[
            jax.ShapeDtypeStruct((n_rows, ML_W), BF16),
            jax.ShapeDtypeStruct((nb, ML_H, ML_DH, ML_DH), F32),
            jax.ShapeDtypeStruct((nb, 8, ML_DH), F32),
            jax.ShapeDtypeStruct((nb, 1, GATE_W), F32),
        ],
        compiler_params=_cparams(("parallel", "arbitrary"), vmem),
        name="mlstm",
    )(lidx, z, z, z, z, gates, hist0, conv_w, conv_b, b_if, out_g, c0, n0, m0, tri)


def _hgrn_level_mats(lc):
    t = np.arange(lc)[:, None]
    u = np.arange(lc)[None, :]
    mats = [(u <= t).astype(np.float32)]
    masks = []
    h = lc // 2
    while h >= 1:
        e = (t // (2 * h)) * (2 * h) + h - 1
        odd = (t % (2 * h)) >= h
        m = np.where(odd, (u > e) & (u <= t), (u > t) & (u <= e))
        mats.append(m.astype(np.float32))
        s = np.arange(lc)[None, :]
        same = (t // (2 * h)) == (s // (2 * h))
        masks.append((same & odd & ((s % (2 * h)) < h)).astype(np.float32))
        h //= 2
    masks.append((t == u).astype(np.float32))
    return np.concatenate(mats, axis=0), np.stack(masks, axis=0)


def _hgrn_kernel(l_ref, zq_ref, zf_ref, zi_ref, zg_ref, lbl_ref, og_ref, s0_ref, lvl_ref, msk_ref,
                 y_ref, s_ref, st_ref, *, tc, lc, nlev):
    c = pl.program_id(1)
    nc = pl.num_programs(1)

    @pl.when(c == 0)
    def _():
        for h in range(HG_H):
            st_ref[h] = s0_ref[0, h].T

    logits = lbl_ref[...]
    e = jnp.exp(logits - jnp.max(logits, axis=0, keepdims=True))
    p = e / jnp.sum(e, axis=0, keepdims=True)
    drow = lax.broadcasted_iota(jnp.int32, p.shape, 0)
    lb = jnp.sum(jnp.where(jnp.logical_and(drow >= 1, drow <= l_ref[0]), p, 0.0), axis=0, keepdims=True)
    log_lb = jnp.log(jnp.maximum(lb, LB_FLOOR))
    log_1m = jnp.log1p(-lb)

    zf = zf_ref[...]
    a2 = log_1m + _log_sigmoid(zf)
    hi = jnp.maximum(log_lb, a2)
    lf = hi + jnp.log1p(jnp.exp(-jnp.abs(log_lb - a2)))
    kk = (1.0 - lb) * jax.nn.sigmoid(-zf) + (lb - jnp.maximum(lb, LB_FLOOR))
    zq = zq_ref[...]
    qq = zq * jax.nn.sigmoid(zq) * (HG_DK ** -0.5)
    lvl = lvl_ref[...]
    og = og_ref[0]

    for ci in range(tc // lc):
        r0 = ci * lc
        ex = _dot_exact01(lvl, lf[r0:r0 + lc, :])
        for h in range(HG_H):
            sl = slice(h * HG_DK, (h + 1) * HG_DK)
            q = qq[r0:r0 + lc, sl]
            k = kk[r0:r0 + lc, sl]
            iv = zi_ref[r0:r0 + lc, sl].astype(BF16)
            a_in = ex[0:lc, sl]
            a_end = a_in[lc - 1:lc, :]
            st = st_ref[h]
            o = _dot_nt((q * jnp.exp(a_in)).astype(BF16), st.astype(BF16))
            att = jnp.where(msk_ref[nlev] > 0, _dot_nt(q.astype(BF16), k.astype(BF16)), 0.0)
            for lv in range(nlev):
                xf = jnp.exp(ex[(1 + lv) * lc:(2 + lv) * lc, sl])
                pm = _dot_nt((q * xf).astype(BF16), (k * xf).astype(BF16))
                att = att + jnp.where(msk_ref[lv] > 0, pm, 0.0)
            o = o + _dot(att.astype(BF16), iv)
            k_end = (k * jnp.exp(a_end - a_in)).astype(BF16)
            st_ref[h] = st * jnp.exp(a_end) + _dot_tn(iv, k_end)
            gsl = zg_ref[r0:r0 + lc, sl]
            y_ref[r0:r0 + lc, sl] = (_rms(o, og) * (gsl * jax.nn.sigmoid(gsl))).astype(BF16)

    @pl.when(c == nc - 1)
    def _():
        for h in range(HG_H):
            s_ref[0, h] = st_ref[h].T


def hgrn_mixer(z, lidx, lb_logits, out_g, s0, *, nb, seq, tc, lc):
    n_rows = nb * seq
    nc = seq // tc
    lvl_np, msk_np = _hgrn_level_mats(lc)
    nlev = msk_np.shape[0] - 1
    lvl = jnp.asarray(lvl_np, BF16)
    msk = jnp.asarray(msk_np, F32)
    zmap = lambda blk: (lambda b, c, l: (b * nc + c, blk))
    vmem = 2 * 5 * tc * HG_W * 4 + 5 * HG_H * HG_DK * HG_DV * 4 + 8 * tc * HG_W * 4 + (8 << 20)
    kern = functools.partial(_hgrn_kernel, tc=tc, lc=lc, nlev=nlev)
    return pl.pallas_call(
        kern,
        grid_spec=pltpu.PrefetchScalarGridSpec(
            num_scalar_prefetch=1,
            grid=(nb, nc),
            in_specs=[
                pl.BlockSpec((tc, HG_W), zmap(ZB_HGQ)),
                pl.BlockSpec((tc, HG_W), zmap(ZB_HGF)),
                pl.BlockSpec((tc, HG_W), zmap(ZB_HGI)),
                pl.BlockSpec((tc, HG_W), zmap(ZB_HGG)),
                pl.BlockSpec((DEPTH, HG_W), lambda b, c, l: (0, 0)),
                pl.BlockSpec((1, 1, HG_DV), lambda b, c, l: (l[0], 0, 0)),
                pl.BlockSpec((1, HG_H, HG_DK, HG_DV), lambda b, c, l: (b, 0, 0, 0)),
                pl.BlockSpec(lvl_np.shape, lambda b, c, l: (0, 0)),
                pl.BlockSpec(msk_np.shape, lambda b, c, l: (0, 0, 0)),
            ],
            out_specs=[
                pl.BlockSpec((tc, HG_W), lambda b, c, l: (b * nc + c, 0)),
                pl.BlockSpec((1, HG_H, HG_DK, HG_DV), lambda b, c, l: (b, 0, 0, 0)),
            ],
            scratch_shapes=[pltpu.VMEM((HG_H, HG_DV, HG_DK), F32)],
        ),
        out_shape=[
            jax.ShapeDtypeStruct((n_rows, HG_W), BF16),
            jax.ShapeDtypeStruct((nb, HG_H, HG_DK, HG_DV), F32),
        ],
        compiler_params=_cparams(("parallel", "arbitrary"), vmem),
        name="hgrn",
    )(lidx, z, z, z, z, lb_logits, out_g, s0, lvl, msk)


def _qk_prep_kernel(l_ref, zq_ref, zk_ref, zv_ref, qg_ref, kg_ref, grp_ref, qn_ref, kn_ref, knb_ref, vb_ref):
    grp = grp_ref[...]
    qn_ref[...] = (_group_norm64(zq_ref[...], qg_ref[0], grp) * (DF_DH ** -0.5)).astype(BF16)
    kn = _group_norm64(zk_ref[...], kg_ref[0], grp)
    kn_ref[...] = kn
    knb_ref[...] = kn.astype(BF16)
    vb_ref[...] = zv_ref[...].astype(BF16)


def qk_prep(z, lidx, qg, kg, *, tm):
    n_rows = z.shape[0]
    lane = np.arange(DF_W)
    grp = jnp.asarray(((lane[:, None] // DF_DH) == (lane[None, :] // DF_DH)).astype(np.float32) / DF_DH, BF16)
    zmap = lambda blk: (lambda i, l: (i, blk))
    row = lambda i, l: (i, 0)
    return pl.pallas_call(
        _qk_prep_kernel,
        grid_spec=pltpu.PrefetchScalarGridSpec(
            num_scalar_prefetch=1,
            grid=(n_rows // tm,),
            in_specs=[
                pl.BlockSpec((tm, DF_W), zmap(ZB_DFQ)),
                pl.BlockSpec((tm, DF_W), zmap(ZB_DFK)),
                pl.BlockSpec((tm, DF_W), zmap(ZB_DFV)),
                pl.BlockSpec((1, 1, DF_W), lambda i, l: (l[0], 0, 0)),
                pl.BlockSpec((1, 1, DF_W), lambda i, l: (l[0], 0, 0)),
                pl.BlockSpec((DF_W, DF_W), lambda i, l: (0, 0)),
            ],
            out_specs=[pl.BlockSpec((tm, DF_W), row)] * 4,
        ),
        out_shape=[
            jax.ShapeDtypeStruct((n_rows, DF_W), BF16),
            jax.ShapeDtypeStruct((n_rows, DF_W), F32),
            jax.ShapeDtypeStruct((n_rows, DF_W), BF16),
            jax.ShapeDtypeStruct((n_rows, DF_W), BF16),
        ],
        compiler_params=_cparams(("parallel",), 32 << 20),
        name="qk_prep",
    )(lidx, z, z, z, qg, kg, grp)


def _group_norm64(x, g, grp):
    x2 = x * x
    hi = x2.astype(BF16)
    lo = (x2 - hi.astype(F32)).astype(BF16)
    ms = _dot(hi, grp) + _dot(lo, grp)
    return x * lax.rsqrt(ms + EPS) * g


BIAS_W = 128


def _qk_prep_t_kernel(l_ref, zq_ref, zk_ref, zv_ref, qg_ref, kg_ref, grp_ref, qt_ref, kn_ref, ke_ref, vt_ref,
                      *, tm, seq):
    i = pl.program_id(0)
    grp = grp_ref[...]
    qn = _group_norm64(zq_ref[...], qg_ref[0], grp) * (DF_DH ** -0.5)
    kn = _group_norm64(zk_ref[...], kg_ref[0], grp)
    kn_ref[...] = kn
    zv = zv_ref[...]
    pos = lax.rem(i * tm, seq) + lax.broadcasted_iota(jnp.int32, (tm, BIAS_W), 0)
    lane = lax.broadcasted_iota(jnp.int32, (tm, BIAS_W), 1)
    hi_part = (pos >> CHUNK_SHIFT).astype(F32) * float(CHUNK)
    lo_part = (pos & (CHUNK - 1)).astype(F32)
    base = jnp.where(lane == 0, hi_part, jnp.where(lane == 1, lo_part, jnp.where(lane == 2, float(CHUNK), 0.0)))
    for h in range(DF_H):
        sl = slice(h * DF_DV, (h + 1) * DF_DV)
        slope = 2.0 ** (-8.0 * (h + 1) / DF_H)
        c0 = h * (DF_DV + BIAS_W)
        ke_ref[:, c0:c0 + DF_DV] = kn[:, sl].astype(BF16)
        ke_ref[:, c0 + DF_DV:c0 + DF_DV + BIAS_W] = (base * slope).astype(BF16)
        qt_ref[h, 0] = qn[:, sl].T.astype(BF16)
        vt_ref[h, 0] = zv[:, sl].T.astype(BF16)


def qk_prep_t(z, lidx, qg, kg, *, tm, seq):
    n_rows = z.shape[0]
    nt = n_rows // tm
    lane = np.arange(DF_W)
    grp = jnp.asarray(((lane[:, None] // DF_DH) == (lane[None, :] // DF_DH)).astype(np.float32) / DF_DH, BF16)
    zmap = lambda blk: (lambda i, l: (i, blk))
    row = lambda i, l: (i, 0)
    tmap = lambda i, l: (0, i, 0, 0)
    kern = functools.partial(_qk_prep_t_kernel, tm=tm, seq=seq)
    return pl.pallas_call(
        kern,
        grid_spec=pltpu.PrefetchScalarGridSpec(
            num_scalar_prefetch=1,
            grid=(nt,),
            in_specs=[
                pl.BlockSpec((tm, DF_W), zmap(ZB_DFQ)),
                pl.BlockSpec((tm, DF_W), zmap(ZB_DFK)),
                pl.BlockSpec((tm, DF_W), zmap(ZB_DFV)),
                pl.BlockSpec((1, 1, DF_W), lambda i, l: (l[0], 0, 0)),
                pl.BlockSpec((1, 1, DF_W), lambda i, l: (l[0], 0, 0)),
                pl.BlockSpec((DF_W, DF_W), lambda i, l: (0, 0)),
            ],
            out_specs=[
                pl.BlockSpec((DF_H, 1, DF_DV, tm), tmap),
                pl.BlockSpec((tm, DF_W), row),
                pl.BlockSpec((tm, DF_H * (DF_DV + BIAS_W)), row),
                pl.BlockSpec((DF_H, 1, DF_DV, tm), tmap),
            ],
        ),
        out_shape=[
            jax.ShapeDtypeStruct((DF_H, nt, DF_DV, tm), BF16),
            jax.ShapeDtypeStruct((n_rows, DF_W), F32),
            jax.ShapeDtypeStruct((n_rows, DF_H * (DF_DV + BIAS_W)), BF16),
            jax.ShapeDtypeStruct((DF_H, nt, DF_DV, tm), BF16),
        ],
        compiler_params=_cparams(("parallel",), 40 << 20),
        name="qk_prep_t",
    )(lidx, z, z, z, qg, kg, grp)


ONES_ROWS = 16


def _flash_t_kernel(l_ref, qt_ref, ke_ref, vt_ref, lam_ref, ogt_ref, y_ref, m_ref, acc_ref, *, tq):
    h = pl.program_id(1)
    qi = pl.program_id(2)
    tk = tq
    slope = jnp.exp2(jnp.full((1, 1), -8.0 / DF_H, F32) * (h + 1).astype(F32))
    qt = qt_ref[0, 0]
    row = lax.broadcasted_iota(jnp.int32, (DF_DV, tq), 0)
    zero = jnp.zeros_like(qt)
    q2t = jnp.concatenate([jnp.where(row < DF_DH, qt, zero), jnp.where(row >= DF_DH, qt, zero)], axis=1)
    brow = lax.broadcasted_iota(jnp.int32, (BIAS_W, 2 * tq), 0)
    a0 = ((qi * tq) >> CHUNK_SHIFT).astype(F32)
    extra = jnp.where(brow < 2, 1.0, jnp.where(brow == 2, -a0, 0.0)).astype(BF16)
    q2e = jnp.concatenate([q2t, extra], axis=0)
    ones = jnp.ones((ONES_ROWS, tk), BF16)

    m_ref[...] = jnp.full(m_ref.shape, NEG_BIG, F32)
    acc_ref[...] = jnp.zeros(acc_ref.shape, F32)

    def update(s, vt_blk):
        m_old = m_ref[...]
        m_new = jnp.maximum(m_old, jnp.max(s, axis=0, keepdims=True))
        alpha = jnp.exp(m_old - m_new)
        p = jnp.exp(s - m_new).astype(BF16)
        acc_ref[...] = alpha * acc_ref[...] + _dot(jnp.concatenate([vt_blk, ones], axis=0), p)
        m_ref[...] = m_new

    def fast_block(kj, carry):
        r0 = pl.multiple_of(kj * tk, tk)
        update(_dot(ke_ref[pl.ds(r0, tk), :], q2e), vt_ref[0, kj])
        return carry

    lax.fori_loop(0, qi, fast_block, 0)

    r0 = pl.multiple_of(qi * tq, tq)
    s = _dot(ke_ref[pl.ds(r0, tq), :], q2e)
    rel_k = lax.broadcasted_iota(jnp.int32, (tq, 2 * tq), 0)
    rel_q = lax.broadcasted_iota(jnp.int32, (tq, 2 * tq), 1)
    rel_q = jnp.where(rel_q >= tq, rel_q - tq, rel_q)
    ahead = rel_k - rel_q
    s = s - jnp.where(ahead > 0, (2.0 * slope) * ahead.astype(F32), 0.0)
    visible = (rel_k >> CHUNK_SHIFT) <= (rel_q >> CHUNK_SHIFT)
    update(jnp.where(visible, s, NEG_BIG), vt_ref[0, qi])

    lam_p = lam_ref[0]
    lam_init = 0.8 - 0.6 * jnp.exp(jnp.full((1, 1), -0.3, F32) * l_ref[0].astype(F32))
    lam = (jnp.exp(jnp.sum(lam_p[0:1] * lam_p[1:2], axis=-1, keepdims=True))
           - jnp.exp(jnp.sum(lam_p[2:3] * lam_p[3:4], axis=-1, keepdims=True)) + lam_init)
    acc = acc_ref[...]
    o_all = acc[0:DF_DV] / acc[DF_DV:DF_DV + 1]
    o = o_all[:, 0:tq] - lam * o_all[:, tq:2 * tq]
    ms = jnp.mean(o * o, axis=0, keepdims=True)
    y = o * lax.rsqrt(ms + EPS) * ogt_ref[0] * (1.0 - lam_init)
    y_ref[...] = y.T.astype(BF16)


def diff_flash_t(qt, ke, vt, lidx, lam_p, out_g_t, *, nb, seq, tq):
    nq = seq // tq
    n_rows = nb * seq
    kew = DF_DV + BIAS_W
    kern = functools.partial(_flash_t_kernel, tq=tq)
    vmem = 2 * seq * kew * 2 + 2 * seq * DF_DV * 2 + 10 * tq * 2 * tq * 4 + (8 << 20)
    return pl.pallas_call(
        kern,
        grid_spec=pltpu.PrefetchScalarGridSpec(
            num_scalar_prefetch=1,
            grid=(nb, DF_H, nq),
            in_specs=[
                pl.BlockSpec((1, 1, DF_DV, tq), lambda b, h, i, l: (h, b * nq + i, 0, 0)),
                pl.BlockSpec((seq, kew), lambda b, h, i, l: (b, h)),
                pl.BlockSpec((1, nq, DF_DV, tq), lambda b, h, i, l: (h, b, 0, 0)),
                pl.BlockSpec((1, 4, DF_DH), lambda b, h, i, l: (l[0], 0, 0)),
                pl.BlockSpec((1, DF_DV, 1), lambda b, h, i, l: (l[0], 0, 0)),
            ],
            out_specs=pl.BlockSpec((tq, DF_DV), lambda b, h, i, l: (b * nq + i, h)),
            scratch_shapes=[pltpu.VMEM((1, 2 * tq), F32), pltpu.VMEM((DF_DV + ONES_ROWS, 2 * tq), F32)],
        ),
        out_shape=jax.ShapeDtypeStruct((n_rows, DF_W), BF16),
        compiler_params=_cparams(("parallel", "parallel", "arbitrary"), vmem),
        name="diff_flash_t",
    )(lidx, qt, ke, vt, lam_p, out_g_t)


def _flash_kernel(l_ref, q_ref, km_ref, vm_ref, kx_ref, vx_ref, lam_ref, og_ref, y_ref,
                  m_ref, l_sum_ref, acc_ref, *, tq, tk, past, prompt):
    h = pl.program_id(1)
    qi = pl.program_id(2)
    slope = jnp.exp2(jnp.full((1, 1), -8.0 / DF_H, F32) * (h + 1).astype(F32))
    q = q_ref[...]
    lane = lax.broadcasted_iota(jnp.int32, (tq, DF_DV), 1)
    zero = jnp.zeros_like(q)
    q2 = jnp.concatenate([jnp.where(lane < DF_DH, q, zero), jnp.where(lane >= DF_DH, q, zero)], axis=0)
    q0 = past + qi * tq

    m_ref[...] = jnp.full(m_ref.shape, NEG_BIG, F32)
    l_sum_ref[...] = jnp.zeros(l_sum_ref.shape, F32)
    acc_ref[...] = jnp.zeros(acc_ref.shape, F32)

    def update(s, v):
        m_old = m_ref[...]
        m_new = jnp.maximum(m_old, jnp.max(s, axis=-1, keepdims=True))
        alpha = jnp.exp(m_old - m_new)
        p = jnp.exp(s - m_new)
        l_sum_ref[...] = alpha * l_sum_ref[...] + jnp.sum(p, axis=-1, keepdims=True)
        acc_ref[...] = alpha * acc_ref[...] + _dot(p.astype(BF16), v)
        m_ref[...] = m_new

    def fast_block(kj, carry):
        r0 = pl.multiple_of(kj * tk, tk)
        k = km_ref[pl.ds(r0, tk), :].astype(BF16)
        v = vm_ref[pl.ds(r0, tk), :].astype(BF16)
        kpos = r0 + lax.broadcasted_iota(jnp.int32, (1, tk), 1)
        s = _dot_nt(q2, k) + slope * (kpos - q0).astype(F32)
        update(s, v)
        return carry

    n_fast = qi if prompt else past // tk
    lax.fori_loop(0, n_fast, fast_block, 0)

    if prompt:
        r0 = pl.multiple_of(qi * tq, tq)
        k = km_ref[pl.ds(r0, tq), :].astype(BF16)
        v = vm_ref[pl.ds(r0, tq), :].astype(BF16)
    else:
        k = kx_ref[...].astype(BF16)
        v = vx_ref[...].astype(BF16)
    rel_k = lax.broadcasted_iota(jnp.int32, (2 * tq, tq), 1)
    rel_q = lax.broadcasted_iota(jnp.int32, (2 * tq, tq), 0)
    rel_q = jnp.where(rel_q >= tq, rel_q - tq, rel_q)
    dist = jnp.abs(rel_q - rel_k).astype(F32)
    s = _dot_nt(q2, k) + slope * (rel_q.astype(F32) - dist)
    visible = ((q0 + rel_k) >> CHUNK_SHIFT) <= ((q0 + rel_q) >> CHUNK_SHIFT)
    update(jnp.where(visible, s, NEG_BIG), v)

    lam_p = lam_ref[0]
    lam_init = 0.8 - 0.6 * jnp.exp(jnp.full((1, 1), -0.3, F32) * l_ref[0].astype(F32))
    lam = (jnp.exp(jnp.sum(lam_p[0:1] * lam_p[1:2], axis=-1, keepdims=True))
           - jnp.exp(jnp.sum(lam_p[2:3] * lam_p[3:4], axis=-1, keepdims=True)) + lam_init)
    o_all = acc_ref[...] / l_sum_ref[...]
    o = o_all[0:tq] - lam * o_all[tq:2 * tq]
    y_ref[...] = (_rms(o, og_ref[0]) * (1.0 - lam_init)).astype(BF16)


def diff_flash(qn, k_main, v_main, k_new, v_new, lidx, lam_p, out_g, *, nb, seq, past, tq, tk, prompt,
               main_rows, main_block):
    nq = seq // tq
    n_rows = nb * seq
    kern = functools.partial(_flash_kernel, tq=tq, tk=tk, past=past, prompt=prompt)
    vmem = 4 * main_rows * DF_DV * k_main.dtype.itemsize + 8 * (2 * tq) * max(tk, tq) * 4 + (8 << 20)
    qmap = lambda b, h, i, l: (b * nq + i, h)
    return pl.pallas_call(
        kern,
        grid_spec=pltpu.PrefetchScalarGridSpec(
            num_scalar_prefetch=1,
            grid=(nb, DF_H, nq),
            in_specs=[
                pl.BlockSpec((tq, DF_DV), qmap),
                pl.BlockSpec((main_rows, DF_DV), lambda b, h, i, l: (main_block(l, b), h)),
                pl.BlockSpec((main_rows, DF_DV), lambda b, h, i, l: (main_block(l, b), h)),
                pl.BlockSpec((tq, DF_DV), qmap),
                pl.BlockSpec((tq, DF_DV), qmap),
                pl.BlockSpec((1, 4, DF_DH), lambda b, h, i, l: (l[0], 0, 0)),
                pl.BlockSpec((1, 1, DF_DV), lambda b, h, i, l: (l[0], 0, 0)),
            ],
            out_specs=pl.BlockSpec((tq, DF_DV), qmap),
            scratch_shapes=[pltpu.VMEM((2 * tq, 1), F32), pltpu.VMEM((2 * tq, 1), F32),
                            pltpu.VMEM((2 * tq, DF_DV), F32)],
        ),
        out_shape=jax.ShapeDtypeStruct((n_rows, DF_W), BF16),
        compiler_params=_cparams(("parallel", "parallel", "arbitrary"), vmem),
        name="diff_flash_prompt" if prompt else "diff_flash_sample",
    )(lidx, qn, k_main, v_main, k_new, v_new, lam_p, out_g)


def _trunk_layer(x, lidx, w, st, cfg):
    nb, seq, past = cfg["nb"], cfg["seq"], cfg["past"]
    z, gates = in_proj(x, lidx, w["norm_mix"], w["w_main"], w["w_gate"], tm=cfg["tm_in"], tn=cfg["tn_in"])
    y_ml, c_new, n_new, m_new = mlstm_mixer(
        z, gates, lidx, st["hist0"], w["conv_w"], w["conv_b"], w["b_if"], w["ml_og"],
        st["c0"], st["n0"], st["m0"], nb=nb, seq=seq, tc=cfg["tc"], lc=cfg["lc_ml"])
    if cfg["prompt"]:
        qt, kn, ke, vt = qk_prep_t(z, lidx, w["df_qg"], w["df_kg"], tm=cfg["tq"], seq=seq)
        y_df = diff_flash_t(qt, ke, vt, lidx, w["df_lam"], w["df_og_t"], nb=nb, seq=seq, tq=cfg["tq"])
    else:
        qn, kn, kn_b, v_b = qk_prep(z, lidx, w["df_qg"], w["df_kg"], tm=cfg["tm_prep"])
        y_df = diff_flash(qn, st["past_k"], st["past_v"], kn_b, v_b, lidx, w["df_lam"], w["df_og"], nb=nb,
                          seq=seq, past=past, tq=seq, tk=cfg["tk_past"], prompt=False, main_rows=past,
                          main_block=lambda l, b: l[0] * nb + b)
    y_hg, s_new = hgrn_mixer(z, lidx, w["hg_lbl"], w["hg_og"], st["s0"], nb=nb, seq=seq, tc=cfg["tc"],
                             lc=cfg["lc_hg"])
    x = post_mix(x, y_ml, y_df, y_hg, lidx, w["w_out"], w["norm_cross"], w["wq"], w["ca_qg"],
                 st["mk"], st["mv"], w["wo"], tm=cfg["tm_post"], rows_per_batch=seq)
    x = ffn(x, lidx, w["norm_ffn"], w["w_gu"], w["w_down"], w["final_norm"], tm=cfg["tm_ffn"], th=cfg["th"])
    z3 = z.reshape(nb, seq, Z_W)
    new = {
        "attn_k": kn.reshape(nb, seq, DF_H, 2 * DF_DH),
        "attn_v": z3[:, :, ZB_DFV * DF_W:(ZB_DFV + 1) * DF_W].reshape(nb, seq, DF_H, DF_DV),
        "conv": z3[:, seq - (CONV_W - 1):, 0:2 * ML_W],
        "C": c_new,
        "n": n_new[:, :ML_H, :],
        "m": m_new[:, 0, :ML_H],
        "S": s_new,
    }
    return x, new


PROMPT_CFG = dict(prompt=True, tm_in=1024, tn_in=768, tc=256, lc_ml=64, lc_hg=64, tm_prep=512, tq=512,
                  tm_post=512, tm_ffn=512, th=512)
SAMPLE_CFG = dict(prompt=False, tm_in=128, tn_in=768, tc=16, lc_ml=16, lc_hg=16, tm_prep=128, tk_past=512,
                  tm_post=16, tm_ffn=128, th=512)


def kernel(x_prompt, x_sample, mem_prompt, cache_attn_k, cache_attn_v, cache_mem_k, cache_mem_v, state_mlstm_conv, state_mlstm_C, state_mlstm_n, state_mlstm_m, state_hgrn_S, norm_mix, w_in, mlstm_conv_w, mlstm_conv_b, mlstm_b_i, mlstm_b_f, mlstm_out_norm, diff_q_norm, diff_k_norm, diff_lambda, diff_out_norm, hgrn_lb_logits, hgrn_out_norm, w_out, norm_cross, norm_mem, cross_wq, cross_wk, cross_wv, cross_q_norm, cross_k_norm, cross_wo, norm_ffn, ffn_w_gate_up, ffn_w_down, final_norm):
    bp, tp = x_prompt.shape[:2]
    bs, ts = x_sample.shape[:2]
    past = cache_attn_k.shape[2]
    depth = w_in.shape[0]
    assert depth == DEPTH and x_prompt.shape[2] == D_MODEL

    g_off = 4 * ML_W
    r3 = lambda a: a.reshape(depth, 1, a.shape[-1])
    w = {
        "norm_mix": r3(norm_mix),
        "w_main": jnp.concatenate([w_in[:, :, :g_off], w_in[:, :, g_off + 2 * ML_H:]], axis=-1).astype(BF16),
        "w_gate": jnp.pad(w_in[:, :, g_off:g_off + 2 * ML_H], ((0, 0), (0, 0), (0, GATE_W - 2 * ML_H))).astype(BF16),
        "conv_w": mlstm_conv_w,
        "conv_b": r3(mlstm_conv_b),
        "b_if": r3(jnp.pad(jnp.concatenate([mlstm_b_i, mlstm_b_f], axis=-1), ((0, 0), (0, GATE_W - 2 * ML_H)))),
        "ml_og": r3(mlstm_out_norm),
        "df_qg": r3(jnp.tile(diff_q_norm, (1, DF_W // DF_DH))),
        "df_kg": r3(jnp.tile(diff_k_norm, (1, DF_W // DF_DH))),
        "df_lam": diff_lambda,
        "df_og": r3(diff_out_norm),
        "df_og_t": diff_out_norm.reshape(depth, DF_DV, 1),
        "hg_lbl": hgrn_lb_logits,
        "hg_og": r3(hgrn_out_norm),
        "w_out": w_out.astype(BF16),
        "norm_cross": r3(norm_cross),
        "wq": cross_wq.astype(BF16),
        "ca_qg": r3(cross_q_norm),
        "wo": cross_wo.astype(BF16),
        "norm_ffn": r3(norm_ffn),
        "w_gu": ffn_w_gate_up.astype(BF16),
        "w_down": ffn_w_down.astype(BF16),
        "final_norm": final_norm.reshape(1, D_MODEL),
    }

    mk_p, mv_p = mem_kv(mem_prompt, r3(norm_mem), cross_wk.astype(BF16), cross_wv.astype(BF16), r3(cross_k_norm))

    def pad_hist(conv):
        pad = [(0, 0)] * (conv.ndim - 2) + [(HIST - (CONV_W - 1), 0), (0, 0)]
        return jnp.pad(conv, pad)

    st_p = {
        "hist0": jnp.zeros((bp, HIST, 2 * ML_W), F32),
        "c0": jnp.zeros((bp, ML_H, ML_DH, ML_DH), F32),
        "n0": jnp.zeros((bp, 8, ML_DH), F32),
        "m0": jnp.zeros((bp, 1, GATE_W), F32),
        "s0": jnp.zeros((bp, HG_H, HG_DK, HG_DV), F32),
        "mk": mk_p, "mv": mv_p,
    }
    hist_s = pad_hist(state_mlstm_conv)
    n_s = jnp.pad(state_mlstm_n, ((0, 0), (0, 0), (0, 8 - ML_H), (0, 0)))
    m_s = jnp.pad(state_mlstm_m, ((0, 0), (0, 0), (0, GATE_W - ML_H))).reshape(depth, bs, 1, GATE_W)
    past_k = cache_attn_k.reshape(depth * bs * past, DF_W)
    past_v = cache_attn_v.reshape(depth * bs * past, DF_W)
    mk_s = cache_mem_k.reshape(depth * bs, N_MEM, CA_W)
    mv_s = cache_mem_v.reshape(depth * bs, N_MEM, CA_W)

    cfg_p = dict(PROMPT_CFG, nb=bp, seq=tp, past=0)
    cfg_s = dict(SAMPLE_CFG, nb=bs, seq=ts, past=past)

    def layer(carry, xs):
        xp, xsm = carry
        l, hist_l, c_l, n_l, m_l, s_l = xs
        lidx = l.reshape(1).astype(jnp.int32)
        xp, new_p = _trunk_layer(xp, lidx, w, st_p, cfg_p)
        st_s = {"hist0": hist_l, "c0": c_l, "n0": n_l, "m0": m_l, "s0": s_l,
                "past_k": past_k, "past_v": past_v, "mk": mk_s, "mv": mv_s}
        xsm, new_s = _trunk_layer(xsm, lidx, w, st_s, cfg_s)
        return (xp, xsm), (new_p, new_s)

    xs = (jnp.arange(depth, dtype=jnp.int32), hist_s, state_mlstm_C, n_s, m_s, state_hgrn_S)
    (xp, xsm), (new_p, new_s) = lax.scan(
        layer, (x_prompt.reshape(bp * tp, D_MODEL), x_sample.reshape(bs * ts, D_MODEL)), xs)

    y_prompt = xp.reshape(bp, tp, D_MODEL)
    y_sample = xsm.reshape(bs, ts, D_MODEL)
    p_mem_k = mk_p.reshape(depth, bp, N_MEM, CA_H, CA_DH)
    p_mem_v = mv_p.reshape(depth, bp, N_MEM, CA_H, CA_DH)
    return (y_prompt, y_sample,
            new_p["attn_k"], new_p["attn_v"], p_mem_k, p_mem_v, new_p["conv"], new_p["C"], new_p["n"],
            new_p["m"], new_p["S"],
            new_s["attn_k"], new_s["attn_v"], new_s["conv"], new_s["C"], new_s["n"], new_s["m"], new_s["S"])
```

```python
import functools
import math

import numpy as np
import jax
import jax.numpy as jnp
from jax import lax
from jax.experimental import pallas as pl
from jax.experimental.pallas import tpu as pltpu

F32 = jnp.float32
BF16 = jnp.bfloat16

D_MODEL = 2048
DEPTH = 4
CHUNK = 64
CHUNK_SHIFT = 6
ML_DH = 128
ML_W = 768
ML_H = 6
CONV_W = 4
DF_DH = 64
DF_DV = 128
DF_W = 512
DF_H = 4
HG_DK = 128
HG_DV = 128
HG_W = 768
HG_H = 6
CA_H = 4
CA_DH = 128
CA_W = 512
N_MEM = 256
FF_HIDDEN = 5632
EPS = 1e-6
NEG_BIG = -1e30
LB_FLOOR = 1e-30

Z_W = 2 * ML_W + 2 * ML_W + 3 * DF_W + 4 * HG_W
GATE_W = 128
ZB_MLQ, ZB_MLK, ZB_MLV, ZB_MLO = 0, 1, 2, 3
ZB_HGQ, ZB_HGF, ZB_HGI, ZB_HGG = 6, 7, 8, 9
ZB_DFQ, ZB_DFK, ZB_DFV = 6, 7, 8

V7X_VMEM_BYTES = 64 * 1024 * 1024
V7X_VMEM_CAP = 60000 * 1024
HIST = 8

NT_DIMS = (((1,), (1,)), ((), ()))
TN_DIMS = (((0,), (0,)), ((), ()))


def _vmem_limit(nbytes):
    return int(min(V7X_VMEM_CAP, max(32 * 1024 * 1024, nbytes)))


def _cparams(sem, vmem_bytes):
    return pltpu.CompilerParams(dimension_semantics=sem, vmem_limit_bytes=_vmem_limit(vmem_bytes))


def _rms(x, g):
    ms = jnp.mean(x * x, axis=-1, keepdims=True)
    return x * lax.rsqrt(ms + EPS) * g


def _dot(a, b):
    return jnp.dot(a, b, preferred_element_type=F32)


def _dot_nt(a, b):
    return lax.dot_general(a, b, NT_DIMS, preferred_element_type=F32)


def _dot_tn(a, b):
    return lax.dot_general(a, b, TN_DIMS, preferred_element_type=F32)


def _log_sigmoid(x):
    return jnp.minimum(x, 0.0) - jnp.log1p(jnp.exp(-jnp.abs(x)))


def _split3(x):
    hi = x.astype(BF16)
    r1 = x - hi.astype(F32)
    mid = r1.astype(BF16)
    lo = (r1 - mid.astype(F32)).astype(BF16)
    return hi, mid, lo


def _dot_exact01(m01, x):
    hi, mid, lo = _split3(x)
    return _dot(m01, hi) + _dot(m01, mid) + _dot(m01, lo)


def _in_proj_kernel(l_ref, x_ref, g_ref, w_ref, wg_ref, z_ref, gate_ref, xn_ref):
    j = pl.program_id(1)

    @pl.when(j == 0)
    def _():
        xn = _rms(x_ref[...], g_ref[0]).astype(BF16)
        xn_ref[...] = xn
        gate_ref[...] = _dot(xn, wg_ref[0])

    z_ref[...] = _dot(xn_ref[...], w_ref[0])


def in_proj(x, lidx, g, w_main, w_gate, *, tm, tn):
    n_rows = x.shape[0]
    grid = (n_rows // tm, Z_W // tn)
    vmem = 2 * tm * D_MODEL * 4 + 2 * D_MODEL * tn * 2 + 2 * tm * tn * 4 + tm * D_MODEL * 2 \
        + 2 * tm * GATE_W * 4 + 2 * D_MODEL * GATE_W * 2 + (4 << 20)
    return pl.pallas_call(
        _in_proj_kernel,
        grid_spec=pltpu.PrefetchScalarGridSpec(
            num_scalar_prefetch=1,
            grid=grid,
            in_specs=[
                pl.BlockSpec((tm, D_MODEL), lambda i, j, l: (i, 0)),
                pl.BlockSpec((1, 1, D_MODEL), lambda i, j, l: (l[0], 0, 0)),
                pl.BlockSpec((1, D_MODEL, tn), lambda i, j, l: (l[0], 0, j)),
                pl.BlockSpec((1, D_MODEL, GATE_W), lambda i, j, l: (l[0], 0, 0)),
            ],
            out_specs=[
                pl.BlockSpec((tm, tn), lambda i, j, l: (i, j)),
                pl.BlockSpec((tm, GATE_W), lambda i, j, l: (i, 0)),
            ],
            scratch_shapes=[pltpu.VMEM((tm, D_MODEL), BF16)],
        ),
        out_shape=[jax.ShapeDtypeStruct((n_rows, Z_W), F32),
                   jax.ShapeDtypeStruct((n_rows, GATE_W), F32)],
        compiler_params=_cparams(("parallel", "arbitrary"), vmem),
        name="in_proj",
    )(lidx, x, g, w_main, w_gate)


def _ffn_kernel(l_ref, x_ref, g_ref, wg_ref, wu_ref, wd_ref, fg_ref, o_ref, xn_ref):
    j = pl.program_id(1)
    nj = pl.num_programs(1)

    @pl.when(j == 0)
    def _():
        xn_ref[...] = _rms(x_ref[...], g_ref[0]).astype(BF16)

    xn = xn_ref[...]
    gt = _dot(xn, wg_ref[0])
    up = _dot(xn, wu_ref[0])
    act = (gt * jax.nn.sigmoid(gt) * up).astype(BF16)
    down = _dot(act, wd_ref[0])

    @pl.when(j == 0)
    def _():
        o_ref[...] = x_ref[...] + down

    @pl.when(j > 0)
    def _():
        o_ref[...] += down

    @pl.when(jnp.logical_and(j == nj - 1, l_ref[0] == DEPTH - 1))
    def _():
        o_ref[...] = _rms(o_ref[...], fg_ref[...])


def ffn(x, lidx, g, w_gu, w_down, final_g, *, tm, th):
    n_rows = x.shape[0]
    nh = FF_HIDDEN // th
    grid = (n_rows // tm, nh)
    vmem = 4 * tm * D_MODEL * 4 + tm * D_MODEL * 2 + 6 * D_MODEL * th * 2 + 3 * tm * th * 4 \
        + tm * D_MODEL * 4 + (4 << 20)
    return pl.pallas_call(
        _ffn_kernel,
        grid_spec=pltpu.PrefetchScalarGridSpec(
            num_scalar_prefetch=1,
            grid=grid,
            in_specs=[
                pl.BlockSpec((tm, D_MODEL), lambda i, j, l: (i, 0)),
                pl.BlockSpec((1, 1, D_MODEL), lambda i, j, l: (l[0], 0, 0)),
                pl.BlockSpec((1, D_MODEL, th), lambda i, j, l: (l[0], 0, j)),
                pl.BlockSpec((1, D_MODEL, th), lambda i, j, l: (l[0], 0, j + nh)),
                pl.BlockSpec((1, th, D_MODEL), lambda i, j, l: (l[0], j, 0)),
                pl.BlockSpec((1, D_MODEL), lambda i, j, l: (0, 0)),
            ],
            out_specs=pl.BlockSpec((tm, D_MODEL), lambda i, j, l: (i, 0)),
            scratch_shapes=[pltpu.VMEM((tm, D_MODEL), BF16)],
        ),
        out_shape=jax.ShapeDtypeStruct((n_rows, D_MODEL), F32),
        compiler_params=_cparams(("parallel", "arbitrary"), vmem),
        name="ffn",
    )(lidx, x, g, w_gu, w_gu, w_down, final_g)


def _mem_kv_kernel(mem_ref, g_ref, wk_ref, wv_ref, kg_ref, k_ref, v_ref):
    mn = _rms(mem_ref[0], g_ref[0]).astype(BF16)
    k = _dot(mn, wk_ref[0])
    for h in range(CA_H):
        sl = slice(h * CA_DH, (h + 1) * CA_DH)
        k_ref[0, :, sl] = _rms(k[:, sl], kg_ref[0])
    v_ref[0] = _dot(mn, wv_ref[0])


def mem_kv(mem, g, wk, wv, kg):
    nb = mem.shape[0]
    out = jax.ShapeDtypeStruct((DEPTH * nb, N_MEM, CA_W), F32)
    return pl.pallas_call(
        _mem_kv_kernel,
        grid=(DEPTH, nb),
        in_specs=[
            pl.BlockSpec((1, N_MEM, D_MODEL), lambda l, b: (b, 0, 0)),
            pl.BlockSpec((1, 1, D_MODEL), lambda l, b: (l, 0, 0)),
            pl.BlockSpec((1, D_MODEL, CA_W), lambda l, b: (l, 0, 0)),
            pl.BlockSpec((1, D_MODEL, CA_W), lambda l, b: (l, 0, 0)),
            pl.BlockSpec((1, 1, CA_DH), lambda l, b: (l, 0, 0)),
        ],
        out_specs=[pl.BlockSpec((1, N_MEM, CA_W), lambda l, b: (l * nb + b, 0, 0)),
                   pl.BlockSpec((1, N_MEM, CA_W), lambda l, b: (l * nb + b, 0, 0))],
        out_shape=[out, out],
        compiler_params=_cparams(("arbitrary", "arbitrary"), 24 << 20),
        name="mem_kv",
    )(mem, g, wk, wv, kg)


def _post_kernel(l_ref, x_ref, yml_ref, ydf_ref, yhg_ref, wout_ref, gx_ref, wq_ref, qg_ref,
                 mk_ref, mv_ref, wo_ref, o_ref):
    x1 = x_ref[...]
    x1 = x1 + _dot(yml_ref[...], wout_ref[0, 0:ML_W, :])
    x1 = x1 + _dot(ydf_ref[...], wout_ref[0, ML_W:ML_W + DF_W, :])
    x1 = x1 + _dot(yhg_ref[...], wout_ref[0, ML_W + DF_W:D_MODEL, :])
    hn = _rms(x1, gx_ref[0]).astype(BF16)
    q = _dot(hn, wq_ref[0])
    heads = []
    for h in range(CA_H):
        sl = slice(h * CA_DH, (h + 1) * CA_DH)
        qh = _rms(q[:, sl], qg_ref[0]).astype(BF16)
        s = _dot_nt(qh, mk_ref[0, :, sl].astype(BF16)) * (CA_DH ** -0.5)
        s = s - jnp.max(s, axis=-1, keepdims=True)
        p = jnp.exp(s)
        p = p / jnp.sum(p, axis=-1, keepdims=True)
        heads.append(_dot(p.astype(BF16), mv_ref[0, :, sl].astype(BF16)))
    o = jnp.concatenate(heads, axis=-1).astype(BF16)
    o_ref[...] = x1 + _dot(o, wo_ref[0])


def post_mix(x, y_ml, y_df, y_hg, lidx, w_out, gx, wq, qg, mk, mv, wo, *, tm, rows_per_batch):
    n_rows = x.shape[0]
    nb = n_rows // rows_per_batch
    tiles_per_batch = rows_per_batch // tm
    grid = (n_rows // tm,)
    w_bytes = (D_MODEL * D_MODEL + 2 * D_MODEL * CA_W) * 2
    vmem = 2 * w_bytes + 6 * tm * D_MODEL * 4 + 2 * tm * D_MODEL * 2 + 4 * N_MEM * CA_W * 4 + (6 << 20)
    row = lambda i, l: (i, 0)
    mem_map = lambda i, l: (l[0] * nb + i // tiles_per_batch, 0, 0)
    return pl.pallas_call(
        _post_kernel,
        grid_spec=pltpu.PrefetchScalarGridSpec(
            num_scalar_prefetch=1,
            grid=grid,
            in_specs=[
                pl.BlockSpec((tm, D_MODEL), row),
                pl.BlockSpec((tm, ML_W), row),
                pl.BlockSpec((tm, DF_W), row),
                pl.BlockSpec((tm, HG_W), row),
                pl.BlockSpec((1, D_MODEL, D_MODEL), lambda i, l: (l[0], 0, 0)),
                pl.BlockSpec((1, 1, D_MODEL), lambda i, l: (l[0], 0, 0)),
                pl.BlockSpec((1, D_MODEL, CA_W), lambda i, l: (l[0], 0, 0)),
                pl.BlockSpec((1, 1, CA_DH), lambda i, l: (l[0], 0, 0)),
                pl.BlockSpec((1, N_MEM, CA_W), mem_map),
                pl.BlockSpec((1, N_MEM, CA_W), mem_map),
                pl.BlockSpec((1, CA_W, D_MODEL), lambda i, l: (l[0], 0, 0)),
            ],
            out_specs=pl.BlockSpec((tm, D_MODEL), row),
        ),
        out_shape=jax.ShapeDtypeStruct((n_rows, D_MODEL), F32),
        compiler_params=_cparams(("parallel",), vmem),
        name="post_mix",
    )(lidx, x, y_ml, y_df, y_hg, w_out, gx, wq, qg, mk, mv, wo)


def _mlstm_kernel(l_ref, zq_ref, zk_ref, zv_ref, zo_ref, gate_ref, hist_ref, cw_ref, cb_ref, bif_ref,
                  og_ref, c0_ref, n0_ref, m0_ref, tri_ref,
                  y_ref, c_ref, n_ref, m_ref, ext_ref, *, tc, lc):
    c = pl.program_id(1)

    @pl.when(c == 0)
    def _():
        c_ref[...] = c0_ref[...]
        n_ref[...] = n0_ref[...]
        m_ref[...] = m0_ref[...]
        ext_ref[0:HIST, :] = hist_ref[0]

    ext_ref[HIST:HIST + tc, 0:ML_W] = zq_ref[...]
    ext_ref[HIST:HIST + tc, ML_W:2 * ML_W] = zk_ref[...]
    acc = cb_ref[0]
    for j in range(CONV_W):
        acc = acc + cw_ref[0, j:j + 1, :] * ext_ref[pl.ds(HIST - (CONV_W - 1) + j, tc), :]
    tail = ext_ref[tc:tc + HIST, :]
    ext_ref[0:HIST, :] = tail
    qk = acc * jax.nn.sigmoid(acc)

    gz = gate_ref[...] + bif_ref[0]
    lf_all = _log_sigmoid(gz)
    tri = tri_ref[...]
    row_i = lax.broadcasted_iota(jnp.int32, (lc, lc), 0)
    col_i = lax.broadcasted_iota(jnp.int32, (lc, lc), 1)
    causal = row_i >= col_i
    diag = row_i == col_i
    og = og_ref[0]

    for ci in range(tc // lc):
        r0 = ci * lc
        ig_c = gz[r0:r0 + lc, :]
        b_c = _dot_exact01(tri, lf_all[r0:r0 + lc, :])
        for h in range(ML_H):
            sl = slice(h * ML_DH, (h + 1) * ML_DH)
            q = qk[r0:r0 + lc, h * ML_DH:(h + 1) * ML_DH]
            k = qk[r0:r0 + lc, ML_W + h * ML_DH:ML_W + (h + 1) * ML_DH] * (ML_DH ** -0.5)
            v = zv_ref[r0:r0 + lc, sl]
            qb = q.astype(BF16)
            b_col = b_c[:, ML_H + h:ML_H + h + 1]
            ig_col = ig_c[:, h:h + 1]
            r_row = jnp.sum(jnp.where(diag, ig_col - b_col, 0.0), axis=0, keepdims=True)
            m_prev = m_ref[0, :, h:h + 1]
            log_d = jnp.where(causal, b_col + r_row, NEG_BIG)
            inter = b_col + m_prev
            m_t = jnp.maximum(inter, jnp.max(log_d, axis=-1, keepdims=True))
            d_mat = jnp.exp(log_d - m_t)
            w_inter = jnp.exp(inter - m_t)
            s = _dot_nt(qb, k.astype(BF16)) * d_mat
            c_old = c_ref[0, h]
            n_old = n_ref[0, h:h + 1, :]
            num = _dot(s.astype(BF16), v.astype(BF16)) + w_inter * _dot(qb, c_old.astype(BF16))
            den = jnp.sum(s, axis=-1, keepdims=True) + w_inter * jnp.sum(q * n_old, axis=-1, keepdims=True)
            hh = num / jnp.maximum(jnp.abs(den), jnp.exp(-m_t))
            m_new = m_t[lc - 1:lc, :]
            b_last = b_col[lc - 1:lc, :]
            w_s = jnp.exp(b_last - b_col + ig_col - m_new)
            decay = jnp.exp(b_last + m_prev - m_new)
            kw = k * w_s
            c_ref[0, h] = decay * c_old + _dot_tn(kw.astype(BF16), v.astype(BF16))
            n_ref[0, h:h + 1, :] = decay * n_old + jnp.sum(kw, axis=0, keepdims=True)
            m_ref[0, :, h:h + 1] = m_new
            o_gate = jax.nn.sigmoid(zo_ref[r0:r0 + lc, sl])
            y_ref[r0:r0 + lc, sl] = (_rms(hh, og) * o_gate).astype(BF16)


def mlstm_mixer(z, gates, lidx, hist0, conv_w, conv_b, b_if, out_g, c0, n0, m0, *, nb, seq, tc, lc):
    n_rows = nb * seq
    nc = seq // tc
    tri = jnp.asarray(np.tril(np.ones((lc, lc), np.float32)), BF16)
    zmap = lambda blk: (lambda b, c, l: (b * nc + c, blk))
    lmap3 = lambda b, c, l: (l[0], 0, 0)
    vmem = 2 * 5 * tc * ML_W * 4 + (HIST + tc) * 2 * ML_W * 4 + 4 * ML_H * ML_DH * ML_DH * 4 \
        + 6 * tc * 2 * ML_W * 4 + (8 << 20)
    kern = functools.partial(_mlstm_kernel, tc=tc, lc=lc)
    return pl.pallas_call(
        kern,
        grid_spec=pltpu.PrefetchScalarGridSpec(
            num_scalar_prefetch=1,
            grid=(nb, nc),
            in_specs=[
                pl.BlockSpec((tc, ML_W), zmap(ZB_MLQ)),
                pl.BlockSpec((tc, ML_W), zmap(ZB_MLK)),
                pl.BlockSpec((tc, ML_W), zmap(ZB_MLV)),
                pl.BlockSpec((tc, ML_W), zmap(ZB_MLO)),
                pl.BlockSpec((tc, GATE_W), lambda b, c, l: (b * nc + c, 0)),
                pl.BlockSpec((1, HIST, 2 * ML_W), lambda b, c, l: (b, 0, 0)),
                pl.BlockSpec((1, CONV_W, 2 * ML_W), lmap3),
                pl.BlockSpec((1, 1, 2 * ML_W), lmap3),
                pl.BlockSpec((1, 1, GATE_W), lmap3),
                pl.BlockSpec((1, 1, ML_DH), lmap3),
                pl.BlockSpec((1, ML_H, ML_DH, ML_DH), lambda b, c, l: (b, 0, 0, 0)),
                pl.BlockSpec((1, 8, ML_DH), lambda b, c, l: (b, 0, 0)),
                pl.BlockSpec((1, 1, GATE_W), lambda b, c, l: (b, 0, 0)),
                pl.BlockSpec((lc, lc), lambda b, c, l: (0, 0)),
            ],
            out_specs=[
                pl.BlockSpec((tc, ML_W), lambda b, c, l: (b * nc + c, 0)),
                pl.BlockSpec((1, ML_H, ML_DH, ML_DH), lambda b, c, l: (b, 0, 0, 0)),
                pl.BlockSpec((1, 8, ML_DH), lambda b, c, l: (b, 0, 0)),
                pl.BlockSpec((1, 1, GATE_W), lambda b, c, l: (b, 0, 0)),
            ],
            scratch_shapes=[pltpu.VMEM((HIST + tc, 2 * ML_W), F32)],
        ),
        out_shape=[
            jax.ShapeDtypeStruct((n_rows, ML_W), BF16),
            jax.ShapeDtypeStruct((nb, ML_H, ML_DH, ML_DH), F32),
            jax.ShapeDtypeStruct((nb, 8, ML_DH), F32),
            jax.ShapeDtypeStruct((nb, 1, GATE_W), F32),
        ],
        compiler_params=_cparams(("parallel", "arbitrary"), vmem),
        name="mlstm",
    )(lidx, z, z, z, z, gates, hist0, conv_w, conv_b, b_if, out_g, c0, n0, m0, tri)


def _hgrn_level_mats(lc):
    t = np.arange(lc)[:, None]
    u = np.arange(lc)[None, :]
    mats = [(u <= t).astype(np.float32)]
    masks = []
    h = lc // 2
    while h >= 1:
        e = (t // (2 * h)) * (2 * h) + h - 1
        odd = (t % (2 * h)) >= h
        m = np.where(odd, (u > e) & (u <= t), (u > t) & (u <= e))
        mats.append(m.astype(np.float32))
        s = np.arange(lc)[None, :]
        same = (t // (2 * h)) == (s // (2 * h))
        masks.append((same & odd & ((s % (2 * h)) < h)).astype(np.float32))
        h //= 2
    masks.append((t == u).astype(np.float32))
    return np.concatenate(mats, axis=0), np.stack(masks, axis=0)


def _hgrn_kernel(l_ref, zq_ref, zf_ref, zi_ref, zg_ref, lbl_ref, og_ref, s0_ref, lvl_ref, msk_ref,
                 y_ref, s_ref, st_ref, *, tc, lc, nlev):
    c = pl.program_id(1)
    nc = pl.num_programs(1)

    @pl.when(c == 0)
    def _():
        for h in range(HG_H):
            st_ref[h] = s0_ref[0, h].T

    logits = lbl_ref[...]
    e = jnp.exp(logits - jnp.max(logits, axis=0, keepdims=True))
    p = e / jnp.sum(e, axis=0, keepdims=True)
    drow = lax.broadcasted_iota(jnp.int32, p.shape, 0)
    lb = jnp.sum(jnp.where(jnp.logical_and(drow >= 1, drow <= l_ref[0]), p, 0.0), axis=0, keepdims=True)
    log_lb = jnp.log(jnp.maximum(lb, LB_FLOOR))
    log_1m = jnp.log1p(-lb)

    zf = zf_ref[...]
    a2 = log_1m + _log_sigmoid(zf)
    hi = jnp.maximum(log_lb, a2)
    lf = hi + jnp.log1p(jnp.exp(-jnp.abs(log_lb - a2)))
    kk = (1.0 - lb) * jax.nn.sigmoid(-zf) + (lb - jnp.maximum(lb, LB_FLOOR))
    zq = zq_ref[...]
    qq = zq * jax.nn.sigmoid(zq) * (HG_DK ** -0.5)
    lvl = lvl_ref[...]
    og = og_ref[0]

    for ci in range(tc // lc):
        r0 = ci * lc
        ex = _dot_exact01(lvl, lf[r0:r0 + lc, :])
        for h in range(HG_H):
            sl = slice(h * HG_DK, (h + 1) * HG_DK)
            q = qq[r0:r0 + lc, sl]
            k = kk[r0:r0 + lc, sl]
            iv = zi_ref[r0:r0 + lc, sl].astype(BF16)
            a_in = ex[0:lc, sl]
            a_end = a_in[lc - 1:lc, :]
            st = st_ref[h]
            o = _dot_nt((q * jnp.exp(a_in)).astype(BF16), st.astype(BF16))
            att = jnp.where(msk_ref[nlev] > 0, _dot_nt(q.astype(BF16), k.astype(BF16)), 0.0)
            for lv in range(nlev):
                xf = jnp.exp(ex[(1 + lv) * lc:(2 + lv) * lc, sl])
                pm = _dot_nt((q * xf).astype(BF16), (k * xf).astype(BF16))
                att = att + jnp.where(msk_ref[lv] > 0, pm, 0.0)
            o = o + _dot(att.astype(BF16), iv)
            k_end = (k * jnp.exp(a_end - a_in)).astype(BF16)
            st_ref[h] = st * jnp.exp(a_end) + _dot_tn(iv, k_end)
            gsl = zg_ref[r0:r0 + lc, sl]
            y_ref[r0:r0 + lc, sl] = (_rms(o, og) * (gsl * jax.nn.sigmoid(gsl))).astype(BF16)

    @pl.when(c == nc - 1)
    def _():
        for h in range(HG_H):
            s_ref[0, h] = st_ref[h].T


def hgrn_mixer(z, lidx, lb_logits, out_g, s0, *, nb, seq, tc, lc):
    n_rows = nb * seq
    nc = seq // tc
    lvl_np, msk_np = _hgrn_level_mats(lc)
    nlev = msk_np.shape[0] - 1
    lvl = jnp.asarray(lvl_np, BF16)
    msk = jnp.asarray(msk_np, F32)
    zmap = lambda blk: (lambda b, c, l: (b * nc + c, blk))
    vmem = 2 * 5 * tc * HG_W * 4 + 5 * HG_H * HG_DK * HG_DV * 4 + 8 * tc * HG_W * 4 + (8 << 20)
    kern = functools.partial(_hgrn_kernel, tc=tc, lc=lc, nlev=nlev)
    return pl.pallas_call(
        kern,
        grid_spec=pltpu.PrefetchScalarGridSpec(
            num_scalar_prefetch=1,
            grid=(nb, nc),
            in_specs=[
                pl.BlockSpec((tc, HG_W), zmap(ZB_HGQ)),
                pl.BlockSpec((tc, HG_W), zmap(ZB_HGF)),
                pl.BlockSpec((tc, HG_W), zmap(ZB_HGI)),
                pl.BlockSpec((tc, HG_W), zmap(ZB_HGG)),
                pl.BlockSpec((DEPTH, HG_W), lambda b, c, l: (0, 0)),
                pl.BlockSpec((1, 1, HG_DV), lambda b, c, l: (l[0], 0, 0)),
                pl.BlockSpec((1, HG_H, HG_DK, HG_DV), lambda b, c, l: (b, 0, 0, 0)),
                pl.BlockSpec(lvl_np.shape, lambda b, c, l: (0, 0)),
                pl.BlockSpec(msk_np.shape, lambda b, c, l: (0, 0, 0)),
            ],
            out_specs=[
                pl.BlockSpec((tc, HG_W), lambda b, c, l: (b * nc + c, 0)),
                pl.BlockSpec((1, HG_H, HG_DK, HG_DV), lambda b, c, l: (b, 0, 0, 0)),
            ],
            scratch_shapes=[pltpu.VMEM((HG_H, HG_DV, HG_DK), F32)],
        ),
        out_shape=[
            jax.ShapeDtypeStruct((n_rows, HG_W), BF16),
            jax.ShapeDtypeStruct((nb, HG_H, HG_DK, HG_DV), F32),
        ],
        compiler_params=_cparams(("parallel", "arbitrary"), vmem),
        name="hgrn",
    )(lidx, z, z, z, z, lb_logits, out_g, s0, lvl, msk)


def _qk_prep_kernel(l_ref, zq_ref, zk_ref, zv_ref, qg_ref, kg_ref, grp_ref, qn_ref, kn_ref, knb_ref, vb_ref):
    grp = grp_ref[...]
    qn_ref[...] = (_group_norm64(zq_ref[...], qg_ref[0], grp) * (DF_DH ** -0.5)).astype(BF16)
    kn = _group_norm64(zk_ref[...], kg_ref[0], grp)
    kn_ref[...] = kn
    knb_ref[...] = kn.astype(BF16)
    vb_ref[...] = zv_ref[...].astype(BF16)


def qk_prep(z, lidx, qg, kg, *, tm):
    n_rows = z.shape[0]
    lane = np.arange(DF_W)
    grp = jnp.asarray(((lane[:, None] // DF_DH) == (lane[None, :] // DF_DH)).astype(np.float32) / DF_DH, BF16)
    zmap = lambda blk: (lambda i, l: (i, blk))
    row = lambda i, l: (i, 0)
    return pl.pallas_call(
        _qk_prep_kernel,
        grid_spec=pltpu.PrefetchScalarGridSpec(
            num_scalar_prefetch=1,
            grid=(n_rows // tm,),
            in_specs=[
                pl.BlockSpec((tm, DF_W), zmap(ZB_DFQ)),
                pl.BlockSpec((tm, DF_W), zmap(ZB_DFK)),
                pl.BlockSpec((tm, DF_W), zmap(ZB_DFV)),
                pl.BlockSpec((1, 1, DF_W), lambda i, l: (l[0], 0, 0)),
                pl.BlockSpec((1, 1, DF_W), lambda i, l: (l[0], 0, 0)),
                pl.BlockSpec((DF_W, DF_W), lambda i, l: (0, 0)),
            ],
            out_specs=[pl.BlockSpec((tm, DF_W), row)] * 4,
        ),
        out_shape=[
            jax.ShapeDtypeStruct((n_rows, DF_W), BF16),
            jax.ShapeDtypeStruct((n_rows, DF_W), F32),
            jax.ShapeDtypeStruct((n_rows, DF_W), BF16),
            jax.ShapeDtypeStruct((n_rows, DF_W), BF16),
        ],
        compiler_params=_cparams(("parallel",), 32 << 20),
        name="qk_prep",
    )(lidx, z, z, z, qg, kg, grp)


def _group_norm64(x, g, grp):
    x2 = x * x
    hi = x2.astype(BF16)
    lo = (x2 - hi.astype(F32)).astype(BF16)
    ms = _dot(hi, grp) + _dot(lo, grp)
    return x * lax.rsqrt(ms + EPS) * g


BIAS_W = 128


def _qk_prep_t_kernel(l_ref, zq_ref, zk_ref, zv_ref, qg_ref, kg_ref, grp_ref, qt_ref, kn_ref, ke_ref, vt_ref,
                      *, tm, seq):
    i = pl.program_id(0)
    grp = grp_ref[...]
    qn = _group_norm64(zq_ref[...], qg_ref[0], grp) * (DF_DH ** -0.5)
    kn = _group_norm64(zk_ref[...], kg_ref[0], grp)
    kn_ref[...] = kn
    zv = zv_ref[...]
    pos = lax.rem(i * tm, seq) + lax.broadcasted_iota(jnp.int32, (tm, BIAS_W), 0)
    lane = lax.broadcasted_iota(jnp.int32, (tm, BIAS_W), 1)
    hi_part = (pos >> CHUNK_SHIFT).astype(F32) * float(CHUNK)
    lo_part = (pos & (CHUNK - 1)).astype(F32)
    base = jnp.where(lane == 0, hi_part, jnp.where(lane == 1, lo_part, jnp.where(lane == 2, float(CHUNK), 0.0)))
    for h in range(DF_H):
        sl = slice(h * DF_DV, (h + 1) * DF_DV)
        slope = 2.0 ** (-8.0 * (h + 1) / DF_H)
        c0 = h * (DF_DV + BIAS_W)
        ke_ref[:, c0:c0 + DF_DV] = kn[:, sl].astype(BF16)
        ke_ref[:, c0 + DF_DV:c0 + DF_DV + BIAS_W] = (base * slope).astype(BF16)
        qt_ref[h, 0] = qn[:, sl].T.astype(BF16)
        vt_ref[h, 0] = zv[:, sl].T.astype(BF16)


def qk_prep_t(z, lidx, qg, kg, *, tm, seq):
    n_rows = z.shape[0]
    nt = n_rows // tm
    lane = np.arange(DF_W)
    grp = jnp.asarray(((lane[:, None] // DF_DH) == (lane[None, :] // DF_DH)).astype(np.float32) / DF_DH, BF16)
    zmap = lambda blk: (lambda i, l: (i, blk))
    row = lambda i, l: (i, 0)
    tmap = lambda i, l: (0, i, 0, 0)
    kern = functools.partial(_qk_prep_t_kernel, tm=tm, seq=seq)
    return pl.pallas_call(
        kern,
        grid_spec=pltpu.PrefetchScalarGridSpec(
            num_scalar_prefetch=1,
            grid=(nt,),
            in_specs=[
                pl.BlockSpec((tm, DF_W), zmap(ZB_DFQ)),
                pl.BlockSpec((tm, DF_W), zmap(ZB_DFK)),
                pl.BlockSpec((tm, DF_W), zmap(ZB_DFV)),
                pl.BlockSpec((1, 1, DF_W), lambda i, l: (l[0], 0, 0)),
                pl.BlockSpec((1, 1, DF_W), lambda i, l: (l[0], 0, 0)),
                pl.BlockSpec((DF_W, DF_W), lambda i, l: (0, 0)),
            ],
            out_specs=[
                pl.BlockSpec((DF_H, 1, DF_DV, tm), tmap),
                pl.BlockSpec((tm, DF_W), row),
                pl.BlockSpec((tm, DF_H * (DF_DV + BIAS_W)), row),
                pl.BlockSpec((DF_H, 1, DF_DV, tm), tmap),
            ],
        ),
        out_shape=[
            jax.ShapeDtypeStruct((DF_H, nt, DF_DV, tm), BF16),
            jax.ShapeDtypeStruct((n_rows, DF_W), F32),
            jax.ShapeDtypeStruct((n_rows, DF_H * (DF_DV + BIAS_W)), BF16),
            jax.ShapeDtypeStruct((DF_H, nt, DF_DV, tm), BF16),
        ],
        compiler_params=_cparams(("parallel",), 40 << 20),
        name="qk_prep_t",
    )(lidx, z, z, z, qg, kg, grp)


ONES_ROWS = 16


def _flash_t_kernel(l_ref, qt_ref, ke_ref, vt_ref, lam_ref, ogt_ref, y_ref, m_ref, acc_ref, sa_ref, sb_ref, *, tq):
    h = pl.program_id(1)
    qi = pl.program_id(2)
    tk = tq
    slope = jnp.exp2(jnp.full((1, 1), -8.0 / DF_H, F32) * (h + 1).astype(F32))
    qt = qt_ref[0, 0]
    row = lax.broadcasted_iota(jnp.int32, (DF_DV, tq), 0)
    zero = jnp.zeros_like(qt)
    q2t = jnp.concatenate([jnp.where(row < DF_DH, qt, zero), jnp.where(row >= DF_DH, qt, zero)], axis=1)
    brow = lax.broadcasted_iota(jnp.int32, (BIAS_W, 2 * tq), 0)
    a0 = ((qi * tq) >> CHUNK_SHIFT).astype(F32)
    extra = jnp.where(brow < 2, 1.0, jnp.where(brow == 2, -a0, 0.0)).astype(BF16)
    q2e = jnp.concatenate([q2t, extra], axis=0)
    ones = jnp.ones((ONES_ROWS, tk), BF16)

    m_ref[...] = jnp.full(m_ref.shape, NEG_BIG, F32)
    acc_ref[...] = jnp.zeros(acc_ref.shape, F32)

    def scores_into(dst_ref, kj):
        r0 = pl.multiple_of(kj * tk, tk)
        dst_ref[...] = _dot(ke_ref[pl.ds(r0, tk), :], q2e)

    def update(s, kj):
        m_old = m_ref[...]
        m_new = jnp.maximum(m_old, jnp.max(s, axis=0, keepdims=True))
        alpha = jnp.exp(m_old - m_new)
        p = jnp.exp(s - m_new).astype(BF16)
        vt_ext = jnp.concatenate([vt_ref[0, kj], ones], axis=0)
        acc_ref[...] = alpha * acc_ref[...] + _dot(vt_ext, p)
        m_ref[...] = m_new

    def own_block(s):
        rel_k = lax.broadcasted_iota(jnp.int32, (tq, 2 * tq), 0)
        rel_q = lax.broadcasted_iota(jnp.int32, (tq, 2 * tq), 1)
        rel_q = jnp.where(rel_q >= tq, rel_q - tq, rel_q)
        ahead = rel_k - rel_q
        s = s - jnp.where(ahead > 0, (2.0 * slope) * ahead.astype(F32), 0.0)
        visible = (rel_k >> CHUNK_SHIFT) <= (rel_q >> CHUNK_SHIFT)
        update(jnp.where(visible, s, NEG_BIG), qi)

    scores_into(sa_ref, 0)

    def block_pair(t, carry):
        k0 = 2 * t
        scores_into(sb_ref, k0 + 1)
        update(sa_ref[...], k0)
        scores_into(sa_ref, k0 + 2)
        update(sb_ref[...], k0 + 1)
        return carry

    lax.fori_loop(0, qi >> 1, block_pair, 0)

    @pl.when((qi & 1) == 1)
    def _():
        scores_into(sb_ref, qi)
        update(sa_ref[...], qi - 1)
        own_block(sb_ref[...])

    @pl.when((qi & 1) == 0)
    def _():
        own_block(sa_ref[...])

    lam_p = lam_ref[0]
    lam_init = 0.8 - 0.6 * jnp.exp(jnp.full((1, 1), -0.3, F32) * l_ref[0].astype(F32))
    lam = (jnp.exp(jnp.sum(lam_p[0:1] * lam_p[1:2], axis=-1, keepdims=True))
           - jnp.exp(jnp.sum(lam_p[2:3] * lam_p[3:4], axis=-1, keepdims=True)) + lam_init)
    acc = acc_ref[...]
    o_all = acc[0:DF_DV] / acc[DF_DV:DF_DV + 1]
    o = o_all[:, 0:tq] - lam * o_all[:, tq:2 * tq]
    ms = jnp.mean(o * o, axis=0, keepdims=True)
    y = o * lax.rsqrt(ms + EPS) * ogt_ref[0] * (1.0 - lam_init)
    y_ref[...] = y.T.astype(BF16)


def diff_flash_t(qt, ke, vt, lidx, lam_p, out_g_t, *, nb, seq, tq):
    assert tq % CHUNK == 0
    nq = seq // tq
    n_rows = nb * seq
    kew = DF_DV + BIAS_W
    kern = functools.partial(_flash_t_kernel, tq=tq)
    vmem = 2 * seq * kew * 2 + 2 * seq * DF_DV * 2 + 10 * tq * 2 * tq * 4 + (8 << 20)
    return pl.pallas_call(
        kern,
        grid_spec=pltpu.PrefetchScalarGridSpec(
            num_scalar_prefetch=1,
            grid=(nb, DF_H, nq),
            in_specs=[
                pl.BlockSpec((1, 1, DF_DV, tq), lambda b, h, i, l: (h, b * nq + i, 0, 0)),
                pl.BlockSpec((seq, kew), lambda b, h, i, l: (b, h)),
                pl.BlockSpec((1, nq, DF_DV, tq), lambda b, h, i, l: (h, b, 0, 0)),
                pl.BlockSpec((1, 4, DF_DH), lambda b, h, i, l: (l[0], 0, 0)),
                pl.BlockSpec((1, DF_DV, 1), lambda b, h, i, l: (l[0], 0, 0)),
            ],
            out_specs=pl.BlockSpec((tq, DF_DV), lambda b, h, i, l: (b * nq + i, h)),
            scratch_shapes=[pltpu.VMEM((1, 2 * tq), F32), pltpu.VMEM((DF_DV + ONES_ROWS, 2 * tq), F32),
                            pltpu.VMEM((tq, 2 * tq), F32), pltpu.VMEM((tq, 2 * tq), F32)],
        ),
        out_shape=jax.ShapeDtypeStruct((n_rows, DF_W), BF16),
        compiler_params=_cparams(("parallel", "parallel", "arbitrary"), vmem),
        name="diff_flash_t",
    )(lidx, qt, ke, vt, lam_p, out_g_t)


def _flash_kernel(l_ref, q_ref, km_ref, vm_ref, kx_ref, vx_ref, lam_ref, og_ref, y_ref,
                  m_ref, l_sum_ref, acc_ref, *, tq, tk, past, prompt):
    h = pl.program_id(1)
    qi = pl.program_id(2)
    slope = jnp.exp2(jnp.full((1, 1), -8.0 / DF_H, F32) * (h + 1).astype(F32))
    q = q_ref[...]
    lane = lax.broadcasted_iota(jnp.int32, (tq, DF_DV), 1)
    zero = jnp.zeros_like(q)
    q2 = jnp.concatenate([jnp.where(lane < DF_DH, q, zero), jnp.where(lane >= DF_DH, q, zero)], axis=0)
    q0 = past + qi * tq

    m_ref[...] = jnp.full(m_ref.shape, NEG_BIG, F32)
    l_sum_ref[...] = jnp.zeros(l_sum_ref.shape, F32)
    acc_ref[...] = jnp.zeros(acc_ref.shape, F32)

    def update(s, v):
        m_old = m_ref[...]
        m_new = jnp.maximum(m_old, jnp.max(s, axis=-1, keepdims=True))
        alpha = jnp.exp(m_old - m_new)
        p = jnp.exp(s - m_new)
        l_sum_ref[...] = alpha * l_sum_ref[...] + jnp.sum(p, axis=-1, keepdims=True)
        acc_ref[...] = alpha * acc_ref[...] + _dot(p.astype(BF16), v)
        m_ref[...] = m_new

    def fast_block(kj, carry):
        r0 = pl.multiple_of(kj * tk, tk)
        k = km_ref[pl.ds(r0, tk), :].astype(BF16)
        v = vm_ref[pl.ds(r0, tk), :].astype(BF16)
        kpos = r0 + lax.broadcasted_iota(jnp.int32, (1, tk), 1)
        s = _dot_nt(q2, k) + slope * (kpos - q0).astype(F32)
        update(s, v)
        return carry

    n_fast = qi if prompt else past // tk
    lax.fori_loop(0, n_fast, fast_block, 0)

    if prompt:
        r0 = pl.multiple_of(qi * tq, tq)
        k = km_ref[pl.ds(r0, tq), :].astype(BF16)
        v = vm_ref[pl.ds(r0, tq), :].astype(BF16)
    else:
        k = kx_ref[...].astype(BF16)
        v = vx_ref[...].astype(BF16)
    rel_k = lax.broadcasted_iota(jnp.int32, (2 * tq, tq), 1)
    rel_q = lax.broadcasted_iota(jnp.int32, (2 * tq, tq), 0)
    rel_q = jnp.where(rel_q >= tq, rel_q - tq, rel_q)
    dist = jnp.abs(rel_q - rel_k).astype(F32)
    s = _dot_nt(q2, k) + slope * (rel_q.astype(F32) - dist)
    visible = ((q0 + rel_k) >> CHUNK_SHIFT) <= ((q0 + rel_q) >> CHUNK_SHIFT)
    update(jnp.where(visible, s, NEG_BIG), v)

    lam_p = lam_ref[0]
    lam_init = 0.8 - 0.6 * jnp.exp(jnp.full((1, 1), -0.3, F32) * l_ref[0].astype(F32))
    lam = (jnp.exp(jnp.sum(lam_p[0:1] * lam_p[1:2], axis=-1, keepdims=True))
           - jnp.exp(jnp.sum(lam_p[2:3] * lam_p[3:4], axis=-1, keepdims=True)) + lam_init)
    o_all = acc_ref[...] / l_sum_ref[...]
    o = o_all[0:tq] - lam * o_all[tq:2 * tq]
    y_ref[...] = (_rms(o, og_ref[0]) * (1.0 - lam_init)).astype(BF16)


def diff_flash(qn, k_main, v_main, k_new, v_new, lidx, lam_p, out_g, *, nb, seq, past, tq, tk, prompt,
               main_rows, main_block):
    nq = seq // tq
    n_rows = nb * seq
    kern = functools.partial(_flash_kernel, tq=tq, tk=tk, past=past, prompt=prompt)
    vmem = 4 * main_rows * DF_DV * k_main.dtype.itemsize + 8 * (2 * tq) * max(tk, tq) * 4 + (8 << 20)
    qmap = lambda b, h, i, l: (b * nq + i, h)
    return pl.pallas_call(
        kern,
        grid_spec=pltpu.PrefetchScalarGridSpec(
            num_scalar_prefetch=1,
            grid=(nb, DF_H, nq),
            in_specs=[
                pl.BlockSpec((tq, DF_DV), qmap),
                pl.BlockSpec((main_rows, DF_DV), lambda b, h, i, l: (main_block(l, b), h)),
                pl.BlockSpec((main_rows, DF_DV), lambda b, h, i, l: (main_block(l, b), h)),
                pl.BlockSpec((tq, DF_DV), qmap),
                pl.BlockSpec((tq, DF_DV), qmap),
                pl.BlockSpec((1, 4, DF_DH), lambda b, h, i, l: (l[0], 0, 0)),
                pl.BlockSpec((1, 1, DF_DV), lambda b, h, i, l: (l[0], 0, 0)),
            ],
            out_specs=pl.BlockSpec((tq, DF_DV), qmap),
            scratch_shapes=[pltpu.VMEM((2 * tq, 1), F32), pltpu.VMEM((2 * tq, 1), F32),
                            pltpu.VMEM((2 * tq, DF_DV), F32)],
        ),
        out_shape=jax.ShapeDtypeStruct((n_rows, DF_W), BF16),
        compiler_params=_cparams(("parallel", "parallel", "arbitrary"), vmem),
        name="diff_flash_prompt" if prompt else "diff_flash_sample",
    )(lidx, qn, k_main, v_main, k_new, v_new, lam_p, out_g)


def _trunk_layer(x, lidx, w, st, cfg):
    nb, seq, past = cfg["nb"], cfg["seq"], cfg["past"]
    z, gates = in_proj(x, lidx, w["norm_mix"], w["w_main"], w["w_gate"], tm=cfg["tm_in"], tn=cfg["tn_in"])
    y_ml, c_new, n_new, m_new = mlstm_mixer(
        z, gates, lidx, st["hist0"], w["conv_w"], w["conv_b"], w["b_if"], w["ml_og"],
        st["c0"], st["n0"], st["m0"], nb=nb, seq=seq, tc=cfg["tc"], lc=cfg["lc_ml"])
    if cfg["prompt"]:
        qt, kn, ke, vt = qk_prep_t(z, lidx, w["df_qg"], w["df_kg"], tm=cfg["tq"], seq=seq)
        y_df = diff_flash_t(qt, ke, vt, lidx, w["df_lam"], w["df_og_t"], nb=nb, seq=seq, tq=cfg["tq"])
    else:
        qn, kn, kn_b, v_b = qk_prep(z, lidx, w["df_qg"], w["df_kg"], tm=cfg["tm_prep"])
        y_df = diff_flash(qn, st["past_k"], st["past_v"], kn_b, v_b, lidx, w["df_lam"], w["df_og"], nb=nb,
                          seq=seq, past=past, tq=seq, tk=cfg["tk_past"], prompt=False, main_rows=past,
                          main_block=lambda l, b: l[0] * nb + b)
    y_hg, s_new = hgrn_mixer(z, lidx, w["hg_lbl"], w["hg_og"], st["s0"], nb=nb, seq=seq, tc=cfg["tc"],
                             lc=cfg["lc_hg"])
    x = post_mix(x, y_ml, y_df, y_hg, lidx, w["w_out"], w["norm_cross"], w["wq"], w["ca_qg"],
                 st["mk"], st["mv"], w["wo"], tm=cfg["tm_post"], rows_per_batch=seq)
    x = ffn(x, lidx, w["norm_ffn"], w["w_gu"], w["w_down"], w["final_norm"], tm=cfg["tm_ffn"], th=cfg["th"])
    z3 = z.reshape(nb, seq, Z_W)
    new = {
        "attn_k": kn.reshape(nb, seq, DF_H, 2 * DF_DH),
        "attn_v": z3[:, :, ZB_DFV * DF_W:(ZB_DFV + 1) * DF_W].reshape(nb, seq, DF_H, DF_DV),
        "conv": z3[:, seq - (CONV_W - 1):, 0:2 * ML_W],
        "C": c_new,
        "n": n_new[:, :ML_H, :],
        "m": m_new[:, 0, :ML_H],
        "S": s_new,
    }
    return x, new


PROMPT_CFG = dict(prompt=True, tm_in=1024, tn_in=768, tc=256, lc_ml=64, lc_hg=64, tm_prep=512, tq=512,
                  tm_post=512, tm_ffn=512, th=512)
SAMPLE_CFG = dict(prompt=False, tm_in=128, tn_in=768, tc=16, lc_ml=16, lc_hg=16, tm_prep=128, tk_past=512,
                  tm_post=16, tm_ffn=128, th=512)


def kernel(x_prompt, x_sample, mem_prompt, cache_attn_k, cache_attn_v, cache_mem_k, cache_mem_v, state_mlstm_conv, state_mlstm_C, state_mlstm_n, state_mlstm_m, state_hgrn_S, norm_mix, w_in, mlstm_conv_w, mlstm_conv_b, mlstm_b_i, mlstm_b_f, mlstm_out_norm, diff_q_norm, diff_k_norm, diff_lambda, diff_out_norm, hgrn_lb_logits, hgrn_out_norm, w_out, norm_cross, norm_mem, cross_wq, cross_wk, cross_wv, cross_q_norm, cross_k_norm, cross_wo, norm_ffn, ffn_w_gate_up, ffn_w_down, final_norm):
    bp, tp = x_prompt.shape[:2]
    bs, ts = x_sample.shape[:2]
    past = cache_attn_k.shape[2]
    depth = w_in.shape[0]
    assert depth == DEPTH and x_prompt.shape[2] == D_MODEL

    g_off = 4 * ML_W
    r3 = lambda a: a.reshape(depth, 1, a.shape[-1])
    w = {
        "norm_mix": r3(norm_mix),
        "w_main": jnp.concatenate([w_in[:, :, :g_off], w_in[:, :, g_off + 2 * ML_H:]], axis=-1).astype(BF16),
        "w_gate": jnp.pad(w_in[:, :, g_off:g_off + 2 * ML_H], ((0, 0), (0, 0), (0, GATE_W - 2 * ML_H))).astype(BF16),
        "conv_w": mlstm_conv_w,
        "conv_b": r3(mlstm_conv_b),
        "b_if": r3(jnp.pad(jnp.concatenate([mlstm_b_i, mlstm_b_f], axis=-1), ((0, 0), (0, GATE_W - 2 * ML_H)))),
        "ml_og": r3(mlstm_out_norm),
        "df_qg": r3(jnp.tile(diff_q_norm, (1, DF_W // DF_DH))),
        "df_kg": r3(jnp.tile(diff_k_norm, (1, DF_W // DF_DH))),
        "df_lam": diff_lambda,
        "df_og": r3(diff_out_norm),
        "df_og_t": diff_out_norm.reshape(depth, DF_DV, 1),
        "hg_lbl": hgrn_lb_logits,
        "hg_og": r3(hgrn_out_norm),
        "w_out": w_out.astype(BF16),
        "norm_cross": r3(norm_cross),
        "wq": cross_wq.astype(BF16),
        "ca_qg": r3(cross_q_norm),
        "wo": cross_wo.astype(BF16),
        "norm_ffn": r3(norm_ffn),
        "w_gu": ffn_w_gate_up.astype(BF16),
        "w_down": ffn_w_down.astype(BF16),
        "final_norm": final_norm.reshape(1, D_MODEL),
    }

    mk_p, mv_p = mem_kv(mem_prompt, r3(norm_mem), cross_wk.astype(BF16), cross_wv.astype(BF16), r3(cross_k_norm))

    def pad_hist(conv):
        pad = [(0, 0)] * (conv.ndim - 2) + [(HIST - (CONV_W - 1), 0), (0, 0)]
        return jnp.pad(conv, pad)

    st_p = {
        "hist0": jnp.zeros((bp, HIST, 2 * ML_W), F32),
        "c0": jnp.zeros((bp, ML_H, ML_DH, ML_DH), F32),
        "n0": jnp.zeros((bp, 8, ML_DH), F32),
        "m0": jnp.zeros((bp, 1, GATE_W), F32),
        "s0": jnp.zeros((bp, HG_H, HG_DK, HG_DV), F32),
        "mk": mk_p, "mv": mv_p,
    }
    hist_s = pad_hist(state_mlstm_conv)
    n_s = jnp.pad(state_mlstm_n, ((0, 0), (0, 0), (0, 8 - ML_H), (0, 0)))
    m_s = jnp.pad(state_mlstm_m, ((0, 0), (0, 0), (0, GATE_W - ML_H))).reshape(depth, bs, 1, GATE_W)
    past_k = cache_attn_k.reshape(depth * bs * past, DF_W)
    past_v = cache_attn_v.reshape(depth * bs * past, DF_W)
    mk_s = cache_mem_k.reshape(depth * bs, N_MEM, CA_W)
    mv_s = cache_mem_v.reshape(depth * bs, N_MEM, CA_W)

    cfg_p = dict(PROMPT_CFG, nb=bp, seq=tp, past=0)
    cfg_s = dict(SAMPLE_CFG, nb=bs, seq=ts, past=past)

    def layer(carry, xs):
        xp, xsm = carry
        l, hist_l, c_l, n_l, m_l, s_l = xs
        lidx = l.reshape(1).astype(jnp.int32)
        xp, new_p = _trunk_layer(xp, lidx, w, st_p, cfg_p)
        st_s = {"hist0": hist_l, "c0": c_l, "n0": n_l, "m0": m_l, "s0": s_l,
                "past_k": past_k, "past_v": past_v, "mk": mk_s, "mv": mv_s}
        xsm, new_s = _trunk_layer(xsm, lidx, w, st_s, cfg_s)
        return (xp, xsm), (new_p, new_s)

    xs = (jnp.arange(depth, dtype=jnp.int32), hist_s, state_mlstm_C, n_s, m_s, state_hgrn_S)
    (xp, xsm), (new_p, new_s) = lax.scan(
        layer, (x_prompt.reshape(bp * tp, D_MODEL), x_sample.reshape(bs * ts, D_MODEL)), xs)

    y_prompt = xp.reshape(bp, tp, D_MODEL)
    y_sample = xsm.reshape(bs, ts, D_MODEL)
    p_mem_k = mk_p.reshape(depth, bp, N_MEM, CA_H, CA_DH)
    p_mem_v = mv_p.reshape(depth, bp, N_MEM, CA_H, CA_DH)
    return (y_prompt, y_sample,
            new_p["attn_k"], new_p["attn_v"], p_mem_k, p_mem_v, new_p["conv"], new_p["C"], new_p["n"],
            new_p["m"], new_p["S"],
            new_s["attn_k"], new_s["attn_v"], new_s["conv"], new_s["C"], new_s["n"], new_s["m"], new_s["S"])
```

```python
import functools
import math

import numpy as np
import jax
import jax.numpy as jnp
from jax import lax
from jax.experimental import pallas as pl
from jax.experimental.pallas import tpu as pltpu

F32 = jnp.float32
BF16 = jnp.bfloat16

D_MODEL = 2048
DEPTH = 4
CHUNK = 64
CHUNK_SHIFT = 6
ML_DH = 128
ML_W = 768
ML_H = 6
CONV_W = 4
DF_DH = 64
DF_DV = 128
DF_W = 512
DF_H = 4
HG_DK = 128
HG_DV = 128
HG_W = 768
HG_H = 6
CA_H = 4
CA_DH = 128
CA_W = 512
N_MEM = 256
FF_HIDDEN = 5632
EPS = 1e-6
NEG_BIG = -1e30
LB_FLOOR = 1e-30

Z_W = 2 * ML_W + 2 * ML_W + 3 * DF_W + 4 * HG_W
GATE_W = 128
ZB_MLQ, ZB_MLK, ZB_MLV, ZB_MLO = 0, 1, 2, 3
ZB_HGQ, ZB_HGF, ZB_HGI, ZB_HGG = 6, 7, 8, 9
ZB_DFQ, ZB_DFK, ZB_DFV = 6, 7, 8

V7X_VMEM_BYTES = 64 * 1024 * 1024
V7X_VMEM_CAP = 58 * 1024 * 1024
VMEM_SLACK = 6 * 1024 * 1024
HIST = 8

NT_DIMS = (((1,), (1,)), ((), ()))
TN_DIMS = (((0,), (0,)), ((), ()))


def _vmem_limit(nbytes):
    return int(min(V7X_VMEM_CAP, max(32 * 1024 * 1024, nbytes + VMEM_SLACK)))


def _cparams(sem, vmem_bytes):
    return pltpu.CompilerParams(dimension_semantics=sem, vmem_limit_bytes=_vmem_limit(vmem_bytes))


def _rms(x, g):
    ms = jnp.mean(x * x, axis=-1, keepdims=True)
    return x * lax.rsqrt(ms + EPS) * g


def _dot(a, b):
    return jnp.dot(a, b, preferred_element_type=F32)


def _dot_nt(a, b):
    return lax.dot_general(a, b, NT_DIMS, preferred_element_type=F32)


def _dot_tn(a, b):
    return lax.dot_general(a, b, TN_DIMS, preferred_element_type=F32)


def _log_sigmoid(x):
    return jnp.minimum(x, 0.0) - jnp.log1p(jnp.exp(-jnp.abs(x)))


def _split3(x):
    hi = x.astype(BF16)
    r1 = x - hi.astype(F32)
    mid = r1.astype(BF16)
    lo = (r1 - mid.astype(F32)).astype(BF16)
    return hi, mid, lo


def _dot_exact01(m01, x):
    hi, mid, lo = _split3(x)
    return _dot(m01, hi) + _dot(m01, mid) + _dot(m01, lo)


def _in_proj_kernel(l_ref, x_ref, g_ref, w_ref, wg_ref, z_ref, gate_ref, xn_ref):
    j = pl.program_id(1)

    @pl.when(j == 0)
    def _():
        xn = _rms(x_ref[...], g_ref[0]).astype(BF16)
        xn_ref[...] = xn
        gate_ref[...] = _dot(xn, wg_ref[0])

    z_ref[...] = _dot(xn_ref[...], w_ref[0])


def in_proj(x, lidx, g, w_main, w_gate, *, tm, tn):
    n_rows = x.shape[0]
    grid = (n_rows // tm, Z_W // tn)
    vmem = 2 * tm * D_MODEL * 4 + 2 * D_MODEL * tn * 2 + 2 * tm * tn * 4 + tm * D_MODEL * 2 \
        + 2 * tm * GATE_W * 4 + 2 * D_MODEL * GATE_W * 2 + (4 << 20)
    return pl.pallas_call(
        _in_proj_kernel,
        grid_spec=pltpu.PrefetchScalarGridSpec(
            num_scalar_prefetch=1,
            grid=grid,
            in_specs=[
                pl.BlockSpec((tm, D_MODEL), lambda i, j, l: (i, 0)),
                pl.BlockSpec((1, 1, D_MODEL), lambda i, j, l: (l[0], 0, 0)),
                pl.BlockSpec((1, D_MODEL, tn), lambda i, j, l: (l[0], 0, j)),
                pl.BlockSpec((1, D_MODEL, GATE_W), lambda i, j, l: (l[0], 0, 0)),
            ],
            out_specs=[
                pl.BlockSpec((tm, tn), lambda i, j, l: (i, j)),
                pl.BlockSpec((tm, GATE_W), lambda i, j, l: (i, 0)),
            ],
            scratch_shapes=[pltpu.VMEM((tm, D_MODEL), BF16)],
        ),
        out_shape=[jax.ShapeDtypeStruct((n_rows, Z_W), F32),
                   jax.ShapeDtypeStruct((n_rows, GATE_W), F32)],
        compiler_params=_cparams(("parallel", "arbitrary"), vmem),
        name="in_proj",
    )(lidx, x, g, w_main, w_gate)


def _ffn_kernel(l_ref, x_ref, g_ref, wg_ref, wu_ref, wd_ref, fg_ref, o_ref, xn_ref):
    j = pl.program_id(1)
    nj = pl.num_programs(1)

    @pl.when(j == 0)
    def _():
        x = x_ref[...]
        xn_ref[...] = _rms(x, g_ref[0]).astype(BF16)
        o_ref[...] = x

    xn = xn_ref[...]
    gt = _dot(xn, wg_ref[0])
    up = _dot(xn, wu_ref[0])
    act = (gt * jax.nn.sigmoid(gt) * up).astype(BF16)
    o_ref[...] += _dot(act, wd_ref[0])

    @pl.when(jnp.logical_and(j == nj - 1, l_ref[0] == DEPTH - 1))
    def _():
        o_ref[...] = _rms(o_ref[...], fg_ref[...])


def ffn(x, lidx, g, w_gu, w_down, final_g, *, tm, th, x_buffers=2):
    n_rows = x.shape[0]
    nh = FF_HIDDEN // th
    grid = (n_rows // tm, nh)
    vmem = (2 + x_buffers) * tm * D_MODEL * 4 + tm * D_MODEL * 2 + 6 * D_MODEL * th * 2 + 3 * tm * th * 4 \
        + tm * D_MODEL * 4 + (4 << 20)
    x_mode = {} if x_buffers == 2 else {"pipeline_mode": pl.Buffered(x_buffers)}
    return pl.pallas_call(
        _ffn_kernel,
        grid_spec=pltpu.PrefetchScalarGridSpec(
            num_scalar_prefetch=1,
            grid=grid,
            in_specs=[
                pl.BlockSpec((tm, D_MODEL), lambda i, j, l: (i, 0), **x_mode),
                pl.BlockSpec((1, 1, D_MODEL), lambda i, j, l: (l[0], 0, 0)),
                pl.BlockSpec((1, D_MODEL, th), lambda i, j, l: (l[0], 0, j)),
                pl.BlockSpec((1, D_MODEL, th), lambda i, j, l: (l[0], 0, j + nh)),
                pl.BlockSpec((1, th, D_MODEL), lambda i, j, l: (l[0], j, 0)),
                pl.BlockSpec((1, D_MODEL), lambda i, j, l: (0, 0)),
            ],
            out_specs=pl.BlockSpec((tm, D_MODEL), lambda i, j, l: (i, 0)),
            scratch_shapes=[pltpu.VMEM((tm, D_MODEL), BF16)],
        ),
        out_shape=jax.ShapeDtypeStruct((n_rows, D_MODEL), F32),
        compiler_params=_cparams(("parallel", "arbitrary"), vmem),
        name="ffn",
    )(lidx, x, g, w_gu, w_gu, w_down, final_g)


def _mem_kv_kernel(mem_ref, g_ref, wk_ref, wv_ref, kg_ref, k_ref, v_ref):
    mn = _rms(mem_ref[0], g_ref[0]).astype(BF16)
    k = _dot(mn, wk_ref[0])
    for h in range(CA_H):
        sl = slice(h * CA_DH, (h + 1) * CA_DH)
        k_ref[0, :, sl] = _rms(k[:, sl], kg_ref[0])
    v_ref[0] = _dot(mn, wv_ref[0])


def mem_kv(mem, g, wk, wv, kg):
    nb = mem.shape[0]
    out = jax.ShapeDtypeStruct((DEPTH * nb, N_MEM, CA_W), F32)
    return pl.pallas_call(
        _mem_kv_kernel,
        grid=(DEPTH, nb),
        in_specs=[
            pl.BlockSpec((1, N_MEM, D_MODEL), lambda l, b: (b, 0, 0)),
            pl.BlockSpec((1, 1, D_MODEL), lambda l, b: (l, 0, 0)),
            pl.BlockSpec((1, D_MODEL, CA_W), lambda l, b: (l, 0, 0)),
            pl.BlockSpec((1, D_MODEL, CA_W), lambda l, b: (l, 0, 0)),
            pl.BlockSpec((1, 1, CA_DH), lambda l, b: (l, 0, 0)),
        ],
        out_specs=[pl.BlockSpec((1, N_MEM, CA_W), lambda l, b: (l * nb + b, 0, 0)),
                   pl.BlockSpec((1, N_MEM, CA_W), lambda l, b: (l * nb + b, 0, 0))],
        out_shape=[out, out],
        compiler_params=_cparams(("arbitrary", "arbitrary"), 24 << 20),
        name="mem_kv",
    )(mem, g, wk, wv, kg)


def _post_kernel(l_ref, x_ref, yml_ref, ydf_ref, yhg_ref, wout_ref, gx_ref, wq_ref, qg_ref,
                 mk_ref, mv_ref, wo_ref, o_ref):
    x1 = x_ref[...]
    x1 = x1 + _dot(yml_ref[...], wout_ref[0, 0:ML_W, :])
    x1 = x1 + _dot(ydf_ref[...], wout_ref[0, ML_W:ML_W + DF_W, :])
    x1 = x1 + _dot(yhg_ref[...], wout_ref[0, ML_W + DF_W:D_MODEL, :])
    hn = _rms(x1, gx_ref[0]).astype(BF16)
    q = _dot(hn, wq_ref[0])
    heads = []
    for h in range(CA_H):
        sl = slice(h * CA_DH, (h + 1) * CA_DH)
        qh = _rms(q[:, sl], qg_ref[0]).astype(BF16)
        s = _dot_nt(qh, mk_ref[0, :, sl].astype(BF16)) * (CA_DH ** -0.5)
        s = s - jnp.max(s, axis=-1, keepdims=True)
        p = jnp.exp(s)
        p = p / jnp.sum(p, axis=-1, keepdims=True)
        heads.append(_dot(p.astype(BF16), mv_ref[0, :, sl].astype(BF16)))
    o = jnp.concatenate(heads, axis=-1).astype(BF16)
    o_ref[...] = x1 + _dot(o, wo_ref[0])


def post_mix(x, y_ml, y_df, y_hg, lidx, w_out, gx, wq, qg, mk, mv, wo, *, tm, rows_per_batch):
    n_rows = x.shape[0]
    nb = n_rows // rows_per_batch
    tiles_per_batch = rows_per_batch // tm
    grid = (n_rows // tm,)
    w_bytes = (D_MODEL * D_MODEL + 2 * D_MODEL * CA_W) * 2
    vmem = 2 * w_bytes + 6 * tm * D_MODEL * 4 + 2 * tm * D_MODEL * 2 + 4 * N_MEM * CA_W * 4 + (6 << 20)
    row = lambda i, l: (i, 0)
    mem_map = lambda i, l: (l[0] * nb + i // tiles_per_batch, 0, 0)
    return pl.pallas_call(
        _post_kernel,
        grid_spec=pltpu.PrefetchScalarGridSpec(
            num_scalar_prefetch=1,
            grid=grid,
            in_specs=[
                pl.BlockSpec((tm, D_MODEL), row),
                pl.BlockSpec((tm, ML_W), row),
                pl.BlockSpec((tm, DF_W), row),
                pl.BlockSpec((tm, HG_W), row),
                pl.BlockSpec((1, D_MODEL, D_MODEL), lambda i, l: (l[0], 0, 0)),
                pl.BlockSpec((1, 1, D_MODEL), lambda i, l: (l[0], 0, 0)),
                pl.BlockSpec((1, D_MODEL, CA_W), lambda i, l: (l[0], 0, 0)),
                pl.BlockSpec((1, 1, CA_DH), lambda i, l: (l[0], 0, 0)),
                pl.BlockSpec((1, N_MEM, CA_W), mem_map),
                pl.BlockSpec((1, N_MEM, CA_W), mem_map),
                pl.BlockSpec((1, CA_W, D_MODEL), lambda i, l: (l[0], 0, 0)),
            ],
            out_specs=pl.BlockSpec((tm, D_MODEL), row),
        ),
        out_shape=jax.ShapeDtypeStruct((n_rows, D_MODEL), F32),
        compiler_params=_cparams(("parallel",), vmem),
        name="post_mix",
    )(lidx, x, y_ml, y_df, y_hg, w_out, gx, wq, qg, mk, mv, wo)


def _mlstm_kernel(l_ref, zq_ref, zk_ref, zv_ref, zo_ref, gate_ref, hist_ref, cw_ref, cb_ref, bif_ref,
                  og_ref, c0_ref, n0_ref, m0_ref, tri_ref,
                  y_ref, c_ref, n_ref, m_ref, ext_ref, *, tc, lc):
    c = pl.program_id(1)

    @pl.when(c == 0)
    def _():
        c_ref[...] = c0_ref[...]
        n_ref[...] = n0_ref[...]
        m_ref[...] = m0_ref[...]
        ext_ref[0:HIST, :] = hist_ref[0]

    ext_ref[HIST:HIST + tc, 0:ML_W] = zq_ref[...]
    ext_ref[HIST:HIST + tc, ML_W:2 * ML_W] = zk_ref[...]
    acc = cb_ref[0]
    for j in range(CONV_W):
        acc = acc + cw_ref[0, j:j + 1, :] * ext_ref[pl.ds(HIST - (CONV_W - 1) + j, tc), :]
    tail = ext_ref[tc:tc + HIST, :]
    ext_ref[0:HIST, :] = tail
    qk = acc * jax.nn.sigmoid(acc)

    gz = gate_ref[...] + bif_ref[0]
    lf_all = _log_sigmoid(gz)
    tri = tri_ref[...]
    row_i = lax.broadcasted_iota(jnp.int32, (lc, lc), 0)
    col_i = lax.broadcasted_iota(jnp.int32, (lc, lc), 1)
    causal = row_i >= col_i
    diag = row_i == col_i
    og = og_ref[0]

    for ci in range(tc // lc):
        r0 = ci * lc
        ig_c = gz[r0:r0 + lc, :]
        b_c = _dot_exact01(tri, lf_all[r0:r0 + lc, :])
        for h in range(ML_H):
            sl = slice(h * ML_DH, (h + 1) * ML_DH)
            q = qk[r0:r0 + lc, h * ML_DH:(h + 1) * ML_DH]
            k = qk[r0:r0 + lc, ML_W + h * ML_DH:ML_W + (h + 1) * ML_DH] * (ML_DH ** -0.5)
            v = zv_ref[r0:r0 + lc, sl]
            qb = q.astype(BF16)
            b_col = b_c[:, ML_H + h:ML_H + h + 1]
            ig_col = ig_c[:, h:h + 1]
            r_row = jnp.sum(jnp.where(diag, ig_col - b_col, 0.0), axis=0, keepdims=True)
            m_prev = m_ref[0, :, h:h + 1]
            log_d = jnp.where(causal, b_col + r_row, NEG_BIG)
            inter = b_col + m_prev
            m_t = jnp.maximum(inter, jnp.max(log_d, axis=-1, keepdims=True))
            d_mat = jnp.exp(log_d - m_t)
            w_inter = jnp.exp(inter - m_t)
            s = _dot_nt(qb, k.astype(BF16)) * d_mat
            c_old = c_ref[0, h]
            n_old = n_ref[0, h:h + 1, :]
            num = _dot(s.astype(BF16), v.astype(BF16)) + w_inter * _dot(qb, c_old.astype(BF16))
            den = jnp.sum(s, axis=-1, keepdims=True) + w_inter * jnp.sum(q * n_old, axis=-1, keepdims=True)
            hh = num / jnp.maximum(jnp.abs(den), jnp.exp(-m_t))
            m_new = m_t[lc - 1:lc, :]
            b_last = b_col[lc - 1:lc, :]
            w_s = jnp.exp(b_last - b_col + ig_col - m_new)
            decay = jnp.exp(b_last + m_prev - m_new)
            kw = k * w_s
            c_ref[0, h] = decay * c_old + _dot_tn(kw.astype(BF16), v.astype(BF16))
            n_ref[0, h:h + 1, :] = decay * n_old + jnp.sum(kw, axis=0, keepdims=True)
            m_ref[0, :, h:h + 1] = m_new
            o_gate = jax.nn.sigmoid(zo_ref[r0:r0 + lc, sl])
            y_ref[r0:r0 + lc, sl] = (_rms(hh, og) * o_gate).astype(BF16)


def mlstm_mixer(z, gates, lidx, hist0, conv_w, conv_b, b_if, out_g, c0, n0, m0, *, nb, seq, tc, lc):
    n_rows = nb * seq
    nc = seq // tc
    tri = jnp.asarray(np.tril(np.ones((lc, lc), np.float32)), BF16)
    zmap = lambda blk: (lambda b, c, l: (b * nc + c, blk))
    lmap3 = lambda b, c, l: (l[0], 0, 0)
    vmem = 2 * 5 * tc * ML_W * 4 + (HIST + tc) * 2 * ML_W * 4 + 4 * ML_H * ML_DH * ML_DH * 4 \
        + 6 * tc * 2 * ML_W * 4 + (8 << 20)
    kern = functools.partial(_mlstm_kernel, tc=tc, lc=lc)
    return pl.pallas_call(
        kern,
        grid_spec=pltpu.PrefetchScalarGridSpec(
            num_scalar_prefetch=1,
            grid=(nb, nc),
            in_specs=[
                pl.BlockSpec((tc, ML_W), zmap(ZB_MLQ)),
                pl.BlockSpec((tc, ML_W), zmap(ZB_MLK)),
                pl.BlockSpec((tc, ML_W), zmap(ZB_MLV)),
                pl.BlockSpec((tc, ML_W), zmap(ZB_MLO)),
                pl.BlockSpec((tc, GATE_W), lambda b, c, l: (b * nc + c, 0)),
                pl.BlockSpec((1, HIST, 2 * ML_W), lambda b, c, l: (b, 0, 0)),
                pl.BlockSpec((1, CONV_W, 2 * ML_W), lmap3),
                pl.BlockSpec((1, 1, 2 * ML_W), lmap3),
                pl.BlockSpec((1, 1, GATE_W), lmap3),
                pl.BlockSpec((1, 1, ML_DH), lmap3),
                pl.BlockSpec((1, ML_H, ML_DH, ML_DH), lambda b, c, l: (b, 0, 0, 0)),
                pl.BlockSpec((1, 8, ML_DH), lambda b, c, l: (b, 0, 0)),
                pl.BlockSpec((1, 1, GATE_W), lambda b, c, l: (b, 0, 0)),
                pl.BlockSpec((lc, lc), lambda b, c, l: (0, 0)),
            ],
            out_specs=[
                pl.BlockSpec((tc, ML_W), lambda b, c, l: (b * nc + c, 0)),
                pl.BlockSpec((1, ML_H, ML_DH, ML_DH), lambda b, c, l: (b, 0, 0, 0)),
                pl.BlockSpec((1, 8, ML_DH), lambda b, c, l: (b, 0, 0)),
                pl.BlockSpec((1, 1, GATE_W), lambda b, c, l: (b, 0, 0)),
            ],
            scratch_shapes=[pltpu.VMEM((HIST + tc, 2 * ML_W), F32)],
        ),
        out_shape=[
            jax.ShapeDtypeStruct((n_rows, ML_W), BF16),
            jax.ShapeDtypeStruct((nb, ML_H, ML_DH, ML_DH), F32),
            jax.ShapeDtypeStruct((nb, 8, ML_DH), F32),
            jax.ShapeDtypeStruct((nb, 1, GATE_W), F32),
        ],
        compiler_params=_cparams(("parallel", "arbitrary"), vmem),
        name="mlstm",
    )(lidx, z, z, z, z, gates, hist0, conv_w, conv_b, b_if, out_g, c0, n0, m0, tri)


def _hgrn_level_mats(lc):
    t = np.arange(lc)[:, None]
    u = np.arange(lc)[None, :]
    mats = [(u <= t).astype(np.float32)]
    masks = []
    h = lc // 2
    while h >= 1:
        e = (t // (2 * h)) * (2 * h) + h - 1
        odd = (t % (2 * h)) >= h
        m = np.where(odd, (u > e) & (u <= t), (u > t) & (u <= e))
        mats.append(m.astype(np.float32))
        s = np.arange(lc)[None, :]
        same = (t // (2 * h)) == (s // (2 * h))
        masks.append((same & odd & ((s % (2 * h)) < h)).astype(np.float32))
        h //= 2
    masks.append((t == u).astype(np.float32))
    return np.concatenate(mats, axis=0), np.stack(masks, axis=0)


def _hgrn_kernel(l_ref, zq_ref, zf_ref, zi_ref, zg_ref, lbl_ref, og_ref, s0_ref, lvl_ref, msk_ref,
                 y_ref, s_ref, st_ref, *, tc, lc, nlev):
    c = pl.program_id(1)
    nc = pl.num_programs(1)

    @pl.when(c == 0)
    def _():
        for h in range(HG_H):
            st_ref[h] = s0_ref[0, h].T

    logits = lbl_ref[...]
    e = jnp.exp(logits - jnp.max(logits, axis=0, keepdims=True))
    p = e / jnp.sum(e, axis=0, keepdims=True)
    drow = lax.broadcasted_iota(jnp.int32, p.shape, 0)
    lb = jnp.sum(jnp.where(jnp.logical_and(drow >= 1, drow <= l_ref[0]), p, 0.0), axis=0, keepdims=True)
    log_lb = jnp.log(jnp.maximum(lb, LB_FLOOR))
    log_1m = jnp.log1p(-lb)

    zf = zf_ref[...]
    a2 = log_1m + _log_sigmoid(zf)
    hi = jnp.maximum(log_lb, a2)
    lf = hi + jnp.log1p(jnp.exp(-jnp.abs(log_lb - a2)))
    kk = (1.0 - lb) * jax.nn.sigmoid(-zf) + (lb - jnp.maximum(lb, LB_FLOOR))
    zq = zq_ref[...]
    qq = zq * jax.nn.sigmoid(zq) * (HG_DK ** -0.5)
    lvl = lvl_ref[...]
    og = og_ref[0]

    for ci in range(tc // lc):
        r0 = ci * lc
        ex = _dot_exact01(lvl, lf[r0:r0 + lc, :])
        for h in range(HG_H):
            sl = slice(h * HG_DK, (h + 1) * HG_DK)
            q = qq[r0:r0 + lc, sl]
            k = kk[r0:r0 + lc, sl]
            iv = zi_ref[r0:r0 + lc, sl].astype(BF16)
            a_in = ex[0:lc, sl]
            a_end = a_in[lc - 1:lc, :]
            st = st_ref[h]
            o = _dot_nt((q * jnp.exp(a_in)).astype(BF16), st.astype(BF16))
            att = jnp.where(msk_ref[nlev] > 0, _dot_nt(q.astype(BF16), k.astype(BF16)), 0.0)
            for lv in range(nlev):
                xf = jnp.exp(ex[(1 + lv) * lc:(2 + lv) * lc, sl])
                pm = _dot_nt((q * xf).astype(BF16), (k * xf).astype(BF16))
                att = att + jnp.where(msk_ref[lv] > 0, pm, 0.0)
            o = o + _dot(att.astype(BF16), iv)
            k_end = (k * jnp.exp(a_end - a_in)).astype(BF16)
            st_ref[h] = st * jnp.exp(a_end) + _dot_tn(iv, k_end)
            gsl = zg_ref[r0:r0 + lc, sl]
            y_ref[r0:r0 + lc, sl] = (_rms(o, og) * (gsl * jax.nn.sigmoid(gsl))).astype(BF16)

    @pl.when(c == nc - 1)
    def _():
        for h in range(HG_H):
            s_ref[0, h] = st_ref[h].T


def hgrn_mixer(z, lidx, lb_logits, out_g, s0, *, nb, seq, tc, lc):
    n_rows = nb * seq
    nc = seq // tc
    lvl_np, msk_np = _hgrn_level_mats(lc)
    nlev = msk_np.shape[0] - 1
    lvl = jnp.asarray(lvl_np, BF16)
    msk = jnp.asarray(msk_np, F32)
    zmap = lambda blk: (lambda b, c, l: (b * nc + c, blk))
    vmem = 2 * 5 * tc * HG_W * 4 + 5 * HG_H * HG_DK * HG_DV * 4 + 8 * tc * HG_W * 4 + (8 << 20)
    kern = functools.partial(_hgrn_kernel, tc=tc, lc=lc, nlev=nlev)
    return pl.pallas_call(
        kern,
        grid_spec=pltpu.PrefetchScalarGridSpec(
            num_scalar_prefetch=1,
            grid=(nb, nc),
            in_specs=[
                pl.BlockSpec((tc, HG_W), zmap(ZB_HGQ)),
                pl.BlockSpec((tc, HG_W), zmap(ZB_HGF)),
                pl.BlockSpec((tc, HG_W), zmap(ZB_HGI)),
                pl.BlockSpec((tc, HG_W), zmap(ZB_HGG)),
                pl.BlockSpec((DEPTH, HG_W), lambda b, c, l: (0, 0)),
                pl.BlockSpec((1, 1, HG_DV), lambda b, c, l: (l[0], 0, 0)),
                pl.BlockSpec((1, HG_H, HG_DK, HG_DV), lambda b, c, l: (b, 0, 0, 0)),
                pl.BlockSpec(lvl_np.shape, lambda b, c, l: (0, 0)),
                pl.BlockSpec(msk_np.shape, lambda b, c, l: (0, 0, 0)),
            ],
            out_specs=[
                pl.BlockSpec((tc, HG_W), lambda b, c, l: (b * nc + c, 0)),
                pl.BlockSpec((1, HG_H, HG_DK, HG_DV), lambda b, c, l: (b, 0, 0, 0)),
            ],
            scratch_shapes=[pltpu.VMEM((HG_H, HG_DV, HG_DK), F32)],
        ),
        out_shape=[
            jax.ShapeDtypeStruct((n_rows, HG_W), BF16),
            jax.ShapeDtypeStruct((nb, HG_H, HG_DK, HG_DV), F32),
        ],
        compiler_params=_cparams(("parallel", "arbitrary"), vmem),
        name="hgrn",
    )(lidx, z, z, z, z, lb_logits, out_g, s0, lvl, msk)


def _qk_prep_kernel(l_ref, zq_ref, zk_ref, zv_ref, qg_ref, kg_ref, grp_ref, qn_ref, kn_ref, knb_ref, vb_ref):
    grp = grp_ref[...]
    qn_ref[...] = (_group_norm64(zq_ref[...], qg_ref[0], grp) * (DF_DH ** -0.5)).astype(BF16)
    kn = _group_norm64(zk_ref[...], kg_ref[0], grp)
    kn_ref[...] = kn
    knb_ref[...] = kn.astype(BF16)
    vb_ref[...] = zv_ref[...].astype(BF16)


def qk_prep(z, lidx, qg, kg, *, tm):
    n_rows = z.shape[0]
    lane = np.arange(DF_W)
    grp = jnp.asarray(((lane[:, None] // DF_DH) == (lane[None, :] // DF_DH)).astype(np.float32) / DF_DH, BF16)
    zmap = lambda blk: (lambda i, l: (i, blk))
    row = lambda i, l: (i, 0)
    return pl.pallas_call(
        _qk_prep_kernel,
        grid_spec=pltpu.PrefetchScalarGridSpec(
            num_scalar_prefetch=1,
            grid=(n_rows // tm,),
            in_specs=[
                pl.BlockSpec((tm, DF_W), zmap(ZB_DFQ)),
                pl.BlockSpec((tm, DF_W), zmap(ZB_DFK)),
                pl.BlockSpec((tm, DF_W), zmap(ZB_DFV)),
                pl.BlockSpec((1, 1, DF_W), lambda i, l: (l[0], 0, 0)),
                pl.BlockSpec((1, 1, DF_W), lambda i, l: (l[0], 0, 0)),
                pl.BlockSpec((DF_W, DF_W), lambda i, l: (0, 0)),
            ],
            out_specs=[pl.BlockSpec((tm, DF_W), row)] * 4,
        ),
        out_shape=[
            jax.ShapeDtypeStruct((n_rows, DF_W), BF16),
            jax.ShapeDtypeStruct((n_rows, DF_W), F32),
            jax.ShapeDtypeStruct((n_rows, DF_W), BF16),
            jax.ShapeDtypeStruct((n_rows, DF_W), BF16),
        ],
        compiler_params=_cparams(("parallel",), 32 << 20),
        name="qk_prep",
    )(lidx, z, z, z, qg, kg, grp)


def _group_norm64(x, g, grp):
    x2 = x * x
    hi = x2.astype(BF16)
    lo = (x2 - hi.astype(F32)).astype(BF16)
    ms = _dot(hi, grp) + _dot(lo, grp)
    return x * lax.rsqrt(ms + EPS) * g


BIAS_W = 128


def _qk_prep_t_kernel(l_ref, zq_ref, zk_ref, zv_ref, qg_ref, kg_ref, grp_ref, pk_in, pv_in,
                      qt_ref, ke_ref, vt_ref, pk_ref, pv_ref, *, tm, seq):
    del pk_in, pv_in
    i = pl.program_id(0)
    grp = grp_ref[...]
    qn = _group_norm64(zq_ref[...], qg_ref[0], grp) * (DF_DH ** -0.5)
    kn = _group_norm64(zk_ref[...], kg_ref[0], grp)
    zv = zv_ref[...]
    pos = lax.rem(i * tm, seq) + lax.broadcasted_iota(jnp.int32, (tm, BIAS_W), 0)
    lane = lax.broadcasted_iota(jnp.int32, (tm, BIAS_W), 1)
    hi_part = (pos >> CHUNK_SHIFT).astype(F32) * float(CHUNK)
    lo_part = (pos & (CHUNK - 1)).astype(F32)
    base = jnp.where(lane == 0, hi_part, jnp.where(lane == 1, lo_part, jnp.where(lane == 2, float(CHUNK), 0.0)))
    for h in range(DF_H):
        sl = slice(h * DF_DV, (h + 1) * DF_DV)
        slope = 2.0 ** (-8.0 * (h + 1) / DF_H)
        c0 = h * (DF_DV + BIAS_W)
        ke_ref[:, c0:c0 + DF_DV] = kn[:, sl].astype(BF16)
        ke_ref[:, c0 + DF_DV:c0 + DF_DV + BIAS_W] = (base * slope).astype(BF16)
        qt_ref[h, 0] = qn[:, sl].T.astype(BF16)
        vt_ref[h, 0] = zv[:, sl].T.astype(BF16)
        rows = pl.ds(h, tm, stride=DF_H)
        pk_ref[rows, :] = kn[:, sl]
        pv_ref[rows, :] = zv[:, sl]


def qk_prep_t(z, lidx, qg, kg, pk_buf, pv_buf, *, tm, seq):
    n_rows = z.shape[0]
    nt = n_rows // tm
    smap = lambda i, l: (l[0] * nt + i, 0)
    any_spec = pl.BlockSpec(memory_space=pl.ANY)
    lane = np.arange(DF_W)
    grp = jnp.asarray(((lane[:, None] // DF_DH) == (lane[None, :] // DF_DH)).astype(np.float32) / DF_DH, BF16)
    zmap = lambda blk: (lambda i, l: (i, blk))
    row = lambda i, l: (i, 0)
    tmap = lambda i, l: (0, i, 0, 0)
    kern = functools.partial(_qk_prep_t_kernel, tm=tm, seq=seq)
    return pl.pallas_call(
        kern,
        grid_spec=pltpu.PrefetchScalarGridSpec(
            num_scalar_prefetch=1,
            grid=(nt,),
            in_specs=[
                pl.BlockSpec((tm, DF_W), zmap(ZB_DFQ)),
                pl.BlockSpec((tm, DF_W), zmap(ZB_DFK)),
                pl.BlockSpec((tm, DF_W), zmap(ZB_DFV)),
                pl.BlockSpec((1, 1, DF_W), lambda i, l: (l[0], 0, 0)),
                pl.BlockSpec((1, 1, DF_W), lambda i, l: (l[0], 0, 0)),
                pl.BlockSpec((DF_W, DF_W), lambda i, l: (0, 0)),
                any_spec,
                any_spec,
            ],
            out_specs=[
                pl.BlockSpec((DF_H, 1, DF_DV, tm), tmap),
                pl.BlockSpec((tm, DF_H * (DF_DV + BIAS_W)), row),
                pl.BlockSpec((DF_H, 1, DF_DV, tm), tmap),
                pl.BlockSpec((tm * DF_H, DF_DV), smap),
                pl.BlockSpec((tm * DF_H, DF_DV), smap),
            ],
        ),
        out_shape=[
            jax.ShapeDtypeStruct((DF_H, nt, DF_DV, tm), BF16),
            jax.ShapeDtypeStruct((n_rows, DF_H * (DF_DV + BIAS_W)), BF16),
            jax.ShapeDtypeStruct((DF_H, nt, DF_DV, tm), BF16),
            jax.ShapeDtypeStruct(pk_buf.shape, F32),
            jax.ShapeDtypeStruct(pv_buf.shape, F32),
        ],
        input_output_aliases={7: 3, 8: 4},
        compiler_params=_cparams(("parallel",), 40 << 20),
        name="qk_prep_t",
    )(lidx, z, z, z, qg, kg, grp, pk_buf, pv_buf)


ONES_ROWS = 16


def _flash_t_kernel(l_ref, qt_ref, ke_ref, vt_ref, lam_ref, ogt_ref, y_ref, m_ref, acc_ref, sa_ref, sb_ref,
                    own_ref, *, tq):
    h = pl.program_id(1)
    qi = pl.program_id(2)
    tk = tq
    slope = jnp.exp2(jnp.full((1, 1), -8.0 / DF_H, F32) * (h + 1).astype(F32))
    qt = qt_ref[0, 0]
    row = lax.broadcasted_iota(jnp.int32, (DF_DV, tq), 0)
    zero = jnp.zeros_like(qt)
    q2t = jnp.concatenate([jnp.where(row < DF_DH, qt, zero), jnp.where(row >= DF_DH, qt, zero)], axis=1)
    brow = lax.broadcasted_iota(jnp.int32, (BIAS_W, 2 * tq), 0)
    a0 = ((qi * tq) >> CHUNK_SHIFT).astype(F32)
    extra = jnp.where(brow < 2, 1.0, jnp.where(brow == 2, -a0, 0.0)).astype(BF16)
    q2e = jnp.concatenate([q2t, extra], axis=0)
    ones = jnp.ones((ONES_ROWS, tk), BF16)

    m_ref[...] = jnp.full(m_ref.shape, NEG_BIG, F32)
    acc_ref[...] = jnp.zeros(acc_ref.shape, F32)

    def scores_into(dst_ref, kj):
        r0 = pl.multiple_of(kj * tk, tk)
        dst_ref[...] = _dot(ke_ref[pl.ds(r0, tk), :], q2e)

    def update(s, kj):
        m_old = m_ref[...]
        m_new = jnp.maximum(m_old, jnp.max(s, axis=0, keepdims=True))
        alpha = jnp.exp(m_old - m_new)
        p = jnp.exp(s - m_new).astype(BF16)
        vt_ext = jnp.concatenate([vt_ref[0, kj], ones], axis=0)
        acc_ref[...] = alpha * acc_ref[...] + _dot(vt_ext, p)
        m_ref[...] = m_new

    @pl.when(qi == 0)
    def _():
        rel_k = lax.broadcasted_iota(jnp.int32, (tq, 2 * tq), 0)
        rel_q = lax.broadcasted_iota(jnp.int32, (tq, 2 * tq), 1)
        rel_q = jnp.where(rel_q >= tq, rel_q - tq, rel_q)
        ahead = rel_k - rel_q
        fix = jnp.where(ahead > 0, (-2.0 * slope) * ahead.astype(F32), 0.0)
        visible = (rel_k >> CHUNK_SHIFT) <= (rel_q >> CHUNK_SHIFT)
        own_ref[...] = jnp.where(visible, fix, NEG_BIG)

    def own_block(s):
        update(s + own_ref[...], qi)

    scores_into(sa_ref, 0)

    def block_pair(t, carry):
        k0 = 2 * t
        scores_into(sb_ref, k0 + 1)
        update(sa_ref[...], k0)
        scores_into(sa_ref, k0 + 2)
        update(sb_ref[...], k0 + 1)
        return carry

    lax.fori_loop(0, qi >> 1, block_pair, 0)

    @pl.when((qi & 1) == 1)
    def _():
        scores_into(sb_ref, qi)
        update(sa_ref[...], qi - 1)
        own_block(sb_ref[...])

    @pl.when((qi & 1) == 0)
    def _():
        own_block(sa_ref[...])

    lam_p = lam_ref[0]
    lam_init = 0.8 - 0.6 * jnp.exp(jnp.full((1, 1), -0.3, F32) * l_ref[0].astype(F32))
    lam = (jnp.exp(jnp.sum(lam_p[0:1] * lam_p[1:2], axis=-1, keepdims=True))
           - jnp.exp(jnp.sum(lam_p[2:3] * lam_p[3:4], axis=-1, keepdims=True)) + lam_init)
    acc = acc_ref[...]
    o_all = acc[0:DF_DV] / acc[DF_DV:DF_DV + 1]
    o = o_all[:, 0:tq] - lam * o_all[:, tq:2 * tq]
    ms = jnp.mean(o * o, axis=0, keepdims=True)
    y = o * lax.rsqrt(ms + EPS) * ogt_ref[0] * (1.0 - lam_init)
    y_ref[...] = y.T.astype(BF16)


def diff_flash_t(qt, ke, vt, lidx, lam_p, out_g_t, *, nb, seq, tq):
    assert tq % CHUNK == 0
    nq = seq // tq
    n_rows = nb * seq
    kew = DF_DV + BIAS_W
    kern = functools.partial(_flash_t_kernel, tq=tq)
    vmem = 2 * seq * kew * 2 + 2 * seq * DF_DV * 2 + 10 * tq * 2 * tq * 4 + (8 << 20)
    return pl.pallas_call(
        kern,
        grid_spec=pltpu.PrefetchScalarGridSpec(
            num_scalar_prefetch=1,
            grid=(nb, DF_H, nq),
            in_specs=[
                pl.BlockSpec((1, 1, DF_DV, tq), lambda b, h, i, l: (h, b * nq + i, 0, 0)),
                pl.BlockSpec((seq, kew), lambda b, h, i, l: (b, h)),
                pl.BlockSpec((1, nq, DF_DV, tq), lambda b, h, i, l: (h, b, 0, 0)),
                pl.BlockSpec((1, 4, DF_DH), lambda b, h, i, l: (l[0], 0, 0)),
                pl.BlockSpec((1, DF_DV, 1), lambda b, h, i, l: (l[0], 0, 0)),
            ],
            out_specs=pl.BlockSpec((tq, DF_DV), lambda b, h, i, l: (b * nq + i, h)),
            scratch_shapes=[pltpu.VMEM((1, 2 * tq), F32), pltpu.VMEM((DF_DV + ONES_ROWS, 2 * tq), F32),
                            pltpu.VMEM((tq, 2 * tq), F32), pltpu.VMEM((tq, 2 * tq), F32),
                            pltpu.VMEM((tq, 2 * tq), F32)],
        ),
        out_shape=jax.ShapeDtypeStruct((n_rows, DF_W), BF16),
        compiler_params=_cparams(("parallel", "parallel", "arbitrary"), vmem),
        name="diff_flash_t",
    )(lidx, qt, ke, vt, lam_p, out_g_t)


def _flash_sample_kernel(l_ref, q_ref, km_ref, vm_ref, kx_ref, vx_ref, lam_ref, og_ref, y_ref,
                         m_ref, l_sum_ref, acc_ref, *, tq, tk, past):
    h = pl.program_id(1)
    slope = jnp.exp2(jnp.full((1, 1), -8.0 / DF_H, F32) * (h + 1).astype(F32))
    q = q_ref[...]
    lane = lax.broadcasted_iota(jnp.int32, (tq, DF_DV), 1)
    zero = jnp.zeros_like(q)
    q2 = jnp.concatenate([jnp.where(lane < DF_DH, q, zero), jnp.where(lane >= DF_DH, q, zero)], axis=0)
    q0 = past

    m_ref[...] = jnp.full(m_ref.shape, NEG_BIG, F32)
    l_sum_ref[...] = jnp.zeros(l_sum_ref.shape, F32)
    acc_ref[...] = jnp.zeros(acc_ref.shape, F32)

    def update(s, v):
        m_old = m_ref[...]
        m_new = jnp.maximum(m_old, jnp.max(s, axis=-1, keepdims=True))
        alpha = jnp.exp(m_old - m_new)
        p = jnp.exp(s - m_new)
        l_sum_ref[...] = alpha * l_sum_ref[...] + jnp.sum(p, axis=-1, keepdims=True)
        acc_ref[...] = alpha * acc_ref[...] + _dot(p.astype(BF16), v)
        m_ref[...] = m_new

    def fast_block(kj, carry):
        r0 = kj * tk
        rows = pl.ds(r0 * DF_H + h, tk, stride=DF_H)
        k = km_ref[rows, :].astype(BF16)
        v = vm_ref[rows, :].astype(BF16)
        kpos = r0 + lax.broadcasted_iota(jnp.int32, (1, tk), 1)
        s = _dot_nt(q2, k) + slope * (kpos - q0).astype(F32)
        update(s, v)
        return carry

    lax.fori_loop(0, past // tk, fast_block, 0)

    k = kx_ref[...].astype(BF16)
    v = vx_ref[...].astype(BF16)
    rel_k = lax.broadcasted_iota(jnp.int32, (2 * tq, tq), 1)
    rel_q = lax.broadcasted_iota(jnp.int32, (2 * tq, tq), 0)
    rel_q = jnp.where(rel_q >= tq, rel_q - tq, rel_q)
    dist = jnp.abs(rel_q - rel_k).astype(F32)
    s = _dot_nt(q2, k) + slope * (rel_q.astype(F32) - dist)
    visible = ((q0 + rel_k) >> CHUNK_SHIFT) <= ((q0 + rel_q) >> CHUNK_SHIFT)
    update(jnp.where(visible, s, NEG_BIG), v)

    lam_p = lam_ref[0]
    lam_init = 0.8 - 0.6 * jnp.exp(jnp.full((1, 1), -0.3, F32) * l_ref[0].astype(F32))
    lam = (jnp.exp(jnp.sum(lam_p[0:1] * lam_p[1:2], axis=-1, keepdims=True))
           - jnp.exp(jnp.sum(lam_p[2:3] * lam_p[3:4], axis=-1, keepdims=True)) + lam_init)
    o_all = acc_ref[...] / l_sum_ref[...]
    o = o_all[0:tq] - lam * o_all[tq:2 * tq]
    y_ref[...] = (_rms(o, og_ref[0]) * (1.0 - lam_init)).astype(BF16)


def diff_flash_sample(qn, k_past, v_past, k_new, v_new, lidx, lam_p, out_g, *, nb, seq, past, tk):
    n_rows = nb * seq
    kern = functools.partial(_flash_sample_kernel, tq=seq, tk=tk, past=past)
    vmem = 4 * past * DF_H * DF_DV * 4 + (8 << 20)
    qmap = lambda b, h, l: (b, h)
    cmap = lambda b, h, l: (l[0] * nb + b, 0)
    return pl.pallas_call(
        kern,
        grid_spec=pltpu.PrefetchScalarGridSpec(
            num_scalar_prefetch=1,
            grid=(nb, DF_H),
            in_specs=[
                pl.BlockSpec((seq, DF_DV), qmap),
                pl.BlockSpec((past * DF_H, DF_DV), cmap),
                pl.BlockSpec((past * DF_H, DF_DV), cmap),
                pl.BlockSpec((seq, DF_DV), qmap),
                pl.BlockSpec((seq, DF_DV), qmap),
                pl.BlockSpec((1, 4, DF_DH), lambda b, h, l: (l[0], 0, 0)),
                pl.BlockSpec((1, 1, DF_DV), lambda b, h, l: (l[0], 0, 0)),
            ],
            out_specs=pl.BlockSpec((seq, DF_DV), qmap),
            scratch_shapes=[pltpu.VMEM((2 * seq, 1), F32), pltpu.VMEM((2 * seq, 1), F32),
                            pltpu.VMEM((2 * seq, DF_DV), F32)],
        ),
        out_shape=jax.ShapeDtypeStruct((n_rows, DF_W), BF16),
        compiler_params=_cparams(("parallel", "arbitrary"), vmem),
        name="diff_flash_sample",
    )(lidx, qn, k_past, v_past, k_new, v_new, lam_p, out_g)


def _trunk_layer(x, lidx, w, st, cfg):
    nb, seq, past = cfg["nb"], cfg["seq"], cfg["past"]
    z, gates = in_proj(x, lidx, w["norm_mix"], w["w_main"], w["w_gate"], tm=cfg["tm_in"], tn=cfg["tn_in"])
    y_ml, c_new, n_new, m_new = mlstm_mixer(
        z, gates, lidx, st["hist0"], w["conv_w"], w["conv_b"], w["b_if"], w["ml_og"],
        st["c0"], st["n0"], st["m0"], nb=nb, seq=seq, tc=cfg["tc"], lc=cfg["lc_ml"])
    z3 = z.reshape(nb, seq, Z_W)
    new = {}
    if cfg["prompt"]:
        qt, ke, vt, new["pk_buf"], new["pv_buf"] = qk_prep_t(
            z, lidx, w["df_qg"], w["df_kg"], st["pk_buf"], st["pv_buf"], tm=cfg["tq"], seq=seq)
        y_df = diff_flash_t(qt, ke, vt, lidx, w["df_lam"], w["df_og_t"], nb=nb, seq=seq, tq=cfg["tq"])
    else:
        qn, kn, kn_b, v_b = qk_prep(z, lidx, w["df_qg"], w["df_kg"], tm=cfg["tm_prep"])
        y_df = diff_flash_sample(qn, st["past_k"], st["past_v"], kn_b, v_b, lidx, w["df_lam"], w["df_og"],
                                 nb=nb, seq=seq, past=past, tk=cfg["tk_past"])
        new["attn_k"] = kn.reshape(nb, seq, DF_H, 2 * DF_DH)
        new["attn_v"] = z3[:, :, ZB_DFV * DF_W:(ZB_DFV + 1) * DF_W].reshape(nb, seq, DF_H, DF_DV)
    y_hg, s_new = hgrn_mixer(z, lidx, w["hg_lbl"], w["hg_og"], st["s0"], nb=nb, seq=seq, tc=cfg["tc"],
                             lc=cfg["lc_hg"])
    x = post_mix(x, y_ml, y_df, y_hg, lidx, w["w_out"], w["norm_cross"], w["wq"], w["ca_qg"],
                 st["mk"], st["mv"], w["wo"], tm=cfg["tm_post"], rows_per_batch=seq)
    x = ffn(x, lidx, w["norm_ffn"], w["w_gu"], w["w_down"], w["final_norm"], tm=cfg["tm_ffn"], th=cfg["th"],
            x_buffers=cfg["ffn_x_buffers"])
    new.update({
        "conv": z3[:, seq - (CONV_W - 1):, 0:2 * ML_W],
        "C": c_new,
        "n": n_new[:, :ML_H, :],
        "m": m_new[:, 0, :ML_H],
        "S": s_new,
    })
    return x, new


PROMPT_CFG = dict(prompt=True, tm_in=1024, tn_in=1536, tc=256, lc_ml=64, lc_hg=64, tm_prep=512, tq=512,
                  tm_post=512, tm_ffn=1024, th=512, ffn_x_buffers=1)
SAMPLE_CFG = dict(prompt=False, tm_in=128, tn_in=768, tc=16, lc_ml=16, lc_hg=16, tm_prep=128, tk_past=512,
                  tm_post=16, tm_ffn=128, th=512, ffn_x_buffers=2)


def kernel(x_prompt, x_sample, mem_prompt, cache_attn_k, cache_attn_v, cache_mem_k, cache_mem_v, state_mlstm_conv, state_mlstm_C, state_mlstm_n, state_mlstm_m, state_hgrn_S, norm_mix, w_in, mlstm_conv_w, mlstm_conv_b, mlstm_b_i, mlstm_b_f, mlstm_out_norm, diff_q_norm, diff_k_norm, diff_lambda, diff_out_norm, hgrn_lb_logits, hgrn_out_norm, w_out, norm_cross, norm_mem, cross_wq, cross_wk, cross_wv, cross_q_norm, cross_k_norm, cross_wo, norm_ffn, ffn_w_gate_up, ffn_w_down, final_norm):
    bp, tp = x_prompt.shape[:2]
    bs, ts = x_sample.shape[:2]
    past = cache_attn_k.shape[2]
    depth = w_in.shape[0]
    assert depth == DEPTH and x_prompt.shape[2] == D_MODEL

    g_off = 4 * ML_W
    r3 = lambda a: a.reshape(depth, 1, a.shape[-1])
    w = {
        "norm_mix": r3(norm_mix),
        "w_main": jnp.concatenate([w_in[:, :, :g_off], w_in[:, :, g_off + 2 * ML_H:]], axis=-1).astype(BF16),
        "w_gate": jnp.pad(w_in[:, :, g_off:g_off + 2 * ML_H], ((0, 0), (0, 0), (0, GATE_W - 2 * ML_H))).astype(BF16),
        "conv_w": mlstm_conv_w,
        "conv_b": r3(mlstm_conv_b),
        "b_if": r3(jnp.pad(jnp.concatenate([mlstm_b_i, mlstm_b_f], axis=-1), ((0, 0), (0, GATE_W - 2 * ML_H)))),
        "ml_og": r3(mlstm_out_norm),
        "df_qg": r3(jnp.tile(diff_q_norm, (1, DF_W // DF_DH))),
        "df_kg": r3(jnp.tile(diff_k_norm, (1, DF_W // DF_DH))),
        "df_lam": diff_lambda,
        "df_og": r3(diff_out_norm),
        "df_og_t": diff_out_norm.reshape(depth, DF_DV, 1),
        "hg_lbl": hgrn_lb_logits,
        "hg_og": r3(hgrn_out_norm),
        "w_out": w_out.astype(BF16),
        "norm_cross": r3(norm_cross),
        "wq": cross_wq.astype(BF16),
        "ca_qg": r3(cross_q_norm),
        "wo": cross_wo.astype(BF16),
        "norm_ffn": r3(norm_ffn),
        "w_gu": ffn_w_gate_up.astype(BF16),
        "w_down": ffn_w_down.astype(BF16),
        "final_norm": final_norm.reshape(1, D_MODEL),
    }

    mk_p, mv_p = mem_kv(mem_prompt, r3(norm_mem), cross_wk.astype(BF16), cross_wv.astype(BF16), r3(cross_k_norm))

    def pad_hist(conv):
        pad = [(0, 0)] * (conv.ndim - 2) + [(HIST - (CONV_W - 1), 0), (0, 0)]
        return jnp.pad(conv, pad)

    st_p = {
        "hist0": jnp.zeros((bp, HIST, 2 * ML_W), F32),
        "c0": jnp.zeros((bp, ML_H, ML_DH, ML_DH), F32),
        "n0": jnp.zeros((bp, 8, ML_DH), F32),
        "m0": jnp.zeros((bp, 1, GATE_W), F32),
        "s0": jnp.zeros((bp, HG_H, HG_DK, HG_DV), F32),
        "mk": mk_p, "mv": mv_p,
    }
    hist_s = pad_hist(state_mlstm_conv)
    n_s = jnp.pad(state_mlstm_n, ((0, 0), (0, 0), (0, 8 - ML_H), (0, 0)))
    m_s = jnp.pad(state_mlstm_m, ((0, 0), (0, 0), (0, GATE_W - ML_H))).reshape(depth, bs, 1, GATE_W)
    past_k = cache_attn_k.reshape(depth * bs * past * DF_H, DF_DV)
    past_v = cache_attn_v.reshape(depth * bs * past * DF_H, DF_DV)
    mk_s = cache_mem_k.reshape(depth * bs, N_MEM, CA_W)
    mv_s = cache_mem_v.reshape(depth * bs, N_MEM, CA_W)

    cfg_p = dict(PROMPT_CFG, nb=bp, seq=tp, past=0)
    cfg_s = dict(SAMPLE_CFG, nb=bs, seq=ts, past=past)

    def layer(carry, xs):
        xp, xsm, pk_buf, pv_buf = carry
        l, hist_l, c_l, n_l, m_l, s_l = xs
        lidx = l.reshape(1).astype(jnp.int32)
        xp, new_p = _trunk_layer(xp, lidx, w, dict(st_p, pk_buf=pk_buf, pv_buf=pv_buf), cfg_p)
        pk_buf, pv_buf = new_p.pop("pk_buf"), new_p.pop("pv_buf")
        st_s = {"hist0": hist_l, "c0": c_l, "n0": n_l, "m0": m_l, "s0": s_l,
                "past_k": past_k, "past_v": past_v, "mk": mk_s, "mv": mv_s}
        xsm, new_s = _trunk_layer(xsm, lidx, w, st_s, cfg_s)
        return (xp, xsm, pk_buf, pv_buf), (new_p, new_s)

    xs = (jnp.arange(depth, dtype=jnp.int32), hist_s, state_mlstm_C, n_s, m_s, state_hgrn_S)
    kv_rows = depth * bp * tp * DF_H
    init = (x_prompt.reshape(bp * tp, D_MODEL), x_sample.reshape(bs * ts, D_MODEL),
            jnp.zeros((kv_rows, DF_DV), F32), jnp.zeros((kv_rows, DF_DV), F32))
    (xp, xsm, pk_buf, pv_buf), (new_p, new_s) = lax.scan(layer, init, xs)

    y_prompt = xp.reshape(bp, tp, D_MODEL)
    y_sample = xsm.reshape(bs, ts, D_MODEL)
    p_mem_k = mk_p.reshape(depth, bp, N_MEM, CA_H, CA_DH)
    p_mem_v = mv_p.reshape(depth, bp, N_MEM, CA_H, CA_DH)
    p_attn_k = pk_buf.reshape(depth, bp, tp, DF_H, 2 * DF_DH)
    p_attn_v = pv_buf.reshape(depth, bp, tp, DF_H, DF_DV)
    return (y_prompt, y_sample,
            p_attn_k, p_attn_v, p_mem_k, p_mem_v, new_p["conv"], new_p["C"], new_p["n"],
            new_p["m"], new_p["S"],
            new_s["attn_k"], new_s["attn_v"], new_s["conv"], new_s["C"], new_s["n"], new_s["m"], new_s["S"])
```

```python
import functools
import math

import numpy as np
import jax
import jax.numpy as jnp
from jax import lax
from jax.experimental import pallas as pl
from jax.experimental.pallas import tpu as pltpu

F32 = jnp.float32
BF16 = jnp.bfloat16

D_MODEL = 2048
DEPTH = 4
CHUNK = 64
CHUNK_SHIFT = 6
ML_DH = 128
ML_W = 768
ML_H = 6
CONV_W = 4
DF_DH = 64
DF_DV = 128
DF_W = 512
DF_H = 4
HG_DK = 128
HG_DV = 128
HG_W = 768
HG_H = 6
CA_H = 4
CA_DH = 128
CA_W = 512
N_MEM = 256
FF_HIDDEN = 5632
EPS = 1e-6
NEG_BIG = -1e30
LB_FLOOR = 1e-30
LOG2E = math.log2(math.e)

Z_W = 2 * ML_W + 2 * ML_W + 3 * DF_W + 4 * HG_W
GATE_W = 128
ZB_MLQ, ZB_MLK, ZB_MLV, ZB_MLO = 0, 1, 2, 3
ZB_HGQ, ZB_HGF, ZB_HGI, ZB_HGG = 6, 7, 8, 9
ZB_DFQ, ZB_DFK, ZB_DFV = 6, 7, 8

V7X_VMEM_BYTES = 64 * 1024 * 1024
V7X_VMEM_CAP = 58 * 1024 * 1024
VMEM_SLACK = 6 * 1024 * 1024
HIST = 8

NT_DIMS = (((1,), (1,)), ((), ()))
TN_DIMS = (((0,), (0,)), ((), ()))


def _vmem_limit(nbytes):
    return int(min(V7X_VMEM_CAP, max(32 * 1024 * 1024, nbytes + VMEM_SLACK)))


def _cparams(sem, vmem_bytes):
    return pltpu.CompilerParams(dimension_semantics=sem, vmem_limit_bytes=_vmem_limit(vmem_bytes))


def _rms(x, g):
    ms = jnp.mean(x * x, axis=-1, keepdims=True)
    return x * lax.rsqrt(ms + EPS) * g


def _dot(a, b):
    return jnp.dot(a, b, preferred_element_type=F32)


def _dot_nt(a, b):
    return lax.dot_general(a, b, NT_DIMS, preferred_element_type=F32)


def _dot_tn(a, b):
    return lax.dot_general(a, b, TN_DIMS, preferred_element_type=F32)


def _log_sigmoid(x):
    return jnp.minimum(x, 0.0) - jnp.log1p(jnp.exp(-jnp.abs(x)))


def _split3(x):
    hi = x.astype(BF16)
    r1 = x - hi.astype(F32)
    mid = r1.astype(BF16)
    lo = (r1 - mid.astype(F32)).astype(BF16)
    return hi, mid, lo


def _dot_exact01(m01, x):
    hi, mid, lo = _split3(x)
    return _dot(m01, hi) + _dot(m01, mid) + _dot(m01, lo)


def _in_proj_kernel(l_ref, x_ref, g_ref, w_ref, wg_ref, z_ref, gate_ref, xn_ref):
    j = pl.program_id(1)

    @pl.when(j == 0)
    def _():
        xn = _rms(x_ref[...], g_ref[0]).astype(BF16)
        xn_ref[...] = xn
        gate_ref[...] = _dot(xn, wg_ref[0])

    z_ref[...] = _dot(xn_ref[...], w_ref[0])


def in_proj(x, lidx, g, w_main, w_gate, *, tm, tn):
    n_rows = x.shape[0]
    grid = (n_rows // tm, Z_W // tn)
    vmem = 2 * tm * D_MODEL * 4 + 2 * D_MODEL * tn * 2 + 2 * tm * tn * 4 + tm * D_MODEL * 2 \
        + 2 * tm * GATE_W * 4 + 2 * D_MODEL * GATE_W * 2 + (4 << 20)
    return pl.pallas_call(
        _in_proj_kernel,
        grid_spec=pltpu.PrefetchScalarGridSpec(
            num_scalar_prefetch=1,
            grid=grid,
            in_specs=[
                pl.BlockSpec((tm, D_MODEL), lambda i, j, l: (i, 0)),
                pl.BlockSpec((1, 1, D_MODEL), lambda i, j, l: (l[0], 0, 0)),
                pl.BlockSpec((1, D_MODEL, tn), lambda i, j, l: (l[0], 0, j)),
                pl.BlockSpec((1, D_MODEL, GATE_W), lambda i, j, l: (l[0], 0, 0)),
            ],
            out_specs=[
                pl.BlockSpec((tm, tn), lambda i, j, l: (i, j)),
                pl.BlockSpec((tm, GATE_W), lambda i, j, l: (i, 0)),
            ],
            scratch_shapes=[pltpu.VMEM((tm, D_MODEL), BF16)],
        ),
        out_shape=[jax.ShapeDtypeStruct((n_rows, Z_W), F32),
                   jax.ShapeDtypeStruct((n_rows, GATE_W), F32)],
        compiler_params=_cparams(("parallel", "arbitrary"), vmem),
        name="in_proj",
    )(lidx, x, g, w_main, w_gate)


def _ffn_kernel(l_ref, x_ref, g_ref, wg_ref, wu_ref, wd_ref, fg_ref, o_ref, xn_ref):
    j = pl.program_id(1)
    nj = pl.num_programs(1)

    @pl.when(j == 0)
    def _():
        x = x_ref[...]
        xn_ref[...] = _rms(x, g_ref[0]).astype(BF16)
        o_ref[...] = x

    xn = xn_ref[...]
    gt = _dot(xn, wg_ref[0])
    up = _dot(xn, wu_ref[0])
    act = (gt * jax.nn.sigmoid(gt) * up).astype(BF16)
    o_ref[...] += _dot(act, wd_ref[0])

    @pl.when(jnp.logical_and(j == nj - 1, l_ref[0] == DEPTH - 1))
    def _():
        o_ref[...] = _rms(o_ref[...], fg_ref[...])


def ffn(x, lidx, g, w_gu, w_down, final_g, *, tm, th, x_buffers=2):
    n_rows = x.shape[0]
    nh = FF_HIDDEN // th
    grid = (n_rows // tm, nh)
    vmem = (2 + x_buffers) * tm * D_MODEL * 4 + tm * D_MODEL * 2 + 6 * D_MODEL * th * 2 + 3 * tm * th * 4 \
        + tm * D_MODEL * 4 + (4 << 20)
    x_mode = {} if x_buffers == 2 else {"pipeline_mode": pl.Buffered(x_buffers)}
    return pl.pallas_call(
        _ffn_kernel,
        grid_spec=pltpu.PrefetchScalarGridSpec(
            num_scalar_prefetch=1,
            grid=grid,
            in_specs=[
                pl.BlockSpec((tm, D_MODEL), lambda i, j, l: (i, 0), **x_mode),
                pl.BlockSpec((1, 1, D_MODEL), lambda i, j, l: (l[0], 0, 0)),
                pl.BlockSpec((1, D_MODEL, th), lambda i, j, l: (l[0], 0, j)),
                pl.BlockSpec((1, D_MODEL, th), lambda i, j, l: (l[0], 0, j + nh)),
                pl.BlockSpec((1, th, D_MODEL), lambda i, j, l: (l[0], j, 0)),
                pl.BlockSpec((1, D_MODEL), lambda i, j, l: (0, 0)),
            ],
            out_specs=pl.BlockSpec((tm, D_MODEL), lambda i, j, l: (i, 0)),
            scratch_shapes=[pltpu.VMEM((tm, D_MODEL), BF16)],
        ),
        out_shape=jax.ShapeDtypeStruct((n_rows, D_MODEL), F32),
        compiler_params=_cparams(("parallel", "arbitrary"), vmem),
        name="ffn",
    )(lidx, x, g, w_gu, w_gu, w_down, final_g)


def _mem_kv_kernel(mem_ref, g_ref, wk_ref, wv_ref, kg_ref, k_ref, v_ref):
    mn = _rms(mem_ref[0], g_ref[0]).astype(BF16)
    k = _dot(mn, wk_ref[0])
    for h in range(CA_H):
        sl = slice(h * CA_DH, (h + 1) * CA_DH)
        k_ref[0, :, sl] = _rms(k[:, sl], kg_ref[0])
    v_ref[0] = _dot(mn, wv_ref[0])


def mem_kv(mem, g, wk, wv, kg):
    nb = mem.shape[0]
    out = jax.ShapeDtypeStruct((DEPTH * nb, N_MEM, CA_W), F32)
    return pl.pallas_call(
        _mem_kv_kernel,
        grid=(DEPTH, nb),
        in_specs=[
            pl.BlockSpec((1, N_MEM, D_MODEL), lambda l, b: (b, 0, 0)),
            pl.BlockSpec((1, 1, D_MODEL), lambda l, b: (l, 0, 0)),
            pl.BlockSpec((1, D_MODEL, CA_W), lambda l, b: (l, 0, 0)),
            pl.BlockSpec((1, D_MODEL, CA_W), lambda l, b: (l, 0, 0)),
            pl.BlockSpec((1, 1, CA_DH), lambda l, b: (l, 0, 0)),
        ],
        out_specs=[pl.BlockSpec((1, N_MEM, CA_W), lambda l, b: (l * nb + b, 0, 0)),
                   pl.BlockSpec((1, N_MEM, CA_W), lambda l, b: (l * nb + b, 0, 0))],
        out_shape=[out, out],
        compiler_params=_cparams(("arbitrary", "arbitrary"), 24 << 20),
        name="mem_kv",
    )(mem, g, wk, wv, kg)


def _post_kernel(l_ref, x_ref, yml_ref, ydf_ref, yhg_ref, wout_ref, gx_ref, wq_ref, qg_ref,
                 mk_ref, mv_ref, wo_ref, o_ref):
    x1 = x_ref[...]
    x1 = x1 + _dot(yml_ref[...], wout_ref[0, 0:ML_W, :])
    x1 = x1 + _dot(ydf_ref[...], wout_ref[0, ML_W:ML_W + DF_W, :])
    x1 = x1 + _dot(yhg_ref[...], wout_ref[0, ML_W + DF_W:D_MODEL, :])
    hn = _rms(x1, gx_ref[0]).astype(BF16)
    q = _dot(hn, wq_ref[0])
    heads = []
    for h in range(CA_H):
        sl = slice(h * CA_DH, (h + 1) * CA_DH)
        qh = _rms(q[:, sl], qg_ref[0]).astype(BF16)
        s = _dot_nt(qh, mk_ref[0, :, sl].astype(BF16)) * (CA_DH ** -0.5)
        s = s - jnp.max(s, axis=-1, keepdims=True)
        p = jnp.exp(s)
        p = p / jnp.sum(p, axis=-1, keepdims=True)
        heads.append(_dot(p.astype(BF16), mv_ref[0, :, sl].astype(BF16)))
    o = jnp.concatenate(heads, axis=-1).astype(BF16)
    o_ref[...] = x1 + _dot(o, wo_ref[0])


def post_mix(x, y_ml, y_df, y_hg, lidx, w_out, gx, wq, qg, mk, mv, wo, *, tm, rows_per_batch):
    n_rows = x.shape[0]
    nb = n_rows // rows_per_batch
    tiles_per_batch = rows_per_batch // tm
    grid = (n_rows // tm,)
    w_bytes = (D_MODEL * D_MODEL + 2 * D_MODEL * CA_W) * 2
    vmem = 2 * w_bytes + 6 * tm * D_MODEL * 4 + 2 * tm * D_MODEL * 2 + 4 * N_MEM * CA_W * 4 + (6 << 20)
    row = lambda i, l: (i, 0)
    mem_map = lambda i, l: (l[0] * nb + i // tiles_per_batch, 0, 0)
    return pl.pallas_call(
        _post_kernel,
        grid_spec=pltpu.PrefetchScalarGridSpec(
            num_scalar_prefetch=1,
            grid=grid,
            in_specs=[
                pl.BlockSpec((tm, D_MODEL), row),
                pl.BlockSpec((tm, ML_W), row),
                pl.BlockSpec((tm, DF_W), row),
                pl.BlockSpec((tm, HG_W), row),
                pl.BlockSpec((1, D_MODEL, D_MODEL), lambda i, l: (l[0], 0, 0)),
                pl.BlockSpec((1, 1, D_MODEL), lambda i, l: (l[0], 0, 0)),
                pl.BlockSpec((1, D_MODEL, CA_W), lambda i, l: (l[0], 0, 0)),
                pl.BlockSpec((1, 1, CA_DH), lambda i, l: (l[0], 0, 0)),
                pl.BlockSpec((1, N_MEM, CA_W), mem_map),
                pl.BlockSpec((1, N_MEM, CA_W), mem_map),
                pl.BlockSpec((1, CA_W, D_MODEL), lambda i, l: (l[0], 0, 0)),
            ],
            out_specs=pl.BlockSpec((tm, D_MODEL), row),
        ),
        out_shape=jax.ShapeDtypeStruct((n_rows, D_MODEL), F32),
        compiler_params=_cparams(("parallel",), vmem),
        name="post_mix",
    )(lidx, x, y_ml, y_df, y_hg, w_out, gx, wq, qg, mk, mv, wo)


def _mlstm_kernel(l_ref, zq_ref, zk_ref, zv_ref, zo_ref, gate_ref, hist_ref, cw_ref, cb_ref, bif_ref,
                  og_ref, c0_ref, n0_ref, m0_ref, tri_ref,
                  y_ref, c_ref, n_ref, m_ref, ext_ref, *, tc, lc):
    c = pl.program_id(1)

    @pl.when(c == 0)
    def _():
        c_ref[...] = c0_ref[...]
        n_ref[...] = n0_ref[...]
        m_ref[...] = m0_ref[...]
        ext_ref[0:HIST, :] = hist_ref[0]

    ext_ref[HIST:HIST + tc, 0:ML_W] = zq_ref[...]
    ext_ref[HIST:HIST + tc, ML_W:2 * ML_W] = zk_ref[...]
    acc = cb_ref[0]
    for j in range(CONV_W):
        acc = acc + cw_ref[0, j:j + 1, :] * ext_ref[pl.ds(HIST - (CONV_W - 1) + j, tc), :]
    tail = ext_ref[tc:tc + HIST, :]
    ext_ref[0:HIST, :] = tail
    qk = acc * jax.nn.sigmoid(acc)

    gz = gate_ref[...] + bif_ref[0]
    lf_all = _log_sigmoid(gz)
    tri = tri_ref[...]
    row_i = lax.broadcasted_iota(jnp.int32, (lc, lc), 0)
    col_i = lax.broadcasted_iota(jnp.int32, (lc, lc), 1)
    causal = row_i >= col_i
    diag = row_i == col_i
    og = og_ref[0]

    for ci in range(tc // lc):
        r0 = ci * lc
        ig_c = gz[r0:r0 + lc, :]
        b_c = _dot_exact01(tri, lf_all[r0:r0 + lc, :])
        for h in range(ML_H):
            sl = slice(h * ML_DH, (h + 1) * ML_DH)
            q = qk[r0:r0 + lc, h * ML_DH:(h + 1) * ML_DH]
            k = qk[r0:r0 + lc, ML_W + h * ML_DH:ML_W + (h + 1) * ML_DH] * (ML_DH ** -0.5)
            v = zv_ref[r0:r0 + lc, sl]
            qb = q.astype(BF16)
            b_col = b_c[:, ML_H + h:ML_H + h + 1]
            ig_col = ig_c[:, h:h + 1]
            r_row = jnp.sum(jnp.where(diag, ig_col - b_col, 0.0), axis=0, keepdims=True)
            m_prev = m_ref[0, :, h:h + 1]
            log_d = jnp.where(causal, b_col + r_row, NEG_BIG)
            inter = b_col + m_prev
            m_t = jnp.maximum(inter, jnp.max(log_d, axis=-1, keepdims=True))
            d_mat = jnp.exp(log_d - m_t)
            w_inter = jnp.exp(inter - m_t)
            s = _dot_nt(qb, k.astype(BF16)) * d_mat
            c_old = c_ref[0, h]
            n_old = n_ref[0, h:h + 1, :]
            num = _dot(s.astype(BF16), v.astype(BF16)) + w_inter * _dot(qb, c_old.astype(BF16))
            den = jnp.sum(s, axis=-1, keepdims=True) + w_inter * jnp.sum(q * n_old, axis=-1, keepdims=True)
            hh = num / jnp.maximum(jnp.abs(den), jnp.exp(-m_t))
            m_new = m_t[lc - 1:lc, :]
            b_last = b_col[lc - 1:lc, :]
            w_s = jnp.exp(b_last - b_col + ig_col - m_new)
            decay = jnp.exp(b_last + m_prev - m_new)
            kw = k * w_s
            c_ref[0, h] = decay * c_old + _dot_tn(kw.astype(BF16), v.astype(BF16))
            n_ref[0, h:h + 1, :] = decay * n_old + jnp.sum(kw, axis=0, keepdims=True)
            m_ref[0, :, h:h + 1] = m_new
            o_gate = jax.nn.sigmoid(zo_ref[r0:r0 + lc, sl])
            y_ref[r0:r0 + lc, sl] = (_rms(hh, og) * o_gate).astype(BF16)


def mlstm_mixer(z, gates, lidx, hist0, conv_w, conv_b, b_if, out_g, c0, n0, m0, *, nb, seq, tc, lc):
    n_rows = nb * seq
    nc = seq // tc
    tri = jnp.asarray(np.tril(np.ones((lc, lc), np.float32)), BF16)
    zmap = lambda blk: (lambda b, c, l: (b * nc + c, blk))
    lmap3 = lambda b, c, l: (l[0], 0, 0)
    vmem = 2 * 5 * tc * ML_W * 4 + (HIST + tc) * 2 * ML_W * 4 + 4 * ML_H * ML_DH * ML_DH * 4 \
        + 6 * tc * 2 * ML_W * 4 + (8 << 20)
    kern = functools.partial(_mlstm_kernel, tc=tc, lc=lc)
    return pl.pallas_call(
        kern,
        grid_spec=pltpu.PrefetchScalarGridSpec(
            num_scalar_prefetch=1,
            grid=(nb, nc),
            in_specs=[
                pl.BlockSpec((tc, ML_W), zmap(ZB_MLQ)),
                pl.BlockSpec((tc, ML_W), zmap(ZB_MLK)),
                pl.BlockSpec((tc, ML_W), zmap(ZB_MLV)),
                pl.BlockSpec((tc, ML_W), zmap(ZB_MLO)),
                pl.BlockSpec((tc, GATE_W), lambda b, c, l: (b * nc + c, 0)),
                pl.BlockSpec((1, HIST, 2 * ML_W), lambda b, c, l: (b, 0, 0)),
                pl.BlockSpec((1, CONV_W, 2 * ML_W), lmap3),
                pl.BlockSpec((1, 1, 2 * ML_W), lmap3),
                pl.BlockSpec((1, 1, GATE_W), lmap3),
                pl.BlockSpec((1, 1, ML_DH), lmap3),
                pl.BlockSpec((1, ML_H, ML_DH, ML_DH), lambda b, c, l: (b, 0, 0, 0)),
                pl.BlockSpec((1, 8, ML_DH), lambda b, c, l: (b, 0, 0)),
                pl.BlockSpec((1, 1, GATE_W), lambda b, c, l: (b, 0, 0)),
                pl.BlockSpec((lc, lc), lambda b, c, l: (0, 0)),
            ],
            out_specs=[
                pl.BlockSpec((tc, ML_W), lambda b, c, l: (b * nc + c, 0)),
                pl.BlockSpec((1, ML_H, ML_DH, ML_DH), lambda b, c, l: (b, 0, 0, 0)),
                pl.BlockSpec((1, 8, ML_DH), lambda b, c, l: (b, 0, 0)),
                pl.BlockSpec((1, 1, GATE_W), lambda b, c, l: (b, 0, 0)),
            ],
            scratch_shapes=[pltpu.VMEM((HIST + tc, 2 * ML_W), F32)],
        ),
        out_shape=[
            jax.ShapeDtypeStruct((n_rows, ML_W), BF16),
            jax.ShapeDtypeStruct((nb, ML_H, ML_DH, ML_DH), F32),
            jax.ShapeDtypeStruct((nb, 8, ML_DH), F32),
            jax.ShapeDtypeStruct((nb, 1, GATE_W), F32),
        ],
        compiler_params=_cparams(("parallel", "arbitrary"), vmem),
        name="mlstm",
    )(lidx, z, z, z, z, gates, hist0, conv_w, conv_b, b_if, out_g, c0, n0, m0, tri)


def _hgrn_level_masks(lc):
    t = np.arange(lc)[:, None]
    s = np.arange(lc)[None, :]
    masks = []
    h = lc // 2
    while h >= 1:
        odd = (t % (2 * h)) >= h
        same = (t // (2 * h)) == (s // (2 * h))
        masks.append((same & odd & ((s % (2 * h)) < h)).astype(np.float32))
        h //= 2
    masks.append((t == s).astype(np.float32))
    return np.stack(masks, axis=0)


def _hgrn_boundary_rows(a, h):
    lc, width = a.shape
    if h >= 8:
        parts = [jnp.broadcast_to(a[g * 2 * h + h - 1:g * 2 * h + h, :], (2 * h, width)) for g in range(lc // (2 * h))]
        return parts[0] if len(parts) == 1 else jnp.concatenate(parts, axis=0)
    a3 = a.reshape(lc // 8, 8, width)
    sub = lax.broadcasted_iota(jnp.int32, a3.shape, 1)
    out = None
    for g in range(8 // (2 * h)):
        src = g * 2 * h + h - 1
        b = jnp.broadcast_to(a3[:, src:src + 1, :], a3.shape)
        out = b if out is None else jnp.where(sub >= g * 2 * h, b, out)
    return out.reshape(lc, width)


def _hgrn_kernel(l_ref, zq_ref, zf_ref, zi_ref, zg_ref, lbl_ref, og_ref, s0_ref, tri_ref, msk_ref,
                 y_ref, s_ref, st_ref, *, tc, lc, nlev):
    c = pl.program_id(1)
    nc = pl.num_programs(1)

    @pl.when(c == 0)
    def _():
        for h in range(HG_H):
            st_ref[h] = s0_ref[0, h].T

    logits = lbl_ref[...]
    e = jnp.exp(logits - jnp.max(logits, axis=0, keepdims=True))
    p = e / jnp.sum(e, axis=0, keepdims=True)
    drow = lax.broadcasted_iota(jnp.int32, p.shape, 0)
    lb = jnp.sum(jnp.where(jnp.logical_and(drow >= 1, drow <= l_ref[0]), p, 0.0), axis=0, keepdims=True)
    log_lb = jnp.log(jnp.maximum(lb, LB_FLOOR))
    log_1m = jnp.log1p(-lb)

    zf = zf_ref[...]
    a2 = log_1m + _log_sigmoid(zf)
    hi = jnp.maximum(log_lb, a2)
    lf = hi + jnp.log1p(jnp.exp(-jnp.abs(log_lb - a2)))
    lf2 = lf * LOG2E
    kk = (1.0 - lb) * jax.nn.sigmoid(-zf) + (lb - jnp.maximum(lb, LB_FLOOR))
    zq = zq_ref[...]
    qq = zq * jax.nn.sigmoid(zq) * (HG_DK ** -0.5)
    tri = tri_ref[...]
    og = og_ref[0]

    for ci in range(tc // lc):
        r0 = ci * lc
        a_all = _dot_exact01(tri, lf2[r0:r0 + lc, :])
        lev_exp = [-jnp.abs(a_all - _hgrn_boundary_rows(a_all, lc >> (lv + 1))) for lv in range(nlev)]
        for h in range(HG_H):
            sl = slice(h * HG_DK, (h + 1) * HG_DK)
            q = qq[r0:r0 + lc, sl]
            k = kk[r0:r0 + lc, sl]
            iv = zi_ref[r0:r0 + lc, sl].astype(BF16)
            a_in = a_all[:, sl]
            a_end = a_in[lc - 1:lc, :]
            st = st_ref[h]
            o = _dot_nt((q * jnp.exp2(a_in)).astype(BF16), st.astype(BF16))
            att = jnp.where(msk_ref[nlev] > 0, _dot_nt(q.astype(BF16), k.astype(BF16)), 0.0)
            for lv in range(nlev):
                xf = jnp.exp2(lev_exp[lv][:, sl])
                pm = _dot_nt((q * xf).astype(BF16), (k * xf).astype(BF16))
                att = att + jnp.where(msk_ref[lv] > 0, pm, 0.0)
            o = o + _dot(att.astype(BF16), iv)
            k_end = (k * jnp.exp2(a_end - a_in)).astype(BF16)
            st_ref[h] = st * jnp.exp2(a_end) + _dot_tn(iv, k_end)
            gsl = zg_ref[r0:r0 + lc, sl]
            y_ref[r0:r0 + lc, sl] = (_rms(o, og) * (gsl * jax.nn.sigmoid(gsl))).astype(BF16)

    @pl.when(c == nc - 1)
    def _():
        for h in range(HG_H):
            s_ref[0, h] = st_ref[h].T


def hgrn_mixer(z, lidx, lb_logits, out_g, s0, *, nb, seq, tc, lc):
    n_rows = nb * seq
    nc = seq // tc
    msk_np = _hgrn_level_masks(lc)
    nlev = msk_np.shape[0] - 1
    tri = jnp.asarray(np.tril(np.ones((lc, lc), np.float32)), BF16)
    msk = jnp.asarray(msk_np, F32)
    zmap = lambda blk: (lambda b, c, l: (b * nc + c, blk))
    vmem = 2 * 5 * tc * HG_W * 4 + 5 * HG_H * HG_DK * HG_DV * 4 + 8 * tc * HG_W * 4 + (8 << 20)
    kern = functools.partial(_hgrn_kernel, tc=tc, lc=lc, nlev=nlev)
    return pl.pallas_call(
        kern,
        grid_spec=pltpu.PrefetchScalarGridSpec(
            num_scalar_prefetch=1,
            grid=(nb, nc),
            in_specs=[
                pl.BlockSpec((tc, HG_W), zmap(ZB_HGQ)),
                pl.BlockSpec((tc, HG_W), zmap(ZB_HGF)),
                pl.BlockSpec((tc, HG_W), zmap(ZB_HGI)),
                pl.BlockSpec((tc, HG_W), zmap(ZB_HGG)),
                pl.BlockSpec((DEPTH, HG_W), lambda b, c, l: (0, 0)),
                pl.BlockSpec((1, 1, HG_DV), lambda b, c, l: (l[0], 0, 0)),
                pl.BlockSpec((1, HG_H, HG_DK, HG_DV), lambda b, c, l: (b, 0, 0, 0)),
                pl.BlockSpec((lc, lc), lambda b, c, l: (0, 0)),
                pl.BlockSpec(msk_np.shape, lambda b, c, l: (0, 0, 0)),
            ],
            out_specs=[
                pl.BlockSpec((tc, HG_W), lambda b, c, l: (b * nc + c, 0)),
                pl.BlockSpec((1, HG_H, HG_DK, HG_DV), lambda b, c, l: (b, 0, 0, 0)),
            ],
            scratch_shapes=[pltpu.VMEM((HG_H, HG_DV, HG_DK), F32)],
        ),
        out_shape=[
            jax.ShapeDtypeStruct((n_rows, HG_W), BF16),
            jax.ShapeDtypeStruct((nb, HG_H, HG_DK, HG_DV), F32),
        ],
        compiler_params=_cparams(("parallel", "arbitrary"), vmem),
        name="hgrn",
    )(lidx, z, z, z, z, lb_logits, out_g, s0, tri, msk)


def _qk_prep_kernel(l_ref, zq_ref, zk_ref, zv_ref, qg_ref, kg_ref, grp_ref, qn_ref, kn_ref, knb_ref, vb_ref):
    grp = grp_ref[...]
    qn_ref[...] = (_group_norm64(zq_ref[...], qg_ref[0], grp) * (DF_DH ** -0.5)).astype(BF16)
    kn = _group_norm64(zk_ref[...], kg_ref[0], grp)
    kn_ref[...] = kn
    knb_ref[...] = kn.astype(BF16)
    vb_ref[...] = zv_ref[...].astype(BF16)


def qk_prep(z, lidx, qg, kg, *, tm):
    n_rows = z.shape[0]
    lane = np.arange(DF_W)
    grp = jnp.asarray(((lane[:, None] // DF_DH) == (lane[None, :] // DF_DH)).astype(np.float32) / DF_DH, BF16)
    zmap = lambda blk: (lambda i, l: (i, blk))
    row = lambda i, l: (i, 0)
    return pl.pallas_call(
        _qk_prep_kernel,
        grid_spec=pltpu.PrefetchScalarGridSpec(
            num_scalar_prefetch=1,
            grid=(n_rows // tm,),
            in_specs=[
                pl.BlockSpec((tm, DF_W), zmap(ZB_DFQ)),
                pl.BlockSpec((tm, DF_W), zmap(ZB_DFK)),
                pl.BlockSpec((tm, DF_W), zmap(ZB_DFV)),
                pl.BlockSpec((1, 1, DF_W), lambda i, l: (l[0], 0, 0)),
                pl.BlockSpec((1, 1, DF_W), lambda i, l: (l[0], 0, 0)),
                pl.BlockSpec((DF_W, DF_W), lambda i, l: (0, 0)),
            ],
            out_specs=[pl.BlockSpec((tm, DF_W), row)] * 4,
        ),
        out_shape=[
            jax.ShapeDtypeStruct((n_rows, DF_W), BF16),
            jax.ShapeDtypeStruct((n_rows, DF_W), F32),
            jax.ShapeDtypeStruct((n_rows, DF_W), BF16),
            jax.ShapeDtypeStruct((n_rows, DF_W), BF16),
        ],
        compiler_params=_cparams(("parallel",), 32 << 20),
        name="qk_prep",
    )(lidx, z, z, z, qg, kg, grp)


def _group_norm64(x, g, grp):
    x2 = x * x
    hi = x2.astype(BF16)
    lo = (x2 - hi.astype(F32)).astype(BF16)
    ms = _dot(hi, grp) + _dot(lo, grp)
    return x * lax.rsqrt(ms + EPS) * g


BIAS_W = 128


def _qk_prep_t_kernel(l_ref, zq_ref, zk_ref, zv_ref, qg_ref, kg_ref, grp_ref, pk_in, pv_in,
                      qt_ref, ke_ref, vt_ref, pk_ref, pv_ref, *, tm, seq):
    del pk_in, pv_in
    i = pl.program_id(0)
    grp = grp_ref[...]
    qn = _group_norm64(zq_ref[...], qg_ref[0], grp) * (DF_DH ** -0.5)
    kn = _group_norm64(zk_ref[...], kg_ref[0], grp)
    zv = zv_ref[...]
    pos = lax.rem(i * tm, seq) + lax.broadcasted_iota(jnp.int32, (tm, BIAS_W), 0)
    lane = lax.broadcasted_iota(jnp.int32, (tm, BIAS_W), 1)
    hi_part = (pos >> CHUNK_SHIFT).astype(F32) * float(CHUNK)
    lo_part = (pos & (CHUNK - 1)).astype(F32)
    base = jnp.where(lane == 0, hi_part, jnp.where(lane == 1, lo_part, jnp.where(lane == 2, float(CHUNK), 0.0)))
    for h in range(DF_H):
        sl = slice(h * DF_DV, (h + 1) * DF_DV)
        slope = 2.0 ** (-8.0 * (h + 1) / DF_H)
        c0 = h * (DF_DV + BIAS_W)
        ke_ref[:, c0:c0 + DF_DV] = kn[:, sl].astype(BF16)
        ke_ref[:, c0 + DF_DV:c0 + DF_DV + BIAS_W] = (base * slope).astype(BF16)
        qt_ref[h, 0] = qn[:, sl].T.astype(BF16)
        vt_ref[h, 0] = zv[:, sl].T.astype(BF16)
        rows = pl.ds(h, tm, stride=DF_H)
        pk_ref[rows, :] = kn[:, sl]
        pv_ref[rows, :] = zv[:, sl]


def qk_prep_t(z, lidx, qg, kg, pk_buf, pv_buf, *, tm, seq):
    n_rows = z.shape[0]
    nt = n_rows // tm
    smap = lambda i, l: (l[0] * nt + i, 0)
    any_spec = pl.BlockSpec(memory_space=pl.ANY)
    lane = np.arange(DF_W)
    grp = jnp.asarray(((lane[:, None] // DF_DH) == (lane[None, :] // DF_DH)).astype(np.float32) / DF_DH, BF16)
    zmap = lambda blk: (lambda i, l: (i, blk))
    row = lambda i, l: (i, 0)
    tmap = lambda i, l: (0, i, 0, 0)
    kern = functools.partial(_qk_prep_t_kernel, tm=tm, seq=seq)
    return pl.pallas_call(
        kern,
        grid_spec=pltpu.PrefetchScalarGridSpec(
            num_scalar_prefetch=1,
            grid=(nt,),
            in_specs=[
                pl.BlockSpec((tm, DF_W), zmap(ZB_DFQ)),
                pl.BlockSpec((tm, DF_W), zmap(ZB_DFK)),
                pl.BlockSpec((tm, DF_W), zmap(ZB_DFV)),
                pl.BlockSpec((1, 1, DF_W), lambda i, l: (l[0], 0, 0)),
                pl.BlockSpec((1, 1, DF_W), lambda i, l: (l[0], 0, 0)),
                pl.BlockSpec((DF_W, DF_W), lambda i, l: (0, 0)),
                any_spec,
                any_spec,
            ],
            out_specs=[
                pl.BlockSpec((DF_H, 1, DF_DV, tm), tmap),
                pl.BlockSpec((tm, DF_H * (DF_DV + BIAS_W)), row),
                pl.BlockSpec((DF_H, 1, DF_DV, tm), tmap),
                pl.BlockSpec((tm * DF_H, DF_DV), smap),
                pl.BlockSpec((tm * DF_H, DF_DV), smap),
            ],
        ),
        out_shape=[
            jax.ShapeDtypeStruct((DF_H, nt, DF_DV, tm), BF16),
            jax.ShapeDtypeStruct((n_rows, DF_H * (DF_DV + BIAS_W)), BF16),
            jax.ShapeDtypeStruct((DF_H, nt, DF_DV, tm), BF16),
            jax.ShapeDtypeStruct(pk_buf.shape, F32),
            jax.ShapeDtypeStruct(pv_buf.shape, F32),
        ],
        input_output_aliases={7: 3, 8: 4},
        compiler_params=_cparams(("parallel",), 40 << 20),
        name="qk_prep_t",
    )(lidx, z, z, z, qg, kg, grp, pk_buf, pv_buf)


ONES_ROWS = 16


def _flash_t_kernel(l_ref, qt_ref, ke_ref, vt_ref, lam_ref, ogt_ref, y_ref, m_ref, acc_ref, sa_ref, sb_ref,
                    own_ref, *, tq):
    h = pl.program_id(1)
    qi = pl.program_id(2)
    tk = tq
    slope = jnp.exp2(jnp.full((1, 1), -8.0 / DF_H, F32) * (h + 1).astype(F32))
    qt = qt_ref[0, 0]
    row = lax.broadcasted_iota(jnp.int32, (DF_DV, tq), 0)
    zero = jnp.zeros_like(qt)
    q2t = jnp.concatenate([jnp.where(row < DF_DH, qt, zero), jnp.where(row >= DF_DH, qt, zero)], axis=1)
    brow = lax.broadcasted_iota(jnp.int32, (BIAS_W, 2 * tq), 0)
    a0 = ((qi * tq) >> CHUNK_SHIFT).astype(F32)
    extra = jnp.where(brow < 2, 1.0, jnp.where(brow == 2, -a0, 0.0)).astype(BF16)
    q2e = jnp.concatenate([q2t, extra], axis=0)
    ones = jnp.ones((ONES_ROWS, tk), BF16)

    m_ref[...] = jnp.full(m_ref.shape, NEG_BIG, F32)
    acc_ref[...] = jnp.zeros(acc_ref.shape, F32)

    def scores_into(dst_ref, kj):
        r0 = pl.multiple_of(kj * tk, tk)
        dst_ref[...] = _dot(ke_ref[pl.ds(r0, tk), :], q2e)

    def update(s, kj):
        m_old = m_ref[...]
        m_new = jnp.maximum(m_old, jnp.max(s, axis=0, keepdims=True))
        alpha = jnp.exp(m_old - m_new)
        p = jnp.exp(s - m_new).astype(BF16)
        vt_ext = jnp.concatenate([vt_ref[0, kj], ones], axis=0)
        acc_ref[...] = alpha * acc_ref[...] + _dot(vt_ext, p)
        m_ref[...] = m_new

    @pl.when(qi == 0)
    def _():
        rel_k = lax.broadcasted_iota(jnp.int32, (tq, 2 * tq), 0)
        rel_q = lax.broadcasted_iota(jnp.int32, (tq, 2 * tq), 1)
        rel_q = jnp.where(rel_q >= tq, rel_q - tq, rel_q)
        ahead = rel_k - rel_q
        fix = jnp.where(ahead > 0, (-2.0 * slope) * ahead.astype(F32), 0.0)
        visible = (rel_k >> CHUNK_SHIFT) <= (rel_q >> CHUNK_SHIFT)
        own_ref[...] = jnp.where(visible, fix, NEG_BIG)

    def own_block(s):
        update(s + own_ref[...], qi)

    scores_into(sa_ref, 0)

    def block_pair(t, carry):
        k0 = 2 * t
        scores_into(sb_ref, k0 + 1)
        update(sa_ref[...], k0)
        scores_into(sa_ref, k0 + 2)
        update(sb_ref[...], k0 + 1)
        return carry

    lax.fori_loop(0, qi >> 1, block_pair, 0)

    @pl.when((qi & 1) == 1)
    def _():
        scores_into(sb_ref, qi)
        update(sa_ref[...], qi - 1)
        own_block(sb_ref[...])

    @pl.when((qi & 1) == 0)
    def _():
        own_block(sa_ref[...])

    lam_p = lam_ref[0]
    lam_init = 0.8 - 0.6 * jnp.exp(jnp.full((1, 1), -0.3, F32) * l_ref[0].astype(F32))
    lam = (jnp.exp(jnp.sum(lam_p[0:1] * lam_p[1:2], axis=-1, keepdims=True))
           - jnp.exp(jnp.sum(lam_p[2:3] * lam_p[3:4], axis=-1, keepdims=True)) + lam_init)
    acc = acc_ref[...]
    o_all = acc[0:DF_DV] / acc[DF_DV:DF_DV + 1]
    o = o_all[:, 0:tq] - lam * o_all[:, tq:2 * tq]
    ms = jnp.mean(o * o, axis=0, keepdims=True)
    y = o * lax.rsqrt(ms + EPS) * ogt_ref[0] * (1.0 - lam_init)
    y_ref[...] = y.T.astype(BF16)


def diff_flash_t(qt, ke, vt, lidx, lam_p, out_g_t, *, nb, seq, tq):
    assert tq % CHUNK == 0
    nq = seq // tq
    n_rows = nb * seq
    kew = DF_DV + BIAS_W
    kern = functools.partial(_flash_t_kernel, tq=tq)
    vmem = 2 * seq * kew * 2 + 2 * seq * DF_DV * 2 + 10 * tq * 2 * tq * 4 + (8 << 20)
    return pl.pallas_call(
        kern,
        grid_spec=pltpu.PrefetchScalarGridSpec(
            num_scalar_prefetch=1,
            grid=(nb, DF_H, nq),
            in_specs=[
                pl.BlockSpec((1, 1, DF_DV, tq), lambda b, h, i, l: (h, b * nq + i, 0, 0)),
                pl.BlockSpec((seq, kew), lambda b, h, i, l: (b, h)),
                pl.BlockSpec((1, nq, DF_DV, tq), lambda b, h, i, l: (h, b, 0, 0)),
                pl.BlockSpec((1, 4, DF_DH), lambda b, h, i, l: (l[0], 0, 0)),
                pl.BlockSpec((1, DF_DV, 1), lambda b, h, i, l: (l[0], 0, 0)),
            ],
            out_specs=pl.BlockSpec((tq, DF_DV), lambda b, h, i, l: (b * nq + i, h)),
            scratch_shapes=[pltpu.VMEM((1, 2 * tq), F32), pltpu.VMEM((DF_DV + ONES_ROWS, 2 * tq), F32),
                            pltpu.VMEM((tq, 2 * tq), F32), pltpu.VMEM((tq, 2 * tq), F32),
                            pltpu.VMEM((tq, 2 * tq), F32)],
        ),
        out_shape=jax.ShapeDtypeStruct((n_rows, DF_W), BF16),
        compiler_params=_cparams(("parallel", "parallel", "arbitrary"), vmem),
        name="diff_flash_t",
    )(lidx, qt, ke, vt, lam_p, out_g_t)


def _flash_sample_kernel(l_ref, q_ref, km_ref, vm_ref, kx_ref, vx_ref, lam_ref, og_ref, y_ref,
                         m_ref, l_sum_ref, acc_ref, *, tq, tk, past):
    h = pl.program_id(1)
    slope = jnp.exp2(jnp.full((1, 1), -8.0 / DF_H, F32) * (h + 1).astype(F32))
    q = q_ref[...]
    lane = lax.broadcasted_iota(jnp.int32, (tq, DF_DV), 1)
    zero = jnp.zeros_like(q)
    q2 = jnp.concatenate([jnp.where(lane < DF_DH, q, zero), jnp.where(lane >= DF_DH, q, zero)], axis=0)
    q0 = past

    m_ref[...] = jnp.full(m_ref.shape, NEG_BIG, F32)
    l_sum_ref[...] = jnp.zeros(l_sum_ref.shape, F32)
    acc_ref[...] = jnp.zeros(acc_ref.shape, F32)

    def update(s, v):
        m_old = m_ref[...]
        m_new = jnp.maximum(m_old, jnp.max(s, axis=-1, keepdims=True))
        alpha = jnp.exp(m_old - m_new)
        p = jnp.exp(s - m_new)
        l_sum_ref[...] = alpha * l_sum_ref[...] + jnp.sum(p, axis=-1, keepdims=True)
        acc_ref[...] = alpha * acc_ref[...] + _dot(p.astype(BF16), v)
        m_ref[...] = m_new

    def fast_block(kj, carry):
        r0 = kj * tk
        rows = pl.ds(r0 * DF_H + h, tk, stride=DF_H)
        k = km_ref[rows, :].astype(BF16)
        v = vm_ref[rows, :].astype(BF16)
        kpos = r0 + lax.broadcasted_iota(jnp.int32, (1, tk), 1)
        s = _dot_nt(q2, k) + slope * (kpos - q0).astype(F32)
        update(s, v)
        return carry

    lax.fori_loop(0, past // tk, fast_block, 0)

    k = kx_ref[...].astype(BF16)
    v = vx_ref[...].astype(BF16)
    rel_k = lax.broadcasted_iota(jnp.int32, (2 * tq, tq), 1)
    rel_q = lax.broadcasted_iota(jnp.int32, (2 * tq, tq), 0)
    rel_q = jnp.where(rel_q >= tq, rel_q - tq, rel_q)
    dist = jnp.abs(rel_q - rel_k).astype(F32)
    s = _dot_nt(q2, k) + slope * (rel_q.astype(F32) - dist)
    visible = ((q0 + rel_k) >> CHUNK_SHIFT) <= ((q0 + rel_q) >> CHUNK_SHIFT)
    update(jnp.where(visible, s, NEG_BIG), v)

    lam_p = lam_ref[0]
    lam_init = 0.8 - 0.6 * jnp.exp(jnp.full((1, 1), -0.3, F32) * l_ref[0].astype(F32))
    lam = (jnp.exp(jnp.sum(lam_p[0:1] * lam_p[1:2], axis=-1, keepdims=True))
           - jnp.exp(jnp.sum(lam_p[2:3] * lam_p[3:4], axis=-1, keepdims=True)) + lam_init)
    o_all = acc_ref[...] / l_sum_ref[...]
    o = o_all[0:tq] - lam * o_all[tq:2 * tq]
    y_ref[...] = (_rms(o, og_ref[0]) * (1.0 - lam_init)).astype(BF16)


def diff_flash_sample(qn, k_past, v_past, k_new, v_new, lidx, lam_p, out_g, *, nb, seq, past, tk):
    n_rows = nb * seq
    kern = functools.partial(_flash_sample_kernel, tq=seq, tk=tk, past=past)
    vmem = 4 * past * DF_H * DF_DV * 4 + (8 << 20)
    qmap = lambda b, h, l: (b, h)
    cmap = lambda b, h, l: (l[0] * nb + b, 0)
    return pl.pallas_call(
        kern,
        grid_spec=pltpu.PrefetchScalarGridSpec(
            num_scalar_prefetch=1,
            grid=(nb, DF_H),
            in_specs=[
                pl.BlockSpec((seq, DF_DV), qmap),
                pl.BlockSpec((past * DF_H, DF_DV), cmap),
                pl.BlockSpec((past * DF_H, DF_DV), cmap),
                pl.BlockSpec((seq, DF_DV), qmap),
                pl.BlockSpec((seq, DF_DV), qmap),
                pl.BlockSpec((1, 4, DF_DH), lambda b, h, l: (l[0], 0, 0)),
                pl.BlockSpec((1, 1, DF_DV), lambda b, h, l: (l[0], 0, 0)),
            ],
            out_specs=pl.BlockSpec((seq, DF_DV), qmap),
            scratch_shapes=[pltpu.VMEM((2 * seq, 1), F32), pltpu.VMEM((2 * seq, 1), F32),
                            pltpu.VMEM((2 * seq, DF_DV), F32)],
        ),
        out_shape=jax.ShapeDtypeStruct((n_rows, DF_W), BF16),
        compiler_params=_cparams(("parallel", "arbitrary"), vmem),
        name="diff_flash_sample",
    )(lidx, qn, k_past, v_past, k_new, v_new, lam_p, out_g)


def _trunk_layer(x, lidx, w, st, cfg):
    nb, seq, past = cfg["nb"], cfg["seq"], cfg["past"]
    z, gates = in_proj(x, lidx, w["norm_mix"], w["w_main"], w["w_gate"], tm=cfg["tm_in"], tn=cfg["tn_in"])
    y_ml, c_new, n_new, m_new = mlstm_mixer(
        z, gates, lidx, st["hist0"], w["conv_w"], w["conv_b"], w["b_if"], w["ml_og"],
        st["c0"], st["n0"], st["m0"], nb=nb, seq=seq, tc=cfg["tc"], lc=cfg["lc_ml"])
    z3 = z.reshape(nb, seq, Z_W)
    new = {}
    if cfg["prompt"]:
        qt, ke, vt, new["pk_buf"], new["pv_buf"] = qk_prep_t(
            z, lidx, w["df_qg"], w["df_kg"], st["pk_buf"], st["pv_buf"], tm=cfg["tq"], seq=seq)
        y_df = diff_flash_t(qt, ke, vt, lidx, w["df_lam"], w["df_og_t"], nb=nb, seq=seq, tq=cfg["tq"])
    else:
        qn, kn, kn_b, v_b = qk_prep(z, lidx, w["df_qg"], w["df_kg"], tm=cfg["tm_prep"])
        y_df = diff_flash_sample(qn, st["past_k"], st["past_v"], kn_b, v_b, lidx, w["df_lam"], w["df_og"],
                                 nb=nb, seq=seq, past=past, tk=cfg["tk_past"])
        new["attn_k"] = kn.reshape(nb, seq, DF_H, 2 * DF_DH)
        new["attn_v"] = z3[:, :, ZB_DFV * DF_W:(ZB_DFV + 1) * DF_W].reshape(nb, seq, DF_H, DF_DV)
    y_hg, s_new = hgrn_mixer(z, lidx, w["hg_lbl"], w["hg_og"], st["s0"], nb=nb, seq=seq, tc=cfg["tc"],
                             lc=cfg["lc_hg"])
    x = post_mix(x, y_ml, y_df, y_hg, lidx, w["w_out"], w["norm_cross"], w["wq"], w["ca_qg"],
                 st["mk"], st["mv"], w["wo"], tm=cfg["tm_post"], rows_per_batch=seq)
    x = ffn(x, lidx, w["norm_ffn"], w["w_gu"], w["w_down"], w["final_norm"], tm=cfg["tm_ffn"], th=cfg["th"],
            x_buffers=cfg["ffn_x_buffers"])
    new.update({
        "conv": z3[:, seq - (CONV_W - 1):, 0:2 * ML_W],
        "C": c_new,
        "n": n_new[:, :ML_H, :],
        "m": m_new[:, 0, :ML_H],
        "S": s_new,
    })
    return x, new


PROMPT_CFG = dict(prompt=True, tm_in=1024, tn_in=1536, tc=256, lc_ml=64, lc_hg=64, tm_prep=512, tq=512,
                  tm_post=512, tm_ffn=1024, th=512, ffn_x_buffers=2)
SAMPLE_CFG = dict(prompt=False, tm_in=128, tn_in=768, tc=16, lc_ml=16, lc_hg=16, tm_prep=128, tk_past=2048,
                  tm_post=16, tm_ffn=128, th=512, ffn_x_buffers=2)


def kernel(x_prompt, x_sample, mem_prompt, cache_attn_k, cache_attn_v, cache_mem_k, cache_mem_v, state_mlstm_conv, state_mlstm_C, state_mlstm_n, state_mlstm_m, state_hgrn_S, norm_mix, w_in, mlstm_conv_w, mlstm_conv_b, mlstm_b_i, mlstm_b_f, mlstm_out_norm, diff_q_norm, diff_k_norm, diff_lambda, diff_out_norm, hgrn_lb_logits, hgrn_out_norm, w_out, norm_cross, norm_mem, cross_wq, cross_wk, cross_wv, cross_q_norm, cross_k_norm, cross_wo, norm_ffn, ffn_w_gate_up, ffn_w_down, final_norm):
    bp, tp = x_prompt.shape[:2]
    bs, ts = x_sample.shape[:2]
    past = cache_attn_k.shape[2]
    depth = w_in.shape[0]
    assert depth == DEPTH and x_prompt.shape[2] == D_MODEL

    g_off = 4 * ML_W
    r3 = lambda a: a.reshape(depth, 1, a.shape[-1])
    w = {
        "norm_mix": r3(norm_mix),
        "w_main": jnp.concatenate([w_in[:, :, :g_off], w_in[:, :, g_off + 2 * ML_H:]], axis=-1).astype(BF16),
        "w_gate": jnp.pad(w_in[:, :, g_off:g_off + 2 * ML_H], ((0, 0), (0, 0), (0, GATE_W - 2 * ML_H))).astype(BF16),
        "conv_w": mlstm_conv_w,
        "conv_b": r3(mlstm_conv_b),
        "b_if": r3(jnp.pad(jnp.concatenate([mlstm_b_i, mlstm_b_f], axis=-1), ((0, 0), (0, GATE_W - 2 * ML_H)))),
        "ml_og": r3(mlstm_out_norm),
        "df_qg": r3(jnp.tile(diff_q_norm, (1, DF_W // DF_DH))),
        "df_kg": r3(jnp.tile(diff_k_norm, (1, DF_W // DF_DH))),
        "df_lam": diff_lambda,
        "df_og": r3(diff_out_norm),
        "df_og_t": diff_out_norm.reshape(depth, DF_DV, 1),
        "hg_lbl": hgrn_lb_logits,
        "hg_og": r3(hgrn_out_norm),
        "w_out": w_out.astype(BF16),
        "norm_cross": r3(norm_cross),
        "wq": cross_wq.astype(BF16),
        "ca_qg": r3(cross_q_norm),
        "wo": cross_wo.astype(BF16),
        "norm_ffn": r3(norm_ffn),
        "w_gu": ffn_w_gate_up.astype(BF16),
        "w_down": ffn_w_down.astype(BF16),
        "final_norm": final_norm.reshape(1, D_MODEL),
    }

    mk_p, mv_p = mem_kv(mem_prompt, r3(norm_mem), cross_wk.astype(BF16), cross_wv.astype(BF16), r3(cross_k_norm))

    def pad_hist(conv):
        pad = [(0, 0)] * (conv.ndim - 2) + [(HIST - (CONV_W - 1), 0), (0, 0)]
        return jnp.pad(conv, pad)

    st_p = {
        "hist0": jnp.zeros((bp, HIST, 2 * ML_W), F32),
        "c0": jnp.zeros((bp, ML_H, ML_DH, ML_DH), F32),
        "n0": jnp.zeros((bp, 8, ML_DH), F32),
        "m0": jnp.zeros((bp, 1, GATE_W), F32),
        "s0": jnp.zeros((bp, HG_H, HG_DK, HG_DV), F32),
        "mk": mk_p, "mv": mv_p,
    }
    hist_s = pad_hist(state_mlstm_conv)
    n_s = jnp.pad(state_mlstm_n, ((0, 0), (0, 0), (0, 8 - ML_H), (0, 0)))
    m_s = jnp.pad(state_mlstm_m, ((0, 0), (0, 0), (0, GATE_W - ML_H))).reshape(depth, bs, 1, GATE_W)
    past_k = cache_attn_k.reshape(depth * bs * past * DF_H, DF_DV)
    past_v = cache_attn_v.reshape(depth * bs * past * DF_H, DF_DV)
    mk_s = cache_mem_k.reshape(depth * bs, N_MEM, CA_W)
    mv_s = cache_mem_v.reshape(depth * bs, N_MEM, CA_W)

    cfg_p = dict(PROMPT_CFG, nb=bp, seq=tp, past=0)
    cfg_s = dict(SAMPLE_CFG, nb=bs, seq=ts, past=past)

    def layer(carry, xs):
        xp, xsm, pk_buf, pv_buf = carry
        l, hist_l, c_l, n_l, m_l, s_l = xs
        lidx = l.reshape(1).astype(jnp.int32)
        xp, new_p = _trunk_layer(xp, lidx, w, dict(st_p, pk_buf=pk_buf, pv_buf=pv_buf), cfg_p)
        pk_buf, pv_buf = new_p.pop("pk_buf"), new_p.pop("pv_buf")
        st_s = {"hist0": hist_l, "c0": c_l, "n0": n_l, "m0": m_l, "s0": s_l,
                "past_k": past_k, "past_v": past_v, "mk": mk_s, "mv": mv_s}
        xsm, new_s = _trunk_layer(xsm, lidx, w, st_s, cfg_s)
        return (xp, xsm, pk_buf, pv_buf), (new_p, new_s)

    xs = (jnp.arange(depth, dtype=jnp.int32), hist_s, state_mlstm_C, n_s, m_s, state_hgrn_S)
    kv_rows = depth * bp * tp * DF_H
    init = (x_prompt.reshape(bp * tp, D_MODEL), x_sample.reshape(bs * ts, D_MODEL),
            lax.empty((kv_rows, DF_DV), F32), lax.empty((kv_rows, DF_DV), F32))
    (xp, xsm, pk_buf, pv_buf), (new_p, new_s) = lax.scan(layer, init, xs)

    y_prompt = xp.reshape(bp, tp, D_MODEL)
    y_sample = xsm.reshape(bs, ts, D_MODEL)
    p_mem_k = mk_p.reshape(depth, bp, N_MEM, CA_H, CA_DH)
    p_mem_v = mv_p.reshape(depth, bp, N_MEM, CA_H, CA_DH)
    p_attn_k = pk_buf.reshape(depth, bp, tp, DF_H, 2 * DF_DH)
    p_attn_v = pv_buf.reshape(depth, bp, tp, DF_H, DF_DV)
    return (y_prompt, y_sample,
            p_attn_k, p_attn_v, p_mem_k, p_mem_v, new_p["conv"], new_p["C"], new_p["n"],
            new_p["m"], new_p["S"],
            new_s["attn_k"], new_s["attn_v"], new_s["conv"], new_s["C"], new_s["n"], new_s["m"], new_s["S"])
```

```python
import functools
import math

import numpy as np
import jax
import jax.numpy as jnp
from jax import lax
from jax.experimental import pallas as pl
from jax.experimental.pallas import tpu as pltpu

F32 = jnp.float32
BF16 = jnp.bfloat16

D_MODEL = 2048
DEPTH = 4
CHUNK = 64
CHUNK_SHIFT = 6
ML_DH = 128
ML_W = 768
ML_H = 6
CONV_W = 4
DF_DH = 64
DF_DV = 128
DF_W = 512
DF_H = 4
HG_DK = 128
HG_DV = 128
HG_W = 768
HG_H = 6
CA_H = 4
CA_DH = 128
CA_W = 512
N_MEM = 256
FF_HIDDEN = 5632
EPS = 1e-6
NEG_BIG = -1e30
LB_FLOOR = 1e-30
LOG2E = math.log2(math.e)

Z_W = 2 * ML_W + 2 * ML_W + 3 * DF_W + 4 * HG_W
GATE_W = 128
ZB_MLQ, ZB_MLK, ZB_MLV, ZB_MLO = 0, 1, 2, 3
ZB_HGQ, ZB_HGF, ZB_HGI, ZB_HGG = 6, 7, 8, 9
ZB_DFQ, ZB_DFK, ZB_DFV = 6, 7, 8

V7X_VMEM_BYTES = 64 * 1024 * 1024
V7X_VMEM_CAP = 58 * 1024 * 1024
VMEM_SLACK = 6 * 1024 * 1024
HIST = 8

NT_DIMS = (((1,), (1,)), ((), ()))
TN_DIMS = (((0,), (0,)), ((), ()))


def _vmem_limit(nbytes):
    return int(min(V7X_VMEM_CAP, max(32 * 1024 * 1024, nbytes + VMEM_SLACK)))


def _cparams(sem, vmem_bytes):
    return pltpu.CompilerParams(dimension_semantics=sem, vmem_limit_bytes=_vmem_limit(vmem_bytes))


def _rms(x, g):
    ms = jnp.mean(x * x, axis=-1, keepdims=True)
    return x * lax.rsqrt(ms + EPS) * g


def _dot(a, b):
    return jnp.dot(a, b, preferred_element_type=F32)


def _dot_nt(a, b):
    return lax.dot_general(a, b, NT_DIMS, preferred_element_type=F32)


def _dot_tn(a, b):
    return lax.dot_general(a, b, TN_DIMS, preferred_element_type=F32)


def _log_sigmoid(x):
    return jnp.minimum(x, 0.0) - jnp.log1p(jnp.exp(-jnp.abs(x)))


def _split3(x):
    hi = x.astype(BF16)
    r1 = x - hi.astype(F32)
    mid = r1.astype(BF16)
    lo = (r1 - mid.astype(F32)).astype(BF16)
    return hi, mid, lo


def _dot_exact01(m01, x):
    hi, mid, lo = _split3(x)
    return _dot(m01, hi) + _dot(m01, mid) + _dot(m01, lo)


def _in_proj_kernel(l_ref, x_ref, g_ref, w_ref, wg_ref, z_ref, gate_ref, xn_ref):
    j = pl.program_id(1)

    @pl.when(j == 0)
    def _():
        xn = _rms(x_ref[...], g_ref[0]).astype(BF16)
        xn_ref[...] = xn
        gate_ref[...] = _dot(xn, wg_ref[0])

    z_ref[...] = _dot(xn_ref[...], w_ref[0])


def in_proj(x, lidx, g, w_main, w_gate, *, tm, tn):
    n_rows = x.shape[0]
    grid = (n_rows // tm, Z_W // tn)
    vmem = 2 * tm * D_MODEL * 4 + 2 * D_MODEL * tn * 2 + 2 * tm * tn * 4 + tm * D_MODEL * 2 \
        + 2 * tm * GATE_W * 4 + 2 * D_MODEL * GATE_W * 2 + (4 << 20)
    return pl.pallas_call(
        _in_proj_kernel,
        grid_spec=pltpu.PrefetchScalarGridSpec(
            num_scalar_prefetch=1,
            grid=grid,
            in_specs=[
                pl.BlockSpec((tm, D_MODEL), lambda i, j, l: (i, 0)),
                pl.BlockSpec((1, 1, D_MODEL), lambda i, j, l: (l[0], 0, 0)),
                pl.BlockSpec((1, D_MODEL, tn), lambda i, j, l: (l[0], 0, j)),
                pl.BlockSpec((1, D_MODEL, GATE_W), lambda i, j, l: (l[0], 0, 0)),
            ],
            out_specs=[
                pl.BlockSpec((tm, tn), lambda i, j, l: (i, j)),
                pl.BlockSpec((tm, GATE_W), lambda i, j, l: (i, 0)),
            ],
            scratch_shapes=[pltpu.VMEM((tm, D_MODEL), BF16)],
        ),
        out_shape=[jax.ShapeDtypeStruct((n_rows, Z_W), F32),
                   jax.ShapeDtypeStruct((n_rows, GATE_W), F32)],
        compiler_params=_cparams(("parallel", "arbitrary"), vmem),
        name="in_proj",
    )(lidx, x, g, w_main, w_gate)


def _ffn_kernel(l_ref, x_ref, g_ref, wg_ref, wu_ref, wd_ref, fg_ref, o_ref, xn_ref):
    j = pl.program_id(1)
    nj = pl.num_programs(1)

    @pl.when(j == 0)
    def _():
        x = x_ref[...]
        xn_ref[...] = _rms(x, g_ref[0]).astype(BF16)
        o_ref[...] = x

    xn = xn_ref[...]
    gt = _dot(xn, wg_ref[0])
    up = _dot(xn, wu_ref[0])
    act = (gt * jax.nn.sigmoid(gt) * up).astype(BF16)
    o_ref[...] += _dot(act, wd_ref[0])

    @pl.when(jnp.logical_and(j == nj - 1, l_ref[0] == DEPTH - 1))
    def _():
        o_ref[...] = _rms(o_ref[...], fg_ref[...])


def ffn(x, lidx, g, w_gu, w_down, final_g, *, tm, th, x_buffers=2):
    n_rows = x.shape[0]
    nh = FF_HIDDEN // th
    grid = (n_rows // tm, nh)
    vmem = (2 + x_buffers) * tm * D_MODEL * 4 + tm * D_MODEL * 2 + 6 * D_MODEL * th * 2 + 3 * tm * th * 4 \
        + tm * D_MODEL * 4 + (4 << 20)
    x_mode = {} if x_buffers == 2 else {"pipeline_mode": pl.Buffered(x_buffers)}
    return pl.pallas_call(
        _ffn_kernel,
        grid_spec=pltpu.PrefetchScalarGridSpec(
            num_scalar_prefetch=1,
            grid=grid,
            in_specs=[
                pl.BlockSpec((tm, D_MODEL), lambda i, j, l: (i, 0), **x_mode),
                pl.BlockSpec((1, 1, D_MODEL), lambda i, j, l: (l[0], 0, 0)),
                pl.BlockSpec((1, D_MODEL, th), lambda i, j, l: (l[0], 0, j)),
                pl.BlockSpec((1, D_MODEL, th), lambda i, j, l: (l[0], 0, j + nh)),
                pl.BlockSpec((1, th, D_MODEL), lambda i, j, l: (l[0], j, 0)),
                pl.BlockSpec((1, D_MODEL), lambda i, j, l: (0, 0)),
            ],
            out_specs=pl.BlockSpec((tm, D_MODEL), lambda i, j, l: (i, 0)),
            scratch_shapes=[pltpu.VMEM((tm, D_MODEL), BF16)],
        ),
        out_shape=jax.ShapeDtypeStruct((n_rows, D_MODEL), F32),
        compiler_params=_cparams(("parallel", "arbitrary"), vmem),
        name="ffn",
    )(lidx, x, g, w_gu, w_gu, w_down, final_g)


def _mem_kv_kernel(mem_ref, g_ref, wk_ref, wv_ref, kg_ref, k_ref, v_ref):
    mn = _rms(mem_ref[0], g_ref[0]).astype(BF16)
    k = _dot(mn, wk_ref[0])
    for h in range(CA_H):
        sl = slice(h * CA_DH, (h + 1) * CA_DH)
        k_ref[0, :, sl] = _rms(k[:, sl], kg_ref[0])
    v_ref[0] = _dot(mn, wv_ref[0])


def mem_kv(mem, g, wk, wv, kg):
    nb = mem.shape[0]
    out = jax.ShapeDtypeStruct((DEPTH * nb, N_MEM, CA_W), F32)
    return pl.pallas_call(
        _mem_kv_kernel,
        grid=(DEPTH, nb),
        in_specs=[
            pl.BlockSpec((1, N_MEM, D_MODEL), lambda l, b: (b, 0, 0)),
            pl.BlockSpec((1, 1, D_MODEL), lambda l, b: (l, 0, 0)),
            pl.BlockSpec((1, D_MODEL, CA_W), lambda l, b: (l, 0, 0)),
            pl.BlockSpec((1, D_MODEL, CA_W), lambda l, b: (l, 0, 0)),
            pl.BlockSpec((1, 1, CA_DH), lambda l, b: (l, 0, 0)),
        ],
        out_specs=[pl.BlockSpec((1, N_MEM, CA_W), lambda l, b: (l * nb + b, 0, 0)),
                   pl.BlockSpec((1, N_MEM, CA_W), lambda l, b: (l * nb + b, 0, 0))],
        out_shape=[out, out],
        compiler_params=_cparams(("arbitrary", "arbitrary"), 24 << 20),
        name="mem_kv",
    )(mem, g, wk, wv, kg)


def _post_kernel(l_ref, x_ref, yml_ref, ydf_ref, yhg_ref, wout_ref, gx_ref, wq_ref, qg_ref,
                 mk_ref, mv_ref, wo_ref, o_ref):
    x1 = x_ref[...]
    x1 = x1 + _dot(yml_ref[...], wout_ref[0, 0:ML_W, :])
    x1 = x1 + _dot(ydf_ref[...], wout_ref[0, ML_W:ML_W + DF_W, :])
    x1 = x1 + _dot(yhg_ref[...], wout_ref[0, ML_W + DF_W:D_MODEL, :])
    hn = _rms(x1, gx_ref[0]).astype(BF16)
    q = _dot(hn, wq_ref[0])
    heads = []
    for h in range(CA_H):
        sl = slice(h * CA_DH, (h + 1) * CA_DH)
        qh = _rms(q[:, sl], qg_ref[0]).astype(BF16)
        s = _dot_nt(qh, mk_ref[0, :, sl].astype(BF16)) * (CA_DH ** -0.5)
        s = s - jnp.max(s, axis=-1, keepdims=True)
        p = jnp.exp(s)
        p = p / jnp.sum(p, axis=-1, keepdims=True)
        heads.append(_dot(p.astype(BF16), mv_ref[0, :, sl].astype(BF16)))
    o = jnp.concatenate(heads, axis=-1).astype(BF16)
    o_ref[...] = x1 + _dot(o, wo_ref[0])


def post_mix(x, y_ml, y_df, y_hg, lidx, w_out, gx, wq, qg, mk, mv, wo, *, tm, rows_per_batch):
    n_rows = x.shape[0]
    nb = n_rows // rows_per_batch
    tiles_per_batch = rows_per_batch // tm
    grid = (n_rows // tm,)
    w_bytes = (D_MODEL * D_MODEL + 2 * D_MODEL * CA_W) * 2
    vmem = 2 * w_bytes + 6 * tm * D_MODEL * 4 + 2 * tm * D_MODEL * 2 + 4 * N_MEM * CA_W * 4 + (6 << 20)
    row = lambda i, l: (i, 0)
    mem_map = lambda i, l: (l[0] * nb + i // tiles_per_batch, 0, 0)
    return pl.pallas_call(
        _post_kernel,
        grid_spec=pltpu.PrefetchScalarGridSpec(
            num_scalar_prefetch=1,
            grid=grid,
            in_specs=[
                pl.BlockSpec((tm, D_MODEL), row),
                pl.BlockSpec((tm, ML_W), row),
                pl.BlockSpec((tm, DF_W), row),
                pl.BlockSpec((tm, HG_W), row),
                pl.BlockSpec((1, D_MODEL, D_MODEL), lambda i, l: (l[0], 0, 0)),
                pl.BlockSpec((1, 1, D_MODEL), lambda i, l: (l[0], 0, 0)),
                pl.BlockSpec((1, D_MODEL, CA_W), lambda i, l: (l[0], 0, 0)),
                pl.BlockSpec((1, 1, CA_DH), lambda i, l: (l[0], 0, 0)),
                pl.BlockSpec((1, N_MEM, CA_W), mem_map),
                pl.BlockSpec((1, N_MEM, CA_W), mem_map),
                pl.BlockSpec((1, CA_W, D_MODEL), lambda i, l: (l[0], 0, 0)),
            ],
            out_specs=pl.BlockSpec((tm, D_MODEL), row),
        ),
        out_shape=jax.ShapeDtypeStruct((n_rows, D_MODEL), F32),
        compiler_params=_cparams(("parallel",), vmem),
        name="post_mix",
    )(lidx, x, y_ml, y_df, y_hg, w_out, gx, wq, qg, mk, mv, wo)


def _mlstm_kernel(l_ref, zq_ref, zk_ref, zv_ref, zo_ref, gate_ref, hist_ref, cw_ref, cb_ref, bif_ref,
                  og_ref, c0_ref, n0_ref, m0_ref, tri_ref,
                  y_ref, c_ref, n_ref, m_ref, ext_ref, *, tc, lc):
    c = pl.program_id(1)

    @pl.when(c == 0)
    def _():
        c_ref[...] = c0_ref[...]
        n_ref[...] = n0_ref[...]
        m_ref[...] = m0_ref[...]
        ext_ref[0:HIST, :] = hist_ref[0]

    ext_ref[HIST:HIST + tc, 0:ML_W] = zq_ref[...]
    ext_ref[HIST:HIST + tc, ML_W:2 * ML_W] = zk_ref[...]
    acc = cb_ref[0]
    for j in range(CONV_W):
        acc = acc + cw_ref[0, j:j + 1, :] * ext_ref[pl.ds(HIST - (CONV_W - 1) + j, tc), :]
    tail = ext_ref[tc:tc + HIST, :]
    ext_ref[0:HIST, :] = tail
    qk = acc * jax.nn.sigmoid(acc)

    gz = gate_ref[...] + bif_ref[0]
    lf_all = pltpu.roll(_log_sigmoid(gz), GATE_W - ML_H, axis=1)
    tri = tri_ref[...]
    row_i = lax.broadcasted_iota(jnp.int32, (lc, lc), 0)
    col_i = lax.broadcasted_iota(jnp.int32, (lc, lc), 1)
    causal = row_i >= col_i
    diag = row_i == col_i
    trow = lax.broadcasted_iota(jnp.int32, (lc, GATE_W), 0)
    og = og_ref[0]

    for ci in range(tc // lc):
        r0 = ci * lc
        b_c = _dot_exact01(tri, lf_all[r0:r0 + lc, :])
        r_c = gz[r0:r0 + lc, :] - b_c
        cm = r_c
        d = 1
        while d < lc:
            cm = jnp.maximum(cm, jnp.where(trow >= d, pltpu.roll(cm, d, axis=0), NEG_BIG))
            d *= 2
        m_prev = m_ref[0]
        mx = jnp.maximum(m_prev, cm)
        m_t = b_c + mx
        w_inter_c = jnp.exp(m_prev - mx)
        emt_c = jnp.exp(-m_t)
        m_new = m_t[lc - 1:lc, :]
        b_last = b_c[lc - 1:lc, :]
        w_s_c = jnp.exp(b_last + r_c - m_new)
        decay_c = jnp.exp(b_last + m_prev - m_new)
        m_ref[0] = m_new
        for h in range(ML_H):
            sl = slice(h * ML_DH, (h + 1) * ML_DH)
            hl = slice(h, h + 1)
            q = qk[r0:r0 + lc, h * ML_DH:(h + 1) * ML_DH]
            k = qk[r0:r0 + lc, ML_W + h * ML_DH:ML_W + (h + 1) * ML_DH] * (ML_DH ** -0.5)
            v = zv_ref[r0:r0 + lc, sl]
            qb = q.astype(BF16)
            r_row = jnp.sum(jnp.where(diag, r_c[:, hl], 0.0), axis=0, keepdims=True)
            d_mat = jnp.exp(jnp.where(causal, r_row - mx[:, hl], NEG_BIG))
            w_inter = w_inter_c[:, hl]
            decay = decay_c[:, hl]
            s = _dot_nt(qb, k.astype(BF16)) * d_mat
            c_old = c_ref[0, h]
            n_old = n_ref[0, h:h + 1, :]
            num = _dot(s.astype(BF16), v.astype(BF16)) + w_inter * _dot(qb, c_old.astype(BF16))
            den = jnp.sum(s, axis=-1, keepdims=True) + w_inter * jnp.sum(q * n_old, axis=-1, keepdims=True)
            hh = num / jnp.maximum(jnp.abs(den), emt_c[:, hl])
            kw = k * w_s_c[:, hl]
            c_ref[0, h] = decay * c_old + _dot_tn(kw.astype(BF16), v.astype(BF16))
            n_ref[0, h:h + 1, :] = decay * n_old + jnp.sum(kw, axis=0, keepdims=True)
            o_gate = jax.nn.sigmoid(zo_ref[r0:r0 + lc, sl])
            y_ref[r0:r0 + lc, sl] = (_rms(hh, og) * o_gate).astype(BF16)


def mlstm_mixer(z, gates, lidx, hist0, conv_w, conv_b, b_if, out_g, c0, n0, m0, *, nb, seq, tc, lc):
    n_rows = nb * seq
    nc = seq // tc
    tri = jnp.asarray(np.tril(np.ones((lc, lc), np.float32)), BF16)
    zmap = lambda blk: (lambda b, c, l: (b * nc + c, blk))
    lmap3 = lambda b, c, l: (l[0], 0, 0)
    vmem = 2 * 5 * tc * ML_W * 4 + (HIST + tc) * 2 * ML_W * 4 + 4 * ML_H * ML_DH * ML_DH * 4 \
        + 6 * tc * 2 * ML_W * 4 + (8 << 20)
    kern = functools.partial(_mlstm_kernel, tc=tc, lc=lc)
    return pl.pallas_call(
        kern,
        grid_spec=pltpu.PrefetchScalarGridSpec(
            num_scalar_prefetch=1,
            grid=(nb, nc),
            in_specs=[
                pl.BlockSpec((tc, ML_W), zmap(ZB_MLQ)),
                pl.BlockSpec((tc, ML_W), zmap(ZB_MLK)),
                pl.BlockSpec((tc, ML_W), zmap(ZB_MLV)),
                pl.BlockSpec((tc, ML_W), zmap(ZB_MLO)),
                pl.BlockSpec((tc, GATE_W), lambda b, c, l: (b * nc + c, 0)),
                pl.BlockSpec((1, HIST, 2 * ML_W), lambda b, c, l: (b, 0, 0)),
                pl.BlockSpec((1, CONV_W, 2 * ML_W), lmap3),
                pl.BlockSpec((1, 1, 2 * ML_W), lmap3),
                pl.BlockSpec((1, 1, GATE_W), lmap3),
                pl.BlockSpec((1, 1, ML_DH), lmap3),
                pl.BlockSpec((1, ML_H, ML_DH, ML_DH), lambda b, c, l: (b, 0, 0, 0)),
                pl.BlockSpec((1, 8, ML_DH), lambda b, c, l: (b, 0, 0)),
                pl.BlockSpec((1, 1, GATE_W), lambda b, c, l: (b, 0, 0)),
                pl.BlockSpec((lc, lc), lambda b, c, l: (0, 0)),
            ],
            out_specs=[
                pl.BlockSpec((tc, ML_W), lambda b, c, l: (b * nc + c, 0)),
                pl.BlockSpec((1, ML_H, ML_DH, ML_DH), lambda b, c, l: (b, 0, 0, 0)),
                pl.BlockSpec((1, 8, ML_DH), lambda b, c, l: (b, 0, 0)),
                pl.BlockSpec((1, 1, GATE_W), lambda b, c, l: (b, 0, 0)),
            ],
            scratch_shapes=[pltpu.VMEM((HIST + tc, 2 * ML_W), F32)],
        ),
        out_shape=[
            jax.ShapeDtypeStruct((n_rows, ML_W), BF16),
            jax.ShapeDtypeStruct((nb, ML_H, ML_DH, ML_DH), F32),
            jax.ShapeDtypeStruct((nb, 8, ML_DH), F32),
            jax.ShapeDtypeStruct((nb, 1, GATE_W), F32),
        ],
        compiler_params=_cparams(("parallel", "arbitrary"), vmem),
        name="mlstm",
    )(lidx, z, z, z, z, gates, hist0, conv_w, conv_b, b_if, out_g, c0, n0, m0, tri)


def _hgrn_level_masks(lc):
    t = np.arange(lc)[:, None]
    s = np.arange(lc)[None, :]
    masks = []
    h = lc // 2
    while h >= 1:
        odd = (t % (2 * h)) >= h
        same = (t // (2 * h)) == (s // (2 * h))
        masks.append((same & odd & ((s % (2 * h)) < h)).astype(np.float32))
        h //= 2
    masks.append((t == s).astype(np.float32))
    return np.stack(masks, axis=0)


def _hgrn_boundary_rows(a, h):
    lc, width = a.shape
    if h >= 8:
        parts = [jnp.broadcast_to(a[g * 2 * h + h - 1:g * 2 * h + h, :], (2 * h, width)) for g in range(lc // (2 * h))]
        return parts[0] if len(parts) == 1 else jnp.concatenate(parts, axis=0)
    a3 = a.reshape(lc // 8, 8, width)
    sub = lax.broadcasted_iota(jnp.int32, a3.shape, 1)
    out = None
    for g in range(8 // (2 * h)):
        src = g * 2 * h + h - 1
        b = jnp.broadcast_to(a3[:, src:src + 1, :], a3.shape)
        out = b if out is None else jnp.where(sub >= g * 2 * h, b, out)
    return out.reshape(lc, width)


def _hgrn_kernel(l_ref, zq_ref, zf_ref, zi_ref, zg_ref, lbl_ref, og_ref, s0_ref, tri_ref, msk_ref,
                 y_ref, s_ref, st_ref, *, tc, lc, nlev):
    c = pl.program_id(1)
    nc = pl.num_programs(1)

    @pl.when(c == 0)
    def _():
        for h in range(HG_H):
            st_ref[h] = s0_ref[0, h].T

    logits = lbl_ref[...]
    e = jnp.exp(logits - jnp.max(logits, axis=0, keepdims=True))
    p = e / jnp.sum(e, axis=0, keepdims=True)
    drow = lax.broadcasted_iota(jnp.int32, p.shape, 0)
    lb = jnp.sum(jnp.where(jnp.logical_and(drow >= 1, drow <= l_ref[0]), p, 0.0), axis=0, keepdims=True)
    log_lb = jnp.log(jnp.maximum(lb, LB_FLOOR))
    log_1m = jnp.log1p(-lb)

    zf = zf_ref[...]
    a2 = log_1m + _log_sigmoid(zf)
    hi = jnp.maximum(log_lb, a2)
    lf = hi + jnp.log1p(jnp.exp(-jnp.abs(log_lb - a2)))
    lf2 = lf * LOG2E
    kk = (1.0 - lb) * jax.nn.sigmoid(-zf) + (lb - jnp.maximum(lb, LB_FLOOR))
    zq = zq_ref[...]
    qq = zq * jax.nn.sigmoid(zq) * (HG_DK ** -0.5)
    tri = tri_ref[...]
    og = og_ref[0]

    for ci in range(tc // lc):
        r0 = ci * lc
        a_all = _dot_exact01(tri, lf2[r0:r0 + lc, :])
        lev_exp = [-jnp.abs(a_all - _hgrn_boundary_rows(a_all, lc >> (lv + 1))) for lv in range(nlev)]
        for h in range(HG_H):
            sl = slice(h * HG_DK, (h + 1) * HG_DK)
            q = qq[r0:r0 + lc, sl]
            k = kk[r0:r0 + lc, sl]
            iv = zi_ref[r0:r0 + lc, sl].astype(BF16)
            a_in = a_all[:, sl]
            a_end = a_in[lc - 1:lc, :]
            st = st_ref[h]
            o = _dot_nt((q * jnp.exp2(a_in)).astype(BF16), st.astype(BF16))
            att = jnp.where(msk_ref[nlev] > 0, _dot_nt(q.astype(BF16), k.astype(BF16)), 0.0)
            for lv in range(nlev):
                xf = jnp.exp2(lev_exp[lv][:, sl])
                pm = _dot_nt((q * xf).astype(BF16), (k * xf).astype(BF16))
                att = att + jnp.where(msk_ref[lv] > 0, pm, 0.0)
            o = o + _dot(att.astype(BF16), iv)
            k_end = (k * jnp.exp2(a_end - a_in)).astype(BF16)
            st_ref[h] = st * jnp.exp2(a_end) + _dot_tn(iv, k_end)
            gsl = zg_ref[r0:r0 + lc, sl]
            y_ref[r0:r0 + lc, sl] = (_rms(o, og) * (gsl * jax.nn.sigmoid(gsl))).astype(BF16)

    @pl.when(c == nc - 1)
    def _():
        for h in range(HG_H):
            s_ref[0, h] = st_ref[h].T


def hgrn_mixer(z, lidx, lb_logits, out_g, s0, *, nb, seq, tc, lc):
    n_rows = nb * seq
    nc = seq // tc
    msk_np = _hgrn_level_masks(lc)
    nlev = msk_np.shape[0] - 1
    tri = jnp.asarray(np.tril(np.ones((lc, lc), np.float32)), BF16)
    msk = jnp.asarray(msk_np, F32)
    zmap = lambda blk: (lambda b, c, l: (b * nc + c, blk))
    vmem = 2 * 5 * tc * HG_W * 4 + 5 * HG_H * HG_DK * HG_DV * 4 + 8 * tc * HG_W * 4 + (8 << 20)
    kern = functools.partial(_hgrn_kernel, tc=tc, lc=lc, nlev=nlev)
    return pl.pallas_call(
        kern,
        grid_spec=pltpu.PrefetchScalarGridSpec(
            num_scalar_prefetch=1,
            grid=(nb, nc),
            in_specs=[
                pl.BlockSpec((tc, HG_W), zmap(ZB_HGQ)),
                pl.BlockSpec((tc, HG_W), zmap(ZB_HGF)),
                pl.BlockSpec((tc, HG_W), zmap(ZB_HGI)),
                pl.BlockSpec((tc, HG_W), zmap(ZB_HGG)),
                pl.BlockSpec((DEPTH, HG_W), lambda b, c, l: (0, 0)),
                pl.BlockSpec((1, 1, HG_DV), lambda b, c, l: (l[0], 0, 0)),
                pl.BlockSpec((1, HG_H, HG_DK, HG_DV), lambda b, c, l: (b, 0, 0, 0)),
                pl.BlockSpec((lc, lc), lambda b, c, l: (0, 0)),
                pl.BlockSpec(msk_np.shape, lambda b, c, l: (0, 0, 0)),
            ],
            out_specs=[
                pl.BlockSpec((tc, HG_W), lambda b, c, l: (b * nc + c, 0)),
                pl.BlockSpec((1, HG_H, HG_DK, HG_DV), lambda b, c, l: (b, 0, 0, 0)),
            ],
            scratch_shapes=[pltpu.VMEM((HG_H, HG_DV, HG_DK), F32)],
        ),
        out_shape=[
            jax.ShapeDtypeStruct((n_rows, HG_W), BF16),
            jax.ShapeDtypeStruct((nb, HG_H, HG_DK, HG_DV), F32),
        ],
        compiler_params=_cparams(("parallel", "arbitrary"), vmem),
        name="hgrn",
    )(lidx, z, z, z, z, lb_logits, out_g, s0, tri, msk)


def _qk_prep_kernel(l_ref, zq_ref, zk_ref, zv_ref, qg_ref, kg_ref, grp_ref, qn_ref, kn_ref, knb_ref, vb_ref):
    grp = grp_ref[...]
    qn_ref[...] = (_group_norm64(zq_ref[...], qg_ref[0], grp) * (DF_DH ** -0.5)).astype(BF16)
    kn = _group_norm64(zk_ref[...], kg_ref[0], grp)
    kn_ref[...] = kn
    knb_ref[...] = kn.astype(BF16)
    vb_ref[...] = zv_ref[...].astype(BF16)


def qk_prep(z, lidx, qg, kg, *, tm):
    n_rows = z.shape[0]
    lane = np.arange(DF_W)
    grp = jnp.asarray(((lane[:, None] // DF_DH) == (lane[None, :] // DF_DH)).astype(np.float32) / DF_DH, BF16)
    zmap = lambda blk: (lambda i, l: (i, blk))
    row = lambda i, l: (i, 0)
    return pl.pallas_call(
        _qk_prep_kernel,
        grid_spec=pltpu.PrefetchScalarGridSpec(
            num_scalar_prefetch=1,
            grid=(n_rows // tm,),
            in_specs=[
                pl.BlockSpec((tm, DF_W), zmap(ZB_DFQ)),
                pl.BlockSpec((tm, DF_W), zmap(ZB_DFK)),
                pl.BlockSpec((tm, DF_W), zmap(ZB_DFV)),
                pl.BlockSpec((1, 1, DF_W), lambda i, l: (l[0], 0, 0)),
                pl.BlockSpec((1, 1, DF_W), lambda i, l: (l[0], 0, 0)),
                pl.BlockSpec((DF_W, DF_W), lambda i, l: (0, 0)),
            ],
            out_specs=[pl.BlockSpec((tm, DF_W), row)] * 4,
        ),
        out_shape=[
            jax.ShapeDtypeStruct((n_rows, DF_W), BF16),
            jax.ShapeDtypeStruct((n_rows, DF_W), F32),
            jax.ShapeDtypeStruct((n_rows, DF_W), BF16),
            jax.ShapeDtypeStruct((n_rows, DF_W), BF16),
        ],
        compiler_params=_cparams(("parallel",), 32 << 20),
        name="qk_prep",
    )(lidx, z, z, z, qg, kg, grp)


def _group_norm64(x, g, grp):
    x2 = x * x
    hi = x2.astype(BF16)
    lo = (x2 - hi.astype(F32)).astype(BF16)
    ms = _dot(hi, grp) + _dot(lo, grp)
    return x * lax.rsqrt(ms + EPS) * g


BIAS_W = 128


def _qk_prep_t_kernel(l_ref, zq_ref, zk_ref, zv_ref, qg_ref, kg_ref, grp_ref, pk_in, pv_in,
                      qt_ref, ke_ref, vt_ref, pk_ref, pv_ref, *, tm, seq):
    del pk_in, pv_in
    i = pl.program_id(0)
    grp = grp_ref[...]
    qn = _group_norm64(zq_ref[...], qg_ref[0], grp) * (DF_DH ** -0.5)
    kn = _group_norm64(zk_ref[...], kg_ref[0], grp)
    zv = zv_ref[...]
    pos = lax.rem(i * tm, seq) + lax.broadcasted_iota(jnp.int32, (tm, BIAS_W), 0)
    lane = lax.broadcasted_iota(jnp.int32, (tm, BIAS_W), 1)
    hi_part = (pos >> CHUNK_SHIFT).astype(F32) * float(CHUNK)
    lo_part = (pos & (CHUNK - 1)).astype(F32)
    base = jnp.where(lane == 0, hi_part, jnp.where(lane == 1, lo_part, jnp.where(lane == 2, float(CHUNK), 0.0)))
    for h in range(DF_H):
        sl = slice(h * DF_DV, (h + 1) * DF_DV)
        slope = 2.0 ** (-8.0 * (h + 1) / DF_H)
        c0 = h * (DF_DV + BIAS_W)
        ke_ref[:, c0:c0 + DF_DV] = kn[:, sl].astype(BF16)
        ke_ref[:, c0 + DF_DV:c0 + DF_DV + BIAS_W] = (base * slope).astype(BF16)
        qt_ref[h, 0] = qn[:, sl].T.astype(BF16)
        vt_ref[h, 0] = zv[:, sl].T.astype(BF16)
        rows = pl.ds(h, tm, stride=DF_H)
        pk_ref[rows, :] = kn[:, sl]
        pv_ref[rows, :] = zv[:, sl]


def qk_prep_t(z, lidx, qg, kg, pk_buf, pv_buf, *, tm, seq):
    n_rows = z.shape[0]
    nt = n_rows // tm
    smap = lambda i, l: (l[0] * nt + i, 0)
    any_spec = pl.BlockSpec(memory_space=pl.ANY)
    lane = np.arange(DF_W)
    grp = jnp.asarray(((lane[:, None] // DF_DH) == (lane[None, :] // DF_DH)).astype(np.float32) / DF_DH, BF16)
    zmap = lambda blk: (lambda i, l: (i, blk))
    row = lambda i, l: (i, 0)
    tmap = lambda i, l: (0, i, 0, 0)
    kern = functools.partial(_qk_prep_t_kernel, tm=tm, seq=seq)
    return pl.pallas_call(
        kern,
        grid_spec=pltpu.PrefetchScalarGridSpec(
            num_scalar_prefetch=1,
            grid=(nt,),
            in_specs=[
                pl.BlockSpec((tm, DF_W), zmap(ZB_DFQ)),
                pl.BlockSpec((tm, DF_W), zmap(ZB_DFK)),
                pl.BlockSpec((tm, DF_W), zmap(ZB_DFV)),
                pl.BlockSpec((1, 1, DF_W), lambda i, l: (l[0], 0, 0)),
                pl.BlockSpec((1, 1, DF_W), lambda i, l: (l[0], 0, 0)),
                pl.BlockSpec((DF_W, DF_W), lambda i, l: (0, 0)),
                any_spec,
                any_spec,
            ],
            out_specs=[
                pl.BlockSpec((DF_H, 1, DF_DV, tm), tmap),
                pl.BlockSpec((tm, DF_H * (DF_DV + BIAS_W)), row),
                pl.BlockSpec((DF_H, 1, DF_DV, tm), tmap),
                pl.BlockSpec((tm * DF_H, DF_DV), smap),
                pl.BlockSpec((tm * DF_H, DF_DV), smap),
            ],
        ),
        out_shape=[
            jax.ShapeDtypeStruct((DF_H, nt, DF_DV, tm), BF16),
            jax.ShapeDtypeStruct((n_rows, DF_H * (DF_DV + BIAS_W)), BF16),
            jax.ShapeDtypeStruct((DF_H, nt, DF_DV, tm), BF16),
            jax.ShapeDtypeStruct(pk_buf.shape, F32),
            jax.ShapeDtypeStruct(pv_buf.shape, F32),
        ],
        input_output_aliases={7: 3, 8: 4},
        compiler_params=_cparams(("parallel",), 40 << 20),
        name="qk_prep_t",
    )(lidx, z, z, z, qg, kg, grp, pk_buf, pv_buf)


ONES_ROWS = 16


def _flash_t_kernel(l_ref, qt_ref, ke_ref, vt_ref, lam_ref, ogt_ref, y_ref, m_ref, acc_ref, sa_ref, sb_ref,
                    own_ref, *, tq):
    h = pl.program_id(1)
    qi = pl.program_id(2)
    tk = tq
    slope = jnp.exp2(jnp.full((1, 1), -8.0 / DF_H, F32) * (h + 1).astype(F32))
    qt = qt_ref[0, 0]
    row = lax.broadcasted_iota(jnp.int32, (DF_DV, tq), 0)
    zero = jnp.zeros_like(qt)
    q2t = jnp.concatenate([jnp.where(row < DF_DH, qt, zero), jnp.where(row >= DF_DH, qt, zero)], axis=1)
    brow = lax.broadcasted_iota(jnp.int32, (BIAS_W, 2 * tq), 0)
    a0 = ((qi * tq) >> CHUNK_SHIFT).astype(F32)
    extra = jnp.where(brow < 2, 1.0, jnp.where(brow == 2, -a0, 0.0)).astype(BF16)
    q2e = jnp.concatenate([q2t, extra], axis=0)
    ones = jnp.ones((ONES_ROWS, tk), BF16)

    m_ref[...] = jnp.full(m_ref.shape, NEG_BIG, F32)
    acc_ref[...] = jnp.zeros(acc_ref.shape, F32)

    def scores_into(dst_ref, kj):
        r0 = pl.multiple_of(kj * tk, tk)
        dst_ref[...] = _dot(ke_ref[pl.ds(r0, tk), :], q2e)

    def update(s, kj):
        m_old = m_ref[...]
        m_new = jnp.maximum(m_old, jnp.max(s, axis=0, keepdims=True))
        alpha = jnp.exp(m_old - m_new)
        p = jnp.exp(s - m_new).astype(BF16)
        vt_ext = jnp.concatenate([vt_ref[0, kj], ones], axis=0)
        acc_ref[...] = alpha * acc_ref[...] + _dot(vt_ext, p)
        m_ref[...] = m_new

    @pl.when(qi == 0)
    def _():
        rel_k = lax.broadcasted_iota(jnp.int32, (tq, 2 * tq), 0)
        rel_q = lax.broadcasted_iota(jnp.int32, (tq, 2 * tq), 1)
        rel_q = jnp.where(rel_q >= tq, rel_q - tq, rel_q)
        ahead = rel_k - rel_q
        fix = jnp.where(ahead > 0, (-2.0 * slope) * ahead.astype(F32), 0.0)
        visible = (rel_k >> CHUNK_SHIFT) <= (rel_q >> CHUNK_SHIFT)
        own_ref[...] = jnp.where(visible, fix, NEG_BIG)

    def own_block(s):
        update(s + own_ref[...], qi)

    scores_into(sa_ref, 0)

    def block_pair(t, carry):
        k0 = 2 * t
        scores_into(sb_ref, k0 + 1)
        update(sa_ref[...], k0)
        scores_into(sa_ref, k0 + 2)
        update(sb_ref[...], k0 + 1)
        return carry

    lax.fori_loop(0, qi >> 1, block_pair, 0)

    @pl.when((qi & 1) == 1)
    def _():
        scores_into(sb_ref, qi)
        update(sa_ref[...], qi - 1)
        own_block(sb_ref[...])

    @pl.when((qi & 1) == 0)
    def _():
        own_block(sa_ref[...])

    lam_p = lam_ref[0]
    lam_init = 0.8 - 0.6 * jnp.exp(jnp.full((1, 1), -0.3, F32) * l_ref[0].astype(F32))
    lam = (jnp.exp(jnp.sum(lam_p[0:1] * lam_p[1:2], axis=-1, keepdims=True))
           - jnp.exp(jnp.sum(lam_p[2:3] * lam_p[3:4], axis=-1, keepdims=True)) + lam_init)
    acc = acc_ref[...]
    o_all = acc[0:DF_DV] / acc[DF_DV:DF_DV + 1]
    o = o_all[:, 0:tq] - lam * o_all[:, tq:2 * tq]
    ms = jnp.mean(o * o, axis=0, keepdims=True)
    y = o * lax.rsqrt(ms + EPS) * ogt_ref[0] * (1.0 - lam_init)
    y_ref[...] = y.T.astype(BF16)


def diff_flash_t(qt, ke, vt, lidx, lam_p, out_g_t, *, nb, seq, tq):
    assert tq % CHUNK == 0
    nq = seq // tq
    n_rows = nb * seq
    kew = DF_DV + BIAS_W
    kern = functools.partial(_flash_t_kernel, tq=tq)
    vmem = 2 * seq * kew * 2 + 2 * seq * DF_DV * 2 + 10 * tq * 2 * tq * 4 + (8 << 20)
    return pl.pallas_call(
        kern,
        grid_spec=pltpu.PrefetchScalarGridSpec(
            num_scalar_prefetch=1,
            grid=(nb, DF_H, nq),
            in_specs=[
                pl.BlockSpec((1, 1, DF_DV, tq), lambda b, h, i, l: (h, b * nq + i, 0, 0)),
                pl.BlockSpec((seq, kew), lambda b, h, i, l: (b, h)),
                pl.BlockSpec((1, nq, DF_DV, tq), lambda b, h, i, l: (h, b, 0, 0)),
                pl.BlockSpec((1, 4, DF_DH), lambda b, h, i, l: (l[0], 0, 0)),
                pl.BlockSpec((1, DF_DV, 1), lambda b, h, i, l: (l[0], 0, 0)),
            ],
            out_specs=pl.BlockSpec((tq, DF_DV), lambda b, h, i, l: (b * nq + i, h)),
            scratch_shapes=[pltpu.VMEM((1, 2 * tq), F32), pltpu.VMEM((DF_DV + ONES_ROWS, 2 * tq), F32),
                            pltpu.VMEM((tq, 2 * tq), F32), pltpu.VMEM((tq, 2 * tq), F32),
                            pltpu.VMEM((tq, 2 * tq), F32)],
        ),
        out_shape=jax.ShapeDtypeStruct((n_rows, DF_W), BF16),
        compiler_params=_cparams(("parallel", "parallel", "arbitrary"), vmem),
        name="diff_flash_t",
    )(lidx, qt, ke, vt, lam_p, out_g_t)


def _flash_sample_kernel(l_ref, q_ref, km_ref, vm_ref, kx_ref, vx_ref, lam_ref, og_ref, y_ref,
                         m_ref, l_sum_ref, acc_ref, *, tq, tk, past):
    h = pl.program_id(1)
    slope = jnp.exp2(jnp.full((1, 1), -8.0 / DF_H, F32) * (h + 1).astype(F32))
    q = q_ref[...]
    lane = lax.broadcasted_iota(jnp.int32, (tq, DF_DV), 1)
    zero = jnp.zeros_like(q)
    q2 = jnp.concatenate([jnp.where(lane < DF_DH, q, zero), jnp.where(lane >= DF_DH, q, zero)], axis=0)
    q0 = past

    m_ref[...] = jnp.full(m_ref.shape, NEG_BIG, F32)
    l_sum_ref[...] = jnp.zeros(l_sum_ref.shape, F32)
    acc_ref[...] = jnp.zeros(acc_ref.shape, F32)

    def update(s, v):
        m_old = m_ref[...]
        m_new = jnp.maximum(m_old, jnp.max(s, axis=-1, keepdims=True))
        alpha = jnp.exp(m_old - m_new)
        p = jnp.exp(s - m_new)
        l_sum_ref[...] = alpha * l_sum_ref[...] + jnp.sum(p, axis=-1, keepdims=True)
        acc_ref[...] = alpha * acc_ref[...] + _dot(p.astype(BF16), v)
        m_ref[...] = m_new

    def fast_block(kj, carry):
        r0 = kj * tk
        rows = pl.ds(r0 * DF_H + h, tk, stride=DF_H)
        k = km_ref[rows, :].astype(BF16)
        v = vm_ref[rows, :].astype(BF16)
        kpos = r0 + lax.broadcasted_iota(jnp.int32, (1, tk), 1)
        s = _dot_nt(q2, k) + slope * (kpos - q0).astype(F32)
        update(s, v)
        return carry

    lax.fori_loop(0, past // tk, fast_block, 0)

    k = kx_ref[...].astype(BF16)
    v = vx_ref[...].astype(BF16)
    rel_k = lax.broadcasted_iota(jnp.int32, (2 * tq, tq), 1)
    rel_q = lax.broadcasted_iota(jnp.int32, (2 * tq, tq), 0)
    rel_q = jnp.where(rel_q >= tq, rel_q - tq, rel_q)
    dist = jnp.abs(rel_q - rel_k).astype(F32)
    s = _dot_nt(q2, k) + slope * (rel_q.astype(F32) - dist)
    visible = ((q0 + rel_k) >> CHUNK_SHIFT) <= ((q0 + rel_q) >> CHUNK_SHIFT)
    update(jnp.where(visible, s, NEG_BIG), v)

    lam_p = lam_ref[0]
    lam_init = 0.8 - 0.6 * jnp.exp(jnp.full((1, 1), -0.3, F32) * l_ref[0].astype(F32))
    lam = (jnp.exp(jnp.sum(lam_p[0:1] * lam_p[1:2], axis=-1, keepdims=True))
           - jnp.exp(jnp.sum(lam_p[2:3] * lam_p[3:4], axis=-1, keepdims=True)) + lam_init)
    o_all = acc_ref[...] / l_sum_ref[...]
    o = o_all[0:tq] - lam * o_all[tq:2 * tq]
    y_ref[...] = (_rms(o, og_ref[0]) * (1.0 - lam_init)).astype(BF16)


def diff_flash_sample(qn, k_past, v_past, k_new, v_new, lidx, lam_p, out_g, *, nb, seq, past, tk):
    n_rows = nb * seq
    kern = functools.partial(_flash_sample_kernel, tq=seq, tk=tk, past=past)
    vmem = 4 * past * DF_H * DF_DV * 4 + (8 << 20)
    qmap = lambda b, h, l: (b, h)
    cmap = lambda b, h, l: (l[0] * nb + b, 0)
    return pl.pallas_call(
        kern,
        grid_spec=pltpu.PrefetchScalarGridSpec(
            num_scalar_prefetch=1,
            grid=(nb, DF_H),
            in_specs=[
                pl.BlockSpec((seq, DF_DV), qmap),
                pl.BlockSpec((past * DF_H, DF_DV), cmap),
                pl.BlockSpec((past * DF_H, DF_DV), cmap),
                pl.BlockSpec((seq, DF_DV), qmap),
                pl.BlockSpec((seq, DF_DV), qmap),
                pl.BlockSpec((1, 4, DF_DH), lambda b, h, l: (l[0], 0, 0)),
                pl.BlockSpec((1, 1, DF_DV), lambda b, h, l: (l[0], 0, 0)),
            ],
            out_specs=pl.BlockSpec((seq, DF_DV), qmap),
            scratch_shapes=[pltpu.VMEM((2 * seq, 1), F32), pltpu.VMEM((2 * seq, 1), F32),
                            pltpu.VMEM((2 * seq, DF_DV), F32)],
        ),
        out_shape=jax.ShapeDtypeStruct((n_rows, DF_W), BF16),
        compiler_params=_cparams(("parallel", "arbitrary"), vmem),
        name="diff_flash_sample",
    )(lidx, qn, k_past, v_past, k_new, v_new, lam_p, out_g)


def _trunk_layer(x, lidx, w, st, cfg):
    nb, seq, past = cfg["nb"], cfg["seq"], cfg["past"]
    z, gates = in_proj(x, lidx, w["norm_mix"], w["w_main"], w["w_gate"], tm=cfg["tm_in"], tn=cfg["tn_in"])
    y_ml, c_new, n_new, m_new = mlstm_mixer(
        z, gates, lidx, st["hist0"], w["conv_w"], w["conv_b"], w["b_if"], w["ml_og"],
        st["c0"], st["n0"], st["m0"], nb=nb, seq=seq, tc=cfg["tc"], lc=cfg["lc_ml"])
    z3 = z.reshape(nb, seq, Z_W)
    new = {}
    if cfg["prompt"]:
        qt, ke, vt, new["pk_buf"], new["pv_buf"] = qk_prep_t(
            z, lidx, w["df_qg"], w["df_kg"], st["pk_buf"], st["pv_buf"], tm=cfg["tq"], seq=seq)
        y_df = diff_flash_t(qt, ke, vt, lidx, w["df_lam"], w["df_og_t"], nb=nb, seq=seq, tq=cfg["tq"])
    else:
        qn, kn, kn_b, v_b = qk_prep(z, lidx, w["df_qg"], w["df_kg"], tm=cfg["tm_prep"])
        y_df = diff_flash_sample(qn, st["past_k"], st["past_v"], kn_b, v_b, lidx, w["df_lam"], w["df_og"],
                                 nb=nb, seq=seq, past=past, tk=cfg["tk_past"])
        new["attn_k"] = kn.reshape(nb, seq, DF_H, 2 * DF_DH)
        new["attn_v"] = z3[:, :, ZB_DFV * DF_W:(ZB_DFV + 1) * DF_W].reshape(nb, seq, DF_H, DF_DV)
    y_hg, s_new = hgrn_mixer(z, lidx, w["hg_lbl"], w["hg_og"], st["s0"], nb=nb, seq=seq, tc=cfg["tc"],
                             lc=cfg["lc_hg"])
    x = post_mix(x, y_ml, y_df, y_hg, lidx, w["w_out"], w["norm_cross"], w["wq"], w["ca_qg"],
                 st["mk"], st["mv"], w["wo"], tm=cfg["tm_post"], rows_per_batch=seq)
    x = ffn(x, lidx, w["norm_ffn"], w["w_gu"], w["w_down"], w["final_norm"], tm=cfg["tm_ffn"], th=cfg["th"],
            x_buffers=cfg["ffn_x_buffers"])
    new.update({
        "conv": z3[:, seq - (CONV_W - 1):, 0:2 * ML_W],
        "C": c_new,
        "n": n_new[:, :ML_H, :],
        "m": m_new[:, 0, :ML_H],
        "S": s_new,
    })
    return x, new


PROMPT_CFG = dict(prompt=True, tm_in=1024, tn_in=1536, tc=256, lc_ml=128, lc_hg=128, tm_prep=512, tq=512,
                  tm_post=512, tm_ffn=1024, th=512, ffn_x_buffers=2)
SAMPLE_CFG = dict(prompt=False, tm_in=128, tn_in=768, tc=16, lc_ml=16, lc_hg=16, tm_prep=128, tk_past=2048,
                  tm_post=16, tm_ffn=128, th=512, ffn_x_buffers=2)


def kernel(x_prompt, x_sample, mem_prompt, cache_attn_k, cache_attn_v, cache_mem_k, cache_mem_v, state_mlstm_conv, state_mlstm_C, state_mlstm_n, state_mlstm_m, state_hgrn_S, norm_mix, w_in, mlstm_conv_w, mlstm_conv_b, mlstm_b_i, mlstm_b_f, mlstm_out_norm, diff_q_norm, diff_k_norm, diff_lambda, diff_out_norm, hgrn_lb_logits, hgrn_out_norm, w_out, norm_cross, norm_mem, cross_wq, cross_wk, cross_wv, cross_q_norm, cross_k_norm, cross_wo, norm_ffn, ffn_w_gate_up, ffn_w_down, final_norm):
    bp, tp = x_prompt.shape[:2]
    bs, ts = x_sample.shape[:2]
    past = cache_attn_k.shape[2]
    depth = w_in.shape[0]
    assert depth == DEPTH and x_prompt.shape[2] == D_MODEL

    g_off = 4 * ML_W
    r3 = lambda a: a.reshape(depth, 1, a.shape[-1])
    w = {
        "norm_mix": r3(norm_mix),
        "w_main": jnp.concatenate([w_in[:, :, :g_off], w_in[:, :, g_off + 2 * ML_H:]], axis=-1).astype(BF16),
        "w_gate": jnp.pad(w_in[:, :, g_off:g_off + 2 * ML_H], ((0, 0), (0, 0), (0, GATE_W - 2 * ML_H))).astype(BF16),
        "conv_w": mlstm_conv_w,
        "conv_b": r3(mlstm_conv_b),
        "b_if": r3(jnp.pad(jnp.concatenate([mlstm_b_i, mlstm_b_f], axis=-1), ((0, 0), (0, GATE_W - 2 * ML_H)))),
        "ml_og": r3(mlstm_out_norm),
        "df_qg": r3(jnp.tile(diff_q_norm, (1, DF_W // DF_DH))),
        "df_kg": r3(jnp.tile(diff_k_norm, (1, DF_W // DF_DH))),
        "df_lam": diff_lambda,
        "df_og": r3(diff_out_norm),
        "df_og_t": diff_out_norm.reshape(depth, DF_DV, 1),
        "hg_lbl": hgrn_lb_logits,
        "hg_og": r3(hgrn_out_norm),
        "w_out": w_out.astype(BF16),
        "norm_cross": r3(norm_cross),
        "wq": cross_wq.astype(BF16),
        "ca_qg": r3(cross_q_norm),
        "wo": cross_wo.astype(BF16),
        "norm_ffn": r3(norm_ffn),
        "w_gu": ffn_w_gate_up.astype(BF16),
        "w_down": ffn_w_down.astype(BF16),
        "final_norm": final_norm.reshape(1, D_MODEL),
    }

    mk_p, mv_p = mem_kv(mem_prompt, r3(norm_mem), cross_wk.astype(BF16), cross_wv.astype(BF16), r3(cross_k_norm))

    def pad_hist(conv):
        pad = [(0, 0)] * (conv.ndim - 2) + [(HIST - (CONV_W - 1), 0), (0, 0)]
        return jnp.pad(conv, pad)

    st_p = {
        "hist0": jnp.zeros((bp, HIST, 2 * ML_W), F32),
        "c0": jnp.zeros((bp, ML_H, ML_DH, ML_DH), F32),
        "n0": jnp.zeros((bp, 8, ML_DH), F32),
        "m0": jnp.zeros((bp, 1, GATE_W), F32),
        "s0": jnp.zeros((bp, HG_H, HG_DK, HG_DV), F32),
        "mk": mk_p, "mv": mv_p,
    }
    hist_s = pad_hist(state_mlstm_conv)
    n_s = jnp.pad(state_mlstm_n, ((0, 0), (0, 0), (0, 8 - ML_H), (0, 0)))
    m_s = jnp.pad(state_mlstm_m, ((0, 0), (0, 0), (0, GATE_W - ML_H))).reshape(depth, bs, 1, GATE_W)
    past_k = cache_attn_k.reshape(depth * bs * past * DF_H, DF_DV)
    past_v = cache_attn_v.reshape(depth * bs * past * DF_H, DF_DV)
    mk_s = cache_mem_k.reshape(depth * bs, N_MEM, CA_W)
    mv_s = cache_mem_v.reshape(depth * bs, N_MEM, CA_W)

    cfg_p = dict(PROMPT_CFG, nb=bp, seq=tp, past=0)
    cfg_s = dict(SAMPLE_CFG, nb=bs, seq=ts, past=past)

    def layer(carry, xs):
        xp, xsm, pk_buf, pv_buf = carry
        l, hist_l, c_l, n_l, m_l, s_l = xs
        lidx = l.reshape(1).astype(jnp.int32)
        xp, new_p = _trunk_layer(xp, lidx, w, dict(st_p, pk_buf=pk_buf, pv_buf=pv_buf), cfg_p)
        pk_buf, pv_buf = new_p.pop("pk_buf"), new_p.pop("pv_buf")
        st_s = {"hist0": hist_l, "c0": c_l, "n0": n_l, "m0": m_l, "s0": s_l,
                "past_k": past_k, "past_v": past_v, "mk": mk_s, "mv": mv_s}
        xsm, new_s = _trunk_layer(xsm, lidx, w, st_s, cfg_s)
        return (xp, xsm, pk_buf, pv_buf), (new_p, new_s)

    xs = (jnp.arange(depth, dtype=jnp.int32), hist_s, state_mlstm_C, n_s, m_s, state_hgrn_S)
    kv_rows = depth * bp * tp * DF_H
    init = (x_prompt.reshape(bp * tp, D_MODEL), x_sample.reshape(bs * ts, D_MODEL),
            lax.empty((kv_rows, DF_DV), F32), lax.empty((kv_rows, DF_DV), F32))
    (xp, xsm, pk_buf, pv_buf), (new_p, new_s) = lax.scan(layer, init, xs)

    y_prompt = xp.reshape(bp, tp, D_MODEL)
    y_sample = xsm.reshape(bs, ts, D_MODEL)
    p_mem_k = mk_p.reshape(depth, bp, N_MEM, CA_H, CA_DH)
    p_mem_v = mv_p.reshape(depth, bp, N_MEM, CA_H, CA_DH)
    p_attn_k = pk_buf.reshape(depth, bp, tp, DF_H, 2 * DF_DH)
    p_attn_v = pv_buf.reshape(depth, bp, tp, DF_H, DF_DV)
    return (y_prompt, y_sample,
            p_attn_k, p_attn_v, p_mem_k, p_mem_v, new_p["conv"], new_p["C"], new_p["n"],
            new_p["m"], new_p["S"],
            new_s["attn_k"], new_s["attn_v"], new_s["conv"], new_s["C"], new_s["n"], new_s["m"], new_s["S"])
```

```python
import functools
import math

import numpy as np
import jax
import jax.numpy as jnp
from jax import lax
from jax.experimental import pallas as pl
from jax.experimental.pallas import tpu as pltpu

F32 = jnp.float32
BF16 = jnp.bfloat16

D_MODEL = 2048
DEPTH = 4
CHUNK = 64
CHUNK_SHIFT = 6
ML_DH = 128
ML_W = 768
ML_H = 6
CONV_W = 4
DF_DH = 64
DF_DV = 128
DF_W = 512
DF_H = 4
HG_DK = 128
HG_DV = 128
HG_W = 768
HG_H = 6
CA_H = 4
CA_DH = 128
CA_W = 512
N_MEM = 256
FF_HIDDEN = 5632
EPS = 1e-6
NEG_BIG = -1e30
LB_FLOOR = 1e-30
LOG2E = math.log2(math.e)

Z_W = 2 * ML_W + 2 * ML_W + 3 * DF_W + 4 * HG_W
GATE_W = 128
ZB_MLQ, ZB_MLK, ZB_MLV, ZB_MLO = 0, 1, 2, 3
ZB_HGQ, ZB_HGF, ZB_HGI, ZB_HGG = 6, 7, 8, 9
ZB_DFQ, ZB_DFK, ZB_DFV = 6, 7, 8

V7X_VMEM_BYTES = 64 * 1024 * 1024
V7X_VMEM_CAP = 58 * 1024 * 1024
VMEM_SLACK = 6 * 1024 * 1024
HIST = 8

NT_DIMS = (((1,), (1,)), ((), ()))
TN_DIMS = (((0,), (0,)), ((), ()))


def _vmem_limit(nbytes):
    return int(min(V7X_VMEM_CAP, max(32 * 1024 * 1024, nbytes + VMEM_SLACK)))


def _cparams(sem, vmem_bytes):
    return pltpu.CompilerParams(dimension_semantics=sem, vmem_limit_bytes=_vmem_limit(vmem_bytes))


def _rms(x, g):
    ms = jnp.mean(x * x, axis=-1, keepdims=True)
    return x * lax.rsqrt(ms + EPS) * g


def _dot(a, b):
    return jnp.dot(a, b, preferred_element_type=F32)


def _dot_nt(a, b):
    return lax.dot_general(a, b, NT_DIMS, preferred_element_type=F32)


def _dot_tn(a, b):
    return lax.dot_general(a, b, TN_DIMS, preferred_element_type=F32)


def _log_sigmoid(x):
    return jnp.minimum(x, 0.0) - jnp.log1p(jnp.exp(-jnp.abs(x)))


def _split3(x):
    hi = x.astype(BF16)
    r1 = x - hi.astype(F32)
    mid = r1.astype(BF16)
    lo = (r1 - mid.astype(F32)).astype(BF16)
    return hi, mid, lo


def _dot_exact01(m01, x):
    hi, mid, lo = _split3(x)
    return _dot(m01, hi) + _dot(m01, mid) + _dot(m01, lo)


def _regroup_kernel(w_ref, main_ref, gate_ref):
    g_off = 4 * ML_W
    n_gate = 2 * ML_H
    rows = w_ref.shape[1]
    main_ref[0, :, 0:g_off] = w_ref[0, :, 0:g_off].astype(BF16)
    main_ref[0, :, g_off:Z_W] = w_ref[0, :, g_off + n_gate:g_off + n_gate + (Z_W - g_off)].astype(BF16)
    gate = w_ref[0, :, g_off:g_off + GATE_W]
    lane = lax.broadcasted_iota(jnp.int32, (rows, GATE_W), 1)
    gate_ref[0] = jnp.where(lane < n_gate, gate, 0.0).astype(BF16)


def regroup_w_in(w_in, *, tr):
    depth, d_in, cols = w_in.shape
    assert cols == Z_W + 2 * ML_H and d_in % tr == 0
    return pl.pallas_call(
        _regroup_kernel,
        grid=(depth, d_in // tr),
        in_specs=[pl.BlockSpec((1, tr, cols), lambda l, r: (l, r, 0))],
        out_specs=[pl.BlockSpec((1, tr, Z_W), lambda l, r: (l, r, 0)),
                   pl.BlockSpec((1, tr, GATE_W), lambda l, r: (l, r, 0))],
        out_shape=[jax.ShapeDtypeStruct((depth, d_in, Z_W), BF16),
                   jax.ShapeDtypeStruct((depth, d_in, GATE_W), BF16)],
        compiler_params=_cparams(("parallel", "parallel"), 2 * tr * cols * 4 + 2 * tr * (Z_W + GATE_W) * 2 + 2 * tr * cols * 4),
        name="regroup_w_in",
    )(w_in)


def _in_proj_kernel(l_ref, x_ref, g_ref, w_ref, wg_ref, z_ref, gate_ref, xn_ref):
    j = pl.program_id(1)

    @pl.when(j == 0)
    def _():
        xn = _rms(x_ref[...], g_ref[0]).astype(BF16)
        xn_ref[...] = xn
        gate_ref[...] = _dot(xn, wg_ref[0])

    z_ref[...] = _dot(xn_ref[...], w_ref[0])


def in_proj(x, lidx, g, w_main, w_gate, *, tm, tn):
    n_rows = x.shape[0]
    grid = (n_rows // tm, Z_W // tn)
    vmem = 2 * tm * D_MODEL * 4 + 2 * D_MODEL * tn * 2 + 2 * tm * tn * 4 + tm * D_MODEL * 2 \
        + 2 * tm * GATE_W * 4 + 2 * D_MODEL * GATE_W * 2 + (4 << 20)
    return pl.pallas_call(
        _in_proj_kernel,
        grid_spec=pltpu.PrefetchScalarGridSpec(
            num_scalar_prefetch=1,
            grid=grid,
            in_specs=[
                pl.BlockSpec((tm, D_MODEL), lambda i, j, l: (i, 0)),
                pl.BlockSpec((1, 1, D_MODEL), lambda i, j, l: (l[0], 0, 0)),
                pl.BlockSpec((1, D_MODEL, tn), lambda i, j, l: (l[0], 0, j)),
                pl.BlockSpec((1, D_MODEL, GATE_W), lambda i, j, l: (l[0], 0, 0)),
            ],
            out_specs=[
                pl.BlockSpec((tm, tn), lambda i, j, l: (i, j)),
                pl.BlockSpec((tm, GATE_W), lambda i, j, l: (i, 0)),
            ],
            scratch_shapes=[pltpu.VMEM((tm, D_MODEL), BF16)],
        ),
        out_shape=[jax.ShapeDtypeStruct((n_rows, Z_W), F32),
                   jax.ShapeDtypeStruct((n_rows, GATE_W), F32)],
        compiler_params=_cparams(("parallel", "arbitrary"), vmem),
        name="in_proj",
    )(lidx, x, g, w_main, w_gate)


def _ffn_kernel(l_ref, x_ref, g_ref, wg_ref, wu_ref, wd_ref, fg_ref, o_ref, xn_ref):
    j = pl.program_id(1)
    nj = pl.num_programs(1)

    @pl.when(j == 0)
    def _():
        x = x_ref[...]
        xn_ref[...] = _rms(x, g_ref[0]).astype(BF16)
        o_ref[...] = x

    xn = xn_ref[...]
    gt = _dot(xn, wg_ref[0])
    up = _dot(xn, wu_ref[0])
    act = (gt * jax.nn.sigmoid(gt) * up).astype(BF16)
    o_ref[...] += _dot(act, wd_ref[0])

    @pl.when(jnp.logical_and(j == nj - 1, l_ref[0] == DEPTH - 1))
    def _():
        o_ref[...] = _rms(o_ref[...], fg_ref[...])


def ffn(x, lidx, g, w_gu, w_down, final_g, *, tm, th, x_buffers=2):
    n_rows = x.shape[0]
    nh = FF_HIDDEN // th
    grid = (n_rows // tm, nh)
    vmem = (2 + x_buffers) * tm * D_MODEL * 4 + tm * D_MODEL * 2 + 6 * D_MODEL * th * 2 + 3 * tm * th * 4 \
        + tm * D_MODEL * 4 + (4 << 20)
    x_mode = {} if x_buffers == 2 else {"pipeline_mode": pl.Buffered(x_buffers)}
    return pl.pallas_call(
        _ffn_kernel,
        grid_spec=pltpu.PrefetchScalarGridSpec(
            num_scalar_prefetch=1,
            grid=grid,
            in_specs=[
                pl.BlockSpec((tm, D_MODEL), lambda i, j, l: (i, 0), **x_mode),
                pl.BlockSpec((1, 1, D_MODEL), lambda i, j, l: (l[0], 0, 0)),
                pl.BlockSpec((1, D_MODEL, th), lambda i, j, l: (l[0], 0, j)),
                pl.BlockSpec((1, D_MODEL, th), lambda i, j, l: (l[0], 0, j + nh)),
                pl.BlockSpec((1, th, D_MODEL), lambda i, j, l: (l[0], j, 0)),
                pl.BlockSpec((1, D_MODEL), lambda i, j, l: (0, 0)),
            ],
            out_specs=pl.BlockSpec((tm, D_MODEL), lambda i, j, l: (i, 0)),
            scratch_shapes=[pltpu.VMEM((tm, D_MODEL), BF16)],
        ),
        out_shape=jax.ShapeDtypeStruct((n_rows, D_MODEL), F32),
        compiler_params=_cparams(("parallel", "arbitrary"), vmem),
        name="ffn",
    )(lidx, x, g, w_gu, w_gu, w_down, final_g)


def _mem_kv_kernel(mem_ref, g_ref, wk_ref, wv_ref, kg_ref, k_ref, v_ref):
    mn = _rms(mem_ref[0], g_ref[0]).astype(BF16)
    k = _dot(mn, wk_ref[0])
    for h in range(CA_H):
        sl = slice(h * CA_DH, (h + 1) * CA_DH)
        k_ref[0, :, sl] = _rms(k[:, sl], kg_ref[0])
    v_ref[0] = _dot(mn, wv_ref[0])


def mem_kv(mem, g, wk, wv, kg):
    nb = mem.shape[0]
    out = jax.ShapeDtypeStruct((DEPTH * nb, N_MEM, CA_W), F32)
    return pl.pallas_call(
        _mem_kv_kernel,
        grid=(DEPTH, nb),
        in_specs=[
            pl.BlockSpec((1, N_MEM, D_MODEL), lambda l, b: (b, 0, 0)),
            pl.BlockSpec((1, 1, D_MODEL), lambda l, b: (l, 0, 0)),
            pl.BlockSpec((1, D_MODEL, CA_W), lambda l, b: (l, 0, 0)),
            pl.BlockSpec((1, D_MODEL, CA_W), lambda l, b: (l, 0, 0)),
            pl.BlockSpec((1, 1, CA_DH), lambda l, b: (l, 0, 0)),
        ],
        out_specs=[pl.BlockSpec((1, N_MEM, CA_W), lambda l, b: (l * nb + b, 0, 0)),
                   pl.BlockSpec((1, N_MEM, CA_W), lambda l, b: (l * nb + b, 0, 0))],
        out_shape=[out, out],
        compiler_params=_cparams(("arbitrary", "arbitrary"), 24 << 20),
        name="mem_kv",
    )(mem, g, wk, wv, kg)


def _post_kernel(l_ref, x_ref, yml_ref, ydf_ref, yhg_ref, wout_ref, gx_ref, wq_ref, qg_ref,
                 mk_ref, mv_ref, wo_ref, o_ref):
    x1 = x_ref[...]
    x1 = x1 + _dot(yml_ref[...], wout_ref[0, 0:ML_W, :])
    x1 = x1 + _dot(ydf_ref[...], wout_ref[0, ML_W:ML_W + DF_W, :])
    x1 = x1 + _dot(yhg_ref[...], wout_ref[0, ML_W + DF_W:D_MODEL, :])
    hn = _rms(x1, gx_ref[0]).astype(BF16)
    q = _dot(hn, wq_ref[0])
    heads = []
    for h in range(CA_H):
        sl = slice(h * CA_DH, (h + 1) * CA_DH)
        qh = _rms(q[:, sl], qg_ref[0]).astype(BF16)
        s = _dot_nt(qh, mk_ref[0, :, sl].astype(BF16)) * (CA_DH ** -0.5)
        s = s - jnp.max(s, axis=-1, keepdims=True)
        p = jnp.exp(s)
        p = p / jnp.sum(p, axis=-1, keepdims=True)
        heads.append(_dot(p.astype(BF16), mv_ref[0, :, sl].astype(BF16)))
    o = jnp.concatenate(heads, axis=-1).astype(BF16)
    o_ref[...] = x1 + _dot(o, wo_ref[0])


def post_mix(x, y_ml, y_df, y_hg, lidx, w_out, gx, wq, qg, mk, mv, wo, *, tm, rows_per_batch):
    n_rows = x.shape[0]
    nb = n_rows // rows_per_batch
    tiles_per_batch = rows_per_batch // tm
    grid = (n_rows // tm,)
    w_bytes = (D_MODEL * D_MODEL + 2 * D_MODEL * CA_W) * 2
    vmem = 2 * w_bytes + 6 * tm * D_MODEL * 4 + 2 * tm * D_MODEL * 2 + 4 * N_MEM * CA_W * 4 + (6 << 20)
    row = lambda i, l: (i, 0)
    mem_map = lambda i, l: (l[0] * nb + i // tiles_per_batch, 0, 0)
    return pl.pallas_call(
        _post_kernel,
        grid_spec=pltpu.PrefetchScalarGridSpec(
            num_scalar_prefetch=1,
            grid=grid,
            in_specs=[
                pl.BlockSpec((tm, D_MODEL), row),
                pl.BlockSpec((tm, ML_W), row),
                pl.BlockSpec((tm, DF_W), row),
                pl.BlockSpec((tm, HG_W), row),
                pl.BlockSpec((1, D_MODEL, D_MODEL), lambda i, l: (l[0], 0, 0)),
                pl.BlockSpec((1, 1, D_MODEL), lambda i, l: (l[0], 0, 0)),
                pl.BlockSpec((1, D_MODEL, CA_W), lambda i, l: (l[0], 0, 0)),
                pl.BlockSpec((1, 1, CA_DH), lambda i, l: (l[0], 0, 0)),
                pl.BlockSpec((1, N_MEM, CA_W), mem_map),
                pl.BlockSpec((1, N_MEM, CA_W), mem_map),
                pl.BlockSpec((1, CA_W, D_MODEL), lambda i, l: (l[0], 0, 0)),
            ],
            out_specs=pl.BlockSpec((tm, D_MODEL), row),
        ),
        out_shape=jax.ShapeDtypeStruct((n_rows, D_MODEL), F32),
        compiler_params=_cparams(("parallel",), vmem),
        name="post_mix",
    )(lidx, x, y_ml, y_df, y_hg, w_out, gx, wq, qg, mk, mv, wo)


def _mlstm_kernel(l_ref, zq_ref, zk_ref, zv_ref, zo_ref, gate_ref, hist_ref, cw_ref, cb_ref, bif_ref,
                  og_ref, c0_ref, n0_ref, m0_ref, tri_ref,
                  y_ref, c_ref, n_ref, m_ref, ext_ref, *, tc, lc):
    c = pl.program_id(1)

    @pl.when(c == 0)
    def _():
        c_ref[...] = c0_ref[...]
        n_ref[...] = n0_ref[...]
        m_ref[...] = m0_ref[...]
        ext_ref[0:HIST, :] = hist_ref[0]

    ext_ref[HIST:HIST + tc, 0:ML_W] = zq_ref[...]
    ext_ref[HIST:HIST + tc, ML_W:2 * ML_W] = zk_ref[...]
    ext = ext_ref[...]
    acc = cb_ref[0] + cw_ref[0, CONV_W - 1:CONV_W, :] * ext[HIST:HIST + tc]
    for j in range(CONV_W - 1):
        acc = acc + cw_ref[0, j:j + 1, :] * pltpu.roll(ext, CONV_W - 1 - j, axis=0)[HIST:HIST + tc]
    tail = ext_ref[tc:tc + HIST, :]
    ext_ref[0:HIST, :] = tail
    qk = acc * jax.nn.sigmoid(acc)

    gz = gate_ref[...] + bif_ref[0]
    lf_all = pltpu.roll(_log_sigmoid(gz), GATE_W - ML_H, axis=1)
    tri = tri_ref[...]
    row_i = lax.broadcasted_iota(jnp.int32, (lc, lc), 0)
    col_i = lax.broadcasted_iota(jnp.int32, (lc, lc), 1)
    causal = row_i >= col_i
    diag = row_i == col_i
    trow = lax.broadcasted_iota(jnp.int32, (lc, GATE_W), 0)
    og = og_ref[0]

    for ci in range(tc // lc):
        r0 = ci * lc
        b_c = _dot_exact01(tri, lf_all[r0:r0 + lc, :])
        r_c = gz[r0:r0 + lc, :] - b_c
        cm = r_c
        d = 1
        while d < lc:
            cm = jnp.maximum(cm, jnp.where(trow >= d, pltpu.roll(cm, d, axis=0), NEG_BIG))
            d *= 2
        m_prev = m_ref[0]
        mx = jnp.maximum(m_prev, cm)
        m_t = b_c + mx
        w_inter_c = jnp.exp(m_prev - mx)
        emt_c = jnp.exp(-m_t)
        m_new = m_t[lc - 1:lc, :]
        b_last = b_c[lc - 1:lc, :]
        w_s_c = jnp.exp(b_last + r_c - m_new)
        decay_c = jnp.exp(b_last + m_prev - m_new)
        m_ref[0] = m_new
        for h in range(ML_H):
            sl = slice(h * ML_DH, (h + 1) * ML_DH)
            hl = slice(h, h + 1)
            q = qk[r0:r0 + lc, h * ML_DH:(h + 1) * ML_DH]
            k = qk[r0:r0 + lc, ML_W + h * ML_DH:ML_W + (h + 1) * ML_DH] * (ML_DH ** -0.5)
            v = zv_ref[r0:r0 + lc, sl]
            qb = q.astype(BF16)
            r_row = jnp.sum(jnp.where(diag, r_c[:, hl], 0.0), axis=0, keepdims=True)
            d_mat = jnp.exp(jnp.where(causal, r_row - mx[:, hl], NEG_BIG))
            w_inter = w_inter_c[:, hl]
            decay = decay_c[:, hl]
            s = _dot_nt(qb, k.astype(BF16)) * d_mat
            c_old = c_ref[0, h]
            n_old = n_ref[0, h:h + 1, :]
            num = _dot(s.astype(BF16), v.astype(BF16)) + w_inter * _dot(qb, c_old.astype(BF16))
            den = jnp.sum(s, axis=-1, keepdims=True) + w_inter * jnp.sum(q * n_old, axis=-1, keepdims=True)
            hh = num / jnp.maximum(jnp.abs(den), emt_c[:, hl])
            kw = k * w_s_c[:, hl]
            c_ref[0, h] = decay * c_old + _dot_tn(kw.astype(BF16), v.astype(BF16))
            n_ref[0, h:h + 1, :] = decay * n_old + jnp.sum(kw, axis=0, keepdims=True)
            o_gate = jax.nn.sigmoid(zo_ref[r0:r0 + lc, sl])
            y_ref[r0:r0 + lc, sl] = (_rms(hh, og) * o_gate).astype(BF16)


def mlstm_mixer(z, gates, lidx, hist0, conv_w, conv_b, b_if, out_g, c0, n0, m0, *, nb, seq, tc, lc):
    n_rows = nb * seq
    nc = seq // tc
    tri = jnp.asarray(np.tril(np.ones((lc, lc), np.float32)), BF16)
    zmap = lambda blk: (lambda b, c, l: (b * nc + c, blk))
    lmap3 = lambda b, c, l: (l[0], 0, 0)
    vmem = 2 * 5 * tc * ML_W * 4 + (HIST + tc) * 2 * ML_W * 4 + 4 * ML_H * ML_DH * ML_DH * 4 \
        + 6 * tc * 2 * ML_W * 4 + (8 << 20)
    kern = functools.partial(_mlstm_kernel, tc=tc, lc=lc)
    return pl.pallas_call(
        kern,
        grid_spec=pltpu.PrefetchScalarGridSpec(
            num_scalar_prefetch=1,
            grid=(nb, nc),
            in_specs=[
                pl.BlockSpec((tc, ML_W), zmap(ZB_MLQ)),
                pl.BlockSpec((tc, ML_W), zmap(ZB_MLK)),
                pl.BlockSpec((tc, ML_W), zmap(ZB_MLV)),
                pl.BlockSpec((tc, ML_W), zmap(ZB_MLO)),
                pl.BlockSpec((tc, GATE_W), lambda b, c, l: (b * nc + c, 0)),
                pl.BlockSpec((1, HIST, 2 * ML_W), lambda b, c, l: (b, 0, 0)),
                pl.BlockSpec((1, CONV_W, 2 * ML_W), lmap3),
                pl.BlockSpec((1, 1, 2 * ML_W), lmap3),
                pl.BlockSpec((1, 1, GATE_W), lmap3),
                pl.BlockSpec((1, 1, ML_DH), lmap3),
                pl.BlockSpec((1, ML_H, ML_DH, ML_DH), lambda b, c, l: (b, 0, 0, 0)),
                pl.BlockSpec((1, 8, ML_DH), lambda b, c, l: (b, 0, 0)),
                pl.BlockSpec((1, 1, GATE_W), lambda b, c, l: (b, 0, 0)),
                pl.BlockSpec((lc, lc), lambda b, c, l: (0, 0)),
            ],
            out_specs=[
                pl.BlockSpec((tc, ML_W), lambda b, c, l: (b * nc + c, 0)),
                pl.BlockSpec((1, ML_H, ML_DH, ML_DH), lambda b, c, l: (b, 0, 0, 0)),
                pl.BlockSpec((1, 8, ML_DH), lambda b, c, l: (b, 0, 0)),
                pl.BlockSpec((1, 1, GATE_W), lambda b, c, l: (b, 0, 0)),
            ],
            scratch_shapes=[pltpu.VMEM((HIST + tc, 2 * ML_W), F32)],
        ),
        out_shape=[
            jax.ShapeDtypeStruct((n_rows, ML_W), BF16),
            jax.ShapeDtypeStruct((nb, ML_H, ML_DH, ML_DH), F32),
            jax.ShapeDtypeStruct((nb, 8, ML_DH), F32),
            jax.ShapeDtypeStruct((nb, 1, GATE_W), F32),
        ],
        compiler_params=_cparams(("parallel", "arbitrary"), vmem),
        name="mlstm",
    )(lidx, z, z, z, z, gates, hist0, conv_w, conv_b, b_if, out_g, c0, n0, m0, tri)


def _hgrn_level_masks(lc):
    t = np.arange(lc)[:, None]
    s = np.arange(lc)[None, :]
    masks = []
    h = lc // 2
    while h >= 1:
        odd = (t % (2 * h)) >= h
        same = (t // (2 * h)) == (s // (2 * h))
        masks.append((same & odd & ((s % (2 * h)) < h)).astype(np.float32))
        h //= 2
    masks.append((t == s).astype(np.float32))
    return np.stack(masks, axis=0)


def _hgrn_boundary_rows(a, h):
    lc, width = a.shape
    if h >= 8:
        parts = [jnp.broadcast_to(a[g * 2 * h + h - 1:g * 2 * h + h, :], (2 * h, width)) for g in range(lc // (2 * h))]
        return parts[0] if len(parts) == 1 else jnp.concatenate(parts, axis=0)
    a3 = a.reshape(lc // 8, 8, width)
    sub = lax.broadcasted_iota(jnp.int32, a3.shape, 1)
    out = None
    for g in range(8 // (2 * h)):
        src = g * 2 * h + h - 1
        b = jnp.broadcast_to(a3[:, src:src + 1, :], a3.shape)
        out = b if out is None else jnp.where(sub >= g * 2 * h, b, out)
    return out.reshape(lc, width)


def _hgrn_kernel(l_ref, zq_ref, zf_ref, zi_ref, zg_ref, lbl_ref, og_ref, s0_ref, tri_ref, msk_ref,
                 y_ref, s_ref, st_ref, *, tc, lc, nlev):
    c = pl.program_id(1)
    nc = pl.num_programs(1)

    @pl.when(c == 0)
    def _():
        for h in range(HG_H):
            st_ref[h] = s0_ref[0, h].T

    logits = lbl_ref[...]
    e = jnp.exp(logits - jnp.max(logits, axis=0, keepdims=True))
    p = e / jnp.sum(e, axis=0, keepdims=True)
    drow = lax.broadcasted_iota(jnp.int32, p.shape, 0)
    lb = jnp.sum(jnp.where(jnp.logical_and(drow >= 1, drow <= l_ref[0]), p, 0.0), axis=0, keepdims=True)
    lb_fl = jnp.maximum(lb, LB_FLOOR)
    one_m = 1.0 - lb

    zf = zf_ref[...]
    ez = jnp.exp(-jnp.abs(zf))
    inv = 1.0 / (1.0 + ez)
    sig_p = jnp.where(zf >= 0.0, inv, ez * inv)
    sig_n = jnp.where(zf >= 0.0, ez * inv, inv)
    lf2 = jnp.log(lb_fl + one_m * sig_p) * LOG2E
    kk = one_m * sig_n + (lb - lb_fl)
    zq = zq_ref[...]
    qq = zq * jax.nn.sigmoid(zq) * (HG_DK ** -0.5)
    tri = tri_ref[...]
    og = og_ref[0]

    for ci in range(tc // lc):
        r0 = ci * lc
        a_all = _dot_exact01(tri, lf2[r0:r0 + lc, :])
        lev_exp = [-jnp.abs(a_all - _hgrn_boundary_rows(a_all, lc >> (lv + 1))) for lv in range(nlev)]
        for h in range(HG_H):
            sl = slice(h * HG_DK, (h + 1) * HG_DK)
            q = qq[r0:r0 + lc, sl]
            k = kk[r0:r0 + lc, sl]
            iv = zi_ref[r0:r0 + lc, sl].astype(BF16)
            a_in = a_all[:, sl]
            a_end = a_in[lc - 1:lc, :]
            st = st_ref[h]
            o = _dot_nt((q * jnp.exp2(a_in)).astype(BF16), st.astype(BF16))
            att = jnp.where(msk_ref[nlev] > 0, _dot_nt(q.astype(BF16), k.astype(BF16)), 0.0)
            for lv in range(nlev):
                xf = jnp.exp2(lev_exp[lv][:, sl])
                pm = _dot_nt((q * xf).astype(BF16), (k * xf).astype(BF16))
                att = att + jnp.where(msk_ref[lv] > 0, pm, 0.0)
            o = o + _dot(att.astype(BF16), iv)
            k_end = (k * jnp.exp2(a_end - a_in)).astype(BF16)
            st_ref[h] = st * jnp.exp2(a_end) + _dot_tn(iv, k_end)
            gsl = zg_ref[r0:r0 + lc, sl]
            y_ref[r0:r0 + lc, sl] = (_rms(o, og) * (gsl * jax.nn.sigmoid(gsl))).astype(BF16)

    @pl.when(c == nc - 1)
    def _():
        for h in range(HG_H):
            s_ref[0, h] = st_ref[h].T


def hgrn_mixer(z, lidx, lb_logits, out_g, s0, *, nb, seq, tc, lc):
    n_rows = nb * seq
    nc = seq // tc
    msk_np = _hgrn_level_masks(lc)
    nlev = msk_np.shape[0] - 1
    tri = jnp.asarray(np.tril(np.ones((lc, lc), np.float32)), BF16)
    msk = jnp.asarray(msk_np, F32)
    zmap = lambda blk: (lambda b, c, l: (b * nc + c, blk))
    vmem = 2 * 5 * tc * HG_W * 4 + 5 * HG_H * HG_DK * HG_DV * 4 + 8 * tc * HG_W * 4 + (8 << 20)
    kern = functools.partial(_hgrn_kernel, tc=tc, lc=lc, nlev=nlev)
    return pl.pallas_call(
        kern,
        grid_spec=pltpu.PrefetchScalarGridSpec(
            num_scalar_prefetch=1,
            grid=(nb, nc),
            in_specs=[
                pl.BlockSpec((tc, HG_W), zmap(ZB_HGQ)),
                pl.BlockSpec((tc, HG_W), zmap(ZB_HGF)),
                pl.BlockSpec((tc, HG_W), zmap(ZB_HGI)),
                pl.BlockSpec((tc, HG_W), zmap(ZB_HGG)),
                pl.BlockSpec((DEPTH, HG_W), lambda b, c, l: (0, 0)),
                pl.BlockSpec((1, 1, HG_DV), lambda b, c, l: (l[0], 0, 0)),
                pl.BlockSpec((1, HG_H, HG_DK, HG_DV), lambda b, c, l: (b, 0, 0, 0)),
                pl.BlockSpec((lc, lc), lambda b, c, l: (0, 0)),
                pl.BlockSpec(msk_np.shape, lambda b, c, l: (0, 0, 0)),
            ],
            out_specs=[
                pl.BlockSpec((tc, HG_W), lambda b, c, l: (b * nc + c, 0)),
                pl.BlockSpec((1, HG_H, HG_DK, HG_DV), lambda b, c, l: (b, 0, 0, 0)),
            ],
            scratch_shapes=[pltpu.VMEM((HG_H, HG_DV, HG_DK), F32)],
        ),
        out_shape=[
            jax.ShapeDtypeStruct((n_rows, HG_W), BF16),
            jax.ShapeDtypeStruct((nb, HG_H, HG_DK, HG_DV), F32),
        ],
        compiler_params=_cparams(("parallel", "arbitrary"), vmem),
        name="hgrn",
    )(lidx, z, z, z, z, lb_logits, out_g, s0, tri, msk)


def _qk_prep_kernel(l_ref, zq_ref, zk_ref, zv_ref, qg_ref, kg_ref, grp_ref, qn_ref, kn_ref, knb_ref, vb_ref):
    grp = grp_ref[...]
    qn_ref[...] = (_group_norm64(zq_ref[...], qg_ref[0], grp) * (DF_DH ** -0.5)).astype(BF16)
    kn = _group_norm64(zk_ref[...], kg_ref[0], grp)
    kn_ref[...] = kn
    knb_ref[...] = kn.astype(BF16)
    vb_ref[...] = zv_ref[...].astype(BF16)


def qk_prep(z, lidx, qg, kg, *, tm):
    n_rows = z.shape[0]
    lane = np.arange(DF_W)
    grp = jnp.asarray(((lane[:, None] // DF_DH) == (lane[None, :] // DF_DH)).astype(np.float32) / DF_DH, BF16)
    zmap = lambda blk: (lambda i, l: (i, blk))
    row = lambda i, l: (i, 0)
    return pl.pallas_call(
        _qk_prep_kernel,
        grid_spec=pltpu.PrefetchScalarGridSpec(
            num_scalar_prefetch=1,
            grid=(n_rows // tm,),
            in_specs=[
                pl.BlockSpec((tm, DF_W), zmap(ZB_DFQ)),
                pl.BlockSpec((tm, DF_W), zmap(ZB_DFK)),
                pl.BlockSpec((tm, DF_W), zmap(ZB_DFV)),
                pl.BlockSpec((1, 1, DF_W), lambda i, l: (l[0], 0, 0)),
                pl.BlockSpec((1, 1, DF_W), lambda i, l: (l[0], 0, 0)),
                pl.BlockSpec((DF_W, DF_W), lambda i, l: (0, 0)),
            ],
            out_specs=[pl.BlockSpec((tm, DF_W), row)] * 4,
        ),
        out_shape=[
            jax.ShapeDtypeStruct((n_rows, DF_W), BF16),
            jax.ShapeDtypeStruct((n_rows, DF_W), F32),
            jax.ShapeDtypeStruct((n_rows, DF_W), BF16),
            jax.ShapeDtypeStruct((n_rows, DF_W), BF16),
        ],
        compiler_params=_cparams(("parallel",), 32 << 20),
        name="qk_prep",
    )(lidx, z, z, z, qg, kg, grp)


def _group_norm64(x, g, grp):
    x2 = x * x
    hi = x2.astype(BF16)
    lo = (x2 - hi.astype(F32)).astype(BF16)
    ms = _dot(hi, grp) + _dot(lo, grp)
    return x * lax.rsqrt(ms + EPS) * g


BIAS_W = 128


def _qk_prep_t_kernel(l_ref, zq_ref, zk_ref, zv_ref, qg_ref, kg_ref, grp_ref, pk_in, pv_in,
                      qt_ref, ke_ref, vt_ref, pk_ref, pv_ref, *, tm, seq):
    del pk_in, pv_in
    i = pl.program_id(0)
    grp = grp_ref[...]
    qn = _group_norm64(zq_ref[...], qg_ref[0], grp) * (DF_DH ** -0.5)
    kn = _group_norm64(zk_ref[...], kg_ref[0], grp)
    zv = zv_ref[...]
    pos = lax.rem(i * tm, seq) + lax.broadcasted_iota(jnp.int32, (tm, BIAS_W), 0)
    lane = lax.broadcasted_iota(jnp.int32, (tm, BIAS_W), 1)
    hi_part = (pos >> CHUNK_SHIFT).astype(F32) * float(CHUNK)
    lo_part = (pos & (CHUNK - 1)).astype(F32)
    base = jnp.where(lane == 0, hi_part, jnp.where(lane == 1, lo_part, jnp.where(lane == 2, float(CHUNK), 0.0)))
    for h in range(DF_H):
        sl = slice(h * DF_DV, (h + 1) * DF_DV)
        slope = 2.0 ** (-8.0 * (h + 1) / DF_H)
        c0 = h * (DF_DV + BIAS_W)
        ke_ref[:, c0:c0 + DF_DV] = kn[:, sl].astype(BF16)
        ke_ref[:, c0 + DF_DV:c0 + DF_DV + BIAS_W] = (base * slope).astype(BF16)
        qt_ref[h, 0] = qn[:, sl].T.astype(BF16)
        vt_ref[h, 0] = zv[:, sl].T.astype(BF16)
        rows = pl.ds(h, tm, stride=DF_H)
        pk_ref[rows, :] = kn[:, sl]
        pv_ref[rows, :] = zv[:, sl]


def qk_prep_t(z, lidx, qg, kg, pk_buf, pv_buf, *, tm, seq):
    n_rows = z.shape[0]
    nt = n_rows // tm
    smap = lambda i, l: (l[0] * nt + i, 0)
    any_spec = pl.BlockSpec(memory_space=pl.ANY)
    lane = np.arange(DF_W)
    grp = jnp.asarray(((lane[:, None] // DF_DH) == (lane[None, :] // DF_DH)).astype(np.float32) / DF_DH, BF16)
    zmap = lambda blk: (lambda i, l: (i, blk))
    row = lambda i, l: (i, 0)
    tmap = lambda i, l: (0, i, 0, 0)
    kern = functools.partial(_qk_prep_t_kernel, tm=tm, seq=seq)
    return pl.pallas_call(
        kern,
        grid_spec=pltpu.PrefetchScalarGridSpec(
            num_scalar_prefetch=1,
            grid=(nt,),
            in_specs=[
                pl.BlockSpec((tm, DF_W), zmap(ZB_DFQ)),
                pl.BlockSpec((tm, DF_W), zmap(ZB_DFK)),
                pl.BlockSpec((tm, DF_W), zmap(ZB_DFV)),
                pl.BlockSpec((1, 1, DF_W), lambda i, l: (l[0], 0, 0)),
                pl.BlockSpec((1, 1, DF_W), lambda i, l: (l[0], 0, 0)),
                pl.BlockSpec((DF_W, DF_W), lambda i, l: (0, 0)),
                any_spec,
                any_spec,
            ],
            out_specs=[
                pl.BlockSpec((DF_H, 1, DF_DV, tm), tmap),
                pl.BlockSpec((tm, DF_H * (DF_DV + BIAS_W)), row),
                pl.BlockSpec((DF_H, 1, DF_DV, tm), tmap),
                pl.BlockSpec((tm * DF_H, DF_DV), smap),
                pl.BlockSpec((tm * DF_H, DF_DV), smap),
            ],
        ),
        out_shape=[
            jax.ShapeDtypeStruct((DF_H, nt, DF_DV, tm), BF16),
            jax.ShapeDtypeStruct((n_rows, DF_H * (DF_DV + BIAS_W)), BF16),
            jax.ShapeDtypeStruct((DF_H, nt, DF_DV, tm), BF16),
            jax.ShapeDtypeStruct(pk_buf.shape, F32),
            jax.ShapeDtypeStruct(pv_buf.shape, F32),
        ],
        input_output_aliases={7: 3, 8: 4},
        compiler_params=_cparams(("parallel",), 40 << 20),
        name="qk_prep_t",
    )(lidx, z, z, z, qg, kg, grp, pk_buf, pv_buf)


ONES_ROWS = 16


def _flash_t_kernel(l_ref, qt_ref, ke_ref, vt_ref, lam_ref, ogt_ref, y_ref, m_ref, acc_ref, sa_ref, sb_ref,
                    own_ref, *, tq):
    h = pl.program_id(1)
    qi = pl.program_id(2)
    tk = tq
    slope = jnp.exp2(jnp.full((1, 1), -8.0 / DF_H, F32) * (h + 1).astype(F32))
    qt = qt_ref[0, 0]
    row = lax.broadcasted_iota(jnp.int32, (DF_DV, tq), 0)
    zero = jnp.zeros_like(qt)
    q2t = jnp.concatenate([jnp.where(row < DF_DH, qt, zero), jnp.where(row >= DF_DH, qt, zero)], axis=1)
    brow = lax.broadcasted_iota(jnp.int32, (BIAS_W, 2 * tq), 0)
    a0 = ((qi * tq) >> CHUNK_SHIFT).astype(F32)
    extra = jnp.where(brow < 2, 1.0, jnp.where(brow == 2, -a0, 0.0)).astype(BF16)
    q2e = jnp.concatenate([q2t, extra], axis=0)
    ones = jnp.ones((ONES_ROWS, tk), BF16)

    m_ref[...] = jnp.full(m_ref.shape, NEG_BIG, F32)
    acc_ref[...] = jnp.zeros(acc_ref.shape, F32)

    def scores_into(dst_ref, kj):
        r0 = pl.multiple_of(kj * tk, tk)
        dst_ref[...] = _dot(ke_ref[pl.ds(r0, tk), :], q2e)

    def update(s, kj):
        m_old = m_ref[...]
        m_new = jnp.maximum(m_old, jnp.max(s, axis=0, keepdims=True))
        alpha = jnp.exp(m_old - m_new)
        p = jnp.exp(s - m_new).astype(BF16)
        vt_ext = jnp.concatenate([vt_ref[0, kj], ones], axis=0)
        acc_ref[...] = alpha * acc_ref[...] + _dot(vt_ext, p)
        m_ref[...] = m_new

    @pl.when(qi == 0)
    def _():
        rel_k = lax.broadcasted_iota(jnp.int32, (tq, 2 * tq), 0)
        rel_q = lax.broadcasted_iota(jnp.int32, (tq, 2 * tq), 1)
        rel_q = jnp.where(rel_q >= tq, rel_q - tq, rel_q)
        ahead = rel_k - rel_q
        fix = jnp.where(ahead > 0, (-2.0 * slope) * ahead.astype(F32), 0.0)
        visible = (rel_k >> CHUNK_SHIFT) <= (rel_q >> CHUNK_SHIFT)
        own_ref[...] = jnp.where(visible, fix, NEG_BIG)

    def own_block(s):
        update(s + own_ref[...], qi)

    scores_into(sa_ref, 0)

    def block_pair(t, carry):
        k0 = 2 * t
        scores_into(sb_ref, k0 + 1)
        update(sa_ref[...], k0)
        scores_into(sa_ref, k0 + 2)
        update(sb_ref[...], k0 + 1)
        return carry

    lax.fori_loop(0, qi >> 1, block_pair, 0)

    @pl.when((qi & 1) == 1)
    def _():
        scores_into(sb_ref, qi)
        update(sa_ref[...], qi - 1)
        own_block(sb_ref[...])

    @pl.when((qi & 1) == 0)
    def _():
        own_block(sa_ref[...])

    lam_p = lam_ref[0]
    lam_init = 0.8 - 0.6 * jnp.exp(jnp.full((1, 1), -0.3, F32) * l_ref[0].astype(F32))
    lam = (jnp.exp(jnp.sum(lam_p[0:1] * lam_p[1:2], axis=-1, keepdims=True))
           - jnp.exp(jnp.sum(lam_p[2:3] * lam_p[3:4], axis=-1, keepdims=True)) + lam_init)
    acc = acc_ref[...]
    o_all = acc[0:DF_DV] / acc[DF_DV:DF_DV + 1]
    o = o_all[:, 0:tq] - lam * o_all[:, tq:2 * tq]
    ms = jnp.mean(o * o, axis=0, keepdims=True)
    y = o * lax.rsqrt(ms + EPS) * ogt_ref[0] * (1.0 - lam_init)
    y_ref[...] = y.T.astype(BF16)


def diff_flash_t(qt, ke, vt, lidx, lam_p, out_g_t, *, nb, seq, tq):
    assert tq % CHUNK == 0
    nq = seq // tq
    n_rows = nb * seq
    kew = DF_DV + BIAS_W
    kern = functools.partial(_flash_t_kernel, tq=tq)
    vmem = 2 * seq * kew * 2 + 2 * seq * DF_DV * 2 + 10 * tq * 2 * tq * 4 + (8 << 20)
    return pl.pallas_call(
        kern,
        grid_spec=pltpu.PrefetchScalarGridSpec(
            num_scalar_prefetch=1,
            grid=(nb, DF_H, nq),
            in_specs=[
                pl.BlockSpec((1, 1, DF_DV, tq), lambda b, h, i, l: (h, b * nq + i, 0, 0)),
                pl.BlockSpec((seq, kew), lambda b, h, i, l: (b, h)),
                pl.BlockSpec((1, nq, DF_DV, tq), lambda b, h, i, l: (h, b, 0, 0)),
                pl.BlockSpec((1, 4, DF_DH), lambda b, h, i, l: (l[0], 0, 0)),
                pl.BlockSpec((1, DF_DV, 1), lambda b, h, i, l: (l[0], 0, 0)),
            ],
            out_specs=pl.BlockSpec((tq, DF_DV), lambda b, h, i, l: (b * nq + i, h)),
            scratch_shapes=[pltpu.VMEM((1, 2 * tq), F32), pltpu.VMEM((DF_DV + ONES_ROWS, 2 * tq), F32),
                            pltpu.VMEM((tq, 2 * tq), F32), pltpu.VMEM((tq, 2 * tq), F32),
                            pltpu.VMEM((tq, 2 * tq), F32)],
        ),
        out_shape=jax.ShapeDtypeStruct((n_rows, DF_W), BF16),
        compiler_params=_cparams(("parallel", "parallel", "arbitrary"), vmem),
        name="diff_flash_t",
    )(lidx, qt, ke, vt, lam_p, out_g_t)


def _flash_sample_kernel(l_ref, q_ref, km_ref, vm_ref, kx_ref, vx_ref, lam_ref, og_ref, y_ref,
                         m_ref, l_sum_ref, acc_ref, *, tq, tk, past):
    h = pl.program_id(1)
    slope = jnp.exp2(jnp.full((1, 1), -8.0 / DF_H, F32) * (h + 1).astype(F32))
    q = q_ref[...]
    lane = lax.broadcasted_iota(jnp.int32, (tq, DF_DV), 1)
    zero = jnp.zeros_like(q)
    q2 = jnp.concatenate([jnp.where(lane < DF_DH, q, zero), jnp.where(lane >= DF_DH, q, zero)], axis=0)
    q0 = past

    m_ref[...] = jnp.full(m_ref.shape, NEG_BIG, F32)
    l_sum_ref[...] = jnp.zeros(l_sum_ref.shape, F32)
    acc_ref[...] = jnp.zeros(acc_ref.shape, F32)

    def update(s, v):
        m_old = m_ref[...]
        m_new = jnp.maximum(m_old, jnp.max(s, axis=-1, keepdims=True))
        alpha = jnp.exp(m_old - m_new)
        p = jnp.exp(s - m_new)
        l_sum_ref[...] = alpha * l_sum_ref[...] + jnp.sum(p, axis=-1, keepdims=True)
        acc_ref[...] = alpha * acc_ref[...] + _dot(p.astype(BF16), v)
        m_ref[...] = m_new

    def fast_block(kj, carry):
        r0 = kj * tk
        rows = pl.ds(r0 * DF_H + h, tk, stride=DF_H)
        k = km_ref[rows, :].astype(BF16)
        v = vm_ref[rows, :].astype(BF16)
        kpos = r0 + lax.broadcasted_iota(jnp.int32, (1, tk), 1)
        s = _dot_nt(q2, k) + slope * (kpos - q0).astype(F32)
        update(s, v)
        return carry

    lax.fori_loop(0, past // tk, fast_block, 0)

    k = kx_ref[...].astype(BF16)
    v = vx_ref[...].astype(BF16)
    rel_k = lax.broadcasted_iota(jnp.int32, (2 * tq, tq), 1)
    rel_q = lax.broadcasted_iota(jnp.int32, (2 * tq, tq), 0)
    rel_q = jnp.where(rel_q >= tq, rel_q - tq, rel_q)
    dist = jnp.abs(rel_q - rel_k).astype(F32)
    s = _dot_nt(q2, k) + slope * (rel_q.astype(F32) - dist)
    visible = ((q0 + rel_k) >> CHUNK_SHIFT) <= ((q0 + rel_q) >> CHUNK_SHIFT)
    update(jnp.where(visible, s, NEG_BIG), v)

    lam_p = lam_ref[0]
    lam_init = 0.8 - 0.6 * jnp.exp(jnp.full((1, 1), -0.3, F32) * l_ref[0].astype(F32))
    lam = (jnp.exp(jnp.sum(lam_p[0:1] * lam_p[1:2], axis=-1, keepdims=True))
           - jnp.exp(jnp.sum(lam_p[2:3] * lam_p[3:4], axis=-1, keepdims=True)) + lam_init)
    o_all = acc_ref[...] / l_sum_ref[...]
    o = o_all[0:tq] - lam * o_all[tq:2 * tq]
    y_ref[...] = (_rms(o, og_ref[0]) * (1.0 - lam_init)).astype(BF16)


def diff_flash_sample(qn, k_past, v_past, k_new, v_new, lidx, lam_p, out_g, *, nb, seq, past, tk):
    n_rows = nb * seq
    kern = functools.partial(_flash_sample_kernel, tq=seq, tk=tk, past=past)
    vmem = 4 * past * DF_H * DF_DV * 4 + (8 << 20)
    qmap = lambda b, h, l: (b, h)
    cmap = lambda b, h, l: (l[0] * nb + b, 0)
    return pl.pallas_call(
        kern,
        grid_spec=pltpu.PrefetchScalarGridSpec(
            num_scalar_prefetch=1,
            grid=(nb, DF_H),
            in_specs=[
                pl.BlockSpec((seq, DF_DV), qmap),
                pl.BlockSpec((past * DF_H, DF_DV), cmap),
                pl.BlockSpec((past * DF_H, DF_DV), cmap),
                pl.BlockSpec((seq, DF_DV), qmap),
                pl.BlockSpec((seq, DF_DV), qmap),
                pl.BlockSpec((1, 4, DF_DH), lambda b, h, l: (l[0], 0, 0)),
                pl.BlockSpec((1, 1, DF_DV), lambda b, h, l: (l[0], 0, 0)),
            ],
            out_specs=pl.BlockSpec((seq, DF_DV), qmap),
            scratch_shapes=[pltpu.VMEM((2 * seq, 1), F32), pltpu.VMEM((2 * seq, 1), F32),
                            pltpu.VMEM((2 * seq, DF_DV), F32)],
        ),
        out_shape=jax.ShapeDtypeStruct((n_rows, DF_W), BF16),
        compiler_params=_cparams(("parallel", "arbitrary"), vmem),
        name="diff_flash_sample",
    )(lidx, qn, k_past, v_past, k_new, v_new, lam_p, out_g)


def _trunk_layer(x, lidx, w, st, cfg):
    nb, seq, past = cfg["nb"], cfg["seq"], cfg["past"]
    z, gates = in_proj(x, lidx, w["norm_mix"], w["w_main"], w["w_gate"], tm=cfg["tm_in"], tn=cfg["tn_in"])
    y_ml, c_new, n_new, m_new = mlstm_mixer(
        z, gates, lidx, st["hist0"], w["conv_w"], w["conv_b"], w["b_if"], w["ml_og"],
        st["c0"], st["n0"], st["m0"], nb=nb, seq=seq, tc=cfg["tc"], lc=cfg["lc_ml"])
    z3 = z.reshape(nb, seq, Z_W)
    new = {}
    if cfg["prompt"]:
        qt, ke, vt, new["pk_buf"], new["pv_buf"] = qk_prep_t(
            z, lidx, w["df_qg"], w["df_kg"], st["pk_buf"], st["pv_buf"], tm=cfg["tq"], seq=seq)
        y_df = diff_flash_t(qt, ke, vt, lidx, w["df_lam"], w["df_og_t"], nb=nb, seq=seq, tq=cfg["tq"])
    else:
        qn, kn, kn_b, v_b = qk_prep(z, lidx, w["df_qg"], w["df_kg"], tm=cfg["tm_prep"])
        y_df = diff_flash_sample(qn, st["past_k"], st["past_v"], kn_b, v_b, lidx, w["df_lam"], w["df_og"],
                                 nb=nb, seq=seq, past=past, tk=cfg["tk_past"])
        new["attn_k"] = kn.reshape(nb, seq, DF_H, 2 * DF_DH)
        new["attn_v"] = z3[:, :, ZB_DFV * DF_W:(ZB_DFV + 1) * DF_W].reshape(nb, seq, DF_H, DF_DV)
    y_hg, s_new = hgrn_mixer(z, lidx, w["hg_lbl"], w["hg_og"], st["s0"], nb=nb, seq=seq, tc=cfg["tc"],
                             lc=cfg["lc_hg"])
    x = post_mix(x, y_ml, y_df, y_hg, lidx, w["w_out"], w["norm_cross"], w["wq"], w["ca_qg"],
                 st["mk"], st["mv"], w["wo"], tm=cfg["tm_post"], rows_per_batch=seq)
    x = ffn(x, lidx, w["norm_ffn"], w["w_gu"], w["w_down"], w["final_norm"], tm=cfg["tm_ffn"], th=cfg["th"],
            x_buffers=cfg["ffn_x_buffers"])
    new.update({
        "conv": z3[:, seq - (CONV_W - 1):, 0:2 * ML_W],
        "C": c_new,
        "n": n_new[:, :ML_H, :],
        "m": m_new[:, 0, :ML_H],
        "S": s_new,
    })
    return x, new


PROMPT_CFG = dict(prompt=True, tm_in=1024, tn_in=1536, tc=256, lc_ml=128, lc_hg=128, tm_prep=512, tq=512,
                  tm_post=512, tm_ffn=1024, th=512, ffn_x_buffers=2)
SAMPLE_CFG = dict(prompt=False, tm_in=128, tn_in=768, tc=16, lc_ml=16, lc_hg=16, tm_prep=128, tk_past=2048,
                  tm_post=16, tm_ffn=128, th=512, ffn_x_buffers=2)


def kernel(x_prompt, x_sample, mem_prompt, cache_attn_k, cache_attn_v, cache_mem_k, cache_mem_v, state_mlstm_conv, state_mlstm_C, state_mlstm_n, state_mlstm_m, state_hgrn_S, norm_mix, w_in, mlstm_conv_w, mlstm_conv_b, mlstm_b_i, mlstm_b_f, mlstm_out_norm, diff_q_norm, diff_k_norm, diff_lambda, diff_out_norm, hgrn_lb_logits, hgrn_out_norm, w_out, norm_cross, norm_mem, cross_wq, cross_wk, cross_wv, cross_q_norm, cross_k_norm, cross_wo, norm_ffn, ffn_w_gate_up, ffn_w_down, final_norm):
    bp, tp = x_prompt.shape[:2]
    bs, ts = x_sample.shape[:2]
    past = cache_attn_k.shape[2]
    depth = w_in.shape[0]
    assert depth == DEPTH and x_prompt.shape[2] == D_MODEL

    w_main, w_gate = regroup_w_in(w_in, tr=256)
    r3 = lambda a: a.reshape(depth, 1, a.shape[-1])
    w = {
        "norm_mix": r3(norm_mix),
        "w_main": w_main,
        "w_gate": w_gate,
        "conv_w": mlstm_conv_w,
        "conv_b": r3(mlstm_conv_b),
        "b_if": r3(jnp.pad(jnp.concatenate([mlstm_b_i, mlstm_b_f], axis=-1), ((0, 0), (0, GATE_W - 2 * ML_H)))),
        "ml_og": r3(mlstm_out_norm),
        "df_qg": r3(jnp.tile(diff_q_norm, (1, DF_W // DF_DH))),
        "df_kg": r3(jnp.tile(diff_k_norm, (1, DF_W // DF_DH))),
        "df_lam": diff_lambda,
        "df_og": r3(diff_out_norm),
        "df_og_t": diff_out_norm.reshape(depth, DF_DV, 1),
        "hg_lbl": hgrn_lb_logits,
        "hg_og": r3(hgrn_out_norm),
        "w_out": w_out.astype(BF16),
        "norm_cross": r3(norm_cross),
        "wq": cross_wq.astype(BF16),
        "ca_qg": r3(cross_q_norm),
        "wo": cross_wo.astype(BF16),
        "norm_ffn": r3(norm_ffn),
        "w_gu": ffn_w_gate_up.astype(BF16),
        "w_down": ffn_w_down.astype(BF16),
        "final_norm": final_norm.reshape(1, D_MODEL),
    }

    mk_p, mv_p = mem_kv(mem_prompt, r3(norm_mem), cross_wk.astype(BF16), cross_wv.astype(BF16), r3(cross_k_norm))

    def pad_hist(conv):
        pad = [(0, 0)] * (conv.ndim - 2) + [(HIST - (CONV_W - 1), 0), (0, 0)]
        return jnp.pad(conv, pad)

    st_p = {
        "hist0": jnp.zeros((bp, HIST, 2 * ML_W), F32),
        "c0": jnp.zeros((bp, ML_H, ML_DH, ML_DH), F32),
        "n0": jnp.zeros((bp, 8, ML_DH), F32),
        "m0": jnp.zeros((bp, 1, GATE_W), F32),
        "s0": jnp.zeros((bp, HG_H, HG_DK, HG_DV), F32),
        "mk": mk_p, "mv": mv_p,
    }
    hist_s = pad_hist(state_mlstm_conv)
    n_s = jnp.pad(state_mlstm_n, ((0, 0), (0, 0), (0, 8 - ML_H), (0, 0)))
    m_s = jnp.pad(state_mlstm_m, ((0, 0), (0, 0), (0, GATE_W - ML_H))).reshape(depth, bs, 1, GATE_W)
    past_k = cache_attn_k.reshape(depth * bs * past * DF_H, DF_DV)
    past_v = cache_attn_v.reshape(depth * bs * past * DF_H, DF_DV)
    mk_s = cache_mem_k.reshape(depth * bs, N_MEM, CA_W)
    mv_s = cache_mem_v.reshape(depth * bs, N_MEM, CA_W)

    cfg_p = dict(PROMPT_CFG, nb=bp, seq=tp, past=0)
    cfg_s = dict(SAMPLE_CFG, nb=bs, seq=ts, past=past)

    def layer(carry, xs):
        xp, xsm, pk_buf, pv_buf = carry
        l, hist_l, c_l, n_l, m_l, s_l = xs
        lidx = l.reshape(1).astype(jnp.int32)
        xp, new_p = _trunk_layer(xp, lidx, w, dict(st_p, pk_buf=pk_buf, pv_buf=pv_buf), cfg_p)
        pk_buf, pv_buf = new_p.pop("pk_buf"), new_p.pop("pv_buf")
        st_s = {"hist0": hist_l, "c0": c_l, "n0": n_l, "m0": m_l, "s0": s_l,
                "past_k": past_k, "past_v": past_v, "mk": mk_s, "mv": mv_s}
        xsm, new_s = _trunk_layer(xsm, lidx, w, st_s, cfg_s)
        return (xp, xsm, pk_buf, pv_buf), (new_p, new_s)

    xs = (jnp.arange(depth, dtype=jnp.int32), hist_s, state_mlstm_C, n_s, m_s, state_hgrn_S)
    kv_rows = depth * bp * tp * DF_H
    init = (x_prompt.reshape(bp * tp, D_MODEL), x_sample.reshape(bs * ts, D_MODEL),
            lax.empty((kv_rows, DF_DV), F32), lax.empty((kv_rows, DF_DV), F32))
    (xp, xsm, pk_buf, pv_buf), (new_p, new_s) = lax.scan(layer, init, xs)

    y_prompt = xp.reshape(bp, tp, D_MODEL)
    y_sample = xsm.reshape(bs, ts, D_MODEL)
    p_mem_k = mk_p.reshape(depth, bp, N_MEM, CA_H, CA_DH)
    p_mem_v = mv_p.reshape(depth, bp, N_MEM, CA_H, CA_DH)
    p_attn_k = pk_buf.reshape(depth, bp, tp, DF_H, 2 * DF_DH)
    p_attn_v = pv_buf.reshape(depth, bp, tp, DF_H, DF_DV)
    return (y_prompt, y_sample,
            p_attn_k, p_attn_v, p_mem_k, p_mem_v, new_p["conv"], new_p["C"], new_p["n"],
            new_p["m"], new_p["S"],
            new_s["attn_k"], new_s["attn_v"], new_s["conv"], new_s["C"], new_s["n"], new_s["m"], new_s["S"])
```

```python
import functools
import math

import numpy as np
import jax
import jax.numpy as jnp
from jax import lax
from jax.experimental import pallas as pl
from jax.experimental.pallas import tpu as pltpu

F32 = jnp.float32
BF16 = jnp.bfloat16

D_MODEL = 2048
DEPTH = 4
CHUNK = 64
CHUNK_SHIFT = 6
ML_DH = 128
ML_W = 768
ML_H = 6
CONV_W = 4
DF_DH = 64
DF_DV = 128
DF_W = 512
DF_H = 4
HG_DK = 128
HG_DV = 128
HG_W = 768
HG_H = 6
CA_H = 4
CA_DH = 128
CA_W = 512
N_MEM = 256
FF_HIDDEN = 5632
EPS = 1e-6
NEG_BIG = -1e30
LB_FLOOR = 1e-30
LOG2E = math.log2(math.e)

Z_W = 2 * ML_W + 2 * ML_W + 3 * DF_W + 4 * HG_W
GATE_W = 128
ZB_MLQ, ZB_MLK, ZB_MLV, ZB_MLO = 0, 1, 2, 3
ZB_HGQ, ZB_HGF, ZB_HGI, ZB_HGG = 6, 7, 8, 9
ZB_DFQ, ZB_DFK, ZB_DFV = 6, 7, 8

V7X_VMEM_BYTES = 64 * 1024 * 1024
V7X_VMEM_CAP = 58 * 1024 * 1024
VMEM_SLACK = 6 * 1024 * 1024
HIST = 8

NT_DIMS = (((1,), (1,)), ((), ()))
TN_DIMS = (((0,), (0,)), ((), ()))


def _vmem_limit(nbytes):
    return int(min(V7X_VMEM_CAP, max(32 * 1024 * 1024, nbytes + VMEM_SLACK)))


def _cparams(sem, vmem_bytes):
    return pltpu.CompilerParams(dimension_semantics=sem, vmem_limit_bytes=_vmem_limit(vmem_bytes))


def _rms(x, g):
    ms = jnp.mean(x * x, axis=-1, keepdims=True)
    return x * lax.rsqrt(ms + EPS) * g


def _dot(a, b):
    return jnp.dot(a, b, preferred_element_type=F32)


def _dot_nt(a, b):
    return lax.dot_general(a, b, NT_DIMS, preferred_element_type=F32)


def _dot_tn(a, b):
    return lax.dot_general(a, b, TN_DIMS, preferred_element_type=F32)


def _log_sigmoid(x):
    return jnp.minimum(x, 0.0) - jnp.log1p(jnp.exp(-jnp.abs(x)))


def _split3(x):
    hi = x.astype(BF16)
    r1 = x - hi.astype(F32)
    mid = r1.astype(BF16)
    lo = (r1 - mid.astype(F32)).astype(BF16)
    return hi, mid, lo


def _dot_exact01(m01, x):
    hi, mid, lo = _split3(x)
    return _dot(m01, hi) + _dot(m01, mid) + _dot(m01, lo)


def _in_proj_kernel(l_ref, x_ref, g_ref, w_ref, wg_ref, z_ref, gate_ref, xn_ref):
    j = pl.program_id(1)

    @pl.when(j == 0)
    def _():
        xn = _rms(x_ref[...], g_ref[0]).astype(BF16)
        xn_ref[...] = xn
        gate_ref[...] = _dot(xn, wg_ref[0])

    z_ref[...] = _dot(xn_ref[...], w_ref[0])


def in_proj(x, lidx, g, w_main, w_gate, *, tm, tn):
    n_rows = x.shape[0]
    grid = (n_rows // tm, Z_W // tn)
    vmem = 2 * tm * D_MODEL * 4 + 2 * D_MODEL * tn * 2 + 2 * tm * tn * 4 + tm * D_MODEL * 2 \
        + 2 * tm * GATE_W * 4 + 2 * D_MODEL * GATE_W * 2 + (4 << 20)
    return pl.pallas_call(
        _in_proj_kernel,
        grid_spec=pltpu.PrefetchScalarGridSpec(
            num_scalar_prefetch=1,
            grid=grid,
            in_specs=[
                pl.BlockSpec((tm, D_MODEL), lambda i, j, l: (i, 0)),
                pl.BlockSpec((1, 1, D_MODEL), lambda i, j, l: (l[0], 0, 0)),
                pl.BlockSpec((1, D_MODEL, tn), lambda i, j, l: (l[0], 0, j)),
                pl.BlockSpec((1, D_MODEL, GATE_W), lambda i, j, l: (l[0], 0, 0)),
            ],
            out_specs=[
                pl.BlockSpec((tm, tn), lambda i, j, l: (i, j)),
                pl.BlockSpec((tm, GATE_W), lambda i, j, l: (i, 0)),
            ],
            scratch_shapes=[pltpu.VMEM((tm, D_MODEL), BF16)],
        ),
        out_shape=[jax.ShapeDtypeStruct((n_rows, Z_W), F32),
                   jax.ShapeDtypeStruct((n_rows, GATE_W), F32)],
        compiler_params=_cparams(("parallel", "arbitrary"), vmem),
        name="in_proj",
    )(lidx, x, g, w_main, w_gate)


def _ffn_kernel(l_ref, x_ref, g_ref, wg_ref, wu_ref, wd_ref, fg_ref, o_ref, xn_ref):
    j = pl.program_id(1)
    nj = pl.num_programs(1)

    @pl.when(j == 0)
    def _():
        x = x_ref[...]
        xn_ref[...] = _rms(x, g_ref[0]).astype(BF16)
        o_ref[...] = x

    xn = xn_ref[...]
    gt = _dot(xn, wg_ref[0])
    up = _dot(xn, wu_ref[0])
    act = (gt * jax.nn.sigmoid(gt) * up).astype(BF16)
    o_ref[...] += _dot(act, wd_ref[0])

    @pl.when(jnp.logical_and(j == nj - 1, l_ref[0] == DEPTH - 1))
    def _():
        o_ref[...] = _rms(o_ref[...], fg_ref[...])


def ffn(x, lidx, g, w_gu, w_down, final_g, *, tm, th, x_buffers=2):
    n_rows = x.shape[0]
    nh = FF_HIDDEN // th
    grid = (n_rows // tm, nh)
    vmem = (2 + x_buffers) * tm * D_MODEL * 4 + tm * D_MODEL * 2 + 6 * D_MODEL * th * 2 + 3 * tm * th * 4 \
        + tm * D_MODEL * 4 + (4 << 20)
    x_mode = {} if x_buffers == 2 else {"pipeline_mode": pl.Buffered(x_buffers)}
    return pl.pallas_call(
        _ffn_kernel,
        grid_spec=pltpu.PrefetchScalarGridSpec(
            num_scalar_prefetch=1,
            grid=grid,
            in_specs=[
                pl.BlockSpec((tm, D_MODEL), lambda i, j, l: (i, 0), **x_mode),
                pl.BlockSpec((1, 1, D_MODEL), lambda i, j, l: (l[0], 0, 0)),
                pl.BlockSpec((1, D_MODEL, th), lambda i, j, l: (l[0], 0, j)),
                pl.BlockSpec((1, D_MODEL, th), lambda i, j, l: (l[0], 0, j + nh)),
                pl.BlockSpec((1, th, D_MODEL), lambda i, j, l: (l[0], j, 0)),
                pl.BlockSpec((1, D_MODEL), lambda i, j, l: (0, 0)),
            ],
            out_specs=pl.BlockSpec((tm, D_MODEL), lambda i, j, l: (i, 0)),
            scratch_shapes=[pltpu.VMEM((tm, D_MODEL), BF16)],
        ),
        out_shape=jax.ShapeDtypeStruct((n_rows, D_MODEL), F32),
        compiler_params=_cparams(("parallel", "arbitrary"), vmem),
        name="ffn",
    )(lidx, x, g, w_gu, w_gu, w_down, final_g)


def _mem_kv_kernel(mem_ref, g_ref, wk_ref, wv_ref, kg_ref, k_ref, v_ref):
    mn = _rms(mem_ref[0], g_ref[0]).astype(BF16)
    k = _dot(mn, wk_ref[0])
    for h in range(CA_H):
        sl = slice(h * CA_DH, (h + 1) * CA_DH)
        k_ref[0, :, sl] = _rms(k[:, sl], kg_ref[0])
    v_ref[0] = _dot(mn, wv_ref[0])


def mem_kv(mem, g, wk, wv, kg):
    nb = mem.shape[0]
    out = jax.ShapeDtypeStruct((DEPTH * nb, N_MEM, CA_W), F32)
    return pl.pallas_call(
        _mem_kv_kernel,
        grid=(DEPTH, nb),
        in_specs=[
            pl.BlockSpec((1, N_MEM, D_MODEL), lambda l, b: (b, 0, 0)),
            pl.BlockSpec((1, 1, D_MODEL), lambda l, b: (l, 0, 0)),
            pl.BlockSpec((1, D_MODEL, CA_W), lambda l, b: (l, 0, 0)),
            pl.BlockSpec((1, D_MODEL, CA_W), lambda l, b: (l, 0, 0)),
            pl.BlockSpec((1, 1, CA_DH), lambda l, b: (l, 0, 0)),
        ],
        out_specs=[pl.BlockSpec((1, N_MEM, CA_W), lambda l, b: (l * nb + b, 0, 0)),
                   pl.BlockSpec((1, N_MEM, CA_W), lambda l, b: (l * nb + b, 0, 0))],
        out_shape=[out, out],
        compiler_params=_cparams(("arbitrary", "arbitrary"), 24 << 20),
        name="mem_kv",
    )(mem, g, wk, wv, kg)


def _post_kernel(l_ref, x_ref, yml_ref, ydf_ref, yhg_ref, wout_ref, gx_ref, wq_ref, qg_ref,
                 mk_ref, mv_ref, wo_ref, o_ref, *, rows_per_stream):
    x1 = x_ref[...]
    x1 = x1 + _dot(yml_ref[...], wout_ref[0, 0:ML_W, :])
    x1 = x1 + _dot(ydf_ref[...], wout_ref[0, ML_W:ML_W + DF_W, :])
    x1 = x1 + _dot(yhg_ref[...], wout_ref[0, ML_W + DF_W:D_MODEL, :])
    hn = _rms(x1, gx_ref[0]).astype(BF16)
    q = _dot(hn, wq_ref[0])
    streams = []
    for si in range(x1.shape[0] // rows_per_stream):
        rs = slice(si * rows_per_stream, (si + 1) * rows_per_stream)
        heads = []
        for h in range(CA_H):
            sl = slice(h * CA_DH, (h + 1) * CA_DH)
            qh = _rms(q[rs, sl], qg_ref[0]).astype(BF16)
            s = _dot_nt(qh, mk_ref[si, :, sl].astype(BF16)) * (CA_DH ** -0.5)
            s = s - jnp.max(s, axis=-1, keepdims=True)
            p = jnp.exp(s)
            p = p / jnp.sum(p, axis=-1, keepdims=True)
            heads.append(_dot(p.astype(BF16), mv_ref[si, :, sl].astype(BF16)))
        streams.append(jnp.concatenate(heads, axis=-1))
    o = (streams[0] if len(streams) == 1 else jnp.concatenate(streams, axis=0)).astype(BF16)
    o_ref[...] = x1 + _dot(o, wo_ref[0])


def post_mix(x, y_ml, y_df, y_hg, lidx, w_out, gx, wq, qg, mk, mv, wo, *, tm, rows_per_batch):
    n_rows = x.shape[0]
    nb = n_rows // rows_per_batch
    grid = (n_rows // tm,)
    w_bytes = (D_MODEL * D_MODEL + 2 * D_MODEL * CA_W) * 2
    row = lambda i, l: (i, 0)
    if tm >= rows_per_batch:
        spt = tm // rows_per_batch
        mem_map = lambda i, l: (l[0] * (nb // spt) + i, 0, 0)
    else:
        spt = 1
        tiles_per_batch = rows_per_batch // tm
        mem_map = lambda i, l: (l[0] * nb + i // tiles_per_batch, 0, 0)
    vmem = 2 * w_bytes + 6 * tm * D_MODEL * 4 + 2 * tm * D_MODEL * 2 + 4 * spt * N_MEM * CA_W * 4 + (6 << 20)
    kern = functools.partial(_post_kernel, rows_per_stream=min(tm, rows_per_batch))
    return pl.pallas_call(
        kern,
        grid_spec=pltpu.PrefetchScalarGridSpec(
            num_scalar_prefetch=1,
            grid=grid,
            in_specs=[
                pl.BlockSpec((tm, D_MODEL), row),
                pl.BlockSpec((tm, ML_W), row),
                pl.BlockSpec((tm, DF_W), row),
                pl.BlockSpec((tm, HG_W), row),
                pl.BlockSpec((1, D_MODEL, D_MODEL), lambda i, l: (l[0], 0, 0)),
                pl.BlockSpec((1, 1, D_MODEL), lambda i, l: (l[0], 0, 0)),
                pl.BlockSpec((1, D_MODEL, CA_W), lambda i, l: (l[0], 0, 0)),
                pl.BlockSpec((1, 1, CA_DH), lambda i, l: (l[0], 0, 0)),
                pl.BlockSpec((spt, N_MEM, CA_W), mem_map),
                pl.BlockSpec((spt, N_MEM, CA_W), mem_map),
                pl.BlockSpec((1, CA_W, D_MODEL), lambda i, l: (l[0], 0, 0)),
            ],
            out_specs=pl.BlockSpec((tm, D_MODEL), row),
        ),
        out_shape=jax.ShapeDtypeStruct((n_rows, D_MODEL), F32),
        compiler_params=_cparams(("parallel",), vmem),
        name="post_mix",
    )(lidx, x, y_ml, y_df, y_hg, w_out, gx, wq, qg, mk, mv, wo)


def _mlstm_kernel(l_ref, zq_ref, zk_ref, zv_ref, zo_ref, gate_ref, hist_ref, cw_ref, cb_ref, bif_ref,
                  og_ref, c0_ref, n0_ref, m0_ref, tri_ref,
                  y_ref, c_ref, n_ref, m_ref, ext_ref, *, tc, lc):
    c = pl.program_id(1)

    @pl.when(c == 0)
    def _():
        c_ref[...] = c0_ref[...]
        n_ref[...] = n0_ref[...]
        m_ref[...] = m0_ref[...]
        ext_ref[0:HIST, :] = hist_ref[0]

    ext_ref[HIST:HIST + tc, 0:ML_W] = zq_ref[...]
    ext_ref[HIST:HIST + tc, ML_W:2 * ML_W] = zk_ref[...]
    ext = ext_ref[...]
    acc = cb_ref[0] + cw_ref[0, CONV_W - 1:CONV_W, :] * ext[HIST:HIST + tc]
    for j in range(CONV_W - 1):
        acc = acc + cw_ref[0, j:j + 1, :] * pltpu.roll(ext, CONV_W - 1 - j, axis=0)[HIST:HIST + tc]
    tail = ext_ref[tc:tc + HIST, :]
    ext_ref[0:HIST, :] = tail
    qk = acc * jax.nn.sigmoid(acc)

    gz = gate_ref[...] + bif_ref[0]
    lf_all = pltpu.roll(_log_sigmoid(gz), GATE_W - ML_H, axis=1)
    tri = tri_ref[...]
    row_i = lax.broadcasted_iota(jnp.int32, (lc, lc), 0)
    col_i = lax.broadcasted_iota(jnp.int32, (lc, lc), 1)
    causal = row_i >= col_i
    diag = row_i == col_i
    trow = lax.broadcasted_iota(jnp.int32, (lc, GATE_W), 0)
    og = og_ref[0]

    for ci in range(tc // lc):
        r0 = ci * lc
        b_c = _dot_exact01(tri, lf_all[r0:r0 + lc, :])
        r_c = gz[r0:r0 + lc, :] - b_c
        cm = r_c
        d = 1
        while d < lc:
            cm = jnp.maximum(cm, jnp.where(trow >= d, pltpu.roll(cm, d, axis=0), NEG_BIG))
            d *= 2
        m_prev = m_ref[0]
        mx = jnp.maximum(m_prev, cm)
        m_t = b_c + mx
        w_inter_c = jnp.exp(m_prev - mx)
        emt_c = jnp.exp(-m_t)
        m_new = m_t[lc - 1:lc, :]
        b_last = b_c[lc - 1:lc, :]
        w_s_c = jnp.exp(b_last + r_c - m_new)
        decay_c = jnp.exp(b_last + m_prev - m_new)
        m_ref[0] = m_new
        for h in range(ML_H):
            sl = slice(h * ML_DH, (h + 1) * ML_DH)
            hl = slice(h, h + 1)
            q = qk[r0:r0 + lc, h * ML_DH:(h + 1) * ML_DH]
            k = qk[r0:r0 + lc, ML_W + h * ML_DH:ML_W + (h + 1) * ML_DH] * (ML_DH ** -0.5)
            v = zv_ref[r0:r0 + lc, sl]
            qb = q.astype(BF16)
            r_row = jnp.sum(jnp.where(diag, r_c[:, hl], 0.0), axis=0, keepdims=True)
            d_mat = jnp.exp(jnp.where(causal, r_row - mx[:, hl], NEG_BIG))
            w_inter = w_inter_c[:, hl]
            decay = decay_c[:, hl]
            s = _dot_nt(qb, k.astype(BF16)) * d_mat
            c_old = c_ref[0, h]
            n_old = n_ref[0, h:h + 1, :]
            num = _dot(s.astype(BF16), v.astype(BF16)) + w_inter * _dot(qb, c_old.astype(BF16))
            den = jnp.sum(s, axis=-1, keepdims=True) + w_inter * jnp.sum(q * n_old, axis=-1, keepdims=True)
            hh = num / jnp.maximum(jnp.abs(den), emt_c[:, hl])
            kw = k * w_s_c[:, hl]
            c_ref[0, h] = decay * c_old + _dot_tn(kw.astype(BF16), v.astype(BF16))
            n_ref[0, h:h + 1, :] = decay * n_old + jnp.sum(kw, axis=0, keepdims=True)
            o_gate = jax.nn.sigmoid(zo_ref[r0:r0 + lc, sl])
            y_ref[r0:r0 + lc, sl] = (_rms(hh, og) * o_gate).astype(BF16)


def mlstm_mixer(z, gates, lidx, hist0, conv_w, conv_b, b_if, out_g, c0, n0, m0, *, nb, seq, tc, lc):
    n_rows = nb * seq
    nc = seq // tc
    tri = jnp.asarray(np.tril(np.ones((lc, lc), np.float32)), BF16)
    zmap = lambda blk: (lambda b, c, l: (b * nc + c, blk))
    lmap3 = lambda b, c, l: (l[0], 0, 0)
    vmem = 2 * 5 * tc * ML_W * 4 + (HIST + tc) * 2 * ML_W * 4 + 4 * ML_H * ML_DH * ML_DH * 4 \
        + 6 * tc * 2 * ML_W * 4 + (8 << 20)
    kern = functools.partial(_mlstm_kernel, tc=tc, lc=lc)
    return pl.pallas_call(
        kern,
        grid_spec=pltpu.PrefetchScalarGridSpec(
            num_scalar_prefetch=1,
            grid=(nb, nc),
            in_specs=[
                pl.BlockSpec((tc, ML_W), zmap(ZB_MLQ)),
                pl.BlockSpec((tc, ML_W), zmap(ZB_MLK)),
                pl.BlockSpec((tc, ML_W), zmap(ZB_MLV)),
                pl.BlockSpec((tc, ML_W), zmap(ZB_MLO)),
                pl.BlockSpec((tc, GATE_W), lambda b, c, l: (b * nc + c, 0)),
                pl.BlockSpec((1, HIST, 2 * ML_W), lambda b, c, l: (b, 0, 0)),
                pl.BlockSpec((1, CONV_W, 2 * ML_W), lmap3),
                pl.BlockSpec((1, 1, 2 * ML_W), lmap3),
                pl.BlockSpec((1, 1, GATE_W), lmap3),
                pl.BlockSpec((1, 1, ML_DH), lmap3),
                pl.BlockSpec((1, ML_H, ML_DH, ML_DH), lambda b, c, l: (b, 0, 0, 0)),
                pl.BlockSpec((1, 8, ML_DH), lambda b, c, l: (b, 0, 0)),
                pl.BlockSpec((1, 1, GATE_W), lambda b, c, l: (b, 0, 0)),
                pl.BlockSpec((lc, lc), lambda b, c, l: (0, 0)),
            ],
            out_specs=[
                pl.BlockSpec((tc, ML_W), lambda b, c, l: (b * nc + c, 0)),
                pl.BlockSpec((1, ML_H, ML_DH, ML_DH), lambda b, c, l: (b, 0, 0, 0)),
                pl.BlockSpec((1, 8, ML_DH), lambda b, c, l: (b, 0, 0)),
                pl.BlockSpec((1, 1, GATE_W), lambda b, c, l: (b, 0, 0)),
            ],
            scratch_shapes=[pltpu.VMEM((HIST + tc, 2 * ML_W), F32)],
        ),
        out_shape=[
            jax.ShapeDtypeStruct((n_rows, ML_W), BF16),
            jax.ShapeDtypeStruct((nb, ML_H, ML_DH, ML_DH), F32),
            jax.ShapeDtypeStruct((nb, 8, ML_DH), F32),
            jax.ShapeDtypeStruct((nb, 1, GATE_W), F32),
        ],
        compiler_params=_cparams(("parallel", "arbitrary"), vmem),
        name="mlstm",
    )(lidx, z, z, z, z, gates, hist0, conv_w, conv_b, b_if, out_g, c0, n0, m0, tri)


def _hgrn_level_masks(lc):
    t = np.arange(lc)[:, None]
    s = np.arange(lc)[None, :]
    masks = []
    h = lc // 2
    while h >= 1:
        odd = (t % (2 * h)) >= h
        same = (t // (2 * h)) == (s // (2 * h))
        masks.append((same & odd & ((s % (2 * h)) < h)).astype(np.float32))
        h //= 2
    masks.append((t == s).astype(np.float32))
    return np.stack(masks, axis=0)


def _hgrn_boundary_rows(a, h):
    lc, width = a.shape
    if h >= 8:
        parts = [jnp.broadcast_to(a[g * 2 * h + h - 1:g * 2 * h + h, :], (2 * h, width)) for g in range(lc // (2 * h))]
        return parts[0] if len(parts) == 1 else jnp.concatenate(parts, axis=0)
    a3 = a.reshape(lc // 8, 8, width)
    sub = lax.broadcasted_iota(jnp.int32, a3.shape, 1)
    out = None
    for g in range(8 // (2 * h)):
        src = g * 2 * h + h - 1
        b = jnp.broadcast_to(a3[:, src:src + 1, :], a3.shape)
        out = b if out is None else jnp.where(sub >= g * 2 * h, b, out)
    return out.reshape(lc, width)


def _hgrn_kernel(l_ref, zq_ref, zf_ref, zi_ref, zg_ref, lbl_ref, og_ref, s0_ref, tri_ref, msk_ref,
                 y_ref, s_ref, st_ref, *, tc, lc, nlev):
    c = pl.program_id(1)
    nc = pl.num_programs(1)

    @pl.when(c == 0)
    def _():
        for h in range(HG_H):
            st_ref[h] = s0_ref[0, h].T

    logits = lbl_ref[...]
    e = jnp.exp(logits - jnp.max(logits, axis=0, keepdims=True))
    p = e / jnp.sum(e, axis=0, keepdims=True)
    drow = lax.broadcasted_iota(jnp.int32, p.shape, 0)
    lb = jnp.sum(jnp.where(jnp.logical_and(drow >= 1, drow <= l_ref[0]), p, 0.0), axis=0, keepdims=True)
    lb_fl = jnp.maximum(lb, LB_FLOOR)
    one_m = 1.0 - lb

    zf = zf_ref[...]
    ez = jnp.exp(-jnp.abs(zf))
    inv = 1.0 / (1.0 + ez)
    sig_p = jnp.where(zf >= 0.0, inv, ez * inv)
    sig_n = jnp.where(zf >= 0.0, ez * inv, inv)
    lf2 = jnp.log(lb_fl + one_m * sig_p) * LOG2E
    kk = one_m * sig_n + (lb - lb_fl)
    zq = zq_ref[...]
    qq = zq * jax.nn.sigmoid(zq) * (HG_DK ** -0.5)
    tri = tri_ref[...]
    og = og_ref[0]

    for ci in range(tc // lc):
        r0 = ci * lc
        a_all = _dot_exact01(tri, lf2[r0:r0 + lc, :])
        lev_exp = [-jnp.abs(a_all - _hgrn_boundary_rows(a_all, lc >> (lv + 1))) for lv in range(nlev)]
        for h in range(HG_H):
            sl = slice(h * HG_DK, (h + 1) * HG_DK)
            q = qq[r0:r0 + lc, sl]
            k = kk[r0:r0 + lc, sl]
            iv = zi_ref[r0:r0 + lc, sl].astype(BF16)
            a_in = a_all[:, sl]
            a_end = a_in[lc - 1:lc, :]
            st = st_ref[h]
            o = _dot_nt((q * jnp.exp2(a_in)).astype(BF16), st.astype(BF16))
            att = jnp.where(msk_ref[nlev] > 0, _dot_nt(q.astype(BF16), k.astype(BF16)), 0.0)
            for lv in range(nlev):
                xf = jnp.exp2(lev_exp[lv][:, sl])
                pm = _dot_nt((q * xf).astype(BF16), (k * xf).astype(BF16))
                att = att + jnp.where(msk_ref[lv] > 0, pm, 0.0)
            o = o + _dot(att.astype(BF16), iv)
            k_end = (k * jnp.exp2(a_end - a_in)).astype(BF16)
            st_ref[h] = st * jnp.exp2(a_end) + _dot_tn(iv, k_end)
            gsl = zg_ref[r0:r0 + lc, sl]
            y_ref[r0:r0 + lc, sl] = (_rms(o, og) * (gsl * jax.nn.sigmoid(gsl))).astype(BF16)

    @pl.when(c == nc - 1)
    def _():
        for h in range(HG_H):
            s_ref[0, h] = st_ref[h].T


def hgrn_mixer(z, lidx, lb_logits, out_g, s0, *, nb, seq, tc, lc):
    n_rows = nb * seq
    nc = seq // tc
    msk_np = _hgrn_level_masks(lc)
    nlev = msk_np.shape[0] - 1
    tri = jnp.asarray(np.tril(np.ones((lc, lc), np.float32)), BF16)
    msk = jnp.asarray(msk_np, F32)
    zmap = lambda blk: (lambda b, c, l: (b * nc + c, blk))
    vmem = 2 * 5 * tc * HG_W * 4 + 5 * HG_H * HG_DK * HG_DV * 4 + 8 * tc * HG_W * 4 + (8 << 20)
    kern = functools.partial(_hgrn_kernel, tc=tc, lc=lc, nlev=nlev)
    return pl.pallas_call(
        kern,
        grid_spec=pltpu.PrefetchScalarGridSpec(
            num_scalar_prefetch=1,
            grid=(nb, nc),
            in_specs=[
                pl.BlockSpec((tc, HG_W), zmap(ZB_HGQ)),
                pl.BlockSpec((tc, HG_W), zmap(ZB_HGF)),
                pl.BlockSpec((tc, HG_W), zmap(ZB_HGI)),
                pl.BlockSpec((tc, HG_W), zmap(ZB_HGG)),
                pl.BlockSpec((DEPTH, HG_W), lambda b, c, l: (0, 0)),
                pl.BlockSpec((1, 1, HG_DV), lambda b, c, l: (l[0], 0, 0)),
                pl.BlockSpec((1, HG_H, HG_DK, HG_DV), lambda b, c, l: (b, 0, 0, 0)),
                pl.BlockSpec((lc, lc), lambda b, c, l: (0, 0)),
                pl.BlockSpec(msk_np.shape, lambda b, c, l: (0, 0, 0)),
            ],
            out_specs=[
                pl.BlockSpec((tc, HG_W), lambda b, c, l: (b * nc + c, 0)),
                pl.BlockSpec((1, HG_H, HG_DK, HG_DV), lambda b, c, l: (b, 0, 0, 0)),
            ],
            scratch_shapes=[pltpu.VMEM((HG_H, HG_DV, HG_DK), F32)],
        ),
        out_shape=[
            jax.ShapeDtypeStruct((n_rows, HG_W), BF16),
            jax.ShapeDtypeStruct((nb, HG_H, HG_DK, HG_DV), F32),
        ],
        compiler_params=_cparams(("parallel", "arbitrary"), vmem),
        name="hgrn",
    )(lidx, z, z, z, z, lb_logits, out_g, s0, tri, msk)


def _qk_prep_kernel(l_ref, zq_ref, zk_ref, zv_ref, qg_ref, kg_ref, grp_ref, qn_ref, kn_ref, knb_ref, vb_ref):
    grp = grp_ref[...]
    qn_ref[...] = (_group_norm64(zq_ref[...], qg_ref[0], grp) * (DF_DH ** -0.5)).astype(BF16)
    kn = _group_norm64(zk_ref[...], kg_ref[0], grp)
    kn_ref[...] = kn
    knb_ref[...] = kn.astype(BF16)
    vb_ref[...] = zv_ref[...].astype(BF16)


def qk_prep(z, lidx, qg, kg, *, tm):
    n_rows = z.shape[0]
    lane = np.arange(DF_W)
    grp = jnp.asarray(((lane[:, None] // DF_DH) == (lane[None, :] // DF_DH)).astype(np.float32) / DF_DH, BF16)
    zmap = lambda blk: (lambda i, l: (i, blk))
    row = lambda i, l: (i, 0)
    return pl.pallas_call(
        _qk_prep_kernel,
        grid_spec=pltpu.PrefetchScalarGridSpec(
            num_scalar_prefetch=1,
            grid=(n_rows // tm,),
            in_specs=[
                pl.BlockSpec((tm, DF_W), zmap(ZB_DFQ)),
                pl.BlockSpec((tm, DF_W), zmap(ZB_DFK)),
                pl.BlockSpec((tm, DF_W), zmap(ZB_DFV)),
                pl.BlockSpec((1, 1, DF_W), lambda i, l: (l[0], 0, 0)),
                pl.BlockSpec((1, 1, DF_W), lambda i, l: (l[0], 0, 0)),
                pl.BlockSpec((DF_W, DF_W), lambda i, l: (0, 0)),
            ],
            out_specs=[pl.BlockSpec((tm, DF_W), row)] * 4,
        ),
        out_shape=[
            jax.ShapeDtypeStruct((n_rows, DF_W), BF16),
            jax.ShapeDtypeStruct((n_rows, DF_W), F32),
            jax.ShapeDtypeStruct((n_rows, DF_W), BF16),
            jax.ShapeDtypeStruct((n_rows, DF_W), BF16),
        ],
        compiler_params=_cparams(("parallel",), 32 << 20),
        name="qk_prep",
    )(lidx, z, z, z, qg, kg, grp)


def _group_norm64(x, g, grp):
    x2 = x * x
    hi = x2.astype(BF16)
    lo = (x2 - hi.astype(F32)).astype(BF16)
    ms = _dot(hi, grp) + _dot(lo, grp)
    return x * lax.rsqrt(ms + EPS) * g


BIAS_W = 128


def _qk_prep_t_kernel(l_ref, zq_ref, zk_ref, zv_ref, qg_ref, kg_ref, grp_ref, pk_in, pv_in,
                      qt_ref, ke_ref, vt_ref, pk_ref, pv_ref, *, tm, seq):
    del pk_in, pv_in
    i = pl.program_id(0)
    grp = grp_ref[...]
    qn = _group_norm64(zq_ref[...], qg_ref[0], grp) * (DF_DH ** -0.5)
    kn = _group_norm64(zk_ref[...], kg_ref[0], grp)
    zv = zv_ref[...]
    pos = lax.rem(i * tm, seq) + lax.broadcasted_iota(jnp.int32, (tm, BIAS_W), 0)
    lane = lax.broadcasted_iota(jnp.int32, (tm, BIAS_W), 1)
    hi_part = (pos >> CHUNK_SHIFT).astype(F32) * float(CHUNK)
    lo_part = (pos & (CHUNK - 1)).astype(F32)
    base = jnp.where(lane == 0, hi_part, jnp.where(lane == 1, lo_part, jnp.where(lane == 2, float(CHUNK), 0.0)))
    for h in range(DF_H):
        sl = slice(h * DF_DV, (h + 1) * DF_DV)
        slope = 2.0 ** (-8.0 * (h + 1) / DF_H)
        c0 = h * (DF_DV + BIAS_W)
        ke_ref[:, c0:c0 + DF_DV] = kn[:, sl].astype(BF16)
        ke_ref[:, c0 + DF_DV:c0 + DF_DV + BIAS_W] = (base * slope).astype(BF16)
        qt_ref[h, 0] = qn[:, sl].T.astype(BF16)
        vt_ref[h, 0] = zv[:, sl].T.astype(BF16)
        rows = pl.ds(h, tm, stride=DF_H)
        pk_ref[rows, :] = kn[:, sl]
        pv_ref[rows, :] = zv[:, sl]


def qk_prep_t(z, lidx, qg, kg, pk_buf, pv_buf, *, tm, seq):
    n_rows = z.shape[0]
    nt = n_rows // tm
    smap = lambda i, l: (l[0] * nt + i, 0)
    any_spec = pl.BlockSpec(memory_space=pl.ANY)
    lane = np.arange(DF_W)
    grp = jnp.asarray(((lane[:, None] // DF_DH) == (lane[None, :] // DF_DH)).astype(np.float32) / DF_DH, BF16)
    zmap = lambda blk: (lambda i, l: (i, blk))
    row = lambda i, l: (i, 0)
    tmap = lambda i, l: (0, i, 0, 0)
    kern = functools.partial(_qk_prep_t_kernel, tm=tm, seq=seq)
    return pl.pallas_call(
        kern,
        grid_spec=pltpu.PrefetchScalarGridSpec(
            num_scalar_prefetch=1,
            grid=(nt,),
            in_specs=[
                pl.BlockSpec((tm, DF_W), zmap(ZB_DFQ)),
                pl.BlockSpec((tm, DF_W), zmap(ZB_DFK)),
                pl.BlockSpec((tm, DF_W), zmap(ZB_DFV)),
                pl.BlockSpec((1, 1, DF_W), lambda i, l: (l[0], 0, 0)),
                pl.BlockSpec((1, 1, DF_W), lambda i, l: (l[0], 0, 0)),
                pl.BlockSpec((DF_W, DF_W), lambda i, l: (0, 0)),
                any_spec,
                any_spec,
            ],
            out_specs=[
                pl.BlockSpec((DF_H, 1, DF_DV, tm), tmap),
                pl.BlockSpec((tm, DF_H * (DF_DV + BIAS_W)), row),
                pl.BlockSpec((DF_H, 1, DF_DV, tm), tmap),
                pl.BlockSpec((tm * DF_H, DF_DV), smap),
                pl.BlockSpec((tm * DF_H, DF_DV), smap),
            ],
        ),
        out_shape=[
            jax.ShapeDtypeStruct((DF_H, nt, DF_DV, tm), BF16),
            jax.ShapeDtypeStruct((n_rows, DF_H * (DF_DV + BIAS_W)), BF16),
            jax.ShapeDtypeStruct((DF_H, nt, DF_DV, tm), BF16),
            jax.ShapeDtypeStruct(pk_buf.shape, F32),
            jax.ShapeDtypeStruct(pv_buf.shape, F32),
        ],
        input_output_aliases={7: 3, 8: 4},
        compiler_params=_cparams(("parallel",), 40 << 20),
        name="qk_prep_t",
    )(lidx, z, z, z, qg, kg, grp, pk_buf, pv_buf)


ONES_ROWS = 16


def _flash_t_kernel(l_ref, qt_ref, ke_ref, vt_ref, lam_ref, ogt_ref, y_ref, m_ref, acc_ref, sa_ref, sb_ref,
                    own_ref, *, tq):
    h = pl.program_id(1)
    qi = pl.program_id(2)
    tk = tq
    slope = jnp.exp2(jnp.full((1, 1), -8.0 / DF_H, F32) * (h + 1).astype(F32))
    qt = qt_ref[0, 0]
    row = lax.broadcasted_iota(jnp.int32, (DF_DV, tq), 0)
    zero = jnp.zeros_like(qt)
    q2t = jnp.concatenate([jnp.where(row < DF_DH, qt, zero), jnp.where(row >= DF_DH, qt, zero)], axis=1)
    brow = lax.broadcasted_iota(jnp.int32, (BIAS_W, 2 * tq), 0)
    a0 = ((qi * tq) >> CHUNK_SHIFT).astype(F32)
    extra = jnp.where(brow < 2, 1.0, jnp.where(brow == 2, -a0, 0.0)).astype(BF16)
    q2e = jnp.concatenate([q2t, extra], axis=0)
    ones = jnp.ones((ONES_ROWS, tk), BF16)

    m_ref[...] = jnp.full(m_ref.shape, NEG_BIG, F32)
    acc_ref[...] = jnp.zeros(acc_ref.shape, F32)

    def scores_into(dst_ref, kj):
        r0 = pl.multiple_of(kj * tk, tk)
        dst_ref[...] = _dot(ke_ref[pl.ds(r0, tk), :], q2e)

    def update(s, kj):
        m_old = m_ref[...]
        m_new = jnp.maximum(m_old, jnp.max(s, axis=0, keepdims=True))
        alpha = jnp.exp(m_old - m_new)
        p = jnp.exp(s - m_new).astype(BF16)
        vt_ext = jnp.concatenate([vt_ref[0, kj], ones], axis=0)
        acc_ref[...] = alpha * acc_ref[...] + _dot(vt_ext, p)
        m_ref[...] = m_new

    @pl.when(qi == 0)
    def _():
        rel_k = lax.broadcasted_iota(jnp.int32, (tq, 2 * tq), 0)
        rel_q = lax.broadcasted_iota(jnp.int32, (tq, 2 * tq), 1)
        rel_q = jnp.where(rel_q >= tq, rel_q - tq, rel_q)
        ahead = rel_k - rel_q
        fix = jnp.where(ahead > 0, (-2.0 * slope) * ahead.astype(F32), 0.0)
        visible = (rel_k >> CHUNK_SHIFT) <= (rel_q >> CHUNK_SHIFT)
        own_ref[...] = jnp.where(visible, fix, NEG_BIG)

    def own_block(s):
        update(s + own_ref[...], qi)

    scores_into(sa_ref, 0)

    def block_pair(t, carry):
        k0 = 2 * t
        scores_into(sb_ref, k0 + 1)
        update(sa_ref[...], k0)
        scores_into(sa_ref, k0 + 2)
        update(sb_ref[...], k0 + 1)
        return carry

    lax.fori_loop(0, qi >> 1, block_pair, 0)

    @pl.when((qi & 1) == 1)
    def _():
        scores_into(sb_ref, qi)
        update(sa_ref[...], qi - 1)
        own_block(sb_ref[...])

    @pl.when((qi & 1) == 0)
    def _():
        own_block(sa_ref[...])

    lam_p = lam_ref[0]
    lam_init = 0.8 - 0.6 * jnp.exp(jnp.full((1, 1), -0.3, F32) * l_ref[0].astype(F32))
    lam = (jnp.exp(jnp.sum(lam_p[0:1] * lam_p[1:2], axis=-1, keepdims=True))
           - jnp.exp(jnp.sum(lam_p[2:3] * lam_p[3:4], axis=-1, keepdims=True)) + lam_init)
    acc = acc_ref[...]
    o_all = acc[0:DF_DV] / acc[DF_DV:DF_DV + 1]
    o = o_all[:, 0:tq] - lam * o_all[:, tq:2 * tq]
    ms = jnp.mean(o * o, axis=0, keepdims=True)
    y = o * lax.rsqrt(ms + EPS) * ogt_ref[0] * (1.0 - lam_init)
    y_ref[...] = y.T.astype(BF16)


def diff_flash_t(qt, ke, vt, lidx, lam_p, out_g_t, *, nb, seq, tq):
    assert tq % CHUNK == 0
    nq = seq // tq
    n_rows = nb * seq
    kew = DF_DV + BIAS_W
    kern = functools.partial(_flash_t_kernel, tq=tq)
    vmem = 2 * seq * kew * 2 + 2 * seq * DF_DV * 2 + 10 * tq * 2 * tq * 4 + (8 << 20)
    return pl.pallas_call(
        kern,
        grid_spec=pltpu.PrefetchScalarGridSpec(
            num_scalar_prefetch=1,
            grid=(nb, DF_H, nq),
            in_specs=[
                pl.BlockSpec((1, 1, DF_DV, tq), lambda b, h, i, l: (h, b * nq + i, 0, 0)),
                pl.BlockSpec((seq, kew), lambda b, h, i, l: (b, h)),
                pl.BlockSpec((1, nq, DF_DV, tq), lambda b, h, i, l: (h, b, 0, 0)),
                pl.BlockSpec((1, 4, DF_DH), lambda b, h, i, l: (l[0], 0, 0)),
                pl.BlockSpec((1, DF_DV, 1), lambda b, h, i, l: (l[0], 0, 0)),
            ],
            out_specs=pl.BlockSpec((tq, DF_DV), lambda b, h, i, l: (b * nq + i, h)),
            scratch_shapes=[pltpu.VMEM((1, 2 * tq), F32), pltpu.VMEM((DF_DV + ONES_ROWS, 2 * tq), F32),
                            pltpu.VMEM((tq, 2 * tq), F32), pltpu.VMEM((tq, 2 * tq), F32),
                            pltpu.VMEM((tq, 2 * tq), F32)],
        ),
        out_shape=jax.ShapeDtypeStruct((n_rows, DF_W), BF16),
        compiler_params=_cparams(("parallel", "parallel", "arbitrary"), vmem),
        name="diff_flash_t",
    )(lidx, qt, ke, vt, lam_p, out_g_t)


def _flash_sample_kernel(l_ref, q_ref, km_ref, vm_ref, kx_ref, vx_ref, lam_ref, og_ref, y_ref, *, tq, tk, past):
    q0 = past
    lam_p = lam_ref[0]
    lam_init = 0.8 - 0.6 * jnp.exp(jnp.full((1, 1), -0.3, F32) * l_ref[0].astype(F32))
    lam = (jnp.exp(jnp.sum(lam_p[0:1] * lam_p[1:2], axis=-1, keepdims=True))
           - jnp.exp(jnp.sum(lam_p[2:3] * lam_p[3:4], axis=-1, keepdims=True)) + lam_init)
    lane = lax.broadcasted_iota(jnp.int32, (tq, DF_DV), 1)
    rel_k = lax.broadcasted_iota(jnp.int32, (2 * tq, tq), 1)
    rel_q = lax.broadcasted_iota(jnp.int32, (2 * tq, tq), 0)
    rel_q = jnp.where(rel_q >= tq, rel_q - tq, rel_q)
    own_bias = rel_q.astype(F32) - jnp.abs(rel_q - rel_k).astype(F32)
    visible = ((q0 + rel_k) >> CHUNK_SHIFT) <= ((q0 + rel_q) >> CHUNK_SHIFT)

    def update(state, s, v):
        m_old, l_old, acc_old = state
        m_new = jnp.maximum(m_old, jnp.max(s, axis=-1, keepdims=True))
        alpha = jnp.exp(m_old - m_new)
        p = jnp.exp(s - m_new)
        return (m_new, alpha * l_old + jnp.sum(p, axis=-1, keepdims=True),
                alpha * acc_old + _dot(p.astype(BF16), v))

    for h in range(DF_H):
        sl = slice(h * DF_DV, (h + 1) * DF_DV)
        slope = 2.0 ** (-8.0 * (h + 1) / DF_H)
        q = q_ref[:, sl]
        zero = jnp.zeros_like(q)
        q2 = jnp.concatenate([jnp.where(lane < DF_DH, q, zero), jnp.where(lane >= DF_DH, q, zero)], axis=0)
        state = (jnp.full((2 * tq, 1), NEG_BIG, F32), jnp.zeros((2 * tq, 1), F32), jnp.zeros((2 * tq, DF_DV), F32))
        for kj in range(past // tk):
            rows = pl.ds(kj * tk * DF_H + h, tk, stride=DF_H)
            kpos = kj * tk + lax.broadcasted_iota(jnp.int32, (1, tk), 1)
            s = _dot_nt(q2, km_ref[rows, :].astype(BF16)) + slope * (kpos - q0).astype(F32)
            state = update(state, s, vm_ref[rows, :].astype(BF16))
        s = _dot_nt(q2, kx_ref[:, sl].astype(BF16)) + slope * own_bias
        _, l_sum, acc = update(state, jnp.where(visible, s, NEG_BIG), vx_ref[:, sl].astype(BF16))
        o_all = acc / l_sum
        o = o_all[0:tq] - lam * o_all[tq:2 * tq]
        y_ref[:, sl] = (_rms(o, og_ref[0]) * (1.0 - lam_init)).astype(BF16)


def diff_flash_sample(qn, k_past, v_past, k_new, v_new, lidx, lam_p, out_g, *, nb, seq, past, tk):
    n_rows = nb * seq
    kern = functools.partial(_flash_sample_kernel, tq=seq, tk=tk, past=past)
    vmem = 4 * past * DF_H * DF_DV * 4 + (8 << 20)
    qmap = lambda b, l: (b, 0)
    cmap = lambda b, l: (l[0] * nb + b, 0)
    return pl.pallas_call(
        kern,
        grid_spec=pltpu.PrefetchScalarGridSpec(
            num_scalar_prefetch=1,
            grid=(nb,),
            in_specs=[
                pl.BlockSpec((seq, DF_W), qmap),
                pl.BlockSpec((past * DF_H, DF_DV), cmap),
                pl.BlockSpec((past * DF_H, DF_DV), cmap),
                pl.BlockSpec((seq, DF_W), qmap),
                pl.BlockSpec((seq, DF_W), qmap),
                pl.BlockSpec((1, 4, DF_DH), lambda b, l: (l[0], 0, 0)),
                pl.BlockSpec((1, 1, DF_DV), lambda b, l: (l[0], 0, 0)),
            ],
            out_specs=pl.BlockSpec((seq, DF_W), qmap),
        ),
        out_shape=jax.ShapeDtypeStruct((n_rows, DF_W), BF16),
        compiler_params=_cparams(("parallel",), vmem),
        name="diff_flash_sample",
    )(lidx, qn, k_past, v_past, k_new, v_new, lam_p, out_g)


def _trunk_layer(x, lidx, w, st, cfg):
    nb, seq, past = cfg["nb"], cfg["seq"], cfg["past"]
    z, gates = in_proj(x, lidx, w["norm_mix"], w["w_main"], w["w_gate"], tm=cfg["tm_in"], tn=cfg["tn_in"])
    y_ml, c_new, n_new, m_new = mlstm_mixer(
        z, gates, lidx, st["hist0"], w["conv_w"], w["conv_b"], w["b_if"], w["ml_og"],
        st["c0"], st["n0"], st["m0"], nb=nb, seq=seq, tc=cfg["tc"], lc=cfg["lc_ml"])
    z3 = z.reshape(nb, seq, Z_W)
    new = {}
    if cfg["prompt"]:
        qt, ke, vt, new["pk_buf"], new["pv_buf"] = qk_prep_t(
            z, lidx, w["df_qg"], w["df_kg"], st["pk_buf"], st["pv_buf"], tm=cfg["tq"], seq=seq)
        y_df = diff_flash_t(qt, ke, vt, lidx, w["df_lam"], w["df_og_t"], nb=nb, seq=seq, tq=cfg["tq"])
    else:
        qn, kn, kn_b, v_b = qk_prep(z, lidx, w["df_qg"], w["df_kg"], tm=cfg["tm_prep"])
        y_df = diff_flash_sample(qn, st["past_k"], st["past_v"], kn_b, v_b, lidx, w["df_lam"], w["df_og"],
                                 nb=nb, seq=seq, past=past, tk=cfg["tk_past"])
        new["attn_k"] = kn.reshape(nb, seq, DF_H, 2 * DF_DH)
        new["attn_v"] = z3[:, :, ZB_DFV * DF_W:(ZB_DFV + 1) * DF_W].reshape(nb, seq, DF_H, DF_DV)
    y_hg, s_new = hgrn_mixer(z, lidx, w["hg_lbl"], w["hg_og"], st["s0"], nb=nb, seq=seq, tc=cfg["tc_hg"],
                             lc=cfg["lc_hg"])
    x = post_mix(x, y_ml, y_df, y_hg, lidx, w["w_out"], w["norm_cross"], w["wq"], w["ca_qg"],
                 st["mk"], st["mv"], w["wo"], tm=cfg["tm_post"], rows_per_batch=seq)
    x = ffn(x, lidx, w["norm_ffn"], w["w_gu"], w["w_down"], w["final_norm"], tm=cfg["tm_ffn"], th=cfg["th"],
            x_buffers=cfg["ffn_x_buffers"])
    new.update({
        "conv": z3[:, seq - (CONV_W - 1):, 0:2 * ML_W],
        "C": c_new,
        "n": n_new[:, :ML_H, :],
        "m": m_new[:, 0, :ML_H],
        "S": s_new,
    })
    return x, new


PROMPT_CFG = dict(prompt=True, tm_in=1024, tn_in=1536, tc=256, tc_hg=512, lc_ml=128, lc_hg=128, tm_prep=512, tq=512,
                  tm_post=512, tm_ffn=1024, th=512, ffn_x_buffers=2)
SAMPLE_CFG = dict(prompt=False, tm_in=128, tn_in=768, tc=16, tc_hg=16, lc_ml=16, lc_hg=16, tm_prep=128, tk_past=2048,
                  tm_post=128, tm_ffn=128, th=512, ffn_x_buffers=2)


def kernel(x_prompt, x_sample, mem_prompt, cache_attn_k, cache_attn_v, cache_mem_k, cache_mem_v, state_mlstm_conv, state_mlstm_C, state_mlstm_n, state_mlstm_m, state_hgrn_S, norm_mix, w_in, mlstm_conv_w, mlstm_conv_b, mlstm_b_i, mlstm_b_f, mlstm_out_norm, diff_q_norm, diff_k_norm, diff_lambda, diff_out_norm, hgrn_lb_logits, hgrn_out_norm, w_out, norm_cross, norm_mem, cross_wq, cross_wk, cross_wv, cross_q_norm, cross_k_norm, cross_wo, norm_ffn, ffn_w_gate_up, ffn_w_down, final_norm):
    bp, tp = x_prompt.shape[:2]
    bs, ts = x_sample.shape[:2]
    past = cache_attn_k.shape[2]
    depth = w_in.shape[0]
    assert depth == DEPTH and x_prompt.shape[2] == D_MODEL

    g_off = 4 * ML_W
    r3 = lambda a: a.reshape(depth, 1, a.shape[-1])
    w = {
        "norm_mix": r3(norm_mix),
        "w_main": jnp.concatenate([w_in[:, :, :g_off], w_in[:, :, g_off + 2 * ML_H:]], axis=-1).astype(BF16),
        "w_gate": jnp.pad(w_in[:, :, g_off:g_off + 2 * ML_H], ((0, 0), (0, 0), (0, GATE_W - 2 * ML_H))).astype(BF16),
        "conv_w": mlstm_conv_w,
        "conv_b": r3(mlstm_conv_b),
        "b_if": r3(jnp.pad(jnp.concatenate([mlstm_b_i, mlstm_b_f], axis=-1), ((0, 0), (0, GATE_W - 2 * ML_H)))),
        "ml_og": r3(mlstm_out_norm),
        "df_qg": r3(jnp.tile(diff_q_norm, (1, DF_W // DF_DH))),
        "df_kg": r3(jnp.tile(diff_k_norm, (1, DF_W // DF_DH))),
        "df_lam": diff_lambda,
        "df_og": r3(diff_out_norm),
        "df_og_t": diff_out_norm.reshape(depth, DF_DV, 1),
        "hg_lbl": hgrn_lb_logits,
        "hg_og": r3(hgrn_out_norm),
        "w_out": w_out.astype(BF16),
        "norm_cross": r3(norm_cross),
        "wq": cross_wq.astype(BF16),
        "ca_qg": r3(cross_q_norm),
        "wo": cross_wo.astype(BF16),
        "norm_ffn": r3(norm_ffn),
        "w_gu": ffn_w_gate_up.astype(BF16),
        "w_down": ffn_w_down.astype(BF16),
        "final_norm": final_norm.reshape(1, D_MODEL),
    }

    mk_p, mv_p = mem_kv(mem_prompt, r3(norm_mem), cross_wk.astype(BF16), cross_wv.astype(BF16), r3(cross_k_norm))

    def pad_hist(conv):
        pad = [(0, 0)] * (conv.ndim - 2) + [(HIST - (CONV_W - 1), 0), (0, 0)]
        return jnp.pad(conv, pad)

    st_p = {
        "hist0": jnp.zeros((bp, HIST, 2 * ML_W), F32),
        "c0": jnp.zeros((bp, ML_H, ML_DH, ML_DH), F32),
        "n0": jnp.zeros((bp, 8, ML_DH), F32),
        "m0": jnp.zeros((bp, 1, GATE_W), F32),
        "s0": jnp.zeros((bp, HG_H, HG_DK, HG_DV), F32),
        "mk": mk_p, "mv": mv_p,
    }
    hist_s = pad_hist(state_mlstm_conv)
    n_s = jnp.pad(state_mlstm_n, ((0, 0), (0, 0), (0, 8 - ML_H), (0, 0)))
    m_s = jnp.pad(state_mlstm_m, ((0, 0), (0, 0), (0, GATE_W - ML_H))).reshape(depth, bs, 1, GATE_W)
    past_k = cache_attn_k.reshape(depth * bs * past * DF_H, DF_DV)
    past_v = cache_attn_v.reshape(depth * bs * past * DF_H, DF_DV)
    mk_s = cache_mem_k.reshape(depth * bs, N_MEM, CA_W)
    mv_s = cache_mem_v.reshape(depth * bs, N_MEM, CA_W)

    cfg_p = dict(PROMPT_CFG, nb=bp, seq=tp, past=0)
    cfg_s = dict(SAMPLE_CFG, nb=bs, seq=ts, past=past)

    def layer(carry, xs):
        xp, xsm, pk_buf, pv_buf = carry
        l, hist_l, c_l, n_l, m_l, s_l = xs
        lidx = l.reshape(1).astype(jnp.int32)
        xp, new_p = _trunk_layer(xp, lidx, w, dict(st_p, pk_buf=pk_buf, pv_buf=pv_buf), cfg_p)
        pk_buf, pv_buf = new_p.pop("pk_buf"), new_p.pop("pv_buf")
        st_s = {"hist0": hist_l, "c0": c_l, "n0": n_l, "m0": m_l, "s0": s_l,
                "past_k": past_k, "past_v": past_v, "mk": mk_s, "mv": mv_s}
        xsm, new_s = _trunk_layer(xsm, lidx, w, st_s, cfg_s)
        return (xp, xsm, pk_buf, pv_buf), (new_p, new_s)

    xs = (jnp.arange(depth, dtype=jnp.int32), hist_s, state_mlstm_C, n_s, m_s, state_hgrn_S)
    kv_rows = depth * bp * tp * DF_H
    init = (x_prompt.reshape(bp * tp, D_MODEL), x_sample.reshape(bs * ts, D_MODEL),
            lax.empty((kv_rows, DF_DV), F32), lax.empty((kv_rows, DF_DV), F32))
    (xp, xsm, pk_buf, pv_buf), (new_p, new_s) = lax.scan(layer, init, xs)

    y_prompt = xp.reshape(bp, tp, D_MODEL)
    y_sample = xsm.reshape(bs, ts, D_MODEL)
    p_mem_k = mk_p.reshape(depth, bp, N_MEM, CA_H, CA_DH)
    p_mem_v = mv_p.reshape(depth, bp, N_MEM, CA_H, CA_DH)
    p_attn_k = pk_buf.reshape(depth, bp, tp, DF_H, 2 * DF_DH)
    p_attn_v = pv_buf.reshape(depth, bp, tp, DF_H, DF_DV)
    return (y_prompt, y_sample,
            p_attn_k, p_attn_v, p_mem_k, p_mem_v, new_p["conv"], new_p["C"], new_p["n"],
            new_p["m"], new_p["S"],
            new_s["attn_k"], new_s["attn_v"], new_s["conv"], new_s["C"], new_s["n"], new_s["m"], new_s["S"])
```

```python
import functools
import math

import numpy as np
import jax
import jax.numpy as jnp
from jax import lax
from jax.experimental import pallas as pl
from jax.experimental.pallas import tpu as pltpu

F32 = jnp.float32
BF16 = jnp.bfloat16

D_MODEL = 2048
DEPTH = 4
CHUNK = 64
CHUNK_SHIFT = 6
ML_DH = 128
ML_W = 768
ML_H = 6
CONV_W = 4
DF_DH = 64
DF_DV = 128
DF_W = 512
DF_H = 4
HG_DK = 128
HG_DV = 128
HG_W = 768
HG_H = 6
CA_H = 4
CA_DH = 128
CA_W = 512
N_MEM = 256
FF_HIDDEN = 5632
EPS = 1e-6
NEG_BIG = -1e30
LB_FLOOR = 1e-30
LOG2E = math.log2(math.e)

Z_W = 2 * ML_W + 2 * ML_W + 3 * DF_W + 4 * HG_W
GATE_W = 128
ZB_MLQ, ZB_MLK, ZB_MLV, ZB_MLO = 0, 1, 2, 3
ZB_HGQ, ZB_HGF, ZB_HGI, ZB_HGG = 6, 7, 8, 9
ZB_DFQ, ZB_DFK, ZB_DFV = 6, 7, 8

V7X_VMEM_BYTES = 64 * 1024 * 1024
V7X_VMEM_CAP = 58 * 1024 * 1024
VMEM_SLACK = 6 * 1024 * 1024
HIST = 8

NT_DIMS = (((1,), (1,)), ((), ()))
TN_DIMS = (((0,), (0,)), ((), ()))


def _vmem_limit(nbytes):
    return int(min(V7X_VMEM_CAP, max(32 * 1024 * 1024, nbytes + VMEM_SLACK)))


def _cparams(sem, vmem_bytes):
    return pltpu.CompilerParams(dimension_semantics=sem, vmem_limit_bytes=_vmem_limit(vmem_bytes))


def _rms(x, g):
    ms = jnp.mean(x * x, axis=-1, keepdims=True)
    return x * lax.rsqrt(ms + EPS) * g


def _dot(a, b):
    return jnp.dot(a, b, preferred_element_type=F32)


def _dot_nt(a, b):
    return lax.dot_general(a, b, NT_DIMS, preferred_element_type=F32)


def _dot_tn(a, b):
    return lax.dot_general(a, b, TN_DIMS, preferred_element_type=F32)


def _log_sigmoid(x):
    return jnp.minimum(x, 0.0) - jnp.log1p(jnp.exp(-jnp.abs(x)))


def _split3(x):
    hi = x.astype(BF16)
    r1 = x - hi.astype(F32)
    mid = r1.astype(BF16)
    lo = (r1 - mid.astype(F32)).astype(BF16)
    return hi, mid, lo


def _dot_exact01(m01, x):
    hi, mid, lo = _split3(x)
    return _dot(m01, hi) + _dot(m01, mid) + _dot(m01, lo)


def _in_proj_kernel(l_ref, x_ref, g_ref, w_ref, wg_ref, z_ref, gate_ref, xn_ref):
    j = pl.program_id(1)

    @pl.when(j == 0)
    def _():
        xn = _rms(x_ref[...], g_ref[0]).astype(BF16)
        xn_ref[...] = xn
        gate_ref[...] = _dot(xn, wg_ref[0])

    z_ref[...] = _dot(xn_ref[...], w_ref[0])


def in_proj(x, lidx, g, w_main, w_gate, *, tm, tn):
    n_rows = x.shape[0]
    grid = (n_rows // tm, Z_W // tn)
    vmem = 2 * tm * D_MODEL * 4 + 2 * D_MODEL * tn * 2 + 2 * tm * tn * 4 + tm * D_MODEL * 2 \
        + 2 * tm * GATE_W * 4 + 2 * D_MODEL * GATE_W * 2 + (4 << 20)
    return pl.pallas_call(
        _in_proj_kernel,
        grid_spec=pltpu.PrefetchScalarGridSpec(
            num_scalar_prefetch=1,
            grid=grid,
            in_specs=[
                pl.BlockSpec((tm, D_MODEL), lambda i, j, l: (i, 0)),
                pl.BlockSpec((1, 1, D_MODEL), lambda i, j, l: (l[0], 0, 0)),
                pl.BlockSpec((1, D_MODEL, tn), lambda i, j, l: (l[0], 0, j)),
                pl.BlockSpec((1, D_MODEL, GATE_W), lambda i, j, l: (l[0], 0, 0)),
            ],
            out_specs=[
                pl.BlockSpec((tm, tn), lambda i, j, l: (i, j)),
                pl.BlockSpec((tm, GATE_W), lambda i, j, l: (i, 0)),
            ],
            scratch_shapes=[pltpu.VMEM((tm, D_MODEL), BF16)],
        ),
        out_shape=[jax.ShapeDtypeStruct((n_rows, Z_W), F32),
                   jax.ShapeDtypeStruct((n_rows, GATE_W), F32)],
        compiler_params=_cparams(("parallel", "arbitrary"), vmem),
        name="in_proj",
    )(lidx, x, g, w_main, w_gate)


def _ffn_kernel(l_ref, x_ref, g_ref, wg_ref, wu_ref, wd_ref, fg_ref, o_ref, xn_ref):
    j = pl.program_id(1)
    nj = pl.num_programs(1)

    @pl.when(j == 0)
    def _():
        x = x_ref[...]
        xn_ref[...] = _rms(x, g_ref[0]).astype(BF16)
        o_ref[...] = x

    xn = xn_ref[...]
    gt = _dot(xn, wg_ref[0])
    up = _dot(xn, wu_ref[0])
    act = (gt * jax.nn.sigmoid(gt) * up).astype(BF16)
    o_ref[...] += _dot(act, wd_ref[0])

    @pl.when(jnp.logical_and(j == nj - 1, l_ref[0] == DEPTH - 1))
    def _():
        o_ref[...] = _rms(o_ref[...], fg_ref[...])


def ffn(x, lidx, g, w_gu, w_down, final_g, *, tm, th, x_buffers=2):
    n_rows = x.shape[0]
    nh = FF_HIDDEN // th
    grid = (n_rows // tm, nh)
    vmem = (2 + x_buffers) * tm * D_MODEL * 4 + tm * D_MODEL * 2 + 6 * D_MODEL * th * 2 + 3 * tm * th * 4 \
        + tm * D_MODEL * 4 + (4 << 20)
    x_mode = {} if x_buffers == 2 else {"pipeline_mode": pl.Buffered(x_buffers)}
    return pl.pallas_call(
        _ffn_kernel,
        grid_spec=pltpu.PrefetchScalarGridSpec(
            num_scalar_prefetch=1,
            grid=grid,
            in_specs=[
                pl.BlockSpec((tm, D_MODEL), lambda i, j, l: (i, 0), **x_mode),
                pl.BlockSpec((1, 1, D_MODEL), lambda i, j, l: (l[0], 0, 0)),
                pl.BlockSpec((1, D_MODEL, th), lambda i, j, l: (l[0], 0, j)),
                pl.BlockSpec((1, D_MODEL, th), lambda i, j, l: (l[0], 0, j + nh)),
                pl.BlockSpec((1, th, D_MODEL), lambda i, j, l: (l[0], j, 0)),
                pl.BlockSpec((1, D_MODEL), lambda i, j, l: (0, 0)),
            ],
            out_specs=pl.BlockSpec((tm, D_MODEL), lambda i, j, l: (i, 0)),
            scratch_shapes=[pltpu.VMEM((tm, D_MODEL), BF16)],
        ),
        out_shape=jax.ShapeDtypeStruct((n_rows, D_MODEL), F32),
        compiler_params=_cparams(("parallel", "arbitrary"), vmem),
        name="ffn",
    )(lidx, x, g, w_gu, w_gu, w_down, final_g)


def _mem_kv_kernel(mem_ref, g_ref, wk_ref, wv_ref, kg_ref, k_ref, v_ref):
    mn = _rms(mem_ref[0], g_ref[0]).astype(BF16)
    k = _dot(mn, wk_ref[0])
    for h in range(CA_H):
        sl = slice(h * CA_DH, (h + 1) * CA_DH)
        k_ref[0, :, sl] = _rms(k[:, sl], kg_ref[0])
    v_ref[0] = _dot(mn, wv_ref[0])


def mem_kv(mem, g, wk, wv, kg):
    nb = mem.shape[0]
    out = jax.ShapeDtypeStruct((DEPTH * nb, N_MEM, CA_W), F32)
    return pl.pallas_call(
        _mem_kv_kernel,
        grid=(DEPTH, nb),
        in_specs=[
            pl.BlockSpec((1, N_MEM, D_MODEL), lambda l, b: (b, 0, 0)),
            pl.BlockSpec((1, 1, D_MODEL), lambda l, b: (l, 0, 0)),
            pl.BlockSpec((1, D_MODEL, CA_W), lambda l, b: (l, 0, 0)),
            pl.BlockSpec((1, D_MODEL, CA_W), lambda l, b: (l, 0, 0)),
            pl.BlockSpec((1, 1, CA_DH), lambda l, b: (l, 0, 0)),
        ],
        out_specs=[pl.BlockSpec((1, N_MEM, CA_W), lambda l, b: (l * nb + b, 0, 0)),
                   pl.BlockSpec((1, N_MEM, CA_W), lambda l, b: (l * nb + b, 0, 0))],
        out_shape=[out, out],
        compiler_params=_cparams(("arbitrary", "arbitrary"), 24 << 20),
        name="mem_kv",
    )(mem, g, wk, wv, kg)


def _post_kernel(l_ref, x_ref, yml_ref, ydf_ref, yhg_ref, wout_ref, gx_ref, wq_ref, qg_ref,
                 mk_ref, mv_ref, wo_ref, o_ref, *, rows_per_stream):
    x1 = x_ref[...]
    x1 = x1 + _dot(yml_ref[...], wout_ref[0, 0:ML_W, :])
    x1 = x1 + _dot(ydf_ref[...], wout_ref[0, ML_W:ML_W + DF_W, :])
    x1 = x1 + _dot(yhg_ref[...], wout_ref[0, ML_W + DF_W:D_MODEL, :])
    hn = _rms(x1, gx_ref[0]).astype(BF16)
    q = _dot(hn, wq_ref[0])
    streams = []
    for si in range(x1.shape[0] // rows_per_stream):
        rs = slice(si * rows_per_stream, (si + 1) * rows_per_stream)
        heads = []
        for h in range(CA_H):
            sl = slice(h * CA_DH, (h + 1) * CA_DH)
            qh = _rms(q[rs, sl], qg_ref[0]).astype(BF16)
            s = _dot_nt(qh, mk_ref[si, :, sl].astype(BF16)) * (CA_DH ** -0.5)
            s = s - jnp.max(s, axis=-1, keepdims=True)
            p = jnp.exp(s)
            p = p / jnp.sum(p, axis=-1, keepdims=True)
            heads.append(_dot(p.astype(BF16), mv_ref[si, :, sl].astype(BF16)))
        streams.append(jnp.concatenate(heads, axis=-1))
    o = (streams[0] if len(streams) == 1 else jnp.concatenate(streams, axis=0)).astype(BF16)
    o_ref[...] = x1 + _dot(o, wo_ref[0])


def post_mix(x, y_ml, y_df, y_hg, lidx, w_out, gx, wq, qg, mk, mv, wo, *, tm, rows_per_batch):
    n_rows = x.shape[0]
    nb = n_rows // rows_per_batch
    grid = (n_rows // tm,)
    w_bytes = (D_MODEL * D_MODEL + 2 * D_MODEL * CA_W) * 2
    row = lambda i, l: (i, 0)
    if tm >= rows_per_batch:
        spt = tm // rows_per_batch
        mem_map = lambda i, l: (l[0] * (nb // spt) + i, 0, 0)
    else:
        spt = 1
        tiles_per_batch = rows_per_batch // tm
        mem_map = lambda i, l: (l[0] * nb + i // tiles_per_batch, 0, 0)
    vmem = 2 * w_bytes + 6 * tm * D_MODEL * 4 + 2 * tm * D_MODEL * 2 + 4 * spt * N_MEM * CA_W * 4 + (6 << 20)
    kern = functools.partial(_post_kernel, rows_per_stream=min(tm, rows_per_batch))
    return pl.pallas_call(
        kern,
        grid_spec=pltpu.PrefetchScalarGridSpec(
            num_scalar_prefetch=1,
            grid=grid,
            in_specs=[
                pl.BlockSpec((tm, D_MODEL), row),
                pl.BlockSpec((tm, ML_W), row),
                pl.BlockSpec((tm, DF_W), row),
                pl.BlockSpec((tm, HG_W), row),
                pl.BlockSpec((1, D_MODEL, D_MODEL), lambda i, l: (l[0], 0, 0)),
                pl.BlockSpec((1, 1, D_MODEL), lambda i, l: (l[0], 0, 0)),
                pl.BlockSpec((1, D_MODEL, CA_W), lambda i, l: (l[0], 0, 0)),
                pl.BlockSpec((1, 1, CA_DH), lambda i, l: (l[0], 0, 0)),
                pl.BlockSpec((spt, N_MEM, CA_W), mem_map),
                pl.BlockSpec((spt, N_MEM, CA_W), mem_map),
                pl.BlockSpec((1, CA_W, D_MODEL), lambda i, l: (l[0], 0, 0)),
            ],
            out_specs=pl.BlockSpec((tm, D_MODEL), row),
        ),
        out_shape=jax.ShapeDtypeStruct((n_rows, D_MODEL), F32),
        compiler_params=_cparams(("parallel",), vmem),
        name="post_mix",
    )(lidx, x, y_ml, y_df, y_hg, w_out, gx, wq, qg, mk, mv, wo)


def _mlstm_kernel(l_ref, zq_ref, zk_ref, zv_ref, zo_ref, gate_ref, hist_ref, cw_ref, cb_ref, bif_ref,
                  og_ref, c0_ref, n0_ref, m0_ref, tri_ref,
                  y_ref, c_ref, n_ref, m_ref, ext_ref, *, tc, lc):
    c = pl.program_id(1)

    @pl.when(c == 0)
    def _():
        c_ref[...] = c0_ref[...]
        n_ref[...] = n0_ref[...]
        m_ref[...] = m0_ref[...]
        ext_ref[0:HIST, :] = hist_ref[0]

    ext_ref[HIST:HIST + tc, 0:ML_W] = zq_ref[...]
    ext_ref[HIST:HIST + tc, ML_W:2 * ML_W] = zk_ref[...]
    ext = ext_ref[...]
    acc = cb_ref[0] + cw_ref[0, CONV_W - 1:CONV_W, :] * ext[HIST:HIST + tc]
    for j in range(CONV_W - 1):
        acc = acc + cw_ref[0, j:j + 1, :] * pltpu.roll(ext, CONV_W - 1 - j, axis=0)[HIST:HIST + tc]
    tail = ext_ref[tc:tc + HIST, :]
    ext_ref[0:HIST, :] = tail
    qk = acc * jax.nn.sigmoid(acc)

    gz = gate_ref[...] + bif_ref[0]
    lf_all = pltpu.roll(_log_sigmoid(gz), GATE_W - ML_H, axis=1)
    tri = tri_ref[...]
    row_i = lax.broadcasted_iota(jnp.int32, (lc, lc), 0)
    col_i = lax.broadcasted_iota(jnp.int32, (lc, lc), 1)
    causal = row_i >= col_i
    diag = row_i == col_i
    trow = lax.broadcasted_iota(jnp.int32, (lc, GATE_W), 0)
    og = og_ref[0]

    for ci in range(tc // lc):
        r0 = ci * lc
        b_c = _dot_exact01(tri, lf_all[r0:r0 + lc, :])
        r_c = gz[r0:r0 + lc, :] - b_c
        cm = r_c
        d = 1
        while d < lc:
            cm = jnp.maximum(cm, jnp.where(trow >= d, pltpu.roll(cm, d, axis=0), NEG_BIG))
            d *= 2
        m_prev = m_ref[0]
        mx = jnp.maximum(m_prev, cm)
        m_t = b_c + mx
        w_inter_c = jnp.exp(m_prev - mx)
        emt_c = jnp.exp(-m_t)
        m_new = m_t[lc - 1:lc, :]
        b_last = b_c[lc - 1:lc, :]
        w_s_c = jnp.exp(b_last + r_c - m_new)
        decay_c = jnp.exp(b_last + m_prev - m_new)
        m_ref[0] = m_new
        for h in range(ML_H):
            sl = slice(h * ML_DH, (h + 1) * ML_DH)
            hl = slice(h, h + 1)
            q = qk[r0:r0 + lc, h * ML_DH:(h + 1) * ML_DH]
            k = qk[r0:r0 + lc, ML_W + h * ML_DH:ML_W + (h + 1) * ML_DH] * (ML_DH ** -0.5)
            v = zv_ref[r0:r0 + lc, sl]
            qb = q.astype(BF16)
            r_row = jnp.sum(jnp.where(diag, r_c[:, hl], 0.0), axis=0, keepdims=True)
            d_mat = jnp.exp(jnp.where(causal, r_row - mx[:, hl], NEG_BIG))
            w_inter = w_inter_c[:, hl]
            decay = decay_c[:, hl]
            s = _dot_nt(qb, k.astype(BF16)) * d_mat
            c_old = c_ref[0, h]
            n_old = n_ref[0, h:h + 1, :]
            num = _dot(s.astype(BF16), v.astype(BF16)) + w_inter * _dot(qb, c_old.astype(BF16))
            den = jnp.sum(s, axis=-1, keepdims=True) + w_inter * jnp.sum(q * n_old, axis=-1, keepdims=True)
            hh = num / jnp.maximum(jnp.abs(den), emt_c[:, hl])
            kw = k * w_s_c[:, hl]
            c_ref[0, h] = decay * c_old + _dot_tn(kw.astype(BF16), v.astype(BF16))
            n_ref[0, h:h + 1, :] = decay * n_old + jnp.sum(kw, axis=0, keepdims=True)
            o_gate = jax.nn.sigmoid(zo_ref[r0:r0 + lc, sl])
            y_ref[r0:r0 + lc, sl] = (_rms(hh, og) * o_gate).astype(BF16)


def mlstm_mixer(z, gates, lidx, hist0, conv_w, conv_b, b_if, out_g, c0, n0, m0, *, nb, seq, tc, lc):
    n_rows = nb * seq
    nc = seq // tc
    tri = jnp.asarray(np.tril(np.ones((lc, lc), np.float32)), BF16)
    zmap = lambda blk: (lambda b, c, l: (b * nc + c, blk))
    lmap3 = lambda b, c, l: (l[0], 0, 0)
    vmem = 2 * 5 * tc * ML_W * 4 + (HIST + tc) * 2 * ML_W * 4 + 4 * ML_H * ML_DH * ML_DH * 4 \
        + 6 * tc * 2 * ML_W * 4 + (8 << 20)
    kern = functools.partial(_mlstm_kernel, tc=tc, lc=lc)
    return pl.pallas_call(
        kern,
        grid_spec=pltpu.PrefetchScalarGridSpec(
            num_scalar_prefetch=1,
            grid=(nb, nc),
            in_specs=[
                pl.BlockSpec((tc, ML_W), zmap(ZB_MLQ)),
                pl.BlockSpec((tc, ML_W), zmap(ZB_MLK)),
                pl.BlockSpec((tc, ML_W), zmap(ZB_MLV)),
                pl.BlockSpec((tc, ML_W), zmap(ZB_MLO)),
                pl.BlockSpec((tc, GATE_W), lambda b, c, l: (b * nc + c, 0)),
                pl.BlockSpec((1, HIST, 2 * ML_W), lambda b, c, l: (b, 0, 0)),
                pl.BlockSpec((1, CONV_W, 2 * ML_W), lmap3),
                pl.BlockSpec((1, 1, 2 * ML_W), lmap3),
                pl.BlockSpec((1, 1, GATE_W), lmap3),
                pl.BlockSpec((1, 1, ML_DH), lmap3),
                pl.BlockSpec((1, ML_H, ML_DH, ML_DH), lambda b, c, l: (b, 0, 0, 0)),
                pl.BlockSpec((1, 8, ML_DH), lambda b, c, l: (b, 0, 0)),
                pl.BlockSpec((1, 1, GATE_W), lambda b, c, l: (b, 0, 0)),
                pl.BlockSpec((lc, lc), lambda b, c, l: (0, 0)),
            ],
            out_specs=[
                pl.BlockSpec((tc, ML_W), lambda b, c, l: (b * nc + c, 0)),
                pl.BlockSpec((1, ML_H, ML_DH, ML_DH), lambda b, c, l: (b, 0, 0, 0)),
                pl.BlockSpec((1, 8, ML_DH), lambda b, c, l: (b, 0, 0)),
                pl.BlockSpec((1, 1, GATE_W), lambda b, c, l: (b, 0, 0)),
            ],
            scratch_shapes=[pltpu.VMEM((HIST + tc, 2 * ML_W), F32)],
        ),
        out_shape=[
            jax.ShapeDtypeStruct((n_rows, ML_W), BF16),
            jax.ShapeDtypeStruct((nb, ML_H, ML_DH, ML_DH), F32),
            jax.ShapeDtypeStruct((nb, 8, ML_DH), F32),
            jax.ShapeDtypeStruct((nb, 1, GATE_W), F32),
        ],
        compiler_params=_cparams(("parallel", "arbitrary"), vmem),
        name="mlstm",
    )(lidx, z, z, z, z, gates, hist0, conv_w, conv_b, b_if, out_g, c0, n0, m0, tri)


def _hgrn_level_masks(lc):
    t = np.arange(lc)[:, None]
    s = np.arange(lc)[None, :]
    masks = []
    h = lc // 2
    while h >= 1:
        odd = (t % (2 * h)) >= h
        same = (t // (2 * h)) == (s // (2 * h))
        masks.append((same & odd & ((s % (2 * h)) < h)).astype(np.float32))
        h //= 2
    masks.append((t == s).astype(np.float32))
    return np.stack(masks, axis=0)


def _hgrn_boundary_rows(a, h):
    lc, width = a.shape
    if h >= 8:
        parts = [jnp.broadcast_to(a[g * 2 * h + h - 1:g * 2 * h + h, :], (2 * h, width)) for g in range(lc // (2 * h))]
        return parts[0] if len(parts) == 1 else jnp.concatenate(parts, axis=0)
    a3 = a.reshape(lc // 8, 8, width)
    sub = lax.broadcasted_iota(jnp.int32, a3.shape, 1)
    out = None
    for g in range(8 // (2 * h)):
        src = g * 2 * h + h - 1
        b = jnp.broadcast_to(a3[:, src:src + 1, :], a3.shape)
        out = b if out is None else jnp.where(sub >= g * 2 * h, b, out)
    return out.reshape(lc, width)


def _hgrn_kernel(l_ref, zq_ref, zf_ref, zi_ref, zg_ref, lbl_ref, og_ref, s0_ref, tri_ref, msk_ref,
                 y_ref, s_ref, st_ref, *, tc, lc, nlev):
    c = pl.program_id(1)
    nc = pl.num_programs(1)

    @pl.when(c == 0)
    def _():
        for h in range(HG_H):
            st_ref[h] = s0_ref[0, h].T

    logits = lbl_ref[...]
    e = jnp.exp(logits - jnp.max(logits, axis=0, keepdims=True))
    p = e / jnp.sum(e, axis=0, keepdims=True)
    drow = lax.broadcasted_iota(jnp.int32, p.shape, 0)
    lb = jnp.sum(jnp.where(jnp.logical_and(drow >= 1, drow <= l_ref[0]), p, 0.0), axis=0, keepdims=True)
    lb_fl = jnp.maximum(lb, LB_FLOOR)
    one_m = 1.0 - lb

    zf = zf_ref[...]
    ez = jnp.exp(-jnp.abs(zf))
    inv = 1.0 / (1.0 + ez)
    sig_p = jnp.where(zf >= 0.0, inv, ez * inv)
    sig_n = jnp.where(zf >= 0.0, ez * inv, inv)
    lf2 = jnp.log(lb_fl + one_m * sig_p) * LOG2E
    kk = one_m * sig_n + (lb - lb_fl)
    zq = zq_ref[...]
    qq = zq * jax.nn.sigmoid(zq) * (HG_DK ** -0.5)
    tri = tri_ref[...]
    og = og_ref[0]

    for ci in range(tc // lc):
        r0 = ci * lc
        a_all = _dot_exact01(tri, lf2[r0:r0 + lc, :])
        lev_exp = [-jnp.abs(a_all - _hgrn_boundary_rows(a_all, lc >> (lv + 1))) for lv in range(nlev)]
        for h in range(HG_H):
            sl = slice(h * HG_DK, (h + 1) * HG_DK)
            q = qq[r0:r0 + lc, sl]
            k = kk[r0:r0 + lc, sl]
            iv = zi_ref[r0:r0 + lc, sl].astype(BF16)
            a_in = a_all[:, sl]
            a_end = a_in[lc - 1:lc, :]
            st = st_ref[h]
            o = _dot_nt((q * jnp.exp2(a_in)).astype(BF16), st.astype(BF16))
            att = jnp.where(msk_ref[nlev] > 0, _dot_nt(q.astype(BF16), k.astype(BF16)), 0.0)
            for lv in range(nlev):
                xf = jnp.exp2(lev_exp[lv][:, sl])
                pm = _dot_nt((q * xf).astype(BF16), (k * xf).astype(BF16))
                att = att + jnp.where(msk_ref[lv] > 0, pm, 0.0)
            o = o + _dot(att.astype(BF16), iv)
            k_end = (k * jnp.exp2(a_end - a_in)).astype(BF16)
            st_ref[h] = st * jnp.exp2(a_end) + _dot_tn(iv, k_end)
            gsl = zg_ref[r0:r0 + lc, sl]
            y_ref[r0:r0 + lc, sl] = (_rms(o, og) * (gsl * jax.nn.sigmoid(gsl))).astype(BF16)

    @pl.when(c == nc - 1)
    def _():
        for h in range(HG_H):
            s_ref[0, h] = st_ref[h].T


def hgrn_mixer(z, lidx, lb_logits, out_g, s0, *, nb, seq, tc, lc):
    n_rows = nb * seq
    nc = seq // tc
    msk_np = _hgrn_level_masks(lc)
    nlev = msk_np.shape[0] - 1
    tri = jnp.asarray(np.tril(np.ones((lc, lc), np.float32)), BF16)
    msk = jnp.asarray(msk_np, F32)
    zmap = lambda blk: (lambda b, c, l: (b * nc + c, blk))
    vmem = 2 * 5 * tc * HG_W * 4 + 5 * HG_H * HG_DK * HG_DV * 4 + 8 * tc * HG_W * 4 + (8 << 20)
    kern = functools.partial(_hgrn_kernel, tc=tc, lc=lc, nlev=nlev)
    return pl.pallas_call(
        kern,
        grid_spec=pltpu.PrefetchScalarGridSpec(
            num_scalar_prefetch=1,
            grid=(nb, nc),
            in_specs=[
                pl.BlockSpec((tc, HG_W), zmap(ZB_HGQ)),
                pl.BlockSpec((tc, HG_W), zmap(ZB_HGF)),
                pl.BlockSpec((tc, HG_W), zmap(ZB_HGI)),
                pl.BlockSpec((tc, HG_W), zmap(ZB_HGG)),
                pl.BlockSpec((DEPTH, HG_W), lambda b, c, l: (0, 0)),
                pl.BlockSpec((1, 1, HG_DV), lambda b, c, l: (l[0], 0, 0)),
                pl.BlockSpec((1, HG_H, HG_DK, HG_DV), lambda b, c, l: (b, 0, 0, 0)),
                pl.BlockSpec((lc, lc), lambda b, c, l: (0, 0)),
                pl.BlockSpec(msk_np.shape, lambda b, c, l: (0, 0, 0)),
            ],
            out_specs=[
                pl.BlockSpec((tc, HG_W), lambda b, c, l: (b * nc + c, 0)),
                pl.BlockSpec((1, HG_H, HG_DK, HG_DV), lambda b, c, l: (b, 0, 0, 0)),
            ],
            scratch_shapes=[pltpu.VMEM((HG_H, HG_DV, HG_DK), F32)],
        ),
        out_shape=[
            jax.ShapeDtypeStruct((n_rows, HG_W), BF16),
            jax.ShapeDtypeStruct((nb, HG_H, HG_DK, HG_DV), F32),
        ],
        compiler_params=_cparams(("parallel", "arbitrary"), vmem),
        name="hgrn",
    )(lidx, z, z, z, z, lb_logits, out_g, s0, tri, msk)


def _qk_prep_kernel(l_ref, zq_ref, zk_ref, zv_ref, qg_ref, kg_ref, grp_ref, qn_ref, kn_ref, knb_ref, vb_ref):
    grp = grp_ref[...]
    qn_ref[...] = (_group_norm64(zq_ref[...], qg_ref[0], grp) * (DF_DH ** -0.5)).astype(BF16)
    kn = _group_norm64(zk_ref[...], kg_ref[0], grp)
    kn_ref[...] = kn
    knb_ref[...] = kn.astype(BF16)
    vb_ref[...] = zv_ref[...].astype(BF16)


def qk_prep(z, lidx, qg, kg, *, tm):
    n_rows = z.shape[0]
    lane = np.arange(DF_W)
    grp = jnp.asarray(((lane[:, None] // DF_DH) == (lane[None, :] // DF_DH)).astype(np.float32) / DF_DH, BF16)
    zmap = lambda blk: (lambda i, l: (i, blk))
    row = lambda i, l: (i, 0)
    return pl.pallas_call(
        _qk_prep_kernel,
        grid_spec=pltpu.PrefetchScalarGridSpec(
            num_scalar_prefetch=1,
            grid=(n_rows // tm,),
            in_specs=[
                pl.BlockSpec((tm, DF_W), zmap(ZB_DFQ)),
                pl.BlockSpec((tm, DF_W), zmap(ZB_DFK)),
                pl.BlockSpec((tm, DF_W), zmap(ZB_DFV)),
                pl.BlockSpec((1, 1, DF_W), lambda i, l: (l[0], 0, 0)),
                pl.BlockSpec((1, 1, DF_W), lambda i, l: (l[0], 0, 0)),
                pl.BlockSpec((DF_W, DF_W), lambda i, l: (0, 0)),
            ],
            out_specs=[pl.BlockSpec((tm, DF_W), row)] * 4,
        ),
        out_shape=[
            jax.ShapeDtypeStruct((n_rows, DF_W), BF16),
            jax.ShapeDtypeStruct((n_rows, DF_W), F32),
            jax.ShapeDtypeStruct((n_rows, DF_W), BF16),
            jax.ShapeDtypeStruct((n_rows, DF_W), BF16),
        ],
        compiler_params=_cparams(("parallel",), 32 << 20),
        name="qk_prep",
    )(lidx, z, z, z, qg, kg, grp)


def _group_norm64(x, g, grp):
    x2 = x * x
    hi = x2.astype(BF16)
    lo = (x2 - hi.astype(F32)).astype(BF16)
    ms = _dot(hi, grp) + _dot(lo, grp)
    return x * lax.rsqrt(ms + EPS) * g


BIAS_W = 128


def _qk_prep_t_kernel(l_ref, zq_ref, zk_ref, zv_ref, qg_ref, kg_ref, grp_ref, pk_in, pv_in,
                      qt_ref, ke_ref, vt_ref, pk_ref, pv_ref, *, tm, seq):
    del pk_in, pv_in
    i = pl.program_id(0)
    grp = grp_ref[...]
    qn = _group_norm64(zq_ref[...], qg_ref[0], grp) * (DF_DH ** -0.5)
    kn = _group_norm64(zk_ref[...], kg_ref[0], grp)
    zv = zv_ref[...]
    pos = lax.rem(i * tm, seq) + lax.broadcasted_iota(jnp.int32, (tm, BIAS_W), 0)
    lane = lax.broadcasted_iota(jnp.int32, (tm, BIAS_W), 1)
    hi_part = (pos >> CHUNK_SHIFT).astype(F32) * float(CHUNK)
    lo_part = (pos & (CHUNK - 1)).astype(F32)
    base = jnp.where(lane == 0, hi_part, jnp.where(lane == 1, lo_part, jnp.where(lane == 2, float(CHUNK), 0.0)))
    for h in range(DF_H):
        sl = slice(h * DF_DV, (h + 1) * DF_DV)
        slope = 2.0 ** (-8.0 * (h + 1) / DF_H)
        c0 = h * (DF_DV + BIAS_W)
        ke_ref[:, c0:c0 + DF_DV] = kn[:, sl].astype(BF16)
        ke_ref[:, c0 + DF_DV:c0 + DF_DV + BIAS_W] = (base * slope).astype(BF16)
        qt_ref[h, 0] = qn[:, sl].T.astype(BF16)
        vt_ref[h, 0] = zv[:, sl].T.astype(BF16)
        rows = pl.ds(h, tm, stride=DF_H)
        pk_ref[rows, :] = kn[:, sl]
        pv_ref[rows, :] = zv[:, sl]


def qk_prep_t(z, lidx, qg, kg, pk_buf, pv_buf, *, tm, seq):
    n_rows = z.shape[0]
    nt = n_rows // tm
    smap = lambda i, l: (l[0] * nt + i, 0)
    any_spec = pl.BlockSpec(memory_space=pl.ANY)
    lane = np.arange(DF_W)
    grp = jnp.asarray(((lane[:, None] // DF_DH) == (lane[None, :] // DF_DH)).astype(np.float32) / DF_DH, BF16)
    zmap = lambda blk: (lambda i, l: (i, blk))
    row = lambda i, l: (i, 0)
    tmap = lambda i, l: (0, i, 0, 0)
    kern = functools.partial(_qk_prep_t_kernel, tm=tm, seq=seq)
    return pl.pallas_call(
        kern,
        grid_spec=pltpu.PrefetchScalarGridSpec(
            num_scalar_prefetch=1,
            grid=(nt,),
            in_specs=[
                pl.BlockSpec((tm, DF_W), zmap(ZB_DFQ)),
                pl.BlockSpec((tm, DF_W), zmap(ZB_DFK)),
                pl.BlockSpec((tm, DF_W), zmap(ZB_DFV)),
                pl.BlockSpec((1, 1, DF_W), lambda i, l: (l[0], 0, 0)),
                pl.BlockSpec((1, 1, DF_W), lambda i, l: (l[0], 0, 0)),
                pl.BlockSpec((DF_W, DF_W), lambda i, l: (0, 0)),
                any_spec,
                any_spec,
            ],
            out_specs=[
                pl.BlockSpec((DF_H, 1, DF_DV, tm), tmap),
                pl.BlockSpec((tm, DF_H * (DF_DV + BIAS_W)), row),
                pl.BlockSpec((DF_H, 1, DF_DV, tm), tmap),
                pl.BlockSpec((tm * DF_H, DF_DV), smap),
                pl.BlockSpec((tm * DF_H, DF_DV), smap),
            ],
        ),
        out_shape=[
            jax.ShapeDtypeStruct((DF_H, nt, DF_DV, tm), BF16),
            jax.ShapeDtypeStruct((n_rows, DF_H * (DF_DV + BIAS_W)), BF16),
            jax.ShapeDtypeStruct((DF_H, nt, DF_DV, tm), BF16),
            jax.ShapeDtypeStruct(pk_buf.shape, F32),
            jax.ShapeDtypeStruct(pv_buf.shape, F32),
        ],
        input_output_aliases={7: 3, 8: 4},
        compiler_params=_cparams(("parallel",), 40 << 20),
        name="qk_prep_t",
    )(lidx, z, z, z, qg, kg, grp, pk_buf, pv_buf)


ONES_ROWS = 16


def _flash_t_kernel(l_ref, qt_ref, ke_ref, vt_ref, lam_ref, ogt_ref, y_ref, m_ref, acc_ref, sa_ref, sb_ref,
                    own_ref, *, tq):
    h = pl.program_id(1)
    qi = pl.program_id(2)
    tk = tq
    slope = jnp.exp2(jnp.full((1, 1), -8.0 / DF_H, F32) * (h + 1).astype(F32))
    qt = qt_ref[0, 0]
    row = lax.broadcasted_iota(jnp.int32, (DF_DV, tq), 0)
    zero = jnp.zeros_like(qt)
    q2t = jnp.concatenate([jnp.where(row < DF_DH, qt, zero), jnp.where(row >= DF_DH, qt, zero)], axis=1)
    brow = lax.broadcasted_iota(jnp.int32, (BIAS_W, 2 * tq), 0)
    a0 = ((qi * tq) >> CHUNK_SHIFT).astype(F32)
    extra = jnp.where(brow < 2, 1.0, jnp.where(brow == 2, -a0, 0.0)).astype(BF16)
    q2e = jnp.concatenate([q2t, extra], axis=0)
    ones = jnp.ones((ONES_ROWS, tk), BF16)

    m_ref[...] = jnp.full(m_ref.shape, NEG_BIG, F32)
    acc_ref[...] = jnp.zeros(acc_ref.shape, F32)

    def scores_into(dst_ref, kj):
        r0 = pl.multiple_of(kj * tk, tk)
        dst_ref[...] = _dot(ke_ref[pl.ds(r0, tk), :], q2e)

    def update(s, kj):
        m_old = m_ref[...]
        m_new = jnp.maximum(m_old, jnp.max(s, axis=0, keepdims=True))
        alpha = jnp.exp(m_old - m_new)
        p = jnp.exp(s - m_new).astype(BF16)
        vt_ext = jnp.concatenate([vt_ref[0, kj], ones], axis=0)
        acc_ref[...] = alpha * acc_ref[...] + _dot(vt_ext, p)
        m_ref[...] = m_new

    @pl.when(qi == 0)
    def _():
        rel_k = lax.broadcasted_iota(jnp.int32, (tq, 2 * tq), 0)
        rel_q = lax.broadcasted_iota(jnp.int32, (tq, 2 * tq), 1)
        rel_q = jnp.where(rel_q >= tq, rel_q - tq, rel_q)
        ahead = rel_k - rel_q
        fix = jnp.where(ahead > 0, (-2.0 * slope) * ahead.astype(F32), 0.0)
        visible = (rel_k >> CHUNK_SHIFT) <= (rel_q >> CHUNK_SHIFT)
        own_ref[...] = jnp.where(visible, fix, NEG_BIG)

    def own_block(s):
        update(s + own_ref[...], qi)

    scores_into(sa_ref, 0)

    def block_pair(t, carry):
        k0 = 2 * t
        scores_into(sb_ref, k0 + 1)
        update(sa_ref[...], k0)
        scores_into(sa_ref, k0 + 2)
        update(sb_ref[...], k0 + 1)
        return carry

    lax.fori_loop(0, qi >> 1, block_pair, 0)

    @pl.when((qi & 1) == 1)
    def _():
        scores_into(sb_ref, qi)
        update(sa_ref[...], qi - 1)
        own_block(sb_ref[...])

    @pl.when((qi & 1) == 0)
    def _():
        own_block(sa_ref[...])

    lam_p = lam_ref[0]
    lam_init = 0.8 - 0.6 * jnp.exp(jnp.full((1, 1), -0.3, F32) * l_ref[0].astype(F32))
    lam = (jnp.exp(jnp.sum(lam_p[0:1] * lam_p[1:2], axis=-1, keepdims=True))
           - jnp.exp(jnp.sum(lam_p[2:3] * lam_p[3:4], axis=-1, keepdims=True)) + lam_init)
    acc = acc_ref[...]
    o_all = acc[0:DF_DV] / acc[DF_DV:DF_DV + 1]
    o = o_all[:, 0:tq] - lam * o_all[:, tq:2 * tq]
    ms = jnp.mean(o * o, axis=0, keepdims=True)
    y = o * lax.rsqrt(ms + EPS) * ogt_ref[0] * (1.0 - lam_init)
    y_ref[...] = y.T.astype(BF16)


def diff_flash_t(qt, ke, vt, lidx, lam_p, out_g_t, *, nb, seq, tq):
    assert tq % CHUNK == 0
    nq = seq // tq
    n_rows = nb * seq
    kew = DF_DV + BIAS_W
    kern = functools.partial(_flash_t_kernel, tq=tq)
    vmem = 2 * seq * kew * 2 + 2 * seq * DF_DV * 2 + 10 * tq * 2 * tq * 4 + (8 << 20)
    return pl.pallas_call(
        kern,
        grid_spec=pltpu.PrefetchScalarGridSpec(
            num_scalar_prefetch=1,
            grid=(nb, DF_H, nq),
            in_specs=[
                pl.BlockSpec((1, 1, DF_DV, tq), lambda b, h, i, l: (h, b * nq + i, 0, 0)),
                pl.BlockSpec((seq, kew), lambda b, h, i, l: (b, h)),
                pl.BlockSpec((1, nq, DF_DV, tq), lambda b, h, i, l: (h, b, 0, 0)),
                pl.BlockSpec((1, 4, DF_DH), lambda b, h, i, l: (l[0], 0, 0)),
                pl.BlockSpec((1, DF_DV, 1), lambda b, h, i, l: (l[0], 0, 0)),
            ],
            out_specs=pl.BlockSpec((tq, DF_DV), lambda b, h, i, l: (b * nq + i, h)),
            scratch_shapes=[pltpu.VMEM((1, 2 * tq), F32), pltpu.VMEM((DF_DV + ONES_ROWS, 2 * tq), F32),
                            pltpu.VMEM((tq, 2 * tq), F32), pltpu.VMEM((tq, 2 * tq), F32),
                            pltpu.VMEM((tq, 2 * tq), F32)],
        ),
        out_shape=jax.ShapeDtypeStruct((n_rows, DF_W), BF16),
        compiler_params=_cparams(("parallel", "parallel", "arbitrary"), vmem),
        name="diff_flash_t",
    )(lidx, qt, ke, vt, lam_p, out_g_t)


def _flash_sample_kernel(l_ref, q_ref, km_ref, vm_ref, kx_ref, vx_ref, lam_ref, og_ref, y_ref, *, tq, tk, past):
    q0 = past
    lam_p = lam_ref[0]
    lam_init = 0.8 - 0.6 * jnp.exp(jnp.full((1, 1), -0.3, F32) * l_ref[0].astype(F32))
    lam = (jnp.exp(jnp.sum(lam_p[0:1] * lam_p[1:2], axis=-1, keepdims=True))
           - jnp.exp(jnp.sum(lam_p[2:3] * lam_p[3:4], axis=-1, keepdims=True)) + lam_init)
    lane = lax.broadcasted_iota(jnp.int32, (tq, DF_DV), 1)
    rel_k = lax.broadcasted_iota(jnp.int32, (2 * tq, tq), 1)
    rel_q = lax.broadcasted_iota(jnp.int32, (2 * tq, tq), 0)
    rel_q = jnp.where(rel_q >= tq, rel_q - tq, rel_q)
    own_bias = rel_q.astype(F32) - jnp.abs(rel_q - rel_k).astype(F32)
    visible = ((q0 + rel_k) >> CHUNK_SHIFT) <= ((q0 + rel_q) >> CHUNK_SHIFT)

    def update(state, s, v):
        m_old, l_old, acc_old = state
        m_new = jnp.maximum(m_old, jnp.max(s, axis=-1, keepdims=True))
        alpha = jnp.exp(m_old - m_new)
        p = jnp.exp(s - m_new)
        return (m_new, alpha * l_old + jnp.sum(p, axis=-1, keepdims=True),
                alpha * acc_old + _dot(p.astype(BF16), v))

    for h in range(DF_H):
        sl = slice(h * DF_DV, (h + 1) * DF_DV)
        slope = 2.0 ** (-8.0 * (h + 1) / DF_H)
        q = q_ref[:, sl]
        zero = jnp.zeros_like(q)
        q2 = jnp.concatenate([jnp.where(lane < DF_DH, q, zero), jnp.where(lane >= DF_DH, q, zero)], axis=0)
        state = (jnp.full((2 * tq, 1), NEG_BIG, F32), jnp.zeros((2 * tq, 1), F32), jnp.zeros((2 * tq, DF_DV), F32))
        for kj in range(past // tk):
            rows = pl.ds(kj * tk * DF_H + h, tk, stride=DF_H)
            kpos = kj * tk + lax.broadcasted_iota(jnp.int32, (1, tk), 1)
            s = _dot_nt(q2, km_ref[rows, :].astype(BF16)) + slope * (kpos - q0).astype(F32)
            state = update(state, s, vm_ref[rows, :].astype(BF16))
        s = _dot_nt(q2, kx_ref[:, sl].astype(BF16)) + slope * own_bias
        _, l_sum, acc = update(state, jnp.where(visible, s, NEG_BIG), vx_ref[:, sl].astype(BF16))
        o_all = acc / l_sum
        o = o_all[0:tq] - lam * o_all[tq:2 * tq]
        y_ref[:, sl] = (_rms(o, og_ref[0]) * (1.0 - lam_init)).astype(BF16)


def diff_flash_sample(qn, k_past, v_past, k_new, v_new, lidx, lam_p, out_g, *, nb, seq, past, tk):
    n_rows = nb * seq
    kern = functools.partial(_flash_sample_kernel, tq=seq, tk=tk, past=past)
    vmem = 4 * past * DF_H * DF_DV * 4 + (8 << 20)
    qmap = lambda b, l: (b, 0)
    cmap = lambda b, l: (l[0] * nb + b, 0)
    return pl.pallas_call(
        kern,
        grid_spec=pltpu.PrefetchScalarGridSpec(
            num_scalar_prefetch=1,
            grid=(nb,),
            in_specs=[
                pl.BlockSpec((seq, DF_W), qmap),
                pl.BlockSpec((past * DF_H, DF_DV), cmap),
                pl.BlockSpec((past * DF_H, DF_DV), cmap),
                pl.BlockSpec((seq, DF_W), qmap),
                pl.BlockSpec((seq, DF_W), qmap),
                pl.BlockSpec((1, 4, DF_DH), lambda b, l: (l[0], 0, 0)),
                pl.BlockSpec((1, 1, DF_DV), lambda b, l: (l[0], 0, 0)),
            ],
            out_specs=pl.BlockSpec((seq, DF_W), qmap),
        ),
        out_shape=jax.ShapeDtypeStruct((n_rows, DF_W), BF16),
        compiler_params=_cparams(("parallel",), vmem),
        name="diff_flash_sample",
    )(lidx, qn, k_past, v_past, k_new, v_new, lam_p, out_g)


def _trunk_layer(x, lidx, w, st, cfg):
    nb, seq, past = cfg["nb"], cfg["seq"], cfg["past"]
    z, gates = in_proj(x, lidx, w["norm_mix"], w["w_main"], w["w_gate"], tm=cfg["tm_in"], tn=cfg["tn_in"])
    y_ml, c_new, n_new, m_new = mlstm_mixer(
        z, gates, lidx, st["hist0"], w["conv_w"], w["conv_b"], w["b_if"], w["ml_og"],
        st["c0"], st["n0"], st["m0"], nb=nb, seq=seq, tc=cfg["tc"], lc=cfg["lc_ml"])
    z3 = z.reshape(nb, seq, Z_W)
    new = {}
    if cfg["prompt"]:
        qt, ke, vt, new["pk_buf"], new["pv_buf"] = qk_prep_t(
            z, lidx, w["df_qg"], w["df_kg"], st["pk_buf"], st["pv_buf"], tm=cfg["tq"], seq=seq)
        y_df = diff_flash_t(qt, ke, vt, lidx, w["df_lam"], w["df_og_t"], nb=nb, seq=seq, tq=cfg["tq"])
    else:
        qn, kn, kn_b, v_b = qk_prep(z, lidx, w["df_qg"], w["df_kg"], tm=cfg["tm_prep"])
        y_df = diff_flash_sample(qn, st["past_k"], st["past_v"], kn_b, v_b, lidx, w["df_lam"], w["df_og"],
                                 nb=nb, seq=seq, past=past, tk=cfg["tk_past"])
        new["attn_k"] = kn.reshape(nb, seq, DF_H, 2 * DF_DH)
        new["attn_v"] = z3[:, :, ZB_DFV * DF_W:(ZB_DFV + 1) * DF_W].reshape(nb, seq, DF_H, DF_DV)
    y_hg, s_new = hgrn_mixer(z, lidx, w["hg_lbl"], w["hg_og"], st["s0"], nb=nb, seq=seq, tc=cfg["tc_hg"],
                             lc=cfg["lc_hg"])
    x = post_mix(x, y_ml, y_df, y_hg, lidx, w["w_out"], w["norm_cross"], w["wq"], w["ca_qg"],
                 st["mk"], st["mv"], w["wo"], tm=cfg["tm_post"], rows_per_batch=seq)
    x = ffn(x, lidx, w["norm_ffn"], w["w_gu"], w["w_down"], w["final_norm"], tm=cfg["tm_ffn"], th=cfg["th"],
            x_buffers=cfg["ffn_x_buffers"])
    new.update({
        "conv": z3[:, seq - (CONV_W - 1):, 0:2 * ML_W],
        "C": c_new,
        "n": n_new[:, :ML_H, :],
        "m": m_new[:, 0, :ML_H],
        "S": s_new,
    })
    return x, new


PROMPT_CFG = dict(prompt=True, tm_in=1024, tn_in=1536, tc=256, tc_hg=512, lc_ml=128, lc_hg=128, tm_prep=512, tq=512,
                  tm_post=512, tm_ffn=1024, th=512, ffn_x_buffers=2)
SAMPLE_CFG = dict(prompt=False, tm_in=128, tn_in=768, tc=16, tc_hg=16, lc_ml=16, lc_hg=16, tm_prep=128, tk_past=2048,
                  tm_post=128, tm_ffn=128, th=512, ffn_x_buffers=2)


def kernel(x_prompt, x_sample, mem_prompt, cache_attn_k, cache_attn_v, cache_mem_k, cache_mem_v, state_mlstm_conv, state_mlstm_C, state_mlstm_n, state_mlstm_m, state_hgrn_S, norm_mix, w_in, mlstm_conv_w, mlstm_conv_b, mlstm_b_i, mlstm_b_f, mlstm_out_norm, diff_q_norm, diff_k_norm, diff_lambda, diff_out_norm, hgrn_lb_logits, hgrn_out_norm, w_out, norm_cross, norm_mem, cross_wq, cross_wk, cross_wv, cross_q_norm, cross_k_norm, cross_wo, norm_ffn, ffn_w_gate_up, ffn_w_down, final_norm):
    bp, tp = x_prompt.shape[:2]
    bs, ts = x_sample.shape[:2]
    past = cache_attn_k.shape[2]
    depth = w_in.shape[0]
    assert depth == DEPTH and x_prompt.shape[2] == D_MODEL

    g_off = 4 * ML_W
    r3 = lambda a: a.reshape(depth, 1, a.shape[-1])
    w = {
        "norm_mix": r3(norm_mix),
        "w_main": jnp.concatenate([w_in[:, :, :g_off], w_in[:, :, g_off + 2 * ML_H:]], axis=-1).astype(BF16),
        "w_gate": jnp.pad(w_in[:, :, g_off:g_off + 2 * ML_H], ((0, 0), (0, 0), (0, GATE_W - 2 * ML_H))).astype(BF16),
        "conv_w": mlstm_conv_w,
        "conv_b": r3(mlstm_conv_b),
        "b_if": r3(jnp.pad(jnp.concatenate([mlstm_b_i, mlstm_b_f], axis=-1), ((0, 0), (0, GATE_W - 2 * ML_H)))),
        "ml_og": r3(mlstm_out_norm),
        "df_qg": r3(jnp.tile(diff_q_norm, (1, DF_W // DF_DH))),
        "df_kg": r3(jnp.tile(diff_k_norm, (1, DF_W // DF_DH))),
        "df_lam": diff_lambda,
        "df_og": r3(diff_out_norm),
        "df_og_t": diff_out_norm.reshape(depth, DF_DV, 1),
        "hg_lbl": hgrn_lb_logits,
        "hg_og": r3(hgrn_out_norm),
        "w_out": w_out.astype(BF16),
        "norm_cross": r3(norm_cross),
        "wq": cross_wq.astype(BF16),
        "ca_qg": r3(cross_q_norm),
        "wo": cross_wo.astype(BF16),
        "norm_ffn": r3(norm_ffn),
        "w_gu": ffn_w_gate_up.astype(BF16),
        "w_down": ffn_w_down.astype(BF16),
        "final_norm": final_norm.reshape(1, D_MODEL),
    }

    mk_p, mv_p = mem_kv(mem_prompt, r3(norm_mem), cross_wk.astype(BF16), cross_wv.astype(BF16), r3(cross_k_norm))

    def pad_hist(conv):
        pad = [(0, 0)] * (conv.ndim - 2) + [(HIST - (CONV_W - 1), 0), (0, 0)]
        return jnp.pad(conv, pad)

    st_p = {
        "hist0": jnp.zeros((bp, HIST, 2 * ML_W), F32),
        "c0": jnp.zeros((bp, ML_H, ML_DH, ML_DH), F32),
        "n0": jnp.zeros((bp, 8, ML_DH), F32),
        "m0": jnp.zeros((bp, 1, GATE_W), F32),
        "s0": jnp.zeros((bp, HG_H, HG_DK, HG_DV), F32),
        "mk": mk_p, "mv": mv_p,
    }
    hist_s = pad_hist(state_mlstm_conv)
    n_s = jnp.pad(state_mlstm_n, ((0, 0), (0, 0), (0, 8 - ML_H), (0, 0)))
    m_s = jnp.pad(state_mlstm_m, ((0, 0), (0, 0), (0, GATE_W - ML_H))).reshape(depth, bs, 1, GATE_W)
    past_k = cache_attn_k.reshape(depth * bs * past * DF_H, DF_DV)
    past_v = cache_attn_v.reshape(depth * bs * past * DF_H, DF_DV)
    mk_s = cache_mem_k.reshape(depth * bs, N_MEM, CA_W)
    mv_s = cache_mem_v.reshape(depth * bs, N_MEM, CA_W)

    cfg_p = dict(PROMPT_CFG, nb=bp, seq=tp, past=0)
    cfg_s = dict(SAMPLE_CFG, nb=bs, seq=ts, past=past)

    def layer(carry, xs):
        xp, xsm, pk_buf, pv_buf = carry
        l, hist_l, c_l, n_l, m_l, s_l = xs
        lidx = l.reshape(1).astype(jnp.int32)
        xp, new_p = _trunk_layer(xp, lidx, w, dict(st_p, pk_buf=pk_buf, pv_buf=pv_buf), cfg_p)
        pk_buf, pv_buf = new_p.pop("pk_buf"), new_p.pop("pv_buf")
        st_s = {"hist0": hist_l, "c0": c_l, "n0": n_l, "m0": m_l, "s0": s_l,
                "past_k": past_k, "past_v": past_v, "mk": mk_s, "mv": mv_s}
        xsm, new_s = _trunk_layer(xsm, lidx, w, st_s, cfg_s)
        return (xp, xsm, pk_buf, pv_buf), (new_p, new_s)

    xs = (jnp.arange(depth, dtype=jnp.int32), hist_s, state_mlstm_C, n_s, m_s, state_hgrn_S)
    kv_rows = depth * bp * tp * DF_H
    init = (x_prompt.reshape(bp * tp, D_MODEL), x_sample.reshape(bs * ts, D_MODEL),
            lax.empty((kv_rows, DF_DV), F32), lax.empty((kv_rows, DF_DV), F32))
    carry, first = layer(init, jax.tree.map(lambda a: a[0], xs))
    (xp, xsm, pk_buf, pv_buf), rest = lax.scan(layer, carry, jax.tree.map(lambda a: a[1:], xs))
    new_p, new_s = jax.tree.map(lambda a, b: jnp.concatenate([a[None], b], axis=0), first, rest)

    y_prompt = xp.reshape(bp, tp, D_MODEL)
    y_sample = xsm.reshape(bs, ts, D_MODEL)
    p_mem_k = mk_p.reshape(depth, bp, N_MEM, CA_H, CA_DH)
    p_mem_v = mv_p.reshape(depth, bp, N_MEM, CA_H, CA_DH)
    p_attn_k = pk_buf.reshape(depth, bp, tp, DF_H, 2 * DF_DH)
    p_attn_v = pv_buf.reshape(depth, bp, tp, DF_H, DF_DV)
    return (y_prompt, y_sample,
            p_attn_k, p_attn_v, p_mem_k, p_mem_v, new_p["conv"], new_p["C"], new_p["n"],
            new_p["m"], new_p["S"],
            new_s["attn_k"], new_s["attn_v"], new_s["conv"], new_s["C"], new_s["n"], new_s["m"], new_s["S"])
```

```python
import functools
import math

import numpy as np
import jax
import jax.numpy as jnp
from jax import lax
from jax.experimental import pallas as pl
from jax.experimental.pallas import tpu as pltpu

F32 = jnp.float32
BF16 = jnp.bfloat16

D_MODEL = 2048
DEPTH = 4
CHUNK = 64
CHUNK_SHIFT = 6
ML_DH = 128
ML_W = 768
ML_H = 6
CONV_W = 4
DF_DH = 64
DF_DV = 128
DF_W = 512
DF_H = 4
HG_DK = 128
HG_DV = 128
HG_W = 768
HG_H = 6
CA_H = 4
CA_DH = 128
CA_W = 512
N_MEM = 256
FF_HIDDEN = 5632
EPS = 1e-6
NEG_BIG = -1e30
LB_FLOOR = 1e-30
LOG2E = math.log2(math.e)

Z_W = 2 * ML_W + 2 * ML_W + 3 * DF_W + 4 * HG_W
GATE_W = 128
ZB_MLQ, ZB_MLK, ZB_MLV, ZB_MLO = 0, 1, 2, 3
ZB_HGQ, ZB_HGF, ZB_HGI, ZB_HGG = 6, 7, 8, 9
ZB_DFQ, ZB_DFK, ZB_DFV = 6, 7, 8

V7X_VMEM_BYTES = 64 * 1024 * 1024
V7X_VMEM_CAP = 58 * 1024 * 1024
VMEM_SLACK = 6 * 1024 * 1024
HIST = 8

NT_DIMS = (((1,), (1,)), ((), ()))
TN_DIMS = (((0,), (0,)), ((), ()))


def _vmem_limit(nbytes):
    return int(min(V7X_VMEM_CAP, max(32 * 1024 * 1024, nbytes + VMEM_SLACK)))


def _cparams(sem, vmem_bytes):
    return pltpu.CompilerParams(dimension_semantics=sem, vmem_limit_bytes=_vmem_limit(vmem_bytes))


def _rms(x, g):
    ms = jnp.mean(x * x, axis=-1, keepdims=True)
    return x * lax.rsqrt(ms + EPS) * g


def _dot(a, b):
    return jnp.dot(a, b, preferred_element_type=F32)


def _dot_nt(a, b):
    return lax.dot_general(a, b, NT_DIMS, preferred_element_type=F32)


def _dot_tn(a, b):
    return lax.dot_general(a, b, TN_DIMS, preferred_element_type=F32)


def _log_sigmoid(x):
    return jnp.minimum(x, 0.0) - jnp.log1p(jnp.exp(-jnp.abs(x)))


def _split3(x):
    hi = x.astype(BF16)
    r1 = x - hi.astype(F32)
    mid = r1.astype(BF16)
    lo = (r1 - mid.astype(F32)).astype(BF16)
    return hi, mid, lo


def _dot_exact01(m01, x):
    hi, mid, lo = _split3(x)
    return _dot(m01, hi) + _dot(m01, mid) + _dot(m01, lo)


def _in_proj_kernel(l_ref, x_ref, g_ref, w_ref, wg_ref, z_ref, gate_ref, xn_ref):
    j = pl.program_id(1)

    @pl.when(j == 0)
    def _():
        xn = _rms(x_ref[...], g_ref[0]).astype(BF16)
        xn_ref[...] = xn
        gate_ref[...] = _dot(xn, wg_ref[0])

    z_ref[...] = _dot(xn_ref[...], w_ref[0])


def in_proj(x, lidx, g, w_main, w_gate, *, tm, tn):
    n_rows = x.shape[0]
    grid = (n_rows // tm, Z_W // tn)
    vmem = 2 * tm * D_MODEL * 4 + 2 * D_MODEL * tn * 2 + 2 * tm * tn * 4 + tm * D_MODEL * 2 \
        + 2 * tm * GATE_W * 4 + 2 * D_MODEL * GATE_W * 2 + (4 << 20)
    return pl.pallas_call(
        _in_proj_kernel,
        grid_spec=pltpu.PrefetchScalarGridSpec(
            num_scalar_prefetch=1,
            grid=grid,
            in_specs=[
                pl.BlockSpec((tm, D_MODEL), lambda i, j, l: (i, 0)),
                pl.BlockSpec((1, 1, D_MODEL), lambda i, j, l: (l[0], 0, 0)),
                pl.BlockSpec((1, D_MODEL, tn), lambda i, j, l: (l[0], 0, j)),
                pl.BlockSpec((1, D_MODEL, GATE_W), lambda i, j, l: (l[0], 0, 0)),
            ],
            out_specs=[
                pl.BlockSpec((tm, tn), lambda i, j, l: (i, j)),
                pl.BlockSpec((tm, GATE_W), lambda i, j, l: (i, 0)),
            ],
            scratch_shapes=[pltpu.VMEM((tm, D_MODEL), BF16)],
        ),
        out_shape=[jax.ShapeDtypeStruct((n_rows, Z_W), F32),
                   jax.ShapeDtypeStruct((n_rows, GATE_W), F32)],
        compiler_params=_cparams(("parallel", "arbitrary"), vmem),
        name="in_proj",
    )(lidx, x, g, w_main, w_gate)


def _ffn_kernel(l_ref, x_ref, g_ref, wg_ref, wu_ref, wd_ref, fg_ref, o_ref, xn_ref):
    j = pl.program_id(1)
    nj = pl.num_programs(1)

    @pl.when(j == 0)
    def _():
        x = x_ref[...]
        xn_ref[...] = _rms(x, g_ref[0]).astype(BF16)
        o_ref[...] = x

    xn = xn_ref[...]
    gt = _dot(xn, wg_ref[0])
    up = _dot(xn, wu_ref[0])
    act = (gt * jax.nn.sigmoid(gt) * up).astype(BF16)
    o_ref[...] += _dot(act, wd_ref[0])

    @pl.when(jnp.logical_and(j == nj - 1, l_ref[0] == DEPTH - 1))
    def _():
        o_ref[...] = _rms(o_ref[...], fg_ref[...])


def ffn(x, lidx, g, w_gu, w_down, final_g, *, tm, th, x_buffers=2):
    n_rows = x.shape[0]
    nh = FF_HIDDEN // th
    grid = (n_rows // tm, nh)
    vmem = (2 + x_buffers) * tm * D_MODEL * 4 + tm * D_MODEL * 2 + 6 * D_MODEL * th * 2 + 3 * tm * th * 4 \
        + tm * D_MODEL * 4 + (4 << 20)
    x_mode = {} if x_buffers == 2 else {"pipeline_mode": pl.Buffered(x_buffers)}
    return pl.pallas_call(
        _ffn_kernel,
        grid_spec=pltpu.PrefetchScalarGridSpec(
            num_scalar_prefetch=1,
            grid=grid,
            in_specs=[
                pl.BlockSpec((tm, D_MODEL), lambda i, j, l: (i, 0), **x_mode),
                pl.BlockSpec((1, 1, D_MODEL), lambda i, j, l: (l[0], 0, 0)),
                pl.BlockSpec((1, D_MODEL, th), lambda i, j, l: (l[0], 0, j)),
                pl.BlockSpec((1, D_MODEL, th), lambda i, j, l: (l[0], 0, j + nh)),
                pl.BlockSpec((1, th, D_MODEL), lambda i, j, l: (l[0], j, 0)),
                pl.BlockSpec((1, D_MODEL), lambda i, j, l: (0, 0)),
            ],
            out_specs=pl.BlockSpec((tm, D_MODEL), lambda i, j, l: (i, 0)),
            scratch_shapes=[pltpu.VMEM((tm, D_MODEL), BF16)],
        ),
        out_shape=jax.ShapeDtypeStruct((n_rows, D_MODEL), F32),
        compiler_params=_cparams(("parallel", "arbitrary"), vmem),
        name="ffn",
    )(lidx, x, g, w_gu, w_gu, w_down, final_g)


def _mem_kv_kernel(mem_ref, g_ref, wk_ref, wv_ref, kg_ref, k_ref, v_ref):
    mn = _rms(mem_ref[0], g_ref[0]).astype(BF16)
    k = _dot(mn, wk_ref[0])
    for h in range(CA_H):
        sl = slice(h * CA_DH, (h + 1) * CA_DH)
        k_ref[0, :, sl] = _rms(k[:, sl], kg_ref[0])
    v_ref[0] = _dot(mn, wv_ref[0])


def mem_kv(mem, g, wk, wv, kg):
    nb = mem.shape[0]
    out = jax.ShapeDtypeStruct((DEPTH * nb, N_MEM, CA_W), F32)
    return pl.pallas_call(
        _mem_kv_kernel,
        grid=(DEPTH, nb),
        in_specs=[
            pl.BlockSpec((1, N_MEM, D_MODEL), lambda l, b: (b, 0, 0)),
            pl.BlockSpec((1, 1, D_MODEL), lambda l, b: (l, 0, 0)),
            pl.BlockSpec((1, D_MODEL, CA_W), lambda l, b: (l, 0, 0)),
            pl.BlockSpec((1, D_MODEL, CA_W), lambda l, b: (l, 0, 0)),
            pl.BlockSpec((1, 1, CA_DH), lambda l, b: (l, 0, 0)),
        ],
        out_specs=[pl.BlockSpec((1, N_MEM, CA_W), lambda l, b: (l * nb + b, 0, 0)),
                   pl.BlockSpec((1, N_MEM, CA_W), lambda l, b: (l * nb + b, 0, 0))],
        out_shape=[out, out],
        compiler_params=_cparams(("arbitrary", "arbitrary"), 24 << 20),
        name="mem_kv",
    )(mem, g, wk, wv, kg)


def _post_kernel(l_ref, x_ref, yml_ref, ydf_ref, yhg_ref, wout_ref, gx_ref, wq_ref, qg_ref,
                 mk_ref, mv_ref, wo_ref, o_ref, *, rows_per_stream):
    x1 = x_ref[...]
    x1 = x1 + _dot(yml_ref[...], wout_ref[0, 0:ML_W, :])
    x1 = x1 + _dot(ydf_ref[...], wout_ref[0, ML_W:ML_W + DF_W, :])
    x1 = x1 + _dot(yhg_ref[...], wout_ref[0, ML_W + DF_W:D_MODEL, :])
    hn = _rms(x1, gx_ref[0]).astype(BF16)
    q = _dot(hn, wq_ref[0])
    streams = []
    for si in range(x1.shape[0] // rows_per_stream):
        rs = slice(si * rows_per_stream, (si + 1) * rows_per_stream)
        heads = []
        for h in range(CA_H):
            sl = slice(h * CA_DH, (h + 1) * CA_DH)
            qh = _rms(q[rs, sl], qg_ref[0]).astype(BF16)
            s = _dot_nt(qh, mk_ref[si, :, sl].astype(BF16)) * (CA_DH ** -0.5)
            s = s - jnp.max(s, axis=-1, keepdims=True)
            p = jnp.exp(s)
            p = p / jnp.sum(p, axis=-1, keepdims=True)
            heads.append(_dot(p.astype(BF16), mv_ref[si, :, sl].astype(BF16)))
        streams.append(jnp.concatenate(heads, axis=-1))
    o = (streams[0] if len(streams) == 1 else jnp.concatenate(streams, axis=0)).astype(BF16)
    o_ref[...] = x1 + _dot(o, wo_ref[0])


def post_mix(x, y_ml, y_df, y_hg, lidx, w_out, gx, wq, qg, mk, mv, wo, *, tm, rows_per_batch):
    n_rows = x.shape[0]
    nb = n_rows // rows_per_batch
    grid = (n_rows // tm,)
    w_bytes = (D_MODEL * D_MODEL + 2 * D_MODEL * CA_W) * 2
    row = lambda i, l: (i, 0)
    if tm >= rows_per_batch:
        spt = tm // rows_per_batch
        mem_map = lambda i, l: (l[0] * (nb // spt) + i, 0, 0)
    else:
        spt = 1
        tiles_per_batch = rows_per_batch // tm
        mem_map = lambda i, l: (l[0] * nb + i // tiles_per_batch, 0, 0)
    vmem = 2 * w_bytes + 6 * tm * D_MODEL * 4 + 2 * tm * D_MODEL * 2 + 4 * spt * N_MEM * CA_W * 4 + (6 << 20)
    kern = functools.partial(_post_kernel, rows_per_stream=min(tm, rows_per_batch))
    return pl.pallas_call(
        kern,
        grid_spec=pltpu.PrefetchScalarGridSpec(
            num_scalar_prefetch=1,
            grid=grid,
            in_specs=[
                pl.BlockSpec((tm, D_MODEL), row),
                pl.BlockSpec((tm, ML_W), row),
                pl.BlockSpec((tm, DF_W), row),
                pl.BlockSpec((tm, HG_W), row),
                pl.BlockSpec((1, D_MODEL, D_MODEL), lambda i, l: (l[0], 0, 0)),
                pl.BlockSpec((1, 1, D_MODEL), lambda i, l: (l[0], 0, 0)),
                pl.BlockSpec((1, D_MODEL, CA_W), lambda i, l: (l[0], 0, 0)),
                pl.BlockSpec((1, 1, CA_DH), lambda i, l: (l[0], 0, 0)),
                pl.BlockSpec((spt, N_MEM, CA_W), mem_map),
                pl.BlockSpec((spt, N_MEM, CA_W), mem_map),
                pl.BlockSpec((1, CA_W, D_MODEL), lambda i, l: (l[0], 0, 0)),
            ],
            out_specs=pl.BlockSpec((tm, D_MODEL), row),
        ),
        out_shape=jax.ShapeDtypeStruct((n_rows, D_MODEL), F32),
        compiler_params=_cparams(("parallel",), vmem),
        name="post_mix",
    )(lidx, x, y_ml, y_df, y_hg, w_out, gx, wq, qg, mk, mv, wo)


def _mlstm_kernel(l_ref, zq_ref, zk_ref, zv_ref, zo_ref, gate_ref, hist_ref, cw_ref, cb_ref, bif_ref,
                  og_ref, c0_ref, n0_ref, m0_ref, tri_ref,
                  y_ref, c_ref, n_ref, m_ref, ext_ref, *, tc, lc, part):
    c = pl.program_id(1)

    if part == "init":
        @pl.when(c == 0)
        def _():
            c_ref[...] = c0_ref[...]
            n_ref[...] = n0_ref[...]
            m_ref[...] = m0_ref[...]
            ext_ref[0:HIST, :] = hist_ref[0]
        return

    ext_ref[HIST:HIST + tc, 0:ML_W] = zq_ref[...]
    ext_ref[HIST:HIST + tc, ML_W:2 * ML_W] = zk_ref[...]
    ext = ext_ref[...]
    acc = cb_ref[0] + cw_ref[0, CONV_W - 1:CONV_W, :] * ext[HIST:HIST + tc]
    for j in range(CONV_W - 1):
        acc = acc + cw_ref[0, j:j + 1, :] * pltpu.roll(ext, CONV_W - 1 - j, axis=0)[HIST:HIST + tc]
    tail = ext_ref[tc:tc + HIST, :]
    ext_ref[0:HIST, :] = tail
    qk = acc * jax.nn.sigmoid(acc)

    gz = gate_ref[...] + bif_ref[0]
    lf_all = pltpu.roll(_log_sigmoid(gz), GATE_W - ML_H, axis=1)
    tri = tri_ref[...]
    row_i = lax.broadcasted_iota(jnp.int32, (lc, lc), 0)
    col_i = lax.broadcasted_iota(jnp.int32, (lc, lc), 1)
    causal = row_i >= col_i
    diag = row_i == col_i
    trow = lax.broadcasted_iota(jnp.int32, (lc, GATE_W), 0)
    og = og_ref[0]

    for ci in range(tc // lc):
        r0 = ci * lc
        b_c = _dot_exact01(tri, lf_all[r0:r0 + lc, :])
        r_c = gz[r0:r0 + lc, :] - b_c
        cm = r_c
        d = 1
        while d < lc:
            cm = jnp.maximum(cm, jnp.where(trow >= d, pltpu.roll(cm, d, axis=0), NEG_BIG))
            d *= 2
        m_prev = m_ref[0]
        mx = jnp.maximum(m_prev, cm)
        m_t = b_c + mx
        w_inter_c = jnp.exp(m_prev - mx)
        emt_c = jnp.exp(-m_t)
        m_new = m_t[lc - 1:lc, :]
        b_last = b_c[lc - 1:lc, :]
        w_s_c = jnp.exp(b_last + r_c - m_new)
        decay_c = jnp.exp(b_last + m_prev - m_new)
        m_ref[0] = m_new
        for h in range(ML_H):
            sl = slice(h * ML_DH, (h + 1) * ML_DH)
            hl = slice(h, h + 1)
            q = qk[r0:r0 + lc, h * ML_DH:(h + 1) * ML_DH]
            k = qk[r0:r0 + lc, ML_W + h * ML_DH:ML_W + (h + 1) * ML_DH] * (ML_DH ** -0.5)
            v = zv_ref[r0:r0 + lc, sl]
            qb = q.astype(BF16)
            r_row = jnp.sum(jnp.where(diag, r_c[:, hl], 0.0), axis=0, keepdims=True)
            d_mat = jnp.exp(jnp.where(causal, r_row - mx[:, hl], NEG_BIG))
            w_inter = w_inter_c[:, hl]
            decay = decay_c[:, hl]
            s = _dot_nt(qb, k.astype(BF16)) * d_mat
            c_old = c_ref[0, h]
            n_old = n_ref[0, h:h + 1, :]
            num = _dot(s.astype(BF16), v.astype(BF16)) + w_inter * _dot(qb, c_old.astype(BF16))
            den = jnp.sum(s, axis=-1, keepdims=True) + w_inter * jnp.sum(q * n_old, axis=-1, keepdims=True)
            hh = num / jnp.maximum(jnp.abs(den), emt_c[:, hl])
            kw = k * w_s_c[:, hl]
            c_ref[0, h] = decay * c_old + _dot_tn(kw.astype(BF16), v.astype(BF16))
            n_ref[0, h:h + 1, :] = decay * n_old + jnp.sum(kw, axis=0, keepdims=True)
            o_gate = jax.nn.sigmoid(zo_ref[r0:r0 + lc, sl])
            y_ref[r0:r0 + lc, sl] = (_rms(hh, og) * o_gate).astype(BF16)


def _hgrn_level_masks(lc):
    t = np.arange(lc)[:, None]
    s = np.arange(lc)[None, :]
    masks = []
    h = lc // 2
    while h >= 1:
        odd = (t % (2 * h)) >= h
        same = (t // (2 * h)) == (s // (2 * h))
        masks.append((same & odd & ((s % (2 * h)) < h)).astype(np.float32))
        h //= 2
    masks.append((t == s).astype(np.float32))
    return np.stack(masks, axis=0)


def _hgrn_boundary_rows(a, h):
    lc, width = a.shape
    if h >= 8:
        parts = [jnp.broadcast_to(a[g * 2 * h + h - 1:g * 2 * h + h, :], (2 * h, width)) for g in range(lc // (2 * h))]
        return parts[0] if len(parts) == 1 else jnp.concatenate(parts, axis=0)
    a3 = a.reshape(lc // 8, 8, width)
    sub = lax.broadcasted_iota(jnp.int32, a3.shape, 1)
    out = None
    for g in range(8 // (2 * h)):
        src = g * 2 * h + h - 1
        b = jnp.broadcast_to(a3[:, src:src + 1, :], a3.shape)
        out = b if out is None else jnp.where(sub >= g * 2 * h, b, out)
    return out.reshape(lc, width)


def _hgrn_kernel(l_ref, zq_ref, zf_ref, zi_ref, zg_ref, lbl_ref, og_ref, s0_ref, tri_ref, msk_ref,
                 y_ref, s_ref, st_ref, *, tc, lc, nlev, part):
    c = pl.program_id(1)
    nc = pl.num_programs(1)

    if part == "init":
        @pl.when(c == 0)
        def _():
            for h in range(HG_H):
                st_ref[h] = s0_ref[0, h].T
        return
    if part == "fini":
        @pl.when(c == nc - 1)
        def _():
            for h in range(HG_H):
                s_ref[0, h] = st_ref[h].T
        return

    logits = lbl_ref[...]
    e = jnp.exp(logits - jnp.max(logits, axis=0, keepdims=True))
    p = e / jnp.sum(e, axis=0, keepdims=True)
    drow = lax.broadcasted_iota(jnp.int32, p.shape, 0)
    lb = jnp.sum(jnp.where(jnp.logical_and(drow >= 1, drow <= l_ref[0]), p, 0.0), axis=0, keepdims=True)
    lb_fl = jnp.maximum(lb, LB_FLOOR)
    one_m = 1.0 - lb

    zf = zf_ref[...]
    ez = jnp.exp(-jnp.abs(zf))
    inv = 1.0 / (1.0 + ez)
    sig_p = jnp.where(zf >= 0.0, inv, ez * inv)
    sig_n = jnp.where(zf >= 0.0, ez * inv, inv)
    lf2 = jnp.log(lb_fl + one_m * sig_p) * LOG2E
    kk = one_m * sig_n + (lb - lb_fl)
    zq = zq_ref[...]
    qq = zq * jax.nn.sigmoid(zq) * (HG_DK ** -0.5)
    tri = tri_ref[...]
    og = og_ref[0]

    for ci in range(tc // lc):
        r0 = ci * lc
        a_all = _dot_exact01(tri, lf2[r0:r0 + lc, :])
        lev_exp = [-jnp.abs(a_all - _hgrn_boundary_rows(a_all, lc >> (lv + 1))) for lv in range(nlev)]
        for h in range(HG_H):
            sl = slice(h * HG_DK, (h + 1) * HG_DK)
            q = qq[r0:r0 + lc, sl]
            k = kk[r0:r0 + lc, sl]
            iv = zi_ref[r0:r0 + lc, sl].astype(BF16)
            a_in = a_all[:, sl]
            a_end = a_in[lc - 1:lc, :]
            st = st_ref[h]
            o = _dot_nt((q * jnp.exp2(a_in)).astype(BF16), st.astype(BF16))
            att = jnp.where(msk_ref[nlev] > 0, _dot_nt(q.astype(BF16), k.astype(BF16)), 0.0)
            for lv in range(nlev):
                xf = jnp.exp2(lev_exp[lv][:, sl])
                pm = _dot_nt((q * xf).astype(BF16), (k * xf).astype(BF16))
                att = att + jnp.where(msk_ref[lv] > 0, pm, 0.0)
            o = o + _dot(att.astype(BF16), iv)
            k_end = (k * jnp.exp2(a_end - a_in)).astype(BF16)
            st_ref[h] = st * jnp.exp2(a_end) + _dot_tn(iv, k_end)
            gsl = zg_ref[r0:r0 + lc, sl]
            y_ref[r0:r0 + lc, sl] = (_rms(o, og) * (gsl * jax.nn.sigmoid(gsl))).astype(BF16)


N_ML_IN, N_HG_IN, N_ML_OUT, N_HG_OUT = 14, 9, 4, 2


def _recurrent_kernel(l_ref, *refs, tc, lc_ml, lc_hg, nlev):
    o0 = N_ML_IN + N_HG_IN
    ml_in, hg_in = refs[0:N_ML_IN], refs[N_ML_IN:o0]
    ml_out, hg_out = refs[o0:o0 + N_ML_OUT], refs[o0 + N_ML_OUT:o0 + N_ML_OUT + N_HG_OUT]
    ml_sc, hg_sc = refs[o0 + N_ML_OUT + N_HG_OUT], refs[o0 + N_ML_OUT + N_HG_OUT + 1]
    ml = functools.partial(_mlstm_kernel, l_ref, *ml_in, *ml_out, ml_sc, tc=tc, lc=lc_ml)
    hg = functools.partial(_hgrn_kernel, l_ref, *hg_in, *hg_out, hg_sc, tc=tc, lc=lc_hg, nlev=nlev)
    ml(part="init")
    hg(part="init")
    ml(part="main")
    hg(part="main")
    hg(part="fini")


def recurrent_mixers(z, gates, lidx, hist0, conv_w, conv_b, b_if, ml_og, c0, n0, m0, lb_logits, hg_og, s0,
                     *, nb, seq, tc, lc_ml, lc_hg):
    n_rows = nb * seq
    nc = seq // tc
    tri_ml = jnp.asarray(np.tril(np.ones((lc_ml, lc_ml), np.float32)), BF16)
    tri_hg = jnp.asarray(np.tril(np.ones((lc_hg, lc_hg), np.float32)), BF16)
    msk_np = _hgrn_level_masks(lc_hg)
    nlev = msk_np.shape[0] - 1
    msk = jnp.asarray(msk_np, F32)
    zmap = lambda blk: (lambda b, c, l: (b * nc + c, blk))
    lmap3 = lambda b, c, l: (l[0], 0, 0)
    bmap3 = lambda b, c, l: (b, 0, 0)
    bmap4 = lambda b, c, l: (b, 0, 0, 0)
    rows = lambda b, c, l: (b * nc + c, 0)
    const2 = lambda b, c, l: (0, 0)
    vmem = 2 * 9 * tc * ML_W * 4 + (HIST + tc) * 2 * ML_W * 4 + 9 * ML_H * ML_DH * ML_DH * 4 \
        + 14 * tc * ML_W * 4 + (8 << 20)
    kern = functools.partial(_recurrent_kernel, tc=tc, lc_ml=lc_ml, lc_hg=lc_hg, nlev=nlev)
    in_specs = [
        pl.BlockSpec((tc, ML_W), zmap(ZB_MLQ)), pl.BlockSpec((tc, ML_W), zmap(ZB_MLK)),
        pl.BlockSpec((tc, ML_W), zmap(ZB_MLV)), pl.BlockSpec((tc, ML_W), zmap(ZB_MLO)),
        pl.BlockSpec((tc, GATE_W), rows),
        pl.BlockSpec((1, HIST, 2 * ML_W), bmap3),
        pl.BlockSpec((1, CONV_W, 2 * ML_W), lmap3), pl.BlockSpec((1, 1, 2 * ML_W), lmap3),
        pl.BlockSpec((1, 1, GATE_W), lmap3), pl.BlockSpec((1, 1, ML_DH), lmap3),
        pl.BlockSpec((1, ML_H, ML_DH, ML_DH), bmap4), pl.BlockSpec((1, 8, ML_DH), bmap3),
        pl.BlockSpec((1, 1, GATE_W), bmap3), pl.BlockSpec((lc_ml, lc_ml), const2),
        pl.BlockSpec((tc, HG_W), zmap(ZB_HGQ)), pl.BlockSpec((tc, HG_W), zmap(ZB_HGF)),
        pl.BlockSpec((tc, HG_W), zmap(ZB_HGI)), pl.BlockSpec((tc, HG_W), zmap(ZB_HGG)),
        pl.BlockSpec((DEPTH, HG_W), const2), pl.BlockSpec((1, 1, HG_DV), lmap3),
        pl.BlockSpec((1, HG_H, HG_DK, HG_DV), bmap4), pl.BlockSpec((lc_hg, lc_hg), const2),
        pl.BlockSpec(msk_np.shape, lambda b, c, l: (0, 0, 0)),
    ]
    assert len(in_specs) == N_ML_IN + N_HG_IN
    out_specs = [
        pl.BlockSpec((tc, ML_W), rows), pl.BlockSpec((1, ML_H, ML_DH, ML_DH), bmap4),
        pl.BlockSpec((1, 8, ML_DH), bmap3), pl.BlockSpec((1, 1, GATE_W), bmap3),
        pl.BlockSpec((tc, HG_W), rows), pl.BlockSpec((1, HG_H, HG_DK, HG_DV), bmap4),
    ]
    out_shape = [
        jax.ShapeDtypeStruct((n_rows, ML_W), BF16), jax.ShapeDtypeStruct((nb, ML_H, ML_DH, ML_DH), F32),
        jax.ShapeDtypeStruct((nb, 8, ML_DH), F32), jax.ShapeDtypeStruct((nb, 1, GATE_W), F32),
        jax.ShapeDtypeStruct((n_rows, HG_W), BF16), jax.ShapeDtypeStruct((nb, HG_H, HG_DK, HG_DV), F32),
    ]
    return pl.pallas_call(
        kern,
        grid_spec=pltpu.PrefetchScalarGridSpec(
            num_scalar_prefetch=1, grid=(nb, nc), in_specs=in_specs, out_specs=out_specs,
            scratch_shapes=[pltpu.VMEM((HIST + tc, 2 * ML_W), F32), pltpu.VMEM((HG_H, HG_DV, HG_DK), F32)]),
        out_shape=out_shape,
        compiler_params=_cparams(("parallel", "arbitrary"), vmem),
        name="recurrent_mixers",
    )(lidx, z, z, z, z, gates, hist0, conv_w, conv_b, b_if, ml_og, c0, n0, m0, tri_ml,
      z, z, z, z, lb_logits, hg_og, s0, tri_hg, msk)


def _qk_prep_kernel(l_ref, zq_ref, zk_ref, zv_ref, qg_ref, kg_ref, grp_ref, qn_ref, kn_ref, knb_ref, vb_ref):
    grp = grp_ref[...]
    qn_ref[...] = (_group_norm64(zq_ref[...], qg_ref[0], grp) * (DF_DH ** -0.5)).astype(BF16)
    kn = _group_norm64(zk_ref[...], kg_ref[0], grp)
    kn_ref[...] = kn
    knb_ref[...] = kn.astype(BF16)
    vb_ref[...] = zv_ref[...].astype(BF16)


def qk_prep(z, lidx, qg, kg, *, tm):
    n_rows = z.shape[0]
    lane = np.arange(DF_W)
    grp = jnp.asarray(((lane[:, None] // DF_DH) == (lane[None, :] // DF_DH)).astype(np.float32) / DF_DH, BF16)
    zmap = lambda blk: (lambda i, l: (i, blk))
    row = lambda i, l: (i, 0)
    return pl.pallas_call(
        _qk_prep_kernel,
        grid_spec=pltpu.PrefetchScalarGridSpec(
            num_scalar_prefetch=1,
            grid=(n_rows // tm,),
            in_specs=[
                pl.BlockSpec((tm, DF_W), zmap(ZB_DFQ)),
                pl.BlockSpec((tm, DF_W), zmap(ZB_DFK)),
                pl.BlockSpec((tm, DF_W), zmap(ZB_DFV)),
                pl.BlockSpec((1, 1, DF_W), lambda i, l: (l[0], 0, 0)),
                pl.BlockSpec((1, 1, DF_W), lambda i, l: (l[0], 0, 0)),
                pl.BlockSpec((DF_W, DF_W), lambda i, l: (0, 0)),
            ],
            out_specs=[pl.BlockSpec((tm, DF_W), row)] * 4,
        ),
        out_shape=[
            jax.ShapeDtypeStruct((n_rows, DF_W), BF16),
            jax.ShapeDtypeStruct((n_rows, DF_W), F32),
            jax.ShapeDtypeStruct((n_rows, DF_W), BF16),
            jax.ShapeDtypeStruct((n_rows, DF_W), BF16),
        ],
        compiler_params=_cparams(("parallel",), 32 << 20),
        name="qk_prep",
    )(lidx, z, z, z, qg, kg, grp)


def _group_norm64(x, g, grp):
    x2 = x * x
    hi = x2.astype(BF16)
    lo = (x2 - hi.astype(F32)).astype(BF16)
    ms = _dot(hi, grp) + _dot(lo, grp)
    return x * lax.rsqrt(ms + EPS) * g


BIAS_W = 128


def _qk_prep_t_kernel(l_ref, zq_ref, zk_ref, zv_ref, qg_ref, kg_ref, grp_ref, pk_in, pv_in,
                      qt_ref, ke_ref, vt_ref, pk_ref, pv_ref, *, tm, seq):
    del pk_in, pv_in
    i = pl.program_id(0)
    grp = grp_ref[...]
    qn = _group_norm64(zq_ref[...], qg_ref[0], grp) * (DF_DH ** -0.5)
    kn = _group_norm64(zk_ref[...], kg_ref[0], grp)
    zv = zv_ref[...]
    pos = lax.rem(i * tm, seq) + lax.broadcasted_iota(jnp.int32, (tm, BIAS_W), 0)
    lane = lax.broadcasted_iota(jnp.int32, (tm, BIAS_W), 1)
    hi_part = (pos >> CHUNK_SHIFT).astype(F32) * float(CHUNK)
    lo_part = (pos & (CHUNK - 1)).astype(F32)
    base = jnp.where(lane == 0, hi_part, jnp.where(lane == 1, lo_part, jnp.where(lane == 2, float(CHUNK), 0.0)))
    for h in range(DF_H):
        sl = slice(h * DF_DV, (h + 1) * DF_DV)
        slope = 2.0 ** (-8.0 * (h + 1) / DF_H)
        c0 = h * (DF_DV + BIAS_W)
        ke_ref[:, c0:c0 + DF_DV] = kn[:, sl].astype(BF16)
        ke_ref[:, c0 + DF_DV:c0 + DF_DV + BIAS_W] = (base * slope).astype(BF16)
        qt_ref[h, 0] = qn[:, sl].T.astype(BF16)
        vt_ref[h, 0] = zv[:, sl].T.astype(BF16)
        rows = pl.ds(h, tm, stride=DF_H)
        pk_ref[rows, :] = kn[:, sl]
        pv_ref[rows, :] = zv[:, sl]


def qk_prep_t(z, lidx, qg, kg, pk_buf, pv_buf, *, tm, seq):
    n_rows = z.shape[0]
    nt = n_rows // tm
    smap = lambda i, l: (l[0] * nt + i, 0)
    any_spec = pl.BlockSpec(memory_space=pl.ANY)
    lane = np.arange(DF_W)
    grp = jnp.asarray(((lane[:, None] // DF_DH) == (lane[None, :] // DF_DH)).astype(np.float32) / DF_DH, BF16)
    zmap = lambda blk: (lambda i, l: (i, blk))
    row = lambda i, l: (i, 0)
    tmap = lambda i, l: (0, i, 0, 0)
    kern = functools.partial(_qk_prep_t_kernel, tm=tm, seq=seq)
    return pl.pallas_call(
        kern,
        grid_spec=pltpu.PrefetchScalarGridSpec(
            num_scalar_prefetch=1,
            grid=(nt,),
            in_specs=[
                pl.BlockSpec((tm, DF_W), zmap(ZB_DFQ)),
                pl.BlockSpec((tm, DF_W), zmap(ZB_DFK)),
                pl.BlockSpec((tm, DF_W), zmap(ZB_DFV)),
                pl.BlockSpec((1, 1, DF_W), lambda i, l: (l[0], 0, 0)),
                pl.BlockSpec((1, 1, DF_W), lambda i, l: (l[0], 0, 0)),
                pl.BlockSpec((DF_W, DF_W), lambda i, l: (0, 0)),
                any_spec,
                any_spec,
            ],
            out_specs=[
                pl.BlockSpec((DF_H, 1, DF_DV, tm), tmap),
                pl.BlockSpec((tm, DF_H * (DF_DV + BIAS_W)), row),
                pl.BlockSpec((DF_H, 1, DF_DV, tm), tmap),
                pl.BlockSpec((tm * DF_H, DF_DV), smap),
                pl.BlockSpec((tm * DF_H, DF_DV), smap),
            ],
        ),
        out_shape=[
            jax.ShapeDtypeStruct((DF_H, nt, DF_DV, tm), BF16),
            jax.ShapeDtypeStruct((n_rows, DF_H * (DF_DV + BIAS_W)), BF16),
            jax.ShapeDtypeStruct((DF_H, nt, DF_DV, tm), BF16),
            jax.ShapeDtypeStruct(pk_buf.shape, F32),
            jax.ShapeDtypeStruct(pv_buf.shape, F32),
        ],
        input_output_aliases={7: 3, 8: 4},
        compiler_params=_cparams(("parallel",), 40 << 20),
        name="qk_prep_t",
    )(lidx, z, z, z, qg, kg, grp, pk_buf, pv_buf)


ONES_ROWS = 16


def _flash_t_kernel(l_ref, qt_ref, ke_ref, vt_ref, lam_ref, ogt_ref, y_ref, m_ref, acc_ref, sa_ref, sb_ref,
                    own_ref, *, tq):
    h = pl.program_id(1)
    qi = pl.program_id(2)
    tk = tq
    slope = jnp.exp2(jnp.full((1, 1), -8.0 / DF_H, F32) * (h + 1).astype(F32))
    qt = qt_ref[0, 0]
    row = lax.broadcasted_iota(jnp.int32, (DF_DV, tq), 0)
    zero = jnp.zeros_like(qt)
    q2t = jnp.concatenate([jnp.where(row < DF_DH, qt, zero), jnp.where(row >= DF_DH, qt, zero)], axis=1)
    brow = lax.broadcasted_iota(jnp.int32, (BIAS_W, 2 * tq), 0)
    a0 = ((qi * tq) >> CHUNK_SHIFT).astype(F32)
    extra = jnp.where(brow < 2, 1.0, jnp.where(brow == 2, -a0, 0.0)).astype(BF16)
    q2e = jnp.concatenate([q2t, extra], axis=0)
    ones = jnp.ones((ONES_ROWS, tk), BF16)

    m_ref[...] = jnp.full(m_ref.shape, NEG_BIG, F32)
    acc_ref[...] = jnp.zeros(acc_ref.shape, F32)

    def scores_into(dst_ref, kj):
        r0 = pl.multiple_of(kj * tk, tk)
        dst_ref[...] = _dot(ke_ref[pl.ds(r0, tk), :], q2e)

    def update(s, kj):
        m_old = m_ref[...]
        m_new = jnp.maximum(m_old, jnp.max(s, axis=0, keepdims=True))
        alpha = jnp.exp(m_old - m_new)
        p = jnp.exp(s - m_new).astype(BF16)
        vt_ext = jnp.concatenate([vt_ref[0, kj], ones], axis=0)
        acc_ref[...] = alpha * acc_ref[...] + _dot(vt_ext, p)
        m_ref[...] = m_new

    @pl.when(qi == 0)
    def _():
        rel_k = lax.broadcasted_iota(jnp.int32, (tq, 2 * tq), 0)
        rel_q = lax.broadcasted_iota(jnp.int32, (tq, 2 * tq), 1)
        rel_q = jnp.where(rel_q >= tq, rel_q - tq, rel_q)
        ahead = rel_k - rel_q
        fix = jnp.where(ahead > 0, (-2.0 * slope) * ahead.astype(F32), 0.0)
        visible = (rel_k >> CHUNK_SHIFT) <= (rel_q >> CHUNK_SHIFT)
        own_ref[...] = jnp.where(visible, fix, NEG_BIG)

    def own_block(s):
        update(s + own_ref[...], qi)

    scores_into(sa_ref, 0)

    def block_pair(t, carry):
        k0 = 2 * t
        scores_into(sb_ref, k0 + 1)
        update(sa_ref[...], k0)
        scores_into(sa_ref, k0 + 2)
        update(sb_ref[...], k0 + 1)
        return carry

    lax.fori_loop(0, qi >> 1, block_pair, 0)

    @pl.when((qi & 1) == 1)
    def _():
        scores_into(sb_ref, qi)
        update(sa_ref[...], qi - 1)
        own_block(sb_ref[...])

    @pl.when((qi & 1) == 0)
    def _():
        own_block(sa_ref[...])

    lam_p = lam_ref[0]
    lam_init = 0.8 - 0.6 * jnp.exp(jnp.full((1, 1), -0.3, F32) * l_ref[0].astype(F32))
    lam = (jnp.exp(jnp.sum(lam_p[0:1] * lam_p[1:2], axis=-1, keepdims=True))
           - jnp.exp(jnp.sum(lam_p[2:3] * lam_p[3:4], axis=-1, keepdims=True)) + lam_init)
    acc = acc_ref[...]
    o_all = acc[0:DF_DV] / acc[DF_DV:DF_DV + 1]
    o = o_all[:, 0:tq] - lam * o_all[:, tq:2 * tq]
    ms = jnp.mean(o * o, axis=0, keepdims=True)
    y = o * lax.rsqrt(ms + EPS) * ogt_ref[0] * (1.0 - lam_init)
    y_ref[...] = y.T.astype(BF16)


def diff_flash_t(qt, ke, vt, lidx, lam_p, out_g_t, *, nb, seq, tq):
    assert tq % CHUNK == 0
    nq = seq // tq
    n_rows = nb * seq
    kew = DF_DV + BIAS_W
    kern = functools.partial(_flash_t_kernel, tq=tq)
    vmem = 2 * seq * kew * 2 + 2 * seq * DF_DV * 2 + 10 * tq * 2 * tq * 4 + (8 << 20)
    return pl.pallas_call(
        kern,
        grid_spec=pltpu.PrefetchScalarGridSpec(
            num_scalar_prefetch=1,
            grid=(nb, DF_H, nq),
            in_specs=[
                pl.BlockSpec((1, 1, DF_DV, tq), lambda b, h, i, l: (h, b * nq + i, 0, 0)),
                pl.BlockSpec((seq, kew), lambda b, h, i, l: (b, h)),
                pl.BlockSpec((1, nq, DF_DV, tq), lambda b, h, i, l: (h, b, 0, 0)),
                pl.BlockSpec((1, 4, DF_DH), lambda b, h, i, l: (l[0], 0, 0)),
                pl.BlockSpec((1, DF_DV, 1), lambda b, h, i, l: (l[0], 0, 0)),
            ],
            out_specs=pl.BlockSpec((tq, DF_DV), lambda b, h, i, l: (b * nq + i, h)),
            scratch_shapes=[pltpu.VMEM((1, 2 * tq), F32), pltpu.VMEM((DF_DV + ONES_ROWS, 2 * tq), F32),
                            pltpu.VMEM((tq, 2 * tq), F32), pltpu.VMEM((tq, 2 * tq), F32),
                            pltpu.VMEM((tq, 2 * tq), F32)],
        ),
        out_shape=jax.ShapeDtypeStruct((n_rows, DF_W), BF16),
        compiler_params=_cparams(("parallel", "parallel", "arbitrary"), vmem),
        name="diff_flash_t",
    )(lidx, qt, ke, vt, lam_p, out_g_t)


def _flash_sample_kernel(l_ref, q_ref, km_ref, vm_ref, kx_ref, vx_ref, lam_ref, og_ref, y_ref, *, tq, tk, past):
    q0 = past
    lam_p = lam_ref[0]
    lam_init = 0.8 - 0.6 * jnp.exp(jnp.full((1, 1), -0.3, F32) * l_ref[0].astype(F32))
    lam = (jnp.exp(jnp.sum(lam_p[0:1] * lam_p[1:2], axis=-1, keepdims=True))
           - jnp.exp(jnp.sum(lam_p[2:3] * lam_p[3:4], axis=-1, keepdims=True)) + lam_init)
    lane = lax.broadcasted_iota(jnp.int32, (tq, DF_DV), 1)
    rel_k = lax.broadcasted_iota(jnp.int32, (2 * tq, tq), 1)
    rel_q = lax.broadcasted_iota(jnp.int32, (2 * tq, tq), 0)
    rel_q = jnp.where(rel_q >= tq, rel_q - tq, rel_q)
    own_bias = rel_q.astype(F32) - jnp.abs(rel_q - rel_k).astype(F32)
    visible = ((q0 + rel_k) >> CHUNK_SHIFT) <= ((q0 + rel_q) >> CHUNK_SHIFT)

    def update(state, s, v):
        m_old, l_old, acc_old = state
        m_new = jnp.maximum(m_old, jnp.max(s, axis=-1, keepdims=True))
        alpha = jnp.exp(m_old - m_new)
        p = jnp.exp(s - m_new)
        return (m_new, alpha * l_old + jnp.sum(p, axis=-1, keepdims=True),
                alpha * acc_old + _dot(p.astype(BF16), v))

    for h in range(DF_H):
        sl = slice(h * DF_DV, (h + 1) * DF_DV)
        slope = 2.0 ** (-8.0 * (h + 1) / DF_H)
        q = q_ref[:, sl]
        zero = jnp.zeros_like(q)
        q2 = jnp.concatenate([jnp.where(lane < DF_DH, q, zero), jnp.where(lane >= DF_DH, q, zero)], axis=0)
        state = (jnp.full((2 * tq, 1), NEG_BIG, F32), jnp.zeros((2 * tq, 1), F32), jnp.zeros((2 * tq, DF_DV), F32))
        for kj in range(past // tk):
            rows = pl.ds(kj * tk * DF_H + h, tk, stride=DF_H)
            kpos = kj * tk + lax.broadcasted_iota(jnp.int32, (1, tk), 1)
            s = _dot_nt(q2, km_ref[rows, :].astype(BF16)) + slope * (kpos - q0).astype(F32)
            state = update(state, s, vm_ref[rows, :].astype(BF16))
        s = _dot_nt(q2, kx_ref[:, sl].astype(BF16)) + slope * own_bias
        _, l_sum, acc = update(state, jnp.where(visible, s, NEG_BIG), vx_ref[:, sl].astype(BF16))
        o_all = acc / l_sum
        o = o_all[0:tq] - lam * o_all[tq:2 * tq]
        y_ref[:, sl] = (_rms(o, og_ref[0]) * (1.0 - lam_init)).astype(BF16)


def diff_flash_sample(qn, k_past, v_past, k_new, v_new, lidx, lam_p, out_g, *, nb, seq, past, tk):
    n_rows = nb * seq
    kern = functools.partial(_flash_sample_kernel, tq=seq, tk=tk, past=past)
    vmem = 4 * past * DF_H * DF_DV * 4 + (8 << 20)
    qmap = lambda b, l: (b, 0)
    cmap = lambda b, l: (l[0] * nb + b, 0)
    return pl.pallas_call(
        kern,
        grid_spec=pltpu.PrefetchScalarGridSpec(
            num_scalar_prefetch=1,
            grid=(nb,),
            in_specs=[
                pl.BlockSpec((seq, DF_W), qmap),
                pl.BlockSpec((past * DF_H, DF_DV), cmap),
                pl.BlockSpec((past * DF_H, DF_DV), cmap),
                pl.BlockSpec((seq, DF_W), qmap),
                pl.BlockSpec((seq, DF_W), qmap),
                pl.BlockSpec((1, 4, DF_DH), lambda b, l: (l[0], 0, 0)),
                pl.BlockSpec((1, 1, DF_DV), lambda b, l: (l[0], 0, 0)),
            ],
            out_specs=pl.BlockSpec((seq, DF_W), qmap),
        ),
        out_shape=jax.ShapeDtypeStruct((n_rows, DF_W), BF16),
        compiler_params=_cparams(("parallel",), vmem),
        name="diff_flash_sample",
    )(lidx, qn, k_past, v_past, k_new, v_new, lam_p, out_g)


def _trunk_layer(x, lidx, w, st, cfg):
    nb, seq, past = cfg["nb"], cfg["seq"], cfg["past"]
    z, gates = in_proj(x, lidx, w["norm_mix"], w["w_main"], w["w_gate"], tm=cfg["tm_in"], tn=cfg["tn_in"])
    y_ml, c_new, n_new, m_new, y_hg, s_new = recurrent_mixers(
        z, gates, lidx, st["hist0"], w["conv_w"], w["conv_b"], w["b_if"], w["ml_og"],
        st["c0"], st["n0"], st["m0"], w["hg_lbl"], w["hg_og"], st["s0"],
        nb=nb, seq=seq, tc=cfg["tc"], lc_ml=cfg["lc_ml"], lc_hg=cfg["lc_hg"])
    z3 = z.reshape(nb, seq, Z_W)
    new = {}
    if cfg["prompt"]:
        qt, ke, vt, new["pk_buf"], new["pv_buf"] = qk_prep_t(
            z, lidx, w["df_qg"], w["df_kg"], st["pk_buf"], st["pv_buf"], tm=cfg["tq"], seq=seq)
        y_df = diff_flash_t(qt, ke, vt, lidx, w["df_lam"], w["df_og_t"], nb=nb, seq=seq, tq=cfg["tq"])
    else:
        qn, kn, kn_b, v_b = qk_prep(z, lidx, w["df_qg"], w["df_kg"], tm=cfg["tm_prep"])
        y_df = diff_flash_sample(qn, st["past_k"], st["past_v"], kn_b, v_b, lidx, w["df_lam"], w["df_og"],
                                 nb=nb, seq=seq, past=past, tk=cfg["tk_past"])
        new["attn_k"] = kn.reshape(nb, seq, DF_H, 2 * DF_DH)
        new["attn_v"] = z3[:, :, ZB_DFV * DF_W:(ZB_DFV + 1) * DF_W].reshape(nb, seq, DF_H, DF_DV)
    x = post_mix(x, y_ml, y_df, y_hg, lidx, w["w_out"], w["norm_cross"], w["wq"], w["ca_qg"],
                 st["mk"], st["mv"], w["wo"], tm=cfg["tm_post"], rows_per_batch=seq)
    x = ffn(x, lidx, w["norm_ffn"], w["w_gu"], w["w_down"], w["final_norm"], tm=cfg["tm_ffn"], th=cfg["th"],
            x_buffers=cfg["ffn_x_buffers"])
    new.update({
        "conv": z3[:, seq - (CONV_W - 1):, 0:2 * ML_W],
        "C": c_new,
        "n": n_new[:, :ML_H, :],
        "m": m_new[:, 0, :ML_H],
        "S": s_new,
    })
    return x, new


PROMPT_CFG = dict(prompt=True, tm_in=1024, tn_in=1536, tc=512, lc_ml=128, lc_hg=128, tm_prep=512, tq=512,
                  tm_post=512, tm_ffn=1024, th=512, ffn_x_buffers=2)
SAMPLE_CFG = dict(prompt=False, tm_in=128, tn_in=768, tc=16, lc_ml=16, lc_hg=16, tm_prep=128, tk_past=2048,
                  tm_post=128, tm_ffn=128, th=512, ffn_x_buffers=2)


def kernel(x_prompt, x_sample, mem_prompt, cache_attn_k, cache_attn_v, cache_mem_k, cache_mem_v, state_mlstm_conv, state_mlstm_C, state_mlstm_n, state_mlstm_m, state_hgrn_S, norm_mix, w_in, mlstm_conv_w, mlstm_conv_b, mlstm_b_i, mlstm_b_f, mlstm_out_norm, diff_q_norm, diff_k_norm, diff_lambda, diff_out_norm, hgrn_lb_logits, hgrn_out_norm, w_out, norm_cross, norm_mem, cross_wq, cross_wk, cross_wv, cross_q_norm, cross_k_norm, cross_wo, norm_ffn, ffn_w_gate_up, ffn_w_down, final_norm):
    bp, tp = x_prompt.shape[:2]
    bs, ts = x_sample.shape[:2]
    past = cache_attn_k.shape[2]
    depth = w_in.shape[0]
    assert depth == DEPTH and x_prompt.shape[2] == D_MODEL

    g_off = 4 * ML_W
    r3 = lambda a: a.reshape(depth, 1, a.shape[-1])
    w = {
        "norm_mix": r3(norm_mix),
        "w_main": jnp.concatenate([w_in[:, :, :g_off], w_in[:, :, g_off + 2 * ML_H:]], axis=-1).astype(BF16),
        "w_gate": jnp.pad(w_in[:, :, g_off:g_off + 2 * ML_H], ((0, 0), (0, 0), (0, GATE_W - 2 * ML_H))).astype(BF16),
        "conv_w": mlstm_conv_w,
        "conv_b": r3(mlstm_conv_b),
        "b_if": r3(jnp.pad(jnp.concatenate([mlstm_b_i, mlstm_b_f], axis=-1), ((0, 0), (0, GATE_W - 2 * ML_H)))),
        "ml_og": r3(mlstm_out_norm),
        "df_qg": r3(jnp.tile(diff_q_norm, (1, DF_W // DF_DH))),
        "df_kg": r3(jnp.tile(diff_k_norm, (1, DF_W // DF_DH))),
        "df_lam": diff_lambda,
        "df_og": r3(diff_out_norm),
        "df_og_t": diff_out_norm.reshape(depth, DF_DV, 1),
        "hg_lbl": hgrn_lb_logits,
        "hg_og": r3(hgrn_out_norm),
        "w_out": w_out.astype(BF16),
        "norm_cross": r3(norm_cross),
        "wq": cross_wq.astype(BF16),
        "ca_qg": r3(cross_q_norm),
        "wo": cross_wo.astype(BF16),
        "norm_ffn": r3(norm_ffn),
        "w_gu": ffn_w_gate_up.astype(BF16),
        "w_down": ffn_w_down.astype(BF16),
        "final_norm": final_norm.reshape(1, D_MODEL),
    }

    mk_p, mv_p = mem_kv(mem_prompt, r3(norm_mem), cross_wk.astype(BF16), cross_wv.astype(BF16), r3(cross_k_norm))

    def pad_hist(conv):
        pad = [(0, 0)] * (conv.ndim - 2) + [(HIST - (CONV_W - 1), 0), (0, 0)]
        return jnp.pad(conv, pad)

    st_p = {
        "hist0": jnp.zeros((bp, HIST, 2 * ML_W), F32),
        "c0": jnp.zeros((bp, ML_H, ML_DH, ML_DH), F32),
        "n0": jnp.zeros((bp, 8, ML_DH), F32),
        "m0": jnp.zeros((bp, 1, GATE_W), F32),
        "s0": jnp.zeros((bp, HG_H, HG_DK, HG_DV), F32),
        "mk": mk_p, "mv": mv_p,
    }
    hist_s = pad_hist(state_mlstm_conv)
    n_s = jnp.pad(state_mlstm_n, ((0, 0), (0, 0), (0, 8 - ML_H), (0, 0)))
    m_s = jnp.pad(state_mlstm_m, ((0, 0), (0, 0), (0, GATE_W - ML_H))).reshape(depth, bs, 1, GATE_W)
    past_k = cache_attn_k.reshape(depth * bs * past * DF_H, DF_DV)
    past_v = cache_attn_v.reshape(depth * bs * past * DF_H, DF_DV)
    mk_s = cache_mem_k.reshape(depth * bs, N_MEM, CA_W)
    mv_s = cache_mem_v.reshape(depth * bs, N_MEM, CA_W)

    cfg_p = dict(PROMPT_CFG, nb=bp, seq=tp, past=0)
    cfg_s = dict(SAMPLE_CFG, nb=bs, seq=ts, past=past)

    def layer(carry, xs):
        xp, xsm, pk_buf, pv_buf = carry
        l, hist_l, c_l, n_l, m_l, s_l = xs
        lidx = l.reshape(1).astype(jnp.int32)
        xp, new_p = _trunk_layer(xp, lidx, w, dict(st_p, pk_buf=pk_buf, pv_buf=pv_buf), cfg_p)
        pk_buf, pv_buf = new_p.pop("pk_buf"), new_p.pop("pv_buf")
        st_s = {"hist0": hist_l, "c0": c_l, "n0": n_l, "m0": m_l, "s0": s_l,
                "past_k": past_k, "past_v": past_v, "mk": mk_s, "mv": mv_s}
        xsm, new_s = _trunk_layer(xsm, lidx, w, st_s, cfg_s)
        return (xp, xsm, pk_buf, pv_buf), (new_p, new_s)

    xs = (jnp.arange(depth, dtype=jnp.int32), hist_s, state_mlstm_C, n_s, m_s, state_hgrn_S)
    kv_rows = depth * bp * tp * DF_H
    init = (x_prompt.reshape(bp * tp, D_MODEL), x_sample.reshape(bs * ts, D_MODEL),
            lax.empty((kv_rows, DF_DV), F32), lax.empty((kv_rows, DF_DV), F32))
    carry, first = layer(init, jax.tree.map(lambda a: a[0], xs))
    (xp, xsm, pk_buf, pv_buf), rest = lax.scan(layer, carry, jax.tree.map(lambda a: a[1:], xs))
    new_p, new_s = jax.tree.map(lambda a, b: jnp.concatenate([a[None], b], axis=0), first, rest)

    y_prompt = xp.reshape(bp, tp, D_MODEL)
    y_sample = xsm.reshape(bs, ts, D_MODEL)
    p_mem_k = mk_p.reshape(depth, bp, N_MEM, CA_H, CA_DH)
    p_mem_v = mv_p.reshape(depth, bp, N_MEM, CA_H, CA_DH)
    p_attn_k = pk_buf.reshape(depth, bp, tp, DF_H, 2 * DF_DH)
    p_attn_v = pv_buf.reshape(depth, bp, tp, DF_H, DF_DV)
    return (y_prompt, y_sample,
            p_attn_k, p_attn_v, p_mem_k, p_mem_v, new_p["conv"], new_p["C"], new_p["n"],
            new_p["m"], new_p["S"],
            new_s["attn_k"], new_s["attn_v"], new_s["conv"], new_s["C"], new_s["n"], new_s["m"], new_s["S"])
```

```python
import functools
import math

import numpy as np
import jax
import jax.numpy as jnp
from jax import lax
from jax.experimental import pallas as pl
from jax.experimental.pallas import tpu as pltpu

F32 = jnp.float32
BF16 = jnp.bfloat16

D_MODEL = 2048
DEPTH = 4
CHUNK = 64
CHUNK_SHIFT = 6
ML_DH = 128
ML_W = 768
ML_H = 6
CONV_W = 4
DF_DH = 64
DF_DV = 128
DF_W = 512
DF_H = 4
HG_DK = 128
HG_DV = 128
HG_W = 768
HG_H = 6
CA_H = 4
CA_DH = 128
CA_W = 512
N_MEM = 256
FF_HIDDEN = 5632
EPS = 1e-6
NEG_BIG = -1e30
LB_FLOOR = 1e-30
LOG2E = math.log2(math.e)

Z_W = 2 * ML_W + 2 * ML_W + 3 * DF_W + 4 * HG_W
GATE_W = 128
ZB_MLQ, ZB_MLK, ZB_MLV, ZB_MLO = 0, 1, 2, 3
ZB_HGQ, ZB_HGF, ZB_HGI, ZB_HGG = 6, 7, 8, 9
ZB_DFQ, ZB_DFK, ZB_DFV = 6, 7, 8

V7X_VMEM_BYTES = 64 * 1024 * 1024
V7X_VMEM_CAP = V7X_VMEM_BYTES - 6 * 1024 * 1024
VMEM_SLACK = 6 * 1024 * 1024
HIST = 8

NT_DIMS = (((1,), (1,)), ((), ()))
TN_DIMS = (((0,), (0,)), ((), ()))


def _vmem_limit(nbytes):
    return int(min(V7X_VMEM_CAP, max(32 * 1024 * 1024, nbytes + VMEM_SLACK)))


def _cparams(sem, vmem_bytes):
    return pltpu.CompilerParams(dimension_semantics=sem, vmem_limit_bytes=_vmem_limit(vmem_bytes))


def _rms(x, g):
    ms = jnp.mean(x * x, axis=-1, keepdims=True)
    return x * lax.rsqrt(ms + EPS) * g


def _dot(a, b):
    return jnp.dot(a, b, preferred_element_type=F32)


def _dot_nt(a, b):
    return lax.dot_general(a, b, NT_DIMS, preferred_element_type=F32)


def _dot_tn(a, b):
    return lax.dot_general(a, b, TN_DIMS, preferred_element_type=F32)


def _log_sigmoid(x):
    return jnp.minimum(x, 0.0) - jnp.log1p(jnp.exp(-jnp.abs(x)))


def _split3(x):
    hi = x.astype(BF16)
    r1 = x - hi.astype(F32)
    mid = r1.astype(BF16)
    lo = (r1 - mid.astype(F32)).astype(BF16)
    return hi, mid, lo


def _dot_exact01(m01, x):
    hi, mid, lo = _split3(x)
    return _dot(m01, hi) + _dot(m01, mid) + _dot(m01, lo)


def _in_proj_kernel(l_ref, x_ref, g_ref, w_ref, wg_ref, z_ref, gate_ref, xn_ref):
    j = pl.program_id(1)

    @pl.when(j == 0)
    def _():
        xn = _rms(x_ref[...], g_ref[0]).astype(BF16)
        xn_ref[...] = xn
        gate_ref[...] = _dot(xn, wg_ref[0])

    z_ref[...] = _dot(xn_ref[...], w_ref[0])


def in_proj(x, lidx, g, w_main, w_gate, *, tm, tn):
    n_rows = x.shape[0]
    grid = (n_rows // tm, Z_W // tn)
    vmem = 2 * tm * D_MODEL * 4 + 2 * D_MODEL * tn * 2 + 2 * tm * tn * 4 + tm * D_MODEL * 2 \
        + 2 * tm * GATE_W * 4 + 2 * D_MODEL * GATE_W * 2 + (4 << 20)
    return pl.pallas_call(
        _in_proj_kernel,
        grid_spec=pltpu.PrefetchScalarGridSpec(
            num_scalar_prefetch=1,
            grid=grid,
            in_specs=[
                pl.BlockSpec((tm, D_MODEL), lambda i, j, l: (i, 0)),
                pl.BlockSpec((1, 1, D_MODEL), lambda i, j, l: (l[0], 0, 0)),
                pl.BlockSpec((1, D_MODEL, tn), lambda i, j, l: (l[0], 0, j)),
                pl.BlockSpec((1, D_MODEL, GATE_W), lambda i, j, l: (l[0], 0, 0)),
            ],
            out_specs=[
                pl.BlockSpec((tm, tn), lambda i, j, l: (i, j)),
                pl.BlockSpec((tm, GATE_W), lambda i, j, l: (i, 0)),
            ],
            scratch_shapes=[pltpu.VMEM((tm, D_MODEL), BF16)],
        ),
        out_shape=[jax.ShapeDtypeStruct((n_rows, Z_W), F32),
                   jax.ShapeDtypeStruct((n_rows, GATE_W), F32)],
        compiler_params=_cparams(("parallel", "arbitrary"), vmem),
        name="in_proj",
    )(lidx, x, g, w_main, w_gate)


def _ffn_kernel(l_ref, x_ref, g_ref, wg_ref, wu_ref, wd_ref, fg_ref, o_ref, xn_ref):
    j = pl.program_id(1)
    nj = pl.num_programs(1)

    @pl.when(j == 0)
    def _():
        x = x_ref[...]
        xn_ref[...] = _rms(x, g_ref[0]).astype(BF16)
        o_ref[...] = x

    xn = xn_ref[...]
    gt = _dot(xn, wg_ref[0])
    up = _dot(xn, wu_ref[0])
    act = (gt * jax.nn.sigmoid(gt) * up).astype(BF16)
    o_ref[...] += _dot(act, wd_ref[0])

    @pl.when(jnp.logical_and(j == nj - 1, l_ref[0] == DEPTH - 1))
    def _():
        o_ref[...] = _rms(o_ref[...], fg_ref[...])


def ffn(x, lidx, g, w_gu, w_down, final_g, *, tm, th, x_buffers=2):
    n_rows = x.shape[0]
    nh = FF_HIDDEN // th
    grid = (n_rows // tm, nh)
    vmem = (2 + x_buffers) * tm * D_MODEL * 4 + tm * D_MODEL * 2 + 6 * D_MODEL * th * 2 + 3 * tm * th * 4 \
        + tm * D_MODEL * 4 + (4 << 20)
    x_mode = {} if x_buffers == 2 else {"pipeline_mode": pl.Buffered(x_buffers)}
    return pl.pallas_call(
        _ffn_kernel,
        grid_spec=pltpu.PrefetchScalarGridSpec(
            num_scalar_prefetch=1,
            grid=grid,
            in_specs=[
                pl.BlockSpec((tm, D_MODEL), lambda i, j, l: (i, 0), **x_mode),
                pl.BlockSpec((1, 1, D_MODEL), lambda i, j, l: (l[0], 0, 0)),
                pl.BlockSpec((1, D_MODEL, th), lambda i, j, l: (l[0], 0, j)),
                pl.BlockSpec((1, D_MODEL, th), lambda i, j, l: (l[0], 0, j + nh)),
                pl.BlockSpec((1, th, D_MODEL), lambda i, j, l: (l[0], j, 0)),
                pl.BlockSpec((1, D_MODEL), lambda i, j, l: (0, 0)),
            ],
            out_specs=pl.BlockSpec((tm, D_MODEL), lambda i, j, l: (i, 0)),
            scratch_shapes=[pltpu.VMEM((tm, D_MODEL), BF16)],
        ),
        out_shape=jax.ShapeDtypeStruct((n_rows, D_MODEL), F32),
        compiler_params=_cparams(("parallel", "arbitrary"), vmem),
        name="ffn",
    )(lidx, x, g, w_gu, w_gu, w_down, final_g)


def _mem_kv_kernel(mem_ref, g_ref, wk_ref, wv_ref, kg_ref, k_ref, v_ref):
    mn = _rms(mem_ref[0], g_ref[0]).astype(BF16)
    k = _dot(mn, wk_ref[0])
    for h in range(CA_H):
        sl = slice(h * CA_DH, (h + 1) * CA_DH)
        k_ref[0, :, sl] = _rms(k[:, sl], kg_ref[0])
    v_ref[0] = _dot(mn, wv_ref[0])


def mem_kv(mem, g, wk, wv, kg):
    nb = mem.shape[0]
    out = jax.ShapeDtypeStruct((DEPTH * nb, N_MEM, CA_W), F32)
    return pl.pallas_call(
        _mem_kv_kernel,
        grid=(DEPTH, nb),
        in_specs=[
            pl.BlockSpec((1, N_MEM, D_MODEL), lambda l, b: (b, 0, 0)),
            pl.BlockSpec((1, 1, D_MODEL), lambda l, b: (l, 0, 0)),
            pl.BlockSpec((1, D_MODEL, CA_W), lambda l, b: (l, 0, 0)),
            pl.BlockSpec((1, D_MODEL, CA_W), lambda l, b: (l, 0, 0)),
            pl.BlockSpec((1, 1, CA_DH), lambda l, b: (l, 0, 0)),
        ],
        out_specs=[pl.BlockSpec((1, N_MEM, CA_W), lambda l, b: (l * nb + b, 0, 0)),
                   pl.BlockSpec((1, N_MEM, CA_W), lambda l, b: (l * nb + b, 0, 0))],
        out_shape=[out, out],
        compiler_params=_cparams(("arbitrary", "arbitrary"), 24 << 20),
        name="mem_kv",
    )(mem, g, wk, wv, kg)


def _post_kernel(l_ref, x_ref, yml_ref, ydf_ref, yhg_ref, wout_ref, gx_ref, wq_ref, qg_ref,
                 mk_ref, mv_ref, wo_ref, o_ref, *, rows_per_stream):
    x1 = x_ref[...]
    x1 = x1 + _dot(yml_ref[...], wout_ref[0, 0:ML_W, :])
    x1 = x1 + _dot(ydf_ref[...], wout_ref[0, ML_W:ML_W + DF_W, :])
    x1 = x1 + _dot(yhg_ref[...], wout_ref[0, ML_W + DF_W:D_MODEL, :])
    hn = _rms(x1, gx_ref[0]).astype(BF16)
    q = _dot(hn, wq_ref[0])
    streams = []
    for si in range(x1.shape[0] // rows_per_stream):
        rs = slice(si * rows_per_stream, (si + 1) * rows_per_stream)
        heads = []
        for h in range(CA_H):
            sl = slice(h * CA_DH, (h + 1) * CA_DH)
            qh = _rms(q[rs, sl], qg_ref[0]).astype(BF16)
            s = _dot_nt(qh, mk_ref[si, :, sl].astype(BF16)) * (CA_DH ** -0.5)
            s = s - jnp.max(s, axis=-1, keepdims=True)
            p = jnp.exp(s)
            p = p / jnp.sum(p, axis=-1, keepdims=True)
            heads.append(_dot(p.astype(BF16), mv_ref[si, :, sl].astype(BF16)))
        streams.append(jnp.concatenate(heads, axis=-1))
    o = (streams[0] if len(streams) == 1 else jnp.concatenate(streams, axis=0)).astype(BF16)
    o_ref[...] = x1 + _dot(o, wo_ref[0])


def post_mix(x, y_ml, y_df, y_hg, lidx, w_out, gx, wq, qg, mk, mv, wo, *, tm, rows_per_batch):
    n_rows = x.shape[0]
    nb = n_rows // rows_per_batch
    grid = (n_rows // tm,)
    w_bytes = (D_MODEL * D_MODEL + 2 * D_MODEL * CA_W) * 2
    row = lambda i, l: (i, 0)
    if tm >= rows_per_batch:
        spt = tm // rows_per_batch
        mem_map = lambda i, l: (l[0] * (nb // spt) + i, 0, 0)
    else:
        spt = 1
        tiles_per_batch = rows_per_batch // tm
        mem_map = lambda i, l: (l[0] * nb + i // tiles_per_batch, 0, 0)
    vmem = 2 * w_bytes + 6 * tm * D_MODEL * 4 + 2 * tm * D_MODEL * 2 + 4 * spt * N_MEM * CA_W * 4 + (6 << 20)
    kern = functools.partial(_post_kernel, rows_per_stream=min(tm, rows_per_batch))
    return pl.pallas_call(
        kern,
        grid_spec=pltpu.PrefetchScalarGridSpec(
            num_scalar_prefetch=1,
            grid=grid,
            in_specs=[
                pl.BlockSpec((tm, D_MODEL), row),
                pl.BlockSpec((tm, ML_W), row),
                pl.BlockSpec((tm, DF_W), row),
                pl.BlockSpec((tm, HG_W), row),
                pl.BlockSpec((1, D_MODEL, D_MODEL), lambda i, l: (l[0], 0, 0)),
                pl.BlockSpec((1, 1, D_MODEL), lambda i, l: (l[0], 0, 0)),
                pl.BlockSpec((1, D_MODEL, CA_W), lambda i, l: (l[0], 0, 0)),
                pl.BlockSpec((1, 1, CA_DH), lambda i, l: (l[0], 0, 0)),
                pl.BlockSpec((spt, N_MEM, CA_W), mem_map),
                pl.BlockSpec((spt, N_MEM, CA_W), mem_map),
                pl.BlockSpec((1, CA_W, D_MODEL), lambda i, l: (l[0], 0, 0)),
            ],
            out_specs=pl.BlockSpec((tm, D_MODEL), row),
        ),
        out_shape=jax.ShapeDtypeStruct((n_rows, D_MODEL), F32),
        compiler_params=_cparams(("parallel",), vmem),
        name="post_mix",
    )(lidx, x, y_ml, y_df, y_hg, w_out, gx, wq, qg, mk, mv, wo)


def _mlstm_kernel(l_ref, zq_ref, zk_ref, zv_ref, zo_ref, gate_ref, hist_ref, cw_ref, cb_ref, bif_ref,
                  og_ref, c0_ref, n0_ref, m0_ref, tri_ref,
                  y_ref, c_ref, n_ref, m_ref, ext_ref, *, tc, lc, part):
    c = pl.program_id(1)

    if part == "init":
        @pl.when(c == 0)
        def _():
            c_ref[...] = c0_ref[...]
            n_ref[...] = n0_ref[...]
            m_ref[...] = m0_ref[...]
            ext_ref[0:HIST, :] = hist_ref[0]
        return

    ext_ref[HIST:HIST + tc, 0:ML_W] = zq_ref[...]
    ext_ref[HIST:HIST + tc, ML_W:2 * ML_W] = zk_ref[...]
    ext = ext_ref[...]
    acc = cb_ref[0] + cw_ref[0, CONV_W - 1:CONV_W, :] * ext[HIST:HIST + tc]
    for j in range(CONV_W - 1):
        acc = acc + cw_ref[0, j:j + 1, :] * pltpu.roll(ext, CONV_W - 1 - j, axis=0)[HIST:HIST + tc]
    tail = ext_ref[tc:tc + HIST, :]
    ext_ref[0:HIST, :] = tail
    qk = acc * jax.nn.sigmoid(acc)

    gz = gate_ref[...] + bif_ref[0]
    lf_all = pltpu.roll(_log_sigmoid(gz), GATE_W - ML_H, axis=1)
    tri = tri_ref[...]
    row_i = lax.broadcasted_iota(jnp.int32, (lc, lc), 0)
    col_i = lax.broadcasted_iota(jnp.int32, (lc, lc), 1)
    causal = row_i >= col_i
    diag = row_i == col_i
    trow = lax.broadcasted_iota(jnp.int32, (lc, GATE_W), 0)
    og = og_ref[0]

    for ci in range(tc // lc):
        r0 = ci * lc
        b_c = _dot_exact01(tri, lf_all[r0:r0 + lc, :])
        r_c = gz[r0:r0 + lc, :] - b_c
        cm = r_c
        d = 1
        while d < lc:
            cm = jnp.maximum(cm, jnp.where(trow >= d, pltpu.roll(cm, d, axis=0), NEG_BIG))
            d *= 2
        m_prev = m_ref[0]
        mx = jnp.maximum(m_prev, cm)
        m_t = b_c + mx
        w_inter_c = jnp.exp(m_prev - mx)
        emt_c = jnp.exp(-m_t)
        m_new = m_t[lc - 1:lc, :]
        b_last = b_c[lc - 1:lc, :]
        w_s_c = jnp.exp(b_last + r_c - m_new)
        decay_c = jnp.exp(b_last + m_prev - m_new)
        m_ref[0] = m_new
        for h in range(ML_H):
            sl = slice(h * ML_DH, (h + 1) * ML_DH)
            hl = slice(h, h + 1)
            q = qk[r0:r0 + lc, h * ML_DH:(h + 1) * ML_DH]
            k = qk[r0:r0 + lc, ML_W + h * ML_DH:ML_W + (h + 1) * ML_DH] * (ML_DH ** -0.5)
            v = zv_ref[r0:r0 + lc, sl]
            qb = q.astype(BF16)
            r_row = jnp.sum(jnp.where(diag, r_c[:, hl], 0.0), axis=0, keepdims=True)
            d_mat = jnp.exp(jnp.where(causal, r_row - mx[:, hl], NEG_BIG))
            w_inter = w_inter_c[:, hl]
            decay = decay_c[:, hl]
            s = _dot_nt(qb, k.astype(BF16)) * d_mat
            c_old = c_ref[0, h]
            n_old = n_ref[0, h:h + 1, :]
            num = _dot(s.astype(BF16), v.astype(BF16)) + w_inter * _dot(qb, c_old.astype(BF16))
            den = jnp.sum(s, axis=-1, keepdims=True) + w_inter * jnp.sum(q * n_old, axis=-1, keepdims=True)
            hh = num / jnp.maximum(jnp.abs(den), emt_c[:, hl])
            kw = k * w_s_c[:, hl]
            c_ref[0, h] = decay * c_old + _dot_tn(kw.astype(BF16), v.astype(BF16))
            n_ref[0, h:h + 1, :] = decay * n_old + jnp.sum(kw, axis=0, keepdims=True)
            o_gate = jax.nn.sigmoid(zo_ref[r0:r0 + lc, sl])
            y_ref[r0:r0 + lc, sl] = (_rms(hh, og) * o_gate).astype(BF16)


def _hgrn_level_masks(lc):
    t = np.arange(lc)[:, None]
    s = np.arange(lc)[None, :]
    masks = []
    h = lc // 2
    while h >= 1:
        odd = (t % (2 * h)) >= h
        same = (t // (2 * h)) == (s // (2 * h))
        masks.append((same & odd & ((s % (2 * h)) < h)).astype(np.float32))
        h //= 2
    masks.append((t == s).astype(np.float32))
    return np.stack(masks, axis=0)


def _hgrn_boundary_rows(a, h):
    lc, width = a.shape
    if h >= 8:
        parts = [jnp.broadcast_to(a[g * 2 * h + h - 1:g * 2 * h + h, :], (2 * h, width)) for g in range(lc // (2 * h))]
        return parts[0] if len(parts) == 1 else jnp.concatenate(parts, axis=0)
    a3 = a.reshape(lc // 8, 8, width)
    sub = lax.broadcasted_iota(jnp.int32, a3.shape, 1)
    out = None
    for g in range(8 // (2 * h)):
        src = g * 2 * h + h - 1
        b = jnp.broadcast_to(a3[:, src:src + 1, :], a3.shape)
        out = b if out is None else jnp.where(sub >= g * 2 * h, b, out)
    return out.reshape(lc, width)


def _hgrn_kernel(l_ref, zq_ref, zf_ref, zi_ref, zg_ref, lbl_ref, og_ref, s0_ref, tri_ref, msk_ref,
                 y_ref, s_ref, st_ref, *, tc, lc, nlev, part):
    c = pl.program_id(1)
    nc = pl.num_programs(1)

    if part == "init":
        @pl.when(c == 0)
        def _():
            for h in range(HG_H):
                st_ref[h] = s0_ref[0, h].T
        return
    if part == "fini":
        @pl.when(c == nc - 1)
        def _():
            for h in range(HG_H):
                s_ref[0, h] = st_ref[h].T
        return

    logits = lbl_ref[...]
    e = jnp.exp(logits - jnp.max(logits, axis=0, keepdims=True))
    p = e / jnp.sum(e, axis=0, keepdims=True)
    drow = lax.broadcasted_iota(jnp.int32, p.shape, 0)
    lb = jnp.sum(jnp.where(jnp.logical_and(drow >= 1, drow <= l_ref[0]), p, 0.0), axis=0, keepdims=True)
    lb_fl = jnp.maximum(lb, LB_FLOOR)
    one_m = 1.0 - lb

    zf = zf_ref[...]
    ez = jnp.exp(-jnp.abs(zf))
    inv = 1.0 / (1.0 + ez)
    sig_p = jnp.where(zf >= 0.0, inv, ez * inv)
    sig_n = jnp.where(zf >= 0.0, ez * inv, inv)
    lf2 = jnp.log(lb_fl + one_m * sig_p) * LOG2E
    kk = one_m * sig_n + (lb - lb_fl)
    zq = zq_ref[...]
    qq = zq * jax.nn.sigmoid(zq) * (HG_DK ** -0.5)
    tri = tri_ref[...]
    og = og_ref[0]

    for ci in range(tc // lc):
        r0 = ci * lc
        a_all = _dot_exact01(tri, lf2[r0:r0 + lc, :])
        lev_exp = [-jnp.abs(a_all - _hgrn_boundary_rows(a_all, lc >> (lv + 1))) for lv in range(nlev)]
        for h in range(HG_H):
            sl = slice(h * HG_DK, (h + 1) * HG_DK)
            q = qq[r0:r0 + lc, sl]
            k = kk[r0:r0 + lc, sl]
            iv = zi_ref[r0:r0 + lc, sl].astype(BF16)
            a_in = a_all[:, sl]
            a_end = a_in[lc - 1:lc, :]
            st = st_ref[h]
            o = _dot_nt((q * jnp.exp2(a_in)).astype(BF16), st.astype(BF16))
            att = jnp.where(msk_ref[nlev] > 0, _dot_nt(q.astype(BF16), k.astype(BF16)), 0.0)
            for lv in range(nlev):
                xf = jnp.exp2(lev_exp[lv][:, sl])
                pm = _dot_nt((q * xf).astype(BF16), (k * xf).astype(BF16))
                att = att + jnp.where(msk_ref[lv] > 0, pm, 0.0)
            o = o + _dot(att.astype(BF16), iv)
            k_end = (k * jnp.exp2(a_end - a_in)).astype(BF16)
            st_ref[h] = st * jnp.exp2(a_end) + _dot_tn(iv, k_end)
            gsl = zg_ref[r0:r0 + lc, sl]
            y_ref[r0:r0 + lc, sl] = (_rms(o, og) * (gsl * jax.nn.sigmoid(gsl))).astype(BF16)


N_ML_IN, N_HG_IN, N_ML_OUT, N_HG_OUT = 14, 9, 4, 2


def _recurrent_kernel(l_ref, *refs, tc, lc_ml, lc_hg, nlev):
    o0 = N_ML_IN + N_HG_IN
    ml_in, hg_in = refs[0:N_ML_IN], refs[N_ML_IN:o0]
    ml_out, hg_out = refs[o0:o0 + N_ML_OUT], refs[o0 + N_ML_OUT:o0 + N_ML_OUT + N_HG_OUT]
    ml_sc, hg_sc = refs[o0 + N_ML_OUT + N_HG_OUT], refs[o0 + N_ML_OUT + N_HG_OUT + 1]
    ml = functools.partial(_mlstm_kernel, l_ref, *ml_in, *ml_out, ml_sc, tc=tc, lc=lc_ml)
    hg = functools.partial(_hgrn_kernel, l_ref, *hg_in, *hg_out, hg_sc, tc=tc, lc=lc_hg, nlev=nlev)
    ml(part="init")
    hg(part="init")
    ml(part="main")
    hg(part="main")
    hg(part="fini")


def recurrent_mixers(z, gates, lidx, hist0, conv_w, conv_b, b_if, ml_og, c0, n0, m0, lb_logits, hg_og, s0,
                     *, nb, seq, tc, lc_ml, lc_hg):
    n_rows = nb * seq
    nc = seq // tc
    tri_ml = jnp.asarray(np.tril(np.ones((lc_ml, lc_ml), np.float32)), BF16)
    tri_hg = jnp.asarray(np.tril(np.ones((lc_hg, lc_hg), np.float32)), BF16)
    msk_np = _hgrn_level_masks(lc_hg)
    nlev = msk_np.shape[0] - 1
    msk = jnp.asarray(msk_np, F32)
    zmap = lambda blk: (lambda b, c, l: (b * nc + c, blk))
    lmap3 = lambda b, c, l: (l[0], 0, 0)
    bmap3 = lambda b, c, l: (b, 0, 0)
    bmap4 = lambda b, c, l: (b, 0, 0, 0)
    rows = lambda b, c, l: (b * nc + c, 0)
    const2 = lambda b, c, l: (0, 0)
    vmem = 2 * 9 * tc * ML_W * 4 + (HIST + tc) * 2 * ML_W * 4 + 9 * ML_H * ML_DH * ML_DH * 4 \
        + 14 * tc * ML_W * 4 + (8 << 20)
    kern = functools.partial(_recurrent_kernel, tc=tc, lc_ml=lc_ml, lc_hg=lc_hg, nlev=nlev)
    in_specs = [
        pl.BlockSpec((tc, ML_W), zmap(ZB_MLQ)), pl.BlockSpec((tc, ML_W), zmap(ZB_MLK)),
        pl.BlockSpec((tc, ML_W), zmap(ZB_MLV)), pl.BlockSpec((tc, ML_W), zmap(ZB_MLO)),
        pl.BlockSpec((tc, GATE_W), rows),
        pl.BlockSpec((1, HIST, 2 * ML_W), bmap3),
        pl.BlockSpec((1, CONV_W, 2 * ML_W), lmap3), pl.BlockSpec((1, 1, 2 * ML_W), lmap3),
        pl.BlockSpec((1, 1, GATE_W), lmap3), pl.BlockSpec((1, 1, ML_DH), lmap3),
        pl.BlockSpec((1, ML_H, ML_DH, ML_DH), bmap4), pl.BlockSpec((1, 8, ML_DH), bmap3),
        pl.BlockSpec((1, 1, GATE_W), bmap3), pl.BlockSpec((lc_ml, lc_ml), const2),
        pl.BlockSpec((tc, HG_W), zmap(ZB_HGQ)), pl.BlockSpec((tc, HG_W), zmap(ZB_HGF)),
        pl.BlockSpec((tc, HG_W), zmap(ZB_HGI)), pl.BlockSpec((tc, HG_W), zmap(ZB_HGG)),
        pl.BlockSpec((DEPTH, HG_W), const2), pl.BlockSpec((1, 1, HG_DV), lmap3),
        pl.BlockSpec((1, HG_H, HG_DK, HG_DV), bmap4), pl.BlockSpec((lc_hg, lc_hg), const2),
        pl.BlockSpec(msk_np.shape, lambda b, c, l: (0, 0, 0)),
    ]
    assert len(in_specs) == N_ML_IN + N_HG_IN
    out_specs = [
        pl.BlockSpec((tc, ML_W), rows), pl.BlockSpec((1, ML_H, ML_DH, ML_DH), bmap4),
        pl.BlockSpec((1, 8, ML_DH), bmap3), pl.BlockSpec((1, 1, GATE_W), bmap3),
        pl.BlockSpec((tc, HG_W), rows), pl.BlockSpec((1, HG_H, HG_DK, HG_DV), bmap4),
    ]
    out_shape = [
        jax.ShapeDtypeStruct((n_rows, ML_W), BF16), jax.ShapeDtypeStruct((nb, ML_H, ML_DH, ML_DH), F32),
        jax.ShapeDtypeStruct((nb, 8, ML_DH), F32), jax.ShapeDtypeStruct((nb, 1, GATE_W), F32),
        jax.ShapeDtypeStruct((n_rows, HG_W), BF16), jax.ShapeDtypeStruct((nb, HG_H, HG_DK, HG_DV), F32),
    ]
    return pl.pallas_call(
        kern,
        grid_spec=pltpu.PrefetchScalarGridSpec(
            num_scalar_prefetch=1, grid=(nb, nc), in_specs=in_specs, out_specs=out_specs,
            scratch_shapes=[pltpu.VMEM((HIST + tc, 2 * ML_W), F32), pltpu.VMEM((HG_H, HG_DV, HG_DK), F32)]),
        out_shape=out_shape,
        compiler_params=_cparams(("parallel", "arbitrary"), vmem),
        name="recurrent_mixers",
    )(lidx, z, z, z, z, gates, hist0, conv_w, conv_b, b_if, ml_og, c0, n0, m0, tri_ml,
      z, z, z, z, lb_logits, hg_og, s0, tri_hg, msk)


def _qk_prep_kernel(l_ref, zq_ref, zk_ref, zv_ref, qg_ref, kg_ref, grp_ref, qn_ref, kn_ref, knb_ref, vb_ref):
    grp = grp_ref[...]
    qn_ref[...] = (_group_norm64(zq_ref[...], qg_ref[0], grp) * (DF_DH ** -0.5)).astype(BF16)
    kn = _group_norm64(zk_ref[...], kg_ref[0], grp)
    kn_ref[...] = kn
    knb_ref[...] = kn.astype(BF16)
    vb_ref[...] = zv_ref[...].astype(BF16)


def qk_prep(z, lidx, qg, kg, *, tm):
    n_rows = z.shape[0]
    lane = np.arange(DF_W)
    grp = jnp.asarray(((lane[:, None] // DF_DH) == (lane[None, :] // DF_DH)).astype(np.float32) / DF_DH, BF16)
    zmap = lambda blk: (lambda i, l: (i, blk))
    row = lambda i, l: (i, 0)
    return pl.pallas_call(
        _qk_prep_kernel,
        grid_spec=pltpu.PrefetchScalarGridSpec(
            num_scalar_prefetch=1,
            grid=(n_rows // tm,),
            in_specs=[
                pl.BlockSpec((tm, DF_W), zmap(ZB_DFQ)),
                pl.BlockSpec((tm, DF_W), zmap(ZB_DFK)),
                pl.BlockSpec((tm, DF_W), zmap(ZB_DFV)),
                pl.BlockSpec((1, 1, DF_W), lambda i, l: (l[0], 0, 0)),
                pl.BlockSpec((1, 1, DF_W), lambda i, l: (l[0], 0, 0)),
                pl.BlockSpec((DF_W, DF_W), lambda i, l: (0, 0)),
            ],
            out_specs=[pl.BlockSpec((tm, DF_W), row)] * 4,
        ),
        out_shape=[
            jax.ShapeDtypeStruct((n_rows, DF_W), BF16),
            jax.ShapeDtypeStruct((n_rows, DF_W), F32),
            jax.ShapeDtypeStruct((n_rows, DF_W), BF16),
            jax.ShapeDtypeStruct((n_rows, DF_W), BF16),
        ],
        compiler_params=_cparams(("parallel",), 32 << 20),
        name="qk_prep",
    )(lidx, z, z, z, qg, kg, grp)


def _group_norm64(x, g, grp):
    x2 = x * x
    hi = x2.astype(BF16)
    lo = (x2 - hi.astype(F32)).astype(BF16)
    ms = _dot(hi, grp) + _dot(lo, grp)
    return x * lax.rsqrt(ms + EPS) * g


BIAS_W = 128


def _qk_prep_t_kernel(l_ref, zq_ref, zk_ref, zv_ref, qg_ref, kg_ref, grp_ref, pk_in, pv_in,
                      qt_ref, ke_ref, vt_ref, pk_ref, pv_ref, *, tm, seq):
    del pk_in, pv_in
    i = pl.program_id(0)
    grp = grp_ref[...]
    qn = _group_norm64(zq_ref[...], qg_ref[0], grp) * (DF_DH ** -0.5)
    kn = _group_norm64(zk_ref[...], kg_ref[0], grp)
    zv = zv_ref[...]
    pos = lax.rem(i * tm, seq) + lax.broadcasted_iota(jnp.int32, (tm, BIAS_W), 0)
    lane = lax.broadcasted_iota(jnp.int32, (tm, BIAS_W), 1)
    hi_part = (pos >> CHUNK_SHIFT).astype(F32) * float(CHUNK)
    lo_part = (pos & (CHUNK - 1)).astype(F32)
    base = jnp.where(lane == 0, hi_part, jnp.where(lane == 1, lo_part, jnp.where(lane == 2, float(CHUNK), 0.0)))
    for h in range(DF_H):
        sl = slice(h * DF_DV, (h + 1) * DF_DV)
        slope = 2.0 ** (-8.0 * (h + 1) / DF_H)
        c0 = h * (DF_DV + BIAS_W)
        ke_ref[:, c0:c0 + DF_DV] = kn[:, sl].astype(BF16)
        ke_ref[:, c0 + DF_DV:c0 + DF_DV + BIAS_W] = (base * slope).astype(BF16)
        qt_ref[h, 0] = qn[:, sl].T.astype(BF16)
        vt_ref[h, 0] = zv[:, sl].T.astype(BF16)
        rows = pl.ds(h, tm, stride=DF_H)
        pk_ref[rows, :] = kn[:, sl]
        pv_ref[rows, :] = zv[:, sl]


def qk_prep_t(z, lidx, qg, kg, pk_buf, pv_buf, *, tm, seq):
    n_rows = z.shape[0]
    nt = n_rows // tm
    smap = lambda i, l: (l[0] * nt + i, 0)
    any_spec = pl.BlockSpec(memory_space=pl.ANY)
    lane = np.arange(DF_W)
    grp = jnp.asarray(((lane[:, None] // DF_DH) == (lane[None, :] // DF_DH)).astype(np.float32) / DF_DH, BF16)
    zmap = lambda blk: (lambda i, l: (i, blk))
    row = lambda i, l: (i, 0)
    tmap = lambda i, l: (0, i, 0, 0)
    kern = functools.partial(_qk_prep_t_kernel, tm=tm, seq=seq)
    return pl.pallas_call(
        kern,
        grid_spec=pltpu.PrefetchScalarGridSpec(
            num_scalar_prefetch=1,
            grid=(nt,),
            in_specs=[
                pl.BlockSpec((tm, DF_W), zmap(ZB_DFQ)),
                pl.BlockSpec((tm, DF_W), zmap(ZB_DFK)),
                pl.BlockSpec((tm, DF_W), zmap(ZB_DFV)),
                pl.BlockSpec((1, 1, DF_W), lambda i, l: (l[0], 0, 0)),
                pl.BlockSpec((1, 1, DF_W), lambda i, l: (l[0], 0, 0)),
                pl.BlockSpec((DF_W, DF_W), lambda i, l: (0, 0)),
                any_spec,
                any_spec,
            ],
            out_specs=[
                pl.BlockSpec((DF_H, 1, DF_DV, tm), tmap),
                pl.BlockSpec((tm, DF_H * (DF_DV + BIAS_W)), row),
                pl.BlockSpec((DF_H, 1, DF_DV, tm), tmap),
                pl.BlockSpec((tm * DF_H, DF_DV), smap),
                pl.BlockSpec((tm * DF_H, DF_DV), smap),
            ],
        ),
        out_shape=[
            jax.ShapeDtypeStruct((DF_H, nt, DF_DV, tm), BF16),
            jax.ShapeDtypeStruct((n_rows, DF_H * (DF_DV + BIAS_W)), BF16),
            jax.ShapeDtypeStruct((DF_H, nt, DF_DV, tm), BF16),
            jax.ShapeDtypeStruct(pk_buf.shape, F32),
            jax.ShapeDtypeStruct(pv_buf.shape, F32),
        ],
        input_output_aliases={7: 3, 8: 4},
        compiler_params=_cparams(("parallel",), 40 << 20),
        name="qk_prep_t",
    )(lidx, z, z, z, qg, kg, grp, pk_buf, pv_buf)


ONES_ROWS = 16


def _flash_t_kernel(l_ref, qt_ref, ke_ref, vt_ref, lam_ref, ogt_ref, y_ref, m_ref, acc_ref, sa_ref, sb_ref,
                    own_ref, *, tq):
    h = pl.program_id(1)
    qi = pl.program_id(2)
    tk = tq
    slope = jnp.exp2(jnp.full((1, 1), -8.0 / DF_H, F32) * (h + 1).astype(F32))
    qt = qt_ref[0, 0]
    row = lax.broadcasted_iota(jnp.int32, (DF_DV, tq), 0)
    zero = jnp.zeros_like(qt)
    q2t = jnp.concatenate([jnp.where(row < DF_DH, qt, zero), jnp.where(row >= DF_DH, qt, zero)], axis=1)
    brow = lax.broadcasted_iota(jnp.int32, (BIAS_W, 2 * tq), 0)
    a0 = ((qi * tq) >> CHUNK_SHIFT).astype(F32)
    extra = jnp.where(brow < 2, 1.0, jnp.where(brow == 2, -a0, 0.0)).astype(BF16)
    q2e = jnp.concatenate([q2t, extra], axis=0)
    ones = jnp.ones((ONES_ROWS, tk), BF16)

    m_ref[...] = jnp.full(m_ref.shape, NEG_BIG, F32)
    acc_ref[...] = jnp.zeros(acc_ref.shape, F32)

    def scores_into(dst_ref, kj):
        r0 = pl.multiple_of(kj * tk, tk)
        dst_ref[...] = _dot(ke_ref[pl.ds(r0, tk), :], q2e)

    def update(s, kj):
        m_old = m_ref[...]
        m_new = jnp.maximum(m_old, jnp.max(s, axis=0, keepdims=True))
        alpha = jnp.exp(m_old - m_new)
        p = jnp.exp(s - m_new).astype(BF16)
        vt_ext = jnp.concatenate([vt_ref[0, kj], ones], axis=0)
        acc_ref[...] = alpha * acc_ref[...] + _dot(vt_ext, p)
        m_ref[...] = m_new

    @pl.when(qi == 0)
    def _():
        rel_k = lax.broadcasted_iota(jnp.int32, (tq, 2 * tq), 0)
        rel_q = lax.broadcasted_iota(jnp.int32, (tq, 2 * tq), 1)
        rel_q = jnp.where(rel_q >= tq, rel_q - tq, rel_q)
        ahead = rel_k - rel_q
        fix = jnp.where(ahead > 0, (-2.0 * slope) * ahead.astype(F32), 0.0)
        visible = (rel_k >> CHUNK_SHIFT) <= (rel_q >> CHUNK_SHIFT)
        own_ref[...] = jnp.where(visible, fix, NEG_BIG)

    def own_block(s):
        update(s + own_ref[...], qi)

    scores_into(sa_ref, 0)

    def block_pair(t, carry):
        k0 = 2 * t
        scores_into(sb_ref, k0 + 1)
        update(sa_ref[...], k0)
        scores_into(sa_ref, k0 + 2)
        update(sb_ref[...], k0 + 1)
        return carry

    lax.fori_loop(0, qi >> 1, block_pair, 0)

    @pl.when((qi & 1) == 1)
    def _():
        scores_into(sb_ref, qi)
        update(sa_ref[...], qi - 1)
        own_block(sb_ref[...])

    @pl.when((qi & 1) == 0)
    def _():
        own_block(sa_ref[...])

    lam_p = lam_ref[0]
    lam_init = 0.8 - 0.6 * jnp.exp(jnp.full((1, 1), -0.3, F32) * l_ref[0].astype(F32))
    lam = (jnp.exp(jnp.sum(lam_p[0:1] * lam_p[1:2], axis=-1, keepdims=True))
           - jnp.exp(jnp.sum(lam_p[2:3] * lam_p[3:4], axis=-1, keepdims=True)) + lam_init)
    acc = acc_ref[...]
    o_all = acc[0:DF_DV] / acc[DF_DV:DF_DV + 1]
    o = o_all[:, 0:tq] - lam * o_all[:, tq:2 * tq]
    ms = jnp.mean(o * o, axis=0, keepdims=True)
    y = o * lax.rsqrt(ms + EPS) * ogt_ref[0] * (1.0 - lam_init)
    y_ref[...] = y.T.astype(BF16)


def diff_flash_t(qt, ke, vt, lidx, lam_p, out_g_t, *, nb, seq, tq):
    assert tq % CHUNK == 0
    nq = seq // tq
    n_rows = nb * seq
    kew = DF_DV + BIAS_W
    kern = functools.partial(_flash_t_kernel, tq=tq)
    vmem = 2 * seq * kew * 2 + 2 * seq * DF_DV * 2 + 10 * tq * 2 * tq * 4 + (8 << 20)
    return pl.pallas_call(
        kern,
        grid_spec=pltpu.PrefetchScalarGridSpec(
            num_scalar_prefetch=1,
            grid=(nb, DF_H, nq),
            in_specs=[
                pl.BlockSpec((1, 1, DF_DV, tq), lambda b, h, i, l: (h, b * nq + i, 0, 0)),
                pl.BlockSpec((seq, kew), lambda b, h, i, l: (b, h)),
                pl.BlockSpec((1, nq, DF_DV, tq), lambda b, h, i, l: (h, b, 0, 0)),
                pl.BlockSpec((1, 4, DF_DH), lambda b, h, i, l: (l[0], 0, 0)),
                pl.BlockSpec((1, DF_DV, 1), lambda b, h, i, l: (l[0], 0, 0)),
            ],
            out_specs=pl.BlockSpec((tq, DF_DV), lambda b, h, i, l: (b * nq + i, h)),
            scratch_shapes=[pltpu.VMEM((1, 2 * tq), F32), pltpu.VMEM((DF_DV + ONES_ROWS, 2 * tq), F32),
                            pltpu.VMEM((tq, 2 * tq), F32), pltpu.VMEM((tq, 2 * tq), F32),
                            pltpu.VMEM((tq, 2 * tq), F32)],
        ),
        out_shape=jax.ShapeDtypeStruct((n_rows, DF_W), BF16),
        compiler_params=_cparams(("parallel", "parallel", "arbitrary"), vmem),
        name="diff_flash_t",
    )(lidx, qt, ke, vt, lam_p, out_g_t)


def _flash_sample_kernel(l_ref, q_ref, km_ref, vm_ref, kx_ref, vx_ref, lam_ref, og_ref, y_ref, *, tq, tk, past):
    q0 = past
    lam_p = lam_ref[0]
    lam_init = 0.8 - 0.6 * jnp.exp(jnp.full((1, 1), -0.3, F32) * l_ref[0].astype(F32))
    lam = (jnp.exp(jnp.sum(lam_p[0:1] * lam_p[1:2], axis=-1, keepdims=True))
           - jnp.exp(jnp.sum(lam_p[2:3] * lam_p[3:4], axis=-1, keepdims=True)) + lam_init)
    lane = lax.broadcasted_iota(jnp.int32, (tq, DF_DV), 1)
    rel_k = lax.broadcasted_iota(jnp.int32, (2 * tq, tq), 1)
    rel_q = lax.broadcasted_iota(jnp.int32, (2 * tq, tq), 0)
    rel_q = jnp.where(rel_q >= tq, rel_q - tq, rel_q)
    own_bias = rel_q.astype(F32) - jnp.abs(rel_q - rel_k).astype(F32)
    visible = ((q0 + rel_k) >> CHUNK_SHIFT) <= ((q0 + rel_q) >> CHUNK_SHIFT)

    def update(state, s, v):
        m_old, l_old, acc_old = state
        m_new = jnp.maximum(m_old, jnp.max(s, axis=-1, keepdims=True))
        alpha = jnp.exp(m_old - m_new)
        p = jnp.exp(s - m_new)
        return (m_new, alpha * l_old + jnp.sum(p, axis=-1, keepdims=True),
                alpha * acc_old + _dot(p.astype(BF16), v))

    for h in range(DF_H):
        sl = slice(h * DF_DV, (h + 1) * DF_DV)
        slope = 2.0 ** (-8.0 * (h + 1) / DF_H)
        q = q_ref[:, sl]
        zero = jnp.zeros_like(q)
        q2 = jnp.concatenate([jnp.where(lane < DF_DH, q, zero), jnp.where(lane >= DF_DH, q, zero)], axis=0)
        state = (jnp.full((2 * tq, 1), NEG_BIG, F32), jnp.zeros((2 * tq, 1), F32), jnp.zeros((2 * tq, DF_DV), F32))
        for kj in range(past // tk):
            rows = pl.ds(kj * tk * DF_H + h, tk, stride=DF_H)
            kpos = kj * tk + lax.broadcasted_iota(jnp.int32, (1, tk), 1)
            s = _dot_nt(q2, km_ref[rows, :].astype(BF16)) + slope * (kpos - q0).astype(F32)
            state = update(state, s, vm_ref[rows, :].astype(BF16))
        s = _dot_nt(q2, kx_ref[:, sl].astype(BF16)) + slope * own_bias
        _, l_sum, acc = update(state, jnp.where(visible, s, NEG_BIG), vx_ref[:, sl].astype(BF16))
        o_all = acc / l_sum
        o = o_all[0:tq] - lam * o_all[tq:2 * tq]
        y_ref[:, sl] = (_rms(o, og_ref[0]) * (1.0 - lam_init)).astype(BF16)


def diff_flash_sample(qn, k_past, v_past, k_new, v_new, lidx, lam_p, out_g, *, nb, seq, past, tk):
    n_rows = nb * seq
    kern = functools.partial(_flash_sample_kernel, tq=seq, tk=tk, past=past)
    vmem = 4 * past * DF_H * DF_DV * 4 + (8 << 20)
    qmap = lambda b, l: (b, 0)
    cmap = lambda b, l: (l[0] * nb + b, 0)
    return pl.pallas_call(
        kern,
        grid_spec=pltpu.PrefetchScalarGridSpec(
            num_scalar_prefetch=1,
            grid=(nb,),
            in_specs=[
                pl.BlockSpec((seq, DF_W), qmap),
                pl.BlockSpec((past * DF_H, DF_DV), cmap),
                pl.BlockSpec((past * DF_H, DF_DV), cmap),
                pl.BlockSpec((seq, DF_W), qmap),
                pl.BlockSpec((seq, DF_W), qmap),
                pl.BlockSpec((1, 4, DF_DH), lambda b, l: (l[0], 0, 0)),
                pl.BlockSpec((1, 1, DF_DV), lambda b, l: (l[0], 0, 0)),
            ],
            out_specs=pl.BlockSpec((seq, DF_W), qmap),
        ),
        out_shape=jax.ShapeDtypeStruct((n_rows, DF_W), BF16),
        compiler_params=_cparams(("parallel",), vmem),
        name="diff_flash_sample",
    )(lidx, qn, k_past, v_past, k_new, v_new, lam_p, out_g)


def _trunk_layer(x, lidx, w, st, cfg):
    nb, seq, past = cfg["nb"], cfg["seq"], cfg["past"]
    z, gates = in_proj(x, lidx, w["norm_mix"], w["w_main"], w["w_gate"], tm=cfg["tm_in"], tn=cfg["tn_in"])
    y_ml, c_new, n_new, m_new, y_hg, s_new = recurrent_mixers(
        z, gates, lidx, st["hist0"], w["conv_w"], w["conv_b"], w["b_if"], w["ml_og"],
        st["c0"], st["n0"], st["m0"], w["hg_lbl"], w["hg_og"], st["s0"],
        nb=nb, seq=seq, tc=cfg["tc"], lc_ml=cfg["lc_ml"], lc_hg=cfg["lc_hg"])
    z3 = z.reshape(nb, seq, Z_W)
    new = {}
    if cfg["prompt"]:
        qt, ke, vt, new["pk_buf"], new["pv_buf"] = qk_prep_t(
            z, lidx, w["df_qg"], w["df_kg"], st["pk_buf"], st["pv_buf"], tm=cfg["tq"], seq=seq)
        y_df = diff_flash_t(qt, ke, vt, lidx, w["df_lam"], w["df_og_t"], nb=nb, seq=seq, tq=cfg["tq"])
    else:
        qn, kn, kn_b, v_b = qk_prep(z, lidx, w["df_qg"], w["df_kg"], tm=cfg["tm_prep"])
        y_df = diff_flash_sample(qn, st["past_k"], st["past_v"], kn_b, v_b, lidx, w["df_lam"], w["df_og"],
                                 nb=nb, seq=seq, past=past, tk=cfg["tk_past"])
        new["attn_k"] = kn.reshape(nb, seq, DF_H, 2 * DF_DH)
        new["attn_v"] = z3[:, :, ZB_DFV * DF_W:(ZB_DFV + 1) * DF_W].reshape(nb, seq, DF_H, DF_DV)
    x = post_mix(x, y_ml, y_df, y_hg, lidx, w["w_out"], w["norm_cross"], w["wq"], w["ca_qg"],
                 st["mk"], st["mv"], w["wo"], tm=cfg["tm_post"], rows_per_batch=seq)
    x = ffn(x, lidx, w["norm_ffn"], w["w_gu"], w["w_down"], w["final_norm"], tm=cfg["tm_ffn"], th=cfg["th"],
            x_buffers=cfg["ffn_x_buffers"])
    new.update({
        "conv": z3[:, seq - (CONV_W - 1):, 0:2 * ML_W],
        "C": c_new,
        "n": n_new[:, :ML_H, :],
        "m": m_new[:, 0, :ML_H],
        "S": s_new,
    })
    return x, new


PROMPT_CFG = dict(prompt=True, tm_in=1024, tn_in=1536, tc=512, lc_ml=128, lc_hg=128, tq=512,
                  tm_post=512, tm_ffn=1024, th=512, ffn_x_buffers=2)
SAMPLE_CFG = dict(prompt=False, tm_in=128, tn_in=768, tc=16, lc_ml=16, lc_hg=16, tm_prep=128, tk_past=2048,
                  tm_post=128, tm_ffn=128, th=512, ffn_x_buffers=2)


def kernel(x_prompt, x_sample, mem_prompt, cache_attn_k, cache_attn_v, cache_mem_k, cache_mem_v, state_mlstm_conv, state_mlstm_C, state_mlstm_n, state_mlstm_m, state_hgrn_S, norm_mix, w_in, mlstm_conv_w, mlstm_conv_b, mlstm_b_i, mlstm_b_f, mlstm_out_norm, diff_q_norm, diff_k_norm, diff_lambda, diff_out_norm, hgrn_lb_logits, hgrn_out_norm, w_out, norm_cross, norm_mem, cross_wq, cross_wk, cross_wv, cross_q_norm, cross_k_norm, cross_wo, norm_ffn, ffn_w_gate_up, ffn_w_down, final_norm):
    bp, tp = x_prompt.shape[:2]
    bs, ts = x_sample.shape[:2]
    past = cache_attn_k.shape[2]
    depth = w_in.shape[0]
    assert depth == DEPTH and x_prompt.shape[2] == D_MODEL

    g_off = 4 * ML_W
    r3 = lambda a: a.reshape(depth, 1, a.shape[-1])
    w = {
        "norm_mix": r3(norm_mix),
        "w_main": jnp.concatenate([w_in[:, :, :g_off], w_in[:, :, g_off + 2 * ML_H:]], axis=-1).astype(BF16),
        "w_gate": jnp.pad(w_in[:, :, g_off:g_off + 2 * ML_H], ((0, 0), (0, 0), (0, GATE_W - 2 * ML_H))).astype(BF16),
        "conv_w": mlstm_conv_w,
        "conv_b": r3(mlstm_conv_b),
        "b_if": r3(jnp.pad(jnp.concatenate([mlstm_b_i, mlstm_b_f], axis=-1), ((0, 0), (0, GATE_W - 2 * ML_H)))),
        "ml_og": r3(mlstm_out_norm),
        "df_qg": r3(jnp.tile(diff_q_norm, (1, DF_W // DF_DH))),
        "df_kg": r3(jnp.tile(diff_k_norm, (1, DF_W // DF_DH))),
        "df_lam": diff_lambda,
        "df_og": r3(diff_out_norm),
        "df_og_t": diff_out_norm.reshape(depth, DF_DV, 1),
        "hg_lbl": hgrn_lb_logits,
        "hg_og": r3(hgrn_out_norm),
        "w_out": w_out.astype(BF16),
        "norm_cross": r3(norm_cross),
        "wq": cross_wq.astype(BF16),
        "ca_qg": r3(cross_q_norm),
        "wo": cross_wo.astype(BF16),
        "norm_ffn": r3(norm_ffn),
        "w_gu": ffn_w_gate_up.astype(BF16),
        "w_down": ffn_w_down.astype(BF16),
        "final_norm": final_norm.reshape(1, D_MODEL),
    }

    mk_p, mv_p = mem_kv(mem_prompt, r3(norm_mem), cross_wk.astype(BF16), cross_wv.astype(BF16), r3(cross_k_norm))

    def pad_hist(conv):
        pad = [(0, 0)] * (conv.ndim - 2) + [(HIST - (CONV_W - 1), 0), (0, 0)]
        return jnp.pad(conv, pad)

    st_p = {
        "hist0": jnp.zeros((bp, HIST, 2 * ML_W), F32),
        "c0": jnp.zeros((bp, ML_H, ML_DH, ML_DH), F32),
        "n0": jnp.zeros((bp, 8, ML_DH), F32),
        "m0": jnp.zeros((bp, 1, GATE_W), F32),
        "s0": jnp.zeros((bp, HG_H, HG_DK, HG_DV), F32),
        "mk": mk_p, "mv": mv_p,
    }
    hist_s = pad_hist(state_mlstm_conv)
    n_s = jnp.pad(state_mlstm_n, ((0, 0), (0, 0), (0, 8 - ML_H), (0, 0)))
    m_s = jnp.pad(state_mlstm_m, ((0, 0), (0, 0), (0, GATE_W - ML_H))).reshape(depth, bs, 1, GATE_W)
    past_k = cache_attn_k.reshape(depth * bs * past * DF_H, DF_DV)
    past_v = cache_attn_v.reshape(depth * bs * past * DF_H, DF_DV)
    mk_s = cache_mem_k.reshape(depth * bs, N_MEM, CA_W)
    mv_s = cache_mem_v.reshape(depth * bs, N_MEM, CA_W)

    cfg_p = dict(PROMPT_CFG, nb=bp, seq=tp, past=0)
    cfg_s = dict(SAMPLE_CFG, nb=bs, seq=ts, past=past)

    def layer(carry, xs):
        xp, xsm, pk_buf, pv_buf = carry
        l, hist_l, c_l, n_l, m_l, s_l = xs
        lidx = l.reshape(1).astype(jnp.int32)
        xp, new_p = _trunk_layer(xp, lidx, w, dict(st_p, pk_buf=pk_buf, pv_buf=pv_buf), cfg_p)
        pk_buf, pv_buf = new_p.pop("pk_buf"), new_p.pop("pv_buf")
        st_s = {"hist0": hist_l, "c0": c_l, "n0": n_l, "m0": m_l, "s0": s_l,
                "past_k": past_k, "past_v": past_v, "mk": mk_s, "mv": mv_s}
        xsm, new_s = _trunk_layer(xsm, lidx, w, st_s, cfg_s)
        return (xp, xsm, pk_buf, pv_buf), (new_p, new_s)

    xs = (jnp.arange(depth, dtype=jnp.int32), hist_s, state_mlstm_C, n_s, m_s, state_hgrn_S)
    kv_rows = depth * bp * tp * DF_H
    init = (x_prompt.reshape(bp * tp, D_MODEL), x_sample.reshape(bs * ts, D_MODEL),
            lax.empty((kv_rows, DF_DV), F32), lax.empty((kv_rows, DF_DV), F32))
    carry, first = layer(init, jax.tree.map(lambda a: a[0], xs))
    (xp, xsm, pk_buf, pv_buf), rest = lax.scan(layer, carry, jax.tree.map(lambda a: a[1:], xs))
    new_p, new_s = jax.tree.map(lambda a, b: jnp.concatenate([a[None], b], axis=0), first, rest)

    y_prompt = xp.reshape(bp, tp, D_MODEL)
    y_sample = xsm.reshape(bs, ts, D_MODEL)
    p_mem_k = mk_p.reshape(depth, bp, N_MEM, CA_H, CA_DH)
    p_mem_v = mv_p.reshape(depth, bp, N_MEM, CA_H, CA_DH)
    p_attn_k = pk_buf.reshape(depth, bp, tp, DF_H, 2 * DF_DH)
    p_attn_v = pv_buf.reshape(depth, bp, tp, DF_H, DF_DV)
    return (y_prompt, y_sample,
            p_attn_k, p_attn_v, p_mem_k, p_mem_v, new_p["conv"], new_p["C"], new_p["n"],
            new_p["m"], new_p["S"],
            new_s["attn_k"], new_s["attn_v"], new_s["conv"], new_s["C"], new_s["n"], new_s["m"], new_s["S"])
```

```python
import functools
import math

import numpy as np
import jax
import jax.numpy as jnp
from jax import lax
from jax.experimental import pallas as pl
from jax.experimental.pallas import tpu as pltpu

F32 = jnp.float32
BF16 = jnp.bfloat16

D_MODEL = 2048
DEPTH = 4
CHUNK = 64
CHUNK_SHIFT = 6
ML_DH = 128
ML_W = 768
ML_H = 6
CONV_W = 4
DF_DH = 64
DF_DV = 128
DF_W = 512
DF_H = 4
HG_DK = 128
HG_DV = 128
HG_W = 768
HG_H = 6
CA_H = 4
CA_DH = 128
CA_W = 512
N_MEM = 256
FF_HIDDEN = 5632
EPS = 1e-6
NEG_BIG = -1e30
LB_FLOOR = 1e-30
LOG2E = math.log2(math.e)

Z_W = 2 * ML_W + 2 * ML_W + 3 * DF_W + 4 * HG_W
GATE_W = 128
ZB_MLQ, ZB_MLK, ZB_MLV, ZB_MLO = 0, 1, 2, 3
ZB_HGQ, ZB_HGF, ZB_HGI, ZB_HGG = 6, 7, 8, 9
ZB_DFQ, ZB_DFK, ZB_DFV = 6, 7, 8

V7X_VMEM_BYTES = 64 * 1024 * 1024
V7X_VMEM_CAP = V7X_VMEM_BYTES - 6 * 1024 * 1024
VMEM_SLACK = 6 * 1024 * 1024
HIST = 8

NT_DIMS = (((1,), (1,)), ((), ()))
TN_DIMS = (((0,), (0,)), ((), ()))


def _vmem_limit(nbytes):
    return int(min(V7X_VMEM_CAP, max(32 * 1024 * 1024, nbytes + VMEM_SLACK)))


def _cparams(sem, vmem_bytes):
    return pltpu.CompilerParams(dimension_semantics=sem, vmem_limit_bytes=_vmem_limit(vmem_bytes))


def _rms(x, g):
    ms = jnp.mean(x * x, axis=-1, keepdims=True)
    return x * lax.rsqrt(ms + EPS) * g


def _dot(a, b):
    return jnp.dot(a, b, preferred_element_type=F32)


def _dot_nt(a, b):
    return lax.dot_general(a, b, NT_DIMS, preferred_element_type=F32)


def _dot_tn(a, b):
    return lax.dot_general(a, b, TN_DIMS, preferred_element_type=F32)


def _log_sigmoid(x):
    return jnp.minimum(x, 0.0) - jnp.log1p(jnp.exp(-jnp.abs(x)))


def _split3(x):
    hi = x.astype(BF16)
    r1 = x - hi.astype(F32)
    mid = r1.astype(BF16)
    lo = (r1 - mid.astype(F32)).astype(BF16)
    return hi, mid, lo


def _dot_exact01(m01, x):
    hi, mid, lo = _split3(x)
    return _dot(m01, hi) + _dot(m01, mid) + _dot(m01, lo)


def _in_proj_kernel(l_ref, x_ref, g_ref, w_ref, wg_ref, z_ref, gate_ref, xn_ref):
    j = pl.program_id(1)

    @pl.when(j == 0)
    def _():
        xn = _rms(x_ref[...], g_ref[0]).astype(BF16)
        xn_ref[...] = xn
        gate_ref[...] = _dot(xn, wg_ref[0])

    z_ref[...] = _dot(xn_ref[...], w_ref[0])


def in_proj(x, lidx, g, w_main, w_gate, *, tm, tn):
    n_rows = x.shape[0]
    grid = (n_rows // tm, Z_W // tn)
    vmem = 2 * tm * D_MODEL * 4 + 2 * D_MODEL * tn * 2 + 2 * tm * tn * 4 + tm * D_MODEL * 2 \
        + 2 * tm * GATE_W * 4 + 2 * D_MODEL * GATE_W * 2 + (4 << 20)
    return pl.pallas_call(
        _in_proj_kernel,
        grid_spec=pltpu.PrefetchScalarGridSpec(
            num_scalar_prefetch=1,
            grid=grid,
            in_specs=[
                pl.BlockSpec((tm, D_MODEL), lambda i, j, l: (i, 0)),
                pl.BlockSpec((1, 1, D_MODEL), lambda i, j, l: (l[0], 0, 0)),
                pl.BlockSpec((1, D_MODEL, tn), lambda i, j, l: (l[0], 0, j)),
                pl.BlockSpec((1, D_MODEL, GATE_W), lambda i, j, l: (l[0], 0, 0)),
            ],
            out_specs=[
                pl.BlockSpec((tm, tn), lambda i, j, l: (i, j)),
                pl.BlockSpec((tm, GATE_W), lambda i, j, l: (i, 0)),
            ],
            scratch_shapes=[pltpu.VMEM((tm, D_MODEL), BF16)],
        ),
        out_shape=[jax.ShapeDtypeStruct((n_rows, Z_W), F32),
                   jax.ShapeDtypeStruct((n_rows, GATE_W), F32)],
        compiler_params=_cparams(("parallel", "arbitrary"), vmem),
        name="in_proj",
    )(lidx, x, g, w_main, w_gate)


def _ffn_kernel(l_ref, x_ref, g_ref, wg_ref, wu_ref, wd_ref, fg_ref, o_ref, xn_ref):
    j = pl.program_id(1)
    nj = pl.num_programs(1)

    @pl.when(j == 0)
    def _():
        x = x_ref[...]
        xn_ref[...] = _rms(x, g_ref[0]).astype(BF16)
        o_ref[...] = x

    xn = xn_ref[...]
    gt = _dot(xn, wg_ref[0])
    up = _dot(xn, wu_ref[0])
    act = (gt * jax.nn.sigmoid(gt) * up).astype(BF16)
    o_ref[...] += _dot(act, wd_ref[0])

    @pl.when(jnp.logical_and(j == nj - 1, l_ref[0] == DEPTH - 1))
    def _():
        o_ref[...] = _rms(o_ref[...], fg_ref[...])


def ffn(x, lidx, g, w_gu, w_down, final_g, *, tm, th, x_buffers=2):
    n_rows = x.shape[0]
    nh = FF_HIDDEN // th
    grid = (n_rows // tm, nh)
    vmem = (2 + x_buffers) * tm * D_MODEL * 4 + tm * D_MODEL * 2 + 6 * D_MODEL * th * 2 + 3 * tm * th * 4 \
        + tm * D_MODEL * 4 + (4 << 20)
    x_mode = {} if x_buffers == 2 else {"pipeline_mode": pl.Buffered(x_buffers)}
    return pl.pallas_call(
        _ffn_kernel,
        grid_spec=pltpu.PrefetchScalarGridSpec(
            num_scalar_prefetch=1,
            grid=grid,
            in_specs=[
                pl.BlockSpec((tm, D_MODEL), lambda i, j, l: (i, 0), **x_mode),
                pl.BlockSpec((1, 1, D_MODEL), lambda i, j, l: (l[0], 0, 0)),
                pl.BlockSpec((1, D_MODEL, th), lambda i, j, l: (l[0], 0, j)),
                pl.BlockSpec((1, D_MODEL, th), lambda i, j, l: (l[0], 0, j + nh)),
                pl.BlockSpec((1, th, D_MODEL), lambda i, j, l: (l[0], j, 0)),
                pl.BlockSpec((1, D_MODEL), lambda i, j, l: (0, 0)),
            ],
            out_specs=pl.BlockSpec((tm, D_MODEL), lambda i, j, l: (i, 0)),
            scratch_shapes=[pltpu.VMEM((tm, D_MODEL), BF16)],
        ),
        out_shape=jax.ShapeDtypeStruct((n_rows, D_MODEL), F32),
        compiler_params=_cparams(("parallel", "arbitrary"), vmem),
        name="ffn",
    )(lidx, x, g, w_gu, w_gu, w_down, final_g)


def _mem_kv_kernel(mem_ref, g_ref, wk_ref, wv_ref, kg_ref, k_ref, v_ref):
    mn = _rms(mem_ref[0], g_ref[0]).astype(BF16)
    k = _dot(mn, wk_ref[0])
    for h in range(CA_H):
        sl = slice(h * CA_DH, (h + 1) * CA_DH)
        k_ref[0, :, sl] = _rms(k[:, sl], kg_ref[0])
    v_ref[0] = _dot(mn, wv_ref[0])


def mem_kv(mem, g, wk, wv, kg):
    nb = mem.shape[0]
    out = jax.ShapeDtypeStruct((DEPTH * nb, N_MEM, CA_W), F32)
    return pl.pallas_call(
        _mem_kv_kernel,
        grid=(DEPTH, nb),
        in_specs=[
            pl.BlockSpec((1, N_MEM, D_MODEL), lambda l, b: (b, 0, 0)),
            pl.BlockSpec((1, 1, D_MODEL), lambda l, b: (l, 0, 0)),
            pl.BlockSpec((1, D_MODEL, CA_W), lambda l, b: (l, 0, 0)),
            pl.BlockSpec((1, D_MODEL, CA_W), lambda l, b: (l, 0, 0)),
            pl.BlockSpec((1, 1, CA_DH), lambda l, b: (l, 0, 0)),
        ],
        out_specs=[pl.BlockSpec((1, N_MEM, CA_W), lambda l, b: (l * nb + b, 0, 0)),
                   pl.BlockSpec((1, N_MEM, CA_W), lambda l, b: (l * nb + b, 0, 0))],
        out_shape=[out, out],
        compiler_params=_cparams(("arbitrary", "arbitrary"), 24 << 20),
        name="mem_kv",
    )(mem, g, wk, wv, kg)


def _post_kernel(l_ref, x_ref, yml_ref, ydf_ref, yhg_ref, wout_ref, gx_ref, wq_ref, qg_ref,
                 mk_ref, mv_ref, wo_ref, o_ref, *, rows_per_stream):
    x1 = x_ref[...]
    x1 = x1 + _dot(yml_ref[...], wout_ref[0, 0:ML_W, :])
    x1 = x1 + _dot(ydf_ref[...], wout_ref[0, ML_W:ML_W + DF_W, :])
    x1 = x1 + _dot(yhg_ref[...], wout_ref[0, ML_W + DF_W:D_MODEL, :])
    hn = _rms(x1, gx_ref[0]).astype(BF16)
    q = _dot(hn, wq_ref[0])
    streams = []
    for si in range(x1.shape[0] // rows_per_stream):
        rs = slice(si * rows_per_stream, (si + 1) * rows_per_stream)
        heads = []
        for h in range(CA_H):
            sl = slice(h * CA_DH, (h + 1) * CA_DH)
            qh = _rms(q[rs, sl], qg_ref[0]).astype(BF16)
            s = _dot_nt(qh, mk_ref[si, :, sl].astype(BF16)) * (CA_DH ** -0.5)
            s = s - jnp.max(s, axis=-1, keepdims=True)
            p = jnp.exp(s)
            p = p / jnp.sum(p, axis=-1, keepdims=True)
            heads.append(_dot(p.astype(BF16), mv_ref[si, :, sl].astype(BF16)))
        streams.append(jnp.concatenate(heads, axis=-1))
    o = (streams[0] if len(streams) == 1 else jnp.concatenate(streams, axis=0)).astype(BF16)
    o_ref[...] = x1 + _dot(o, wo_ref[0])


def post_mix(x, y_ml, y_df, y_hg, lidx, w_out, gx, wq, qg, mk, mv, wo, *, tm, rows_per_batch):
    n_rows = x.shape[0]
    nb = n_rows // rows_per_batch
    grid = (n_rows // tm,)
    w_bytes = (D_MODEL * D_MODEL + 2 * D_MODEL * CA_W) * 2
    row = lambda i, l: (i, 0)
    if tm >= rows_per_batch:
        spt = tm // rows_per_batch
        mem_map = lambda i, l: (l[0] * (nb // spt) + i, 0, 0)
    else:
        spt = 1
        tiles_per_batch = rows_per_batch // tm
        mem_map = lambda i, l: (l[0] * nb + i // tiles_per_batch, 0, 0)
    vmem = 2 * w_bytes + 6 * tm * D_MODEL * 4 + 2 * tm * D_MODEL * 2 + 4 * spt * N_MEM * CA_W * 4 + (6 << 20)
    kern = functools.partial(_post_kernel, rows_per_stream=min(tm, rows_per_batch))
    return pl.pallas_call(
        kern,
        grid_spec=pltpu.PrefetchScalarGridSpec(
            num_scalar_prefetch=1,
            grid=grid,
            in_specs=[
                pl.BlockSpec((tm, D_MODEL), row),
                pl.BlockSpec((tm, ML_W), row),
                pl.BlockSpec((tm, DF_W), row),
                pl.BlockSpec((tm, HG_W), row),
                pl.BlockSpec((1, D_MODEL, D_MODEL), lambda i, l: (l[0], 0, 0)),
                pl.BlockSpec((1, 1, D_MODEL), lambda i, l: (l[0], 0, 0)),
                pl.BlockSpec((1, D_MODEL, CA_W), lambda i, l: (l[0], 0, 0)),
                pl.BlockSpec((1, 1, CA_DH), lambda i, l: (l[0], 0, 0)),
                pl.BlockSpec((spt, N_MEM, CA_W), mem_map),
                pl.BlockSpec((spt, N_MEM, CA_W), mem_map),
                pl.BlockSpec((1, CA_W, D_MODEL), lambda i, l: (l[0], 0, 0)),
            ],
            out_specs=pl.BlockSpec((tm, D_MODEL), row),
        ),
        out_shape=jax.ShapeDtypeStruct((n_rows, D_MODEL), F32),
        compiler_params=_cparams(("parallel",), vmem),
        name="post_mix",
    )(lidx, x, y_ml, y_df, y_hg, w_out, gx, wq, qg, mk, mv, wo)


def _mlstm_kernel(l_ref, zq_ref, zk_ref, zv_ref, zo_ref, gate_ref, hist_ref, cw_ref, cb_ref, bif_ref,
                  og_ref, c0_ref, n0_ref, m0_ref, tri_ref,
                  y_ref, c_ref, n_ref, m_ref, ext_ref, *, tc, lc, part):
    c = pl.program_id(1)

    if part == "init":
        @pl.when(c == 0)
        def _():
            c_ref[...] = c0_ref[...]
            n_ref[...] = n0_ref[...]
            m_ref[...] = m0_ref[...]
            ext_ref[0:HIST, :] = hist_ref[0]
        return

    ext_ref[HIST:HIST + tc, 0:ML_W] = zq_ref[...]
    ext_ref[HIST:HIST + tc, ML_W:2 * ML_W] = zk_ref[...]
    ext = ext_ref[...]
    acc = cb_ref[0] + cw_ref[0, CONV_W - 1:CONV_W, :] * ext[HIST:HIST + tc]
    for j in range(CONV_W - 1):
        acc = acc + cw_ref[0, j:j + 1, :] * pltpu.roll(ext, CONV_W - 1 - j, axis=0)[HIST:HIST + tc]
    tail = ext_ref[tc:tc + HIST, :]
    ext_ref[0:HIST, :] = tail
    qk = acc * jax.nn.sigmoid(acc)

    gz = gate_ref[...] + bif_ref[0]
    lf_all = pltpu.roll(_log_sigmoid(gz), GATE_W - ML_H, axis=1)
    tri = tri_ref[...]
    row_i = lax.broadcasted_iota(jnp.int32, (lc, lc), 0)
    col_i = lax.broadcasted_iota(jnp.int32, (lc, lc), 1)
    causal = row_i >= col_i
    diag = row_i == col_i
    trow = lax.broadcasted_iota(jnp.int32, (lc, GATE_W), 0)
    og = og_ref[0]

    for ci in range(tc // lc):
        r0 = ci * lc
        b_c = _dot_exact01(tri, lf_all[r0:r0 + lc, :])
        r_c = gz[r0:r0 + lc, :] - b_c
        cm = r_c
        d = 1
        while d < lc:
            cm = jnp.maximum(cm, jnp.where(trow >= d, pltpu.roll(cm, d, axis=0), NEG_BIG))
            d *= 2
        m_prev = m_ref[0]
        mx = jnp.maximum(m_prev, cm)
        m_t = b_c + mx
        w_inter_c = jnp.exp(m_prev - mx)
        emt_c = jnp.exp(-m_t)
        m_new = m_t[lc - 1:lc, :]
        b_last = b_c[lc - 1:lc, :]
        w_s_c = jnp.exp(b_last + r_c - m_new)
        decay_c = jnp.exp(b_last + m_prev - m_new)
        m_ref[0] = m_new
        for h in range(ML_H):
            sl = slice(h * ML_DH, (h + 1) * ML_DH)
            hl = slice(h, h + 1)
            q = qk[r0:r0 + lc, h * ML_DH:(h + 1) * ML_DH]
            k = qk[r0:r0 + lc, ML_W + h * ML_DH:ML_W + (h + 1) * ML_DH] * (ML_DH ** -0.5)
            v = zv_ref[r0:r0 + lc, sl]
            qb = q.astype(BF16)
            r_row = jnp.sum(jnp.where(diag, r_c[:, hl], 0.0), axis=0, keepdims=True)
            d_mat = jnp.exp(jnp.where(causal, r_row - mx[:, hl], NEG_BIG))
            w_inter = w_inter_c[:, hl]
            decay = decay_c[:, hl]
            s = _dot_nt(qb, k.astype(BF16)) * d_mat
            c_old = c_ref[0, h]
            n_old = n_ref[0, h:h + 1, :]
            num = _dot(s.astype(BF16), v.astype(BF16)) + w_inter * _dot(qb, c_old.astype(BF16))
            den = jnp.sum(s, axis=-1, keepdims=True) + w_inter * jnp.sum(q * n_old, axis=-1, keepdims=True)
            hh = num / jnp.maximum(jnp.abs(den), emt_c[:, hl])
            kw = k * w_s_c[:, hl]
            c_ref[0, h] = decay * c_old + _dot_tn(kw.astype(BF16), v.astype(BF16))
            n_ref[0, h:h + 1, :] = decay * n_old + jnp.sum(kw, axis=0, keepdims=True)
            o_gate = jax.nn.sigmoid(zo_ref[r0:r0 + lc, sl])
            y_ref[r0:r0 + lc, sl] = (_rms(hh, og) * o_gate).astype(BF16)


def _hgrn_level_masks(lc):
    t = np.arange(lc)[:, None]
    s = np.arange(lc)[None, :]
    masks = []
    h = lc // 2
    while h >= 1:
        odd = (t % (2 * h)) >= h
        same = (t // (2 * h)) == (s // (2 * h))
        masks.append((same & odd & ((s % (2 * h)) < h)).astype(np.float32))
        h //= 2
    masks.append((t == s).astype(np.float32))
    return np.stack(masks, axis=0)


def _hgrn_boundary_rows(a, h):
    lc, width = a.shape
    if h >= 8:
        parts = [jnp.broadcast_to(a[g * 2 * h + h - 1:g * 2 * h + h, :], (2 * h, width)) for g in range(lc // (2 * h))]
        return parts[0] if len(parts) == 1 else jnp.concatenate(parts, axis=0)
    a3 = a.reshape(lc // 8, 8, width)
    sub = lax.broadcasted_iota(jnp.int32, a3.shape, 1)
    out = None
    for g in range(8 // (2 * h)):
        src = g * 2 * h + h - 1
        b = jnp.broadcast_to(a3[:, src:src + 1, :], a3.shape)
        out = b if out is None else jnp.where(sub >= g * 2 * h, b, out)
    return out.reshape(lc, width)


def _hgrn_kernel(l_ref, zq_ref, zf_ref, zi_ref, zg_ref, lbl_ref, og_ref, s0_ref, tri_ref, msk_ref,
                 y_ref, s_ref, st_ref, *, tc, lc, nlev, part):
    c = pl.program_id(1)
    nc = pl.num_programs(1)

    if part == "init":
        @pl.when(c == 0)
        def _():
            for h in range(HG_H):
                st_ref[h] = s0_ref[0, h].T
        return
    if part == "fini":
        @pl.when(c == nc - 1)
        def _():
            for h in range(HG_H):
                s_ref[0, h] = st_ref[h].T
        return

    logits = lbl_ref[...]
    e = jnp.exp(logits - jnp.max(logits, axis=0, keepdims=True))
    p = e / jnp.sum(e, axis=0, keepdims=True)
    drow = lax.broadcasted_iota(jnp.int32, p.shape, 0)
    lb = jnp.sum(jnp.where(jnp.logical_and(drow >= 1, drow <= l_ref[0]), p, 0.0), axis=0, keepdims=True)
    lb_fl = jnp.maximum(lb, LB_FLOOR)
    one_m = 1.0 - lb

    zf = zf_ref[...]
    ez = jnp.exp(-jnp.abs(zf))
    inv = 1.0 / (1.0 + ez)
    sig_p = jnp.where(zf >= 0.0, inv, ez * inv)
    sig_n = jnp.where(zf >= 0.0, ez * inv, inv)
    lf2 = jnp.log(lb_fl + one_m * sig_p) * LOG2E
    kk = one_m * sig_n + (lb - lb_fl)
    zq = zq_ref[...]
    qq = zq * jax.nn.sigmoid(zq) * (HG_DK ** -0.5)
    tri = tri_ref[...]
    og = og_ref[0]

    for ci in range(tc // lc):
        r0 = ci * lc
        a_all = _dot_exact01(tri, lf2[r0:r0 + lc, :])
        lev_exp = [-jnp.abs(a_all - _hgrn_boundary_rows(a_all, lc >> (lv + 1))) for lv in range(nlev)]
        for h in range(HG_H):
            sl = slice(h * HG_DK, (h + 1) * HG_DK)
            q = qq[r0:r0 + lc, sl]
            k = kk[r0:r0 + lc, sl]
            iv = zi_ref[r0:r0 + lc, sl].astype(BF16)
            a_in = a_all[:, sl]
            a_end = a_in[lc - 1:lc, :]
            st = st_ref[h]
            o = _dot_nt((q * jnp.exp2(a_in)).astype(BF16), st.astype(BF16))
            att = jnp.where(msk_ref[nlev] > 0, _dot_nt(q.astype(BF16), k.astype(BF16)), 0.0)
            for lv in range(nlev):
                xf = jnp.exp2(lev_exp[lv][:, sl])
                pm = _dot_nt((q * xf).astype(BF16), (k * xf).astype(BF16))
                att = att + jnp.where(msk_ref[lv] > 0, pm, 0.0)
            o = o + _dot(att.astype(BF16), iv)
            k_end = (k * jnp.exp2(a_end - a_in)).astype(BF16)
            st_ref[h] = st * jnp.exp2(a_end) + _dot_tn(iv, k_end)
            gsl = zg_ref[r0:r0 + lc, sl]
            y_ref[r0:r0 + lc, sl] = (_rms(o, og) * (gsl * jax.nn.sigmoid(gsl))).astype(BF16)


N_ML_IN, N_HG_IN, N_ML_OUT, N_HG_OUT = 14, 9, 4, 2


N_PREP_IN, N_PREP_OUT = 8, 5


def _recurrent_kernel(l_ref, *refs, tc, lc_ml, lc_hg, nlev, prep_seq):
    n_pi, n_po = (N_PREP_IN, N_PREP_OUT) if prep_seq else (0, 0)
    o0 = N_ML_IN + N_HG_IN + n_pi
    ml_in, hg_in, pr_in = refs[0:N_ML_IN], refs[N_ML_IN:N_ML_IN + N_HG_IN], refs[N_ML_IN + N_HG_IN:o0]
    o1 = o0 + N_ML_OUT
    o2 = o1 + N_HG_OUT
    ml_out, hg_out, pr_out = refs[o0:o1], refs[o1:o2], refs[o2:o2 + n_po]
    ml_sc, hg_sc = refs[o2 + n_po], refs[o2 + n_po + 1]
    ml = functools.partial(_mlstm_kernel, l_ref, *ml_in, *ml_out, ml_sc, tc=tc, lc=lc_ml)
    hg = functools.partial(_hgrn_kernel, l_ref, *hg_in, *hg_out, hg_sc, tc=tc, lc=lc_hg, nlev=nlev)
    ml(part="init")
    hg(part="init")
    ml(part="main")
    hg(part="main")
    if prep_seq:
        tile = pl.program_id(0) * pl.num_programs(1) + pl.program_id(1)
        _qk_prep_t_kernel(l_ref, *pr_in, *pr_out, tm=tc, seq=prep_seq, tile=tile)
    hg(part="fini")


def recurrent_mixers(z, gates, lidx, hist0, conv_w, conv_b, b_if, ml_og, c0, n0, m0, lb_logits, hg_og, s0,
                     *, nb, seq, tc, lc_ml, lc_hg, prep=None):
    n_rows = nb * seq
    nc = seq // tc
    tri_ml = jnp.asarray(np.tril(np.ones((lc_ml, lc_ml), np.float32)), BF16)
    tri_hg = jnp.asarray(np.tril(np.ones((lc_hg, lc_hg), np.float32)), BF16)
    msk_np = _hgrn_level_masks(lc_hg)
    nlev = msk_np.shape[0] - 1
    msk = jnp.asarray(msk_np, F32)
    zmap = lambda blk: (lambda b, c, l: (b * nc + c, blk))
    lmap3 = lambda b, c, l: (l[0], 0, 0)
    bmap3 = lambda b, c, l: (b, 0, 0)
    bmap4 = lambda b, c, l: (b, 0, 0, 0)
    rows = lambda b, c, l: (b * nc + c, 0)
    const2 = lambda b, c, l: (0, 0)
    vmem = 2 * 9 * tc * ML_W * 4 + (HIST + tc) * 2 * ML_W * 4 + 9 * ML_H * ML_DH * ML_DH * 4 \
        + 14 * tc * ML_W * 4 + (8 << 20)
    kern = functools.partial(_recurrent_kernel, tc=tc, lc_ml=lc_ml, lc_hg=lc_hg, nlev=nlev,
                             prep_seq=seq if prep is not None else 0)
    in_specs = [
        pl.BlockSpec((tc, ML_W), zmap(ZB_MLQ)), pl.BlockSpec((tc, ML_W), zmap(ZB_MLK)),
        pl.BlockSpec((tc, ML_W), zmap(ZB_MLV)), pl.BlockSpec((tc, ML_W), zmap(ZB_MLO)),
        pl.BlockSpec((tc, GATE_W), rows),
        pl.BlockSpec((1, HIST, 2 * ML_W), bmap3),
        pl.BlockSpec((1, CONV_W, 2 * ML_W), lmap3), pl.BlockSpec((1, 1, 2 * ML_W), lmap3),
        pl.BlockSpec((1, 1, GATE_W), lmap3), pl.BlockSpec((1, 1, ML_DH), lmap3),
        pl.BlockSpec((1, ML_H, ML_DH, ML_DH), bmap4), pl.BlockSpec((1, 8, ML_DH), bmap3),
        pl.BlockSpec((1, 1, GATE_W), bmap3), pl.BlockSpec((lc_ml, lc_ml), const2),
        pl.BlockSpec((tc, HG_W), zmap(ZB_HGQ)), pl.BlockSpec((tc, HG_W), zmap(ZB_HGF)),
        pl.BlockSpec((tc, HG_W), zmap(ZB_HGI)), pl.BlockSpec((tc, HG_W), zmap(ZB_HGG)),
        pl.BlockSpec((DEPTH, HG_W), const2), pl.BlockSpec((1, 1, HG_DV), lmap3),
        pl.BlockSpec((1, HG_H, HG_DK, HG_DV), bmap4), pl.BlockSpec((lc_hg, lc_hg), const2),
        pl.BlockSpec(msk_np.shape, lambda b, c, l: (0, 0, 0)),
    ]
    assert len(in_specs) == N_ML_IN + N_HG_IN
    out_specs = [
        pl.BlockSpec((tc, ML_W), rows), pl.BlockSpec((1, ML_H, ML_DH, ML_DH), bmap4),
        pl.BlockSpec((1, 8, ML_DH), bmap3), pl.BlockSpec((1, 1, GATE_W), bmap3),
        pl.BlockSpec((tc, HG_W), rows), pl.BlockSpec((1, HG_H, HG_DK, HG_DV), bmap4),
    ]
    out_shape = [
        jax.ShapeDtypeStruct((n_rows, ML_W), BF16), jax.ShapeDtypeStruct((nb, ML_H, ML_DH, ML_DH), F32),
        jax.ShapeDtypeStruct((nb, 8, ML_DH), F32), jax.ShapeDtypeStruct((nb, 1, GATE_W), F32),
        jax.ShapeDtypeStruct((n_rows, HG_W), BF16), jax.ShapeDtypeStruct((nb, HG_H, HG_DK, HG_DV), F32),
    ]
    operands = [lidx, z, z, z, z, gates, hist0, conv_w, conv_b, b_if, ml_og, c0, n0, m0, tri_ml,
                z, z, z, z, lb_logits, hg_og, s0, tri_hg, msk]
    aliases = {}
    if prep is not None:
        nt = n_rows // tc
        vmem += 2 * 3 * tc * DF_W * 4 + 2 * tc * DF_W * 2 * 4 + 2 * 2 * tc * DF_W * 4
        lane = np.arange(DF_W)
        grp = jnp.asarray(((lane[:, None] // DF_DH) == (lane[None, :] // DF_DH)).astype(np.float32) / DF_DH, BF16)
        z512 =lambda blk: (lambda b, c, l: (b * nc + c, blk))
        tmap = lambda b, c, l: (0, b * nc + c, 0, 0)
        smap = lambda b, c, l: (l[0] * nt + b * nc + c, 0)
        any_spec = pl.BlockSpec(memory_space=pl.ANY)
        in_specs += [
            pl.BlockSpec((tc, DF_W), z512(ZB_DFQ)), pl.BlockSpec((tc, DF_W), z512(ZB_DFK)),
            pl.BlockSpec((tc, DF_W), z512(ZB_DFV)),
            pl.BlockSpec((1, 1, DF_W), lmap3), pl.BlockSpec((1, 1, DF_W), lmap3),
            pl.BlockSpec((DF_W, DF_W), const2), any_spec, any_spec,
        ]
        aliases = {len(operands) + 6: len(out_specs) + 3, len(operands) + 7: len(out_specs) + 4}
        operands += [z, z, z, prep["qg"], prep["kg"], grp, prep["pk_buf"], prep["pv_buf"]]
        out_specs += [
            pl.BlockSpec((DF_H, 1, DF_DV, tc), tmap), pl.BlockSpec((tc, DF_H * (DF_DV + BIAS_W)), rows),
            pl.BlockSpec((DF_H, 1, DF_DV, tc), tmap),
            pl.BlockSpec((tc * DF_H, DF_DV), smap), pl.BlockSpec((tc * DF_H, DF_DV), smap),
        ]
        out_shape += [
            jax.ShapeDtypeStruct((DF_H, nt, DF_DV, tc), BF16),
            jax.ShapeDtypeStruct((n_rows, DF_H * (DF_DV + BIAS_W)), BF16),
            jax.ShapeDtypeStruct((DF_H, nt, DF_DV, tc), BF16),
            jax.ShapeDtypeStruct(prep["pk_buf"].shape, F32), jax.ShapeDtypeStruct(prep["pv_buf"].shape, F32),
        ]
    return pl.pallas_call(
        kern,
        grid_spec=pltpu.PrefetchScalarGridSpec(
            num_scalar_prefetch=1, grid=(nb, nc), in_specs=in_specs, out_specs=out_specs,
            scratch_shapes=[pltpu.VMEM((HIST + tc, 2 * ML_W), F32), pltpu.VMEM((HG_H, HG_DV, HG_DK), F32)]),
        out_shape=out_shape,
        input_output_aliases=aliases,
        compiler_params=_cparams(("parallel", "arbitrary"), vmem),
        name="recurrent_mixers",
    )(*operands)


def _qk_prep_kernel(l_ref, zq_ref, zk_ref, zv_ref, qg_ref, kg_ref, grp_ref, qn_ref, kn_ref, knb_ref, vb_ref):
    grp = grp_ref[...]
    qn_ref[...] = (_group_norm64(zq_ref[...], qg_ref[0], grp) * (DF_DH ** -0.5)).astype(BF16)
    kn = _group_norm64(zk_ref[...], kg_ref[0], grp)
    kn_ref[...] = kn
    knb_ref[...] = kn.astype(BF16)
    vb_ref[...] = zv_ref[...].astype(BF16)


def qk_prep(z, lidx, qg, kg, *, tm):
    n_rows = z.shape[0]
    lane = np.arange(DF_W)
    grp = jnp.asarray(((lane[:, None] // DF_DH) == (lane[None, :] // DF_DH)).astype(np.float32) / DF_DH, BF16)
    zmap = lambda blk: (lambda i, l: (i, blk))
    row = lambda i, l: (i, 0)
    return pl.pallas_call(
        _qk_prep_kernel,
        grid_spec=pltpu.PrefetchScalarGridSpec(
            num_scalar_prefetch=1,
            grid=(n_rows // tm,),
            in_specs=[
                pl.BlockSpec((tm, DF_W), zmap(ZB_DFQ)),
                pl.BlockSpec((tm, DF_W), zmap(ZB_DFK)),
                pl.BlockSpec((tm, DF_W), zmap(ZB_DFV)),
                pl.BlockSpec((1, 1, DF_W), lambda i, l: (l[0], 0, 0)),
                pl.BlockSpec((1, 1, DF_W), lambda i, l: (l[0], 0, 0)),
                pl.BlockSpec((DF_W, DF_W), lambda i, l: (0, 0)),
            ],
            out_specs=[pl.BlockSpec((tm, DF_W), row)] * 4,
        ),
        out_shape=[
            jax.ShapeDtypeStruct((n_rows, DF_W), BF16),
            jax.ShapeDtypeStruct((n_rows, DF_W), F32),
            jax.ShapeDtypeStruct((n_rows, DF_W), BF16),
            jax.ShapeDtypeStruct((n_rows, DF_W), BF16),
        ],
        compiler_params=_cparams(("parallel",), 32 << 20),
        name="qk_prep",
    )(lidx, z, z, z, qg, kg, grp)


def _group_norm64(x, g, grp):
    x2 = x * x
    hi = x2.astype(BF16)
    lo = (x2 - hi.astype(F32)).astype(BF16)
    ms = _dot(hi, grp) + _dot(lo, grp)
    return x * lax.rsqrt(ms + EPS) * g


BIAS_W = 128


def _qk_prep_t_kernel(l_ref, zq_ref, zk_ref, zv_ref, qg_ref, kg_ref, grp_ref, pk_in, pv_in,
                      qt_ref, ke_ref, vt_ref, pk_ref, pv_ref, *, tm, seq, tile=None):
    del pk_in, pv_in
    i = pl.program_id(0) if tile is None else tile
    grp = grp_ref[...]
    qn = _group_norm64(zq_ref[...], qg_ref[0], grp) * (DF_DH ** -0.5)
    kn = _group_norm64(zk_ref[...], kg_ref[0], grp)
    zv = zv_ref[...]
    pos = lax.rem(i * tm, seq) + lax.broadcasted_iota(jnp.int32, (tm, BIAS_W), 0)
    lane = lax.broadcasted_iota(jnp.int32, (tm, BIAS_W), 1)
    hi_part = (pos >> CHUNK_SHIFT).astype(F32) * float(CHUNK)
    lo_part = (pos & (CHUNK - 1)).astype(F32)
    base = jnp.where(lane == 0, hi_part, jnp.where(lane == 1, lo_part, jnp.where(lane == 2, float(CHUNK), 0.0)))
    for h in range(DF_H):
        sl = slice(h * DF_DV, (h + 1) * DF_DV)
        slope = 2.0 ** (-8.0 * (h + 1) / DF_H)
        c0 = h * (DF_DV + BIAS_W)
        ke_ref[:, c0:c0 + DF_DV] = kn[:, sl].astype(BF16)
        ke_ref[:, c0 + DF_DV:c0 + DF_DV + BIAS_W] = (base * slope).astype(BF16)
        qt_ref[h, 0] = qn[:, sl].T.astype(BF16)
        vt_ref[h, 0] = zv[:, sl].T.astype(BF16)
        rows = pl.ds(h, tm, stride=DF_H)
        pk_ref[rows, :] = kn[:, sl]
        pv_ref[rows, :] = zv[:, sl]


def qk_prep_t(z, lidx, qg, kg, pk_buf, pv_buf, *, tm, seq):
    n_rows = z.shape[0]
    nt = n_rows // tm
    smap = lambda i, l: (l[0] * nt + i, 0)
    any_spec = pl.BlockSpec(memory_space=pl.ANY)
    lane = np.arange(DF_W)
    grp = jnp.asarray(((lane[:, None] // DF_DH) == (lane[None, :] // DF_DH)).astype(np.float32) / DF_DH, BF16)
    zmap = lambda blk: (lambda i, l: (i, blk))
    row = lambda i, l: (i, 0)
    tmap = lambda i, l: (0, i, 0, 0)
    kern = functools.partial(_qk_prep_t_kernel, tm=tm, seq=seq)
    return pl.pallas_call(
        kern,
        grid_spec=pltpu.PrefetchScalarGridSpec(
            num_scalar_prefetch=1,
            grid=(nt,),
            in_specs=[
                pl.BlockSpec((tm, DF_W), zmap(ZB_DFQ)),
                pl.BlockSpec((tm, DF_W), zmap(ZB_DFK)),
                pl.BlockSpec((tm, DF_W), zmap(ZB_DFV)),
                pl.BlockSpec((1, 1, DF_W), lambda i, l: (l[0], 0, 0)),
                pl.BlockSpec((1, 1, DF_W), lambda i, l: (l[0], 0, 0)),
                pl.BlockSpec((DF_W, DF_W), lambda i, l: (0, 0)),
                any_spec,
                any_spec,
            ],
            out_specs=[
                pl.BlockSpec((DF_H, 1, DF_DV, tm), tmap),
                pl.BlockSpec((tm, DF_H * (DF_DV + BIAS_W)), row),
                pl.BlockSpec((DF_H, 1, DF_DV, tm), tmap),
                pl.BlockSpec((tm * DF_H, DF_DV), smap),
                pl.BlockSpec((tm * DF_H, DF_DV), smap),
            ],
        ),
        out_shape=[
            jax.ShapeDtypeStruct((DF_H, nt, DF_DV, tm), BF16),
            jax.ShapeDtypeStruct((n_rows, DF_H * (DF_DV + BIAS_W)), BF16),
            jax.ShapeDtypeStruct((DF_H, nt, DF_DV, tm), BF16),
            jax.ShapeDtypeStruct(pk_buf.shape, F32),
            jax.ShapeDtypeStruct(pv_buf.shape, F32),
        ],
        input_output_aliases={7: 3, 8: 4},
        compiler_params=_cparams(("parallel",), 40 << 20),
        name="qk_prep_t",
    )(lidx, z, z, z, qg, kg, grp, pk_buf, pv_buf)


ONES_ROWS = 16


def _flash_t_kernel(l_ref, qt_ref, ke_ref, vt_ref, lam_ref, ogt_ref, y_ref, m_ref, acc_ref, sa_ref, sb_ref,
                    own_ref, *, tq):
    h = pl.program_id(1)
    qi = pl.program_id(2)
    tk = tq
    slope = jnp.exp2(jnp.full((1, 1), -8.0 / DF_H, F32) * (h + 1).astype(F32))
    qt = qt_ref[0, 0]
    row = lax.broadcasted_iota(jnp.int32, (DF_DV, tq), 0)
    zero = jnp.zeros_like(qt)
    q2t = jnp.concatenate([jnp.where(row < DF_DH, qt, zero), jnp.where(row >= DF_DH, qt, zero)], axis=1)
    brow = lax.broadcasted_iota(jnp.int32, (BIAS_W, 2 * tq), 0)
    a0 = ((qi * tq) >> CHUNK_SHIFT).astype(F32)
    extra = jnp.where(brow < 2, 1.0, jnp.where(brow == 2, -a0, 0.0)).astype(BF16)
    q2e = jnp.concatenate([q2t, extra], axis=0)
    ones = jnp.ones((ONES_ROWS, tk), BF16)

    m_ref[...] = jnp.full(m_ref.shape, NEG_BIG, F32)
    acc_ref[...] = jnp.zeros(acc_ref.shape, F32)

    def scores_into(dst_ref, kj):
        r0 = pl.multiple_of(kj * tk, tk)
        dst_ref[...] = _dot(ke_ref[pl.ds(r0, tk), :], q2e)

    def update(s, kj):
        m_old = m_ref[...]
        m_new = jnp.maximum(m_old, jnp.max(s, axis=0, keepdims=True))
        alpha = jnp.exp(m_old - m_new)
        p = jnp.exp(s - m_new).astype(BF16)
        vt_ext = jnp.concatenate([vt_ref[0, kj], ones], axis=0)
        acc_ref[...] = alpha * acc_ref[...] + _dot(vt_ext, p)
        m_ref[...] = m_new

    @pl.when(qi == 0)
    def _():
        rel_k = lax.broadcasted_iota(jnp.int32, (tq, 2 * tq), 0)
        rel_q = lax.broadcasted_iota(jnp.int32, (tq, 2 * tq), 1)
        rel_q = jnp.where(rel_q >= tq, rel_q - tq, rel_q)
        ahead = rel_k - rel_q
        fix = jnp.where(ahead > 0, (-2.0 * slope) * ahead.astype(F32), 0.0)
        visible = (rel_k >> CHUNK_SHIFT) <= (rel_q >> CHUNK_SHIFT)
        own_ref[...] = jnp.where(visible, fix, NEG_BIG)

    def own_block(s):
        update(s + own_ref[...], qi)

    scores_into(sa_ref, 0)

    def block_pair(t, carry):
        k0 = 2 * t
        scores_into(sb_ref, k0 + 1)
        update(sa_ref[...], k0)
        scores_into(sa_ref, k0 + 2)
        update(sb_ref[...], k0 + 1)
        return carry

    lax.fori_loop(0, qi >> 1, block_pair, 0)

    @pl.when((qi & 1) == 1)
    def _():
        scores_into(sb_ref, qi)
        update(sa_ref[...], qi - 1)
        own_block(sb_ref[...])

    @pl.when((qi & 1) == 0)
    def _():
        own_block(sa_ref[...])

    lam_p = lam_ref[0]
    lam_init = 0.8 - 0.6 * jnp.exp(jnp.full((1, 1), -0.3, F32) * l_ref[0].astype(F32))
    lam = (jnp.exp(jnp.sum(lam_p[0:1] * lam_p[1:2], axis=-1, keepdims=True))
           - jnp.exp(jnp.sum(lam_p[2:3] * lam_p[3:4], axis=-1, keepdims=True)) + lam_init)
    acc = acc_ref[...]
    o_all = acc[0:DF_DV] / acc[DF_DV:DF_DV + 1]
    o = o_all[:, 0:tq] - lam * o_all[:, tq:2 * tq]
    ms = jnp.mean(o * o, axis=0, keepdims=True)
    y = o * lax.rsqrt(ms + EPS) * ogt_ref[0] * (1.0 - lam_init)
    y_ref[...] = y.T.astype(BF16)


def diff_flash_t(qt, ke, vt, lidx, lam_p, out_g_t, *, nb, seq, tq):
    assert tq % CHUNK == 0
    nq = seq // tq
    n_rows = nb * seq
    kew = DF_DV + BIAS_W
    kern = functools.partial(_flash_t_kernel, tq=tq)
    vmem = 2 * seq * kew * 2 + 2 * seq * DF_DV * 2 + 10 * tq * 2 * tq * 4 + (8 << 20)
    return pl.pallas_call(
        kern,
        grid_spec=pltpu.PrefetchScalarGridSpec(
            num_scalar_prefetch=1,
            grid=(nb, DF_H, nq),
            in_specs=[
                pl.BlockSpec((1, 1, DF_DV, tq), lambda b, h, i, l: (h, b * nq + i, 0, 0)),
                pl.BlockSpec((seq, kew), lambda b, h, i, l: (b, h)),
                pl.BlockSpec((1, nq, DF_DV, tq), lambda b, h, i, l: (h, b, 0, 0)),
                pl.BlockSpec((1, 4, DF_DH), lambda b, h, i, l: (l[0], 0, 0)),
                pl.BlockSpec((1, DF_DV, 1), lambda b, h, i, l: (l[0], 0, 0)),
            ],
            out_specs=pl.BlockSpec((tq, DF_DV), lambda b, h, i, l: (b * nq + i, h)),
            scratch_shapes=[pltpu.VMEM((1, 2 * tq), F32), pltpu.VMEM((DF_DV + ONES_ROWS, 2 * tq), F32),
                            pltpu.VMEM((tq, 2 * tq), F32), pltpu.VMEM((tq, 2 * tq), F32),
                            pltpu.VMEM((tq, 2 * tq), F32)],
        ),
        out_shape=jax.ShapeDtypeStruct((n_rows, DF_W), BF16),
        compiler_params=_cparams(("parallel", "parallel", "arbitrary"), vmem),
        name="diff_flash_t",
    )(lidx, qt, ke, vt, lam_p, out_g_t)


def _flash_sample_kernel(l_ref, q_ref, km_ref, vm_ref, kx_ref, vx_ref, lam_ref, og_ref, y_ref, *, tq, tk, past):
    q0 = past
    lam_p = lam_ref[0]
    lam_init = 0.8 - 0.6 * jnp.exp(jnp.full((1, 1), -0.3, F32) * l_ref[0].astype(F32))
    lam = (jnp.exp(jnp.sum(lam_p[0:1] * lam_p[1:2], axis=-1, keepdims=True))
           - jnp.exp(jnp.sum(lam_p[2:3] * lam_p[3:4], axis=-1, keepdims=True)) + lam_init)
    lane = lax.broadcasted_iota(jnp.int32, (tq, DF_DV), 1)
    rel_k = lax.broadcasted_iota(jnp.int32, (2 * tq, tq), 1)
    rel_q = lax.broadcasted_iota(jnp.int32, (2 * tq, tq), 0)
    rel_q = jnp.where(rel_q >= tq, rel_q - tq, rel_q)
    own_bias = rel_q.astype(F32) - jnp.abs(rel_q - rel_k).astype(F32)
    visible = ((q0 + rel_k) >> CHUNK_SHIFT) <= ((q0 + rel_q) >> CHUNK_SHIFT)

    def update(state, s, v):
        m_old, l_old, acc_old = state
        m_new = jnp.maximum(m_old, jnp.max(s, axis=-1, keepdims=True))
        alpha = jnp.exp(m_old - m_new)
        p = jnp.exp(s - m_new)
        return (m_new, alpha * l_old + jnp.sum(p, axis=-1, keepdims=True),
                alpha * acc_old + _dot(p.astype(BF16), v))

    for h in range(DF_H):
        sl = slice(h * DF_DV, (h + 1) * DF_DV)
        slope = 2.0 ** (-8.0 * (h + 1) / DF_H)
        q = q_ref[:, sl]
        zero = jnp.zeros_like(q)
        q2 = jnp.concatenate([jnp.where(lane < DF_DH, q, zero), jnp.where(lane >= DF_DH, q, zero)], axis=0)
        state = (jnp.full((2 * tq, 1), NEG_BIG, F32), jnp.zeros((2 * tq, 1), F32), jnp.zeros((2 * tq, DF_DV), F32))
        for kj in range(past // tk):
            rows = pl.ds(kj * tk * DF_H + h, tk, stride=DF_H)
            kpos = kj * tk + lax.broadcasted_iota(jnp.int32, (1, tk), 1)
            s = _dot_nt(q2, km_ref[rows, :].astype(BF16)) + slope * (kpos - q0).astype(F32)
            state = update(state, s, vm_ref[rows, :].astype(BF16))
        s = _dot_nt(q2, kx_ref[:, sl].astype(BF16)) + slope * own_bias
        _, l_sum, acc = update(state, jnp.where(visible, s, NEG_BIG), vx_ref[:, sl].astype(BF16))
        o_all = acc / l_sum
        o = o_all[0:tq] - lam * o_all[tq:2 * tq]
        y_ref[:, sl] = (_rms(o, og_ref[0]) * (1.0 - lam_init)).astype(BF16)


def diff_flash_sample(qn, k_past, v_past, k_new, v_new, lidx, lam_p, out_g, *, nb, seq, past, tk):
    n_rows = nb * seq
    kern = functools.partial(_flash_sample_kernel, tq=seq, tk=tk, past=past)
    vmem = 4 * past * DF_H * DF_DV * 4 + (8 << 20)
    qmap = lambda b, l: (b, 0)
    cmap = lambda b, l: (l[0] * nb + b, 0)
    return pl.pallas_call(
        kern,
        grid_spec=pltpu.PrefetchScalarGridSpec(
            num_scalar_prefetch=1,
            grid=(nb,),
            in_specs=[
                pl.BlockSpec((seq, DF_W), qmap),
                pl.BlockSpec((past * DF_H, DF_DV), cmap),
                pl.BlockSpec((past * DF_H, DF_DV), cmap),
                pl.BlockSpec((seq, DF_W), qmap),
                pl.BlockSpec((seq, DF_W), qmap),
                pl.BlockSpec((1, 4, DF_DH), lambda b, l: (l[0], 0, 0)),
                pl.BlockSpec((1, 1, DF_DV), lambda b, l: (l[0], 0, 0)),
            ],
            out_specs=pl.BlockSpec((seq, DF_W), qmap),
        ),
        out_shape=jax.ShapeDtypeStruct((n_rows, DF_W), BF16),
        compiler_params=_cparams(("parallel",), vmem),
        name="diff_flash_sample",
    )(lidx, qn, k_past, v_past, k_new, v_new, lam_p, out_g)


def _trunk_layer(x, lidx, w, st, cfg):
    nb, seq, past = cfg["nb"], cfg["seq"], cfg["past"]
    z, gates = in_proj(x, lidx, w["norm_mix"], w["w_main"], w["w_gate"], tm=cfg["tm_in"], tn=cfg["tn_in"])
    prep = None
    if cfg["prompt"]:
        assert cfg["tc"] == cfg["tq"]
        prep = {"qg": w["df_qg"], "kg": w["df_kg"], "pk_buf": st["pk_buf"], "pv_buf": st["pv_buf"]}
    y_ml, c_new, n_new, m_new, y_hg, s_new, *prep_out = recurrent_mixers(
        z, gates, lidx, st["hist0"], w["conv_w"], w["conv_b"], w["b_if"], w["ml_og"],
        st["c0"], st["n0"], st["m0"], w["hg_lbl"], w["hg_og"], st["s0"],
        nb=nb, seq=seq, tc=cfg["tc"], lc_ml=cfg["lc_ml"], lc_hg=cfg["lc_hg"], prep=prep)
    z3 = z.reshape(nb, seq, Z_W)
    new = {}
    if cfg["prompt"]:
        qt, ke, vt, new["pk_buf"], new["pv_buf"] = prep_out
        y_df = diff_flash_t(qt, ke, vt, lidx, w["df_lam"], w["df_og_t"], nb=nb, seq=seq, tq=cfg["tq"])
    else:
        qn, kn, kn_b, v_b = qk_prep(z, lidx, w["df_qg"], w["df_kg"], tm=cfg["tm_prep"])
        y_df = diff_flash_sample(qn, st["past_k"], st["past_v"], kn_b, v_b, lidx, w["df_lam"], w["df_og"],
                                 nb=nb, seq=seq, past=past, tk=cfg["tk_past"])
        new["attn_k"] = kn.reshape(nb, seq, DF_H, 2 * DF_DH)
        new["attn_v"] = z3[:, :, ZB_DFV * DF_W:(ZB_DFV + 1) * DF_W].reshape(nb, seq, DF_H, DF_DV)
    x = post_mix(x, y_ml, y_df, y_hg, lidx, w["w_out"], w["norm_cross"], w["wq"], w["ca_qg"],
                 st["mk"], st["mv"], w["wo"], tm=cfg["tm_post"], rows_per_batch=seq)
    x = ffn(x, lidx, w["norm_ffn"], w["w_gu"], w["w_down"], w["final_norm"], tm=cfg["tm_ffn"], th=cfg["th"],
            x_buffers=cfg["ffn_x_buffers"])
    new.update({
        "conv": z3[:, seq - (CONV_W - 1):, 0:2 * ML_W],
        "C": c_new,
        "n": n_new[:, :ML_H, :],
        "m": m_new[:, 0, :ML_H],
        "S": s_new,
    })
    return x, new


PROMPT_CFG = dict(prompt=True, tm_in=1024, tn_in=1536, tc=512, lc_ml=128, lc_hg=128, tq=512,
                  tm_post=512, tm_ffn=1024, th=512, ffn_x_buffers=2)
SAMPLE_CFG = dict(prompt=False, tm_in=128, tn_in=768, tc=16, lc_ml=16, lc_hg=16, tm_prep=128, tk_past=2048,
                  tm_post=128, tm_ffn=128, th=512, ffn_x_buffers=2)


def kernel(x_prompt, x_sample, mem_prompt, cache_attn_k, cache_attn_v, cache_mem_k, cache_mem_v, state_mlstm_conv, state_mlstm_C, state_mlstm_n, state_mlstm_m, state_hgrn_S, norm_mix, w_in, mlstm_conv_w, mlstm_conv_b, mlstm_b_i, mlstm_b_f, mlstm_out_norm, diff_q_norm, diff_k_norm, diff_lambda, diff_out_norm, hgrn_lb_logits, hgrn_out_norm, w_out, norm_cross, norm_mem, cross_wq, cross_wk, cross_wv, cross_q_norm, cross_k_norm, cross_wo, norm_ffn, ffn_w_gate_up, ffn_w_down, final_norm):
    bp, tp = x_prompt.shape[:2]
    bs, ts = x_sample.shape[:2]
    past = cache_attn_k.shape[2]
    depth = w_in.shape[0]
    assert depth == DEPTH and x_prompt.shape[2] == D_MODEL

    g_off = 4 * ML_W
    r3 = lambda a: a.reshape(depth, 1, a.shape[-1])
    w = {
        "norm_mix": r3(norm_mix),
        "w_main": jnp.concatenate([w_in[:, :, :g_off], w_in[:, :, g_off + 2 * ML_H:]], axis=-1).astype(BF16),
        "w_gate": jnp.pad(w_in[:, :, g_off:g_off + 2 * ML_H], ((0, 0), (0, 0), (0, GATE_W - 2 * ML_H))).astype(BF16),
        "conv_w": mlstm_conv_w,
        "conv_b": r3(mlstm_conv_b),
        "b_if": r3(jnp.pad(jnp.concatenate([mlstm_b_i, mlstm_b_f], axis=-1), ((0, 0), (0, GATE_W - 2 * ML_H)))),
        "ml_og": r3(mlstm_out_norm),
        "df_qg": r3(jnp.tile(diff_q_norm, (1, DF_W // DF_DH))),
        "df_kg": r3(jnp.tile(diff_k_norm, (1, DF_W // DF_DH))),
        "df_lam": diff_lambda,
        "df_og": r3(diff_out_norm),
        "df_og_t": diff_out_norm.reshape(depth, DF_DV, 1),
        "hg_lbl": hgrn_lb_logits,
        "hg_og": r3(hgrn_out_norm),
        "w_out": w_out.astype(BF16),
        "norm_cross": r3(norm_cross),
        "wq": cross_wq.astype(BF16),
        "ca_qg": r3(cross_q_norm),
        "wo": cross_wo.astype(BF16),
        "norm_ffn": r3(norm_ffn),
        "w_gu": ffn_w_gate_up.astype(BF16),
        "w_down": ffn_w_down.astype(BF16),
        "final_norm": final_norm.reshape(1, D_MODEL),
    }

    mk_p, mv_p = mem_kv(mem_prompt, r3(norm_mem), cross_wk.astype(BF16), cross_wv.astype(BF16), r3(cross_k_norm))

    def pad_hist(conv):
        pad = [(0, 0)] * (conv.ndim - 2) + [(HIST - (CONV_W - 1), 0), (0, 0)]
        return jnp.pad(conv, pad)

    st_p = {
        "hist0": jnp.zeros((bp, HIST, 2 * ML_W), F32),
        "c0": jnp.zeros((bp, ML_H, ML_DH, ML_DH), F32),
        "n0": jnp.zeros((bp, 8, ML_DH), F32),
        "m0": jnp.zeros((bp, 1, GATE_W), F32),
        "s0": jnp.zeros((bp, HG_H, HG_DK, HG_DV), F32),
        "mk": mk_p, "mv": mv_p,
    }
    hist_s = pad_hist(state_mlstm_conv)
    n_s = jnp.pad(state_mlstm_n, ((0, 0), (0, 0), (0, 8 - ML_H), (0, 0)))
    m_s = jnp.pad(state_mlstm_m, ((0, 0), (0, 0), (0, GATE_W - ML_H))).reshape(depth, bs, 1, GATE_W)
    past_k = cache_attn_k.reshape(depth * bs * past * DF_H, DF_DV)
    past_v = cache_attn_v.reshape(depth * bs * past * DF_H, DF_DV)
    mk_s = cache_mem_k.reshape(depth * bs, N_MEM, CA_W)
    mv_s = cache_mem_v.reshape(depth * bs, N_MEM, CA_W)

    cfg_p = dict(PROMPT_CFG, nb=bp, seq=tp, past=0)
    cfg_s = dict(SAMPLE_CFG, nb=bs, seq=ts, past=past)

    def layer(carry, xs):
        xp, xsm, pk_buf, pv_buf = carry
        l, hist_l, c_l, n_l, m_l, s_l = xs
        lidx = l.reshape(1).astype(jnp.int32)
        xp, new_p = _trunk_layer(xp, lidx, w, dict(st_p, pk_buf=pk_buf, pv_buf=pv_buf), cfg_p)
        pk_buf, pv_buf = new_p.pop("pk_buf"), new_p.pop("pv_buf")
        st_s = {"hist0": hist_l, "c0": c_l, "n0": n_l, "m0": m_l, "s0": s_l,
                "past_k": past_k, "past_v": past_v, "mk": mk_s, "mv": mv_s}
        xsm, new_s = _trunk_layer(xsm, lidx, w, st_s, cfg_s)
        return (xp, xsm, pk_buf, pv_buf), (new_p, new_s)

    xs = (jnp.arange(depth, dtype=jnp.int32), hist_s, state_mlstm_C, n_s, m_s, state_hgrn_S)
    kv_rows = depth * bp * tp * DF_H
    init = (x_prompt.reshape(bp * tp, D_MODEL), x_sample.reshape(bs * ts, D_MODEL),
            lax.empty((kv_rows, DF_DV), F32), lax.empty((kv_rows, DF_DV), F32))
    carry, first = layer(init, jax.tree.map(lambda a: a[0], xs))
    (xp, xsm, pk_buf, pv_buf), rest = lax.scan(layer, carry, jax.tree.map(lambda a: a[1:], xs))
    new_p, new_s = jax.tree.map(lambda a, b: jnp.concatenate([a[None], b], axis=0), first, rest)

    y_prompt = xp.reshape(bp, tp, D_MODEL)
    y_sample = xsm.reshape(bs, ts, D_MODEL)
    p_mem_k = mk_p.reshape(depth, bp, N_MEM, CA_H, CA_DH)
    p_mem_v = mv_p.reshape(depth, bp, N_MEM, CA_H, CA_DH)
    p_attn_k = pk_buf.reshape(depth, bp, tp, DF_H, 2 * DF_DH)
    p_attn_v = pv_buf.reshape(depth, bp, tp, DF_H, DF_DV)
    return (y_prompt, y_sample,
            p_attn_k, p_attn_v, p_mem_k, p_mem_v, new_p["conv"], new_p["C"], new_p["n"],
            new_p["m"], new_p["S"],
            new_s["attn_k"], new_s["attn_v"], new_s["conv"], new_s["C"], new_s["n"], new_s["m"], new_s["S"])
```

```python
import functools
import math

import numpy as np
import jax
import jax.numpy as jnp
from jax import lax
from jax.experimental import pallas as pl
from jax.experimental.pallas import tpu as pltpu

F32 = jnp.float32
BF16 = jnp.bfloat16

D_MODEL = 2048
DEPTH = 4
CHUNK = 64
CHUNK_SHIFT = 6
ML_DH = 128
ML_W = 768
ML_H = 6
CONV_W = 4
DF_DH = 64
DF_DV = 128
DF_W = 512
DF_H = 4
HG_DK = 128
HG_DV = 128
HG_W = 768
HG_H = 6
CA_H = 4
CA_DH = 128
CA_W = 512
N_MEM = 256
FF_HIDDEN = 5632
EPS = 1e-6
NEG_BIG = -1e30
LB_FLOOR = 1e-30
LOG2E = math.log2(math.e)

Z_W = 2 * ML_W + 2 * ML_W + 3 * DF_W + 4 * HG_W
GATE_W = 128
ZB_MLQ, ZB_MLK, ZB_MLV, ZB_MLO = 0, 1, 2, 3
ZB_HGQ, ZB_HGF, ZB_HGI, ZB_HGG = 6, 7, 8, 9
ZB_DFQ, ZB_DFK, ZB_DFV = 6, 7, 8

V7X_VMEM_BYTES = 64 * 1024 * 1024
V7X_VMEM_CAP = V7X_VMEM_BYTES - 6 * 1024 * 1024
VMEM_SLACK = 6 * 1024 * 1024
HIST = 8

NT_DIMS = (((1,), (1,)), ((), ()))
TN_DIMS = (((0,), (0,)), ((), ()))


def _vmem_limit(nbytes):
    return int(min(V7X_VMEM_CAP, max(32 * 1024 * 1024, nbytes + VMEM_SLACK)))


def _cparams(sem, vmem_bytes):
    return pltpu.CompilerParams(dimension_semantics=sem, vmem_limit_bytes=_vmem_limit(vmem_bytes))


def _rms(x, g):
    ms = jnp.mean(x * x, axis=-1, keepdims=True)
    return x * lax.rsqrt(ms + EPS) * g


def _dot(a, b):
    return jnp.dot(a, b, preferred_element_type=F32)


def _dot_nt(a, b):
    return lax.dot_general(a, b, NT_DIMS, preferred_element_type=F32)


def _dot_tn(a, b):
    return lax.dot_general(a, b, TN_DIMS, preferred_element_type=F32)


def _log_sigmoid(x):
    return jnp.minimum(x, 0.0) - jnp.log1p(jnp.exp(-jnp.abs(x)))


def _split3(x):
    hi = x.astype(BF16)
    r1 = x - hi.astype(F32)
    mid = r1.astype(BF16)
    lo = (r1 - mid.astype(F32)).astype(BF16)
    return hi, mid, lo


def _dot_exact01(m01, x):
    hi, mid, lo = _split3(x)
    return _dot(m01, hi) + _dot(m01, mid) + _dot(m01, lo)


def _in_proj_kernel(l_ref, x_ref, g_ref, w_ref, wg_ref, z_ref, gate_ref, xn_ref):
    j = pl.program_id(1)

    @pl.when(j == 0)
    def _():
        xn = _rms(x_ref[...], g_ref[0]).astype(BF16)
        xn_ref[...] = xn
        gate_ref[...] = _dot(xn, wg_ref[0])

    z_ref[...] = _dot(xn_ref[...], w_ref[0])


def in_proj(x, lidx, g, w_main, w_gate, *, tm, tn):
    n_rows = x.shape[0]
    grid = (n_rows // tm, Z_W // tn)
    vmem = 2 * tm * D_MODEL * 4 + 2 * D_MODEL * tn * 2 + 2 * tm * tn * 4 + tm * D_MODEL * 2 \
        + 2 * tm * GATE_W * 4 + 2 * D_MODEL * GATE_W * 2 + (4 << 20)
    return pl.pallas_call(
        _in_proj_kernel,
        grid_spec=pltpu.PrefetchScalarGridSpec(
            num_scalar_prefetch=1,
            grid=grid,
            in_specs=[
                pl.BlockSpec((tm, D_MODEL), lambda i, j, l: (i, 0)),
                pl.BlockSpec((1, 1, D_MODEL), lambda i, j, l: (l[0], 0, 0)),
                pl.BlockSpec((1, D_MODEL, tn), lambda i, j, l: (l[0], 0, j)),
                pl.BlockSpec((1, D_MODEL, GATE_W), lambda i, j, l: (l[0], 0, 0)),
            ],
            out_specs=[
                pl.BlockSpec((tm, tn), lambda i, j, l: (i, j)),
                pl.BlockSpec((tm, GATE_W), lambda i, j, l: (i, 0)),
            ],
            scratch_shapes=[pltpu.VMEM((tm, D_MODEL), BF16)],
        ),
        out_shape=[jax.ShapeDtypeStruct((n_rows, Z_W), F32),
                   jax.ShapeDtypeStruct((n_rows, GATE_W), F32)],
        compiler_params=_cparams(("parallel", "arbitrary"), vmem),
        name="in_proj",
    )(lidx, x, g, w_main, w_gate)


def _ffn_kernel(l_ref, x_ref, g_ref, wg_ref, wu_ref, wd_ref, fg_ref, o_ref, xn_ref):
    j = pl.program_id(1)
    nj = pl.num_programs(1)

    @pl.when(j == 0)
    def _():
        x = x_ref[...]
        xn_ref[...] = _rms(x, g_ref[0]).astype(BF16)
        o_ref[...] = x

    xn = xn_ref[...]
    gt = _dot(xn, wg_ref[0])
    up = _dot(xn, wu_ref[0])
    act = (gt * jax.nn.sigmoid(gt) * up).astype(BF16)
    o_ref[...] += _dot(act, wd_ref[0])

    @pl.when(jnp.logical_and(j == nj - 1, l_ref[0] == DEPTH - 1))
    def _():
        o_ref[...] = _rms(o_ref[...], fg_ref[...])


def ffn(x, lidx, g, w_gu, w_down, final_g, *, tm, th, x_buffers=2):
    n_rows = x.shape[0]
    nh = FF_HIDDEN // th
    grid = (n_rows // tm, nh)
    vmem = (2 + x_buffers) * tm * D_MODEL * 4 + tm * D_MODEL * 2 + 6 * D_MODEL * th * 2 + 3 * tm * th * 4 \
        + tm * D_MODEL * 4 + (4 << 20)
    x_mode = {} if x_buffers == 2 else {"pipeline_mode": pl.Buffered(x_buffers)}
    return pl.pallas_call(
        _ffn_kernel,
        grid_spec=pltpu.PrefetchScalarGridSpec(
            num_scalar_prefetch=1,
            grid=grid,
            in_specs=[
                pl.BlockSpec((tm, D_MODEL), lambda i, j, l: (i, 0), **x_mode),
                pl.BlockSpec((1, 1, D_MODEL), lambda i, j, l: (l[0], 0, 0)),
                pl.BlockSpec((1, D_MODEL, th), lambda i, j, l: (l[0], 0, j)),
                pl.BlockSpec((1, D_MODEL, th), lambda i, j, l: (l[0], 0, j + nh)),
                pl.BlockSpec((1, th, D_MODEL), lambda i, j, l: (l[0], j, 0)),
                pl.BlockSpec((1, D_MODEL), lambda i, j, l: (0, 0)),
            ],
            out_specs=pl.BlockSpec((tm, D_MODEL), lambda i, j, l: (i, 0)),
            scratch_shapes=[pltpu.VMEM((tm, D_MODEL), BF16)],
        ),
        out_shape=jax.ShapeDtypeStruct((n_rows, D_MODEL), F32),
        compiler_params=_cparams(("parallel", "arbitrary"), vmem),
        name="ffn",
    )(lidx, x, g, w_gu, w_gu, w_down, final_g)


def _mem_kv_kernel(mem_ref, g_ref, wk_ref, wv_ref, kg_ref, k_ref, v_ref):
    mn = _rms(mem_ref[0], g_ref[0]).astype(BF16)
    k = _dot(mn, wk_ref[0])
    for h in range(CA_H):
        sl = slice(h * CA_DH, (h + 1) * CA_DH)
        k_ref[0, :, sl] = _rms(k[:, sl], kg_ref[0])
    v_ref[0] = _dot(mn, wv_ref[0])


def mem_kv(mem, g, wk, wv, kg):
    nb = mem.shape[0]
    out = jax.ShapeDtypeStruct((DEPTH * nb, N_MEM, CA_W), F32)
    return pl.pallas_call(
        _mem_kv_kernel,
        grid=(DEPTH, nb),
        in_specs=[
            pl.BlockSpec((1, N_MEM, D_MODEL), lambda l, b: (b, 0, 0)),
            pl.BlockSpec((1, 1, D_MODEL), lambda l, b: (l, 0, 0)),
            pl.BlockSpec((1, D_MODEL, CA_W), lambda l, b: (l, 0, 0)),
            pl.BlockSpec((1, D_MODEL, CA_W), lambda l, b: (l, 0, 0)),
            pl.BlockSpec((1, 1, CA_DH), lambda l, b: (l, 0, 0)),
        ],
        out_specs=[pl.BlockSpec((1, N_MEM, CA_W), lambda l, b: (l * nb + b, 0, 0)),
                   pl.BlockSpec((1, N_MEM, CA_W), lambda l, b: (l * nb + b, 0, 0))],
        out_shape=[out, out],
        compiler_params=_cparams(("arbitrary", "arbitrary"), 24 << 20),
        name="mem_kv",
    )(mem, g, wk, wv, kg)


def _post_kernel(l_ref, x_ref, yml_ref, ydf_ref, yhg_ref, wout_ref, gx_ref, wq_ref, qg_ref,
                 mk_ref, mv_ref, wo_ref, o_ref, *, rows_per_stream):
    x1 = x_ref[...]
    x1 = x1 + _dot(yml_ref[...], wout_ref[0, 0:ML_W, :])
    x1 = x1 + _dot(ydf_ref[...], wout_ref[0, ML_W:ML_W + DF_W, :])
    x1 = x1 + _dot(yhg_ref[...], wout_ref[0, ML_W + DF_W:D_MODEL, :])
    hn = _rms(x1, gx_ref[0]).astype(BF16)
    q = _dot(hn, wq_ref[0])
    streams = []
    for si in range(x1.shape[0] // rows_per_stream):
        rs = slice(si * rows_per_stream, (si + 1) * rows_per_stream)
        heads = []
        for h in range(CA_H):
            sl = slice(h * CA_DH, (h + 1) * CA_DH)
            qh = _rms(q[rs, sl], qg_ref[0]).astype(BF16)
            s = _dot_nt(qh, mk_ref[si, :, sl].astype(BF16)) * (CA_DH ** -0.5)
            s = s - jnp.max(s, axis=-1, keepdims=True)
            p = jnp.exp(s)
            p = p / jnp.sum(p, axis=-1, keepdims=True)
            heads.append(_dot(p.astype(BF16), mv_ref[si, :, sl].astype(BF16)))
        streams.append(jnp.concatenate(heads, axis=-1))
    o = (streams[0] if len(streams) == 1 else jnp.concatenate(streams, axis=0)).astype(BF16)
    o_ref[...] = x1 + _dot(o, wo_ref[0])


def post_mix(x, y_ml, y_df, y_hg, lidx, w_out, gx, wq, qg, mk, mv, wo, *, tm, rows_per_batch):
    n_rows = x.shape[0]
    nb = n_rows // rows_per_batch
    grid = (n_rows // tm,)
    w_bytes = (D_MODEL * D_MODEL + 2 * D_MODEL * CA_W) * 2
    row = lambda i, l: (i, 0)
    if tm >= rows_per_batch:
        spt = tm // rows_per_batch
        mem_map = lambda i, l: (l[0] * (nb // spt) + i, 0, 0)
    else:
        spt = 1
        tiles_per_batch = rows_per_batch // tm
        mem_map = lambda i, l: (l[0] * nb + i // tiles_per_batch, 0, 0)
    vmem = 2 * w_bytes + 6 * tm * D_MODEL * 4 + 2 * tm * D_MODEL * 2 + 4 * spt * N_MEM * CA_W * 4 + (6 << 20)
    kern = functools.partial(_post_kernel, rows_per_stream=min(tm, rows_per_batch))
    return pl.pallas_call(
        kern,
        grid_spec=pltpu.PrefetchScalarGridSpec(
            num_scalar_prefetch=1,
            grid=grid,
            in_specs=[
                pl.BlockSpec((tm, D_MODEL), row),
                pl.BlockSpec((tm, ML_W), row),
                pl.BlockSpec((tm, DF_W), row),
                pl.BlockSpec((tm, HG_W), row),
                pl.BlockSpec((1, D_MODEL, D_MODEL), lambda i, l: (l[0], 0, 0)),
                pl.BlockSpec((1, 1, D_MODEL), lambda i, l: (l[0], 0, 0)),
                pl.BlockSpec((1, D_MODEL, CA_W), lambda i, l: (l[0], 0, 0)),
                pl.BlockSpec((1, 1, CA_DH), lambda i, l: (l[0], 0, 0)),
                pl.BlockSpec((spt, N_MEM, CA_W), mem_map),
                pl.BlockSpec((spt, N_MEM, CA_W), mem_map),
                pl.BlockSpec((1, CA_W, D_MODEL), lambda i, l: (l[0], 0, 0)),
            ],
            out_specs=pl.BlockSpec((tm, D_MODEL), row),
        ),
        out_shape=jax.ShapeDtypeStruct((n_rows, D_MODEL), F32),
        compiler_params=_cparams(("parallel",), vmem),
        name="post_mix",
    )(lidx, x, y_ml, y_df, y_hg, w_out, gx, wq, qg, mk, mv, wo)


def _mlstm_kernel(l_ref, zq_ref, zk_ref, zv_ref, zo_ref, gate_ref, hist_ref, cw_ref, cb_ref, bif_ref,
                  og_ref, c0_ref, n0_ref, m0_ref, tri_ref,
                  y_ref, c_ref, n_ref, m_ref, ext_ref, *, tc, lc, part):
    c = pl.program_id(1)

    if part == "init":
        @pl.when(c == 0)
        def _():
            c_ref[...] = c0_ref[...]
            n_ref[...] = n0_ref[...]
            m_ref[...] = m0_ref[...]
            ext_ref[0:HIST, :] = hist_ref[0]
        return

    ext_ref[HIST:HIST + tc, 0:ML_W] = zq_ref[...]
    ext_ref[HIST:HIST + tc, ML_W:2 * ML_W] = zk_ref[...]
    ext = ext_ref[...]
    acc = cb_ref[0] + cw_ref[0, CONV_W - 1:CONV_W, :] * ext[HIST:HIST + tc]
    for j in range(CONV_W - 1):
        acc = acc + cw_ref[0, j:j + 1, :] * pltpu.roll(ext, CONV_W - 1 - j, axis=0)[HIST:HIST + tc]
    tail = ext_ref[tc:tc + HIST, :]
    ext_ref[0:HIST, :] = tail
    qk = acc * jax.nn.sigmoid(acc)

    gz = gate_ref[...] + bif_ref[0]
    lf_all = pltpu.roll(_log_sigmoid(gz), GATE_W - ML_H, axis=1)
    tri = tri_ref[...]
    row_i = lax.broadcasted_iota(jnp.int32, (lc, lc), 0)
    col_i = lax.broadcasted_iota(jnp.int32, (lc, lc), 1)
    causal = row_i >= col_i
    diag = row_i == col_i
    trow = lax.broadcasted_iota(jnp.int32, (lc, GATE_W), 0)
    og = og_ref[0]

    for ci in range(tc // lc):
        r0 = ci * lc
        b_c = _dot_exact01(tri, lf_all[r0:r0 + lc, :])
        r_c = gz[r0:r0 + lc, :] - b_c
        cm = r_c
        d = 1
        while d < lc:
            cm = jnp.maximum(cm, jnp.where(trow >= d, pltpu.roll(cm, d, axis=0), NEG_BIG))
            d *= 2
        m_prev = m_ref[0]
        mx = jnp.maximum(m_prev, cm)
        m_t = b_c + mx
        w_inter_c = jnp.exp(m_prev - mx)
        emt_c = jnp.exp(-m_t)
        m_new = m_t[lc - 1:lc, :]
        b_last = b_c[lc - 1:lc, :]
        w_s_c = jnp.exp(b_last + r_c - m_new)
        decay_c = jnp.exp(b_last + m_prev - m_new)
        m_ref[0] = m_new
        for h in range(ML_H):
            sl = slice(h * ML_DH, (h + 1) * ML_DH)
            hl = slice(h, h + 1)
            q = qk[r0:r0 + lc, h * ML_DH:(h + 1) * ML_DH]
            k = qk[r0:r0 + lc, ML_W + h * ML_DH:ML_W + (h + 1) * ML_DH] * (ML_DH ** -0.5)
            v = zv_ref[r0:r0 + lc, sl]
            qb = q.astype(BF16)
            r_row = jnp.sum(jnp.where(diag, r_c[:, hl], 0.0), axis=0, keepdims=True)
            d_mat = jnp.exp(jnp.where(causal, r_row - mx[:, hl], NEG_BIG))
            w_inter = w_inter_c[:, hl]
            decay = decay_c[:, hl]
            s = _dot_nt(qb, k.astype(BF16)) * d_mat
            c_old = c_ref[0, h]
            n_old = n_ref[0, h:h + 1, :]
            num = _dot(s.astype(BF16), v.astype(BF16)) + w_inter * _dot(qb, c_old.astype(BF16))
            den = jnp.sum(s, axis=-1, keepdims=True) + w_inter * jnp.sum(q * n_old, axis=-1, keepdims=True)
            hh = num / jnp.maximum(jnp.abs(den), emt_c[:, hl])
            kw = k * w_s_c[:, hl]
            c_ref[0, h] = decay * c_old + _dot_tn(kw.astype(BF16), v.astype(BF16))
            n_ref[0, h:h + 1, :] = decay * n_old + jnp.sum(kw, axis=0, keepdims=True)
            o_gate = jax.nn.sigmoid(zo_ref[r0:r0 + lc, sl])
            y_ref[r0:r0 + lc, sl] = (_rms(hh, og) * o_gate).astype(BF16)


def _hgrn_level_masks(lc):
    t = np.arange(lc)[:, None]
    s = np.arange(lc)[None, :]
    masks = []
    h = lc // 2
    while h >= 1:
        odd = (t % (2 * h)) >= h
        same = (t // (2 * h)) == (s // (2 * h))
        masks.append((same & odd & ((s % (2 * h)) < h)).astype(np.float32))
        h //= 2
    masks.append((t == s).astype(np.float32))
    return np.stack(masks, axis=0)


def _hgrn_boundary_rows(a, h):
    lc, width = a.shape
    if h >= 8:
        parts = [jnp.broadcast_to(a[g * 2 * h + h - 1:g * 2 * h + h, :], (2 * h, width)) for g in range(lc // (2 * h))]
        return parts[0] if len(parts) == 1 else jnp.concatenate(parts, axis=0)
    a3 = a.reshape(lc // 8, 8, width)
    sub = lax.broadcasted_iota(jnp.int32, a3.shape, 1)
    out = None
    for g in range(8 // (2 * h)):
        src = g * 2 * h + h - 1
        b = jnp.broadcast_to(a3[:, src:src + 1, :], a3.shape)
        out = b if out is None else jnp.where(sub >= g * 2 * h, b, out)
    return out.reshape(lc, width)


def _hgrn_kernel(l_ref, zq_ref, zf_ref, zi_ref, zg_ref, lbl_ref, og_ref, s0_ref, tri_ref, msk_ref,
                 y_ref, s_ref, st_ref, *, tc, lc, nlev, part):
    c = pl.program_id(1)
    nc = pl.num_programs(1)

    if part == "init":
        @pl.when(c == 0)
        def _():
            for h in range(HG_H):
                st_ref[h] = s0_ref[0, h].T
        return
    if part == "fini":
        @pl.when(c == nc - 1)
        def _():
            for h in range(HG_H):
                s_ref[0, h] = st_ref[h].T
        return

    logits = lbl_ref[...]
    e = jnp.exp(logits - jnp.max(logits, axis=0, keepdims=True))
    p = e / jnp.sum(e, axis=0, keepdims=True)
    drow = lax.broadcasted_iota(jnp.int32, p.shape, 0)
    lb = jnp.sum(jnp.where(jnp.logical_and(drow >= 1, drow <= l_ref[0]), p, 0.0), axis=0, keepdims=True)
    lb_fl = jnp.maximum(lb, LB_FLOOR)
    one_m = 1.0 - lb

    zf = zf_ref[...]
    ez = jnp.exp(-jnp.abs(zf))
    inv = 1.0 / (1.0 + ez)
    sig_p = jnp.where(zf >= 0.0, inv, ez * inv)
    sig_n = jnp.where(zf >= 0.0, ez * inv, inv)
    lf2 = jnp.log(lb_fl + one_m * sig_p) * LOG2E
    kk = one_m * sig_n + (lb - lb_fl)
    zq = zq_ref[...]
    qq = zq * jax.nn.sigmoid(zq) * (HG_DK ** -0.5)
    tri = tri_ref[...]
    og = og_ref[0]

    for ci in range(tc // lc):
        r0 = ci * lc
        a_all = _dot_exact01(tri, lf2[r0:r0 + lc, :])
        lev_exp = [-jnp.abs(a_all - _hgrn_boundary_rows(a_all, lc >> (lv + 1))) for lv in range(nlev)]
        for h in range(HG_H):
            sl = slice(h * HG_DK, (h + 1) * HG_DK)
            q = qq[r0:r0 + lc, sl]
            k = kk[r0:r0 + lc, sl]
            iv = zi_ref[r0:r0 + lc, sl].astype(BF16)
            a_in = a_all[:, sl]
            a_end = a_in[lc - 1:lc, :]
            st = st_ref[h]
            o = _dot_nt((q * jnp.exp2(a_in)).astype(BF16), st.astype(BF16))
            att = jnp.where(msk_ref[nlev] > 0, _dot_nt(q.astype(BF16), k.astype(BF16)), 0.0)
            for lv in range(nlev):
                xf = jnp.exp2(lev_exp[lv][:, sl])
                pm = _dot_nt((q * xf).astype(BF16), (k * xf).astype(BF16))
                att = att + jnp.where(msk_ref[lv] > 0, pm, 0.0)
            o = o + _dot(att.astype(BF16), iv)
            k_end = (k * jnp.exp2(a_end - a_in)).astype(BF16)
            st_ref[h] = st * jnp.exp2(a_end) + _dot_tn(iv, k_end)
            gsl = zg_ref[r0:r0 + lc, sl]
            y_ref[r0:r0 + lc, sl] = (_rms(o, og) * (gsl * jax.nn.sigmoid(gsl))).astype(BF16)


N_ML_IN, N_HG_IN, N_ML_OUT, N_HG_OUT = 14, 9, 4, 2


N_PREP_IN, N_PREP_OUT = 8, 5
N_PREP_ROWS_IN, N_PREP_ROWS_OUT = 6, 4


def _recurrent_kernel(l_ref, *refs, tc, lc_ml, lc_hg, nlev, prep_seq):
    n_pi, n_po = (N_PREP_IN, N_PREP_OUT) if prep_seq else (N_PREP_ROWS_IN, N_PREP_ROWS_OUT)
    o0 = N_ML_IN + N_HG_IN + n_pi
    ml_in, hg_in, pr_in = refs[0:N_ML_IN], refs[N_ML_IN:N_ML_IN + N_HG_IN], refs[N_ML_IN + N_HG_IN:o0]
    o1 = o0 + N_ML_OUT
    o2 = o1 + N_HG_OUT
    ml_out, hg_out, pr_out = refs[o0:o1], refs[o1:o2], refs[o2:o2 + n_po]
    ml_sc, hg_sc = refs[o2 + n_po], refs[o2 + n_po + 1]
    ml = functools.partial(_mlstm_kernel, l_ref, *ml_in, *ml_out, ml_sc, tc=tc, lc=lc_ml)
    hg = functools.partial(_hgrn_kernel, l_ref, *hg_in, *hg_out, hg_sc, tc=tc, lc=lc_hg, nlev=nlev)
    ml(part="init")
    hg(part="init")
    ml(part="main")
    hg(part="main")
    if prep_seq:
        tile = pl.program_id(0) * pl.num_programs(1) + pl.program_id(1)
        _qk_prep_t_kernel(l_ref, *pr_in, *pr_out, tm=tc, seq=prep_seq, tile=tile)
    else:
        _qk_prep_kernel(l_ref, *pr_in, *pr_out)
    hg(part="fini")


def recurrent_mixers(z, gates, lidx, hist0, conv_w, conv_b, b_if, ml_og, c0, n0, m0, lb_logits, hg_og, s0,
                     *, nb, seq, tc, lc_ml, lc_hg, prep):
    n_rows = nb * seq
    nc = seq // tc
    tri_ml = jnp.asarray(np.tril(np.ones((lc_ml, lc_ml), np.float32)), BF16)
    tri_hg = jnp.asarray(np.tril(np.ones((lc_hg, lc_hg), np.float32)), BF16)
    msk_np = _hgrn_level_masks(lc_hg)
    nlev = msk_np.shape[0] - 1
    msk = jnp.asarray(msk_np, F32)
    zmap = lambda blk: (lambda b, c, l: (b * nc + c, blk))
    lmap3 = lambda b, c, l: (l[0], 0, 0)
    bmap3 = lambda b, c, l: (b, 0, 0)
    bmap4 = lambda b, c, l: (b, 0, 0, 0)
    rows = lambda b, c, l: (b * nc + c, 0)
    const2 = lambda b, c, l: (0, 0)
    vmem = 2 * 9 * tc * ML_W * 4 + (HIST + tc) * 2 * ML_W * 4 + 9 * ML_H * ML_DH * ML_DH * 4 \
        + 14 * tc * ML_W * 4 + (8 << 20)
    kern = functools.partial(_recurrent_kernel, tc=tc, lc_ml=lc_ml, lc_hg=lc_hg, nlev=nlev,
                             prep_seq=seq if "pk_buf" in prep else 0)
    in_specs = [
        pl.BlockSpec((tc, ML_W), zmap(ZB_MLQ)), pl.BlockSpec((tc, ML_W), zmap(ZB_MLK)),
        pl.BlockSpec((tc, ML_W), zmap(ZB_MLV)), pl.BlockSpec((tc, ML_W), zmap(ZB_MLO)),
        pl.BlockSpec((tc, GATE_W), rows),
        pl.BlockSpec((1, HIST, 2 * ML_W), bmap3),
        pl.BlockSpec((1, CONV_W, 2 * ML_W), lmap3), pl.BlockSpec((1, 1, 2 * ML_W), lmap3),
        pl.BlockSpec((1, 1, GATE_W), lmap3), pl.BlockSpec((1, 1, ML_DH), lmap3),
        pl.BlockSpec((1, ML_H, ML_DH, ML_DH), bmap4), pl.BlockSpec((1, 8, ML_DH), bmap3),
        pl.BlockSpec((1, 1, GATE_W), bmap3), pl.BlockSpec((lc_ml, lc_ml), const2),
        pl.BlockSpec((tc, HG_W), zmap(ZB_HGQ)), pl.BlockSpec((tc, HG_W), zmap(ZB_HGF)),
        pl.BlockSpec((tc, HG_W), zmap(ZB_HGI)), pl.BlockSpec((tc, HG_W), zmap(ZB_HGG)),
        pl.BlockSpec((DEPTH, HG_W), const2), pl.BlockSpec((1, 1, HG_DV), lmap3),
        pl.BlockSpec((1, HG_H, HG_DK, HG_DV), bmap4), pl.BlockSpec((lc_hg, lc_hg), const2),
        pl.BlockSpec(msk_np.shape, lambda b, c, l: (0, 0, 0)),
    ]
    assert len(in_specs) == N_ML_IN + N_HG_IN
    out_specs = [
        pl.BlockSpec((tc, ML_W), rows), pl.BlockSpec((1, ML_H, ML_DH, ML_DH), bmap4),
        pl.BlockSpec((1, 8, ML_DH), bmap3), pl.BlockSpec((1, 1, GATE_W), bmap3),
        pl.BlockSpec((tc, HG_W), rows), pl.BlockSpec((1, HG_H, HG_DK, HG_DV), bmap4),
    ]
    out_shape = [
        jax.ShapeDtypeStruct((n_rows, ML_W), BF16), jax.ShapeDtypeStruct((nb, ML_H, ML_DH, ML_DH), F32),
        jax.ShapeDtypeStruct((nb, 8, ML_DH), F32), jax.ShapeDtypeStruct((nb, 1, GATE_W), F32),
        jax.ShapeDtypeStruct((n_rows, HG_W), BF16), jax.ShapeDtypeStruct((nb, HG_H, HG_DK, HG_DV), F32),
    ]
    operands = [lidx, z, z, z, z, gates, hist0, conv_w, conv_b, b_if, ml_og, c0, n0, m0, tri_ml,
                z, z, z, z, lb_logits, hg_og, s0, tri_hg, msk]
    aliases = {}
    lane = np.arange(DF_W)
    grp = jnp.asarray(((lane[:, None] // DF_DH) == (lane[None, :] // DF_DH)).astype(np.float32) / DF_DH, BF16)
    z512 = lambda blk: (lambda b, c, l: (b * nc + c, blk))
    if "pk_buf" not in prep:
        in_specs += [
            pl.BlockSpec((tc, DF_W), z512(ZB_DFQ)), pl.BlockSpec((tc, DF_W), z512(ZB_DFK)),
            pl.BlockSpec((tc, DF_W), z512(ZB_DFV)),
            pl.BlockSpec((1, 1, DF_W), lmap3), pl.BlockSpec((1, 1, DF_W), lmap3),
            pl.BlockSpec((DF_W, DF_W), const2),
        ]
        operands += [z, z, z, prep["qg"], prep["kg"], grp]
        out_specs += [pl.BlockSpec((tc, DF_W), rows)] * 4
        out_shape += [jax.ShapeDtypeStruct((n_rows, DF_W), dt) for dt in (BF16, F32, BF16, BF16)]
    else:
        nt = n_rows // tc
        vmem += 2 * 3 * tc * DF_W * 4 + 2 * tc * DF_W * 2 * 4 + 2 * 2 * tc * DF_W * 4
        tmap = lambda b, c, l: (0, b * nc + c, 0, 0)
        smap = lambda b, c, l: (l[0] * nt + b * nc + c, 0)
        any_spec = pl.BlockSpec(memory_space=pl.ANY)
        in_specs += [
            pl.BlockSpec((tc, DF_W), z512(ZB_DFQ)), pl.BlockSpec((tc, DF_W), z512(ZB_DFK)),
            pl.BlockSpec((tc, DF_W), z512(ZB_DFV)),
            pl.BlockSpec((1, 1, DF_W), lmap3), pl.BlockSpec((1, 1, DF_W), lmap3),
            pl.BlockSpec((DF_W, DF_W), const2), any_spec, any_spec,
        ]
        aliases = {len(operands) + 6: len(out_specs) + 3, len(operands) + 7: len(out_specs) + 4}
        operands += [z, z, z, prep["qg"], prep["kg"], grp, prep["pk_buf"], prep["pv_buf"]]
        out_specs += [
            pl.BlockSpec((DF_H, 1, DF_DV, tc), tmap), pl.BlockSpec((tc, DF_H * (DF_DV + BIAS_W)), rows),
            pl.BlockSpec((DF_H, 1, DF_DV, tc), tmap),
            pl.BlockSpec((tc * DF_H, DF_DV), smap), pl.BlockSpec((tc * DF_H, DF_DV), smap),
        ]
        out_shape += [
            jax.ShapeDtypeStruct((DF_H, nt, DF_DV, tc), BF16),
            jax.ShapeDtypeStruct((n_rows, DF_H * (DF_DV + BIAS_W)), BF16),
            jax.ShapeDtypeStruct((DF_H, nt, DF_DV, tc), BF16),
            jax.ShapeDtypeStruct(prep["pk_buf"].shape, F32), jax.ShapeDtypeStruct(prep["pv_buf"].shape, F32),
        ]
    return pl.pallas_call(
        kern,
        grid_spec=pltpu.PrefetchScalarGridSpec(
            num_scalar_prefetch=1, grid=(nb, nc), in_specs=in_specs, out_specs=out_specs,
            scratch_shapes=[pltpu.VMEM((HIST + tc, 2 * ML_W), F32), pltpu.VMEM((HG_H, HG_DV, HG_DK), F32)]),
        out_shape=out_shape,
        input_output_aliases=aliases,
        compiler_params=_cparams(("parallel", "arbitrary"), vmem),
        name="recurrent_mixers",
    )(*operands)


def _qk_prep_kernel(l_ref, zq_ref, zk_ref, zv_ref, qg_ref, kg_ref, grp_ref, qn_ref, kn_ref, knb_ref, vb_ref):
    grp = grp_ref[...]
    qn_ref[...] = (_group_norm64(zq_ref[...], qg_ref[0], grp) * (DF_DH ** -0.5)).astype(BF16)
    kn = _group_norm64(zk_ref[...], kg_ref[0], grp)
    kn_ref[...] = kn
    knb_ref[...] = kn.astype(BF16)
    vb_ref[...] = zv_ref[...].astype(BF16)


def qk_prep(z, lidx, qg, kg, *, tm):
    n_rows = z.shape[0]
    lane = np.arange(DF_W)
    grp = jnp.asarray(((lane[:, None] // DF_DH) == (lane[None, :] // DF_DH)).astype(np.float32) / DF_DH, BF16)
    zmap = lambda blk: (lambda i, l: (i, blk))
    row = lambda i, l: (i, 0)
    return pl.pallas_call(
        _qk_prep_kernel,
        grid_spec=pltpu.PrefetchScalarGridSpec(
            num_scalar_prefetch=1,
            grid=(n_rows // tm,),
            in_specs=[
                pl.BlockSpec((tm, DF_W), zmap(ZB_DFQ)),
                pl.BlockSpec((tm, DF_W), zmap(ZB_DFK)),
                pl.BlockSpec((tm, DF_W), zmap(ZB_DFV)),
                pl.BlockSpec((1, 1, DF_W), lambda i, l: (l[0], 0, 0)),
                pl.BlockSpec((1, 1, DF_W), lambda i, l: (l[0], 0, 0)),
                pl.BlockSpec((DF_W, DF_W), lambda i, l: (0, 0)),
            ],
            out_specs=[pl.BlockSpec((tm, DF_W), row)] * 4,
        ),
        out_shape=[
            jax.ShapeDtypeStruct((n_rows, DF_W), BF16),
            jax.ShapeDtypeStruct((n_rows, DF_W), F32),
            jax.ShapeDtypeStruct((n_rows, DF_W), BF16),
            jax.ShapeDtypeStruct((n_rows, DF_W), BF16),
        ],
        compiler_params=_cparams(("parallel",), 32 << 20),
        name="qk_prep",
    )(lidx, z, z, z, qg, kg, grp)


def _group_norm64(x, g, grp):
    x2 = x * x
    hi = x2.astype(BF16)
    lo = (x2 - hi.astype(F32)).astype(BF16)
    ms = _dot(hi, grp) + _dot(lo, grp)
    return x * lax.rsqrt(ms + EPS) * g


BIAS_W = 128


def _qk_prep_t_kernel(l_ref, zq_ref, zk_ref, zv_ref, qg_ref, kg_ref, grp_ref, pk_in, pv_in,
                      qt_ref, ke_ref, vt_ref, pk_ref, pv_ref, *, tm, seq, tile=None):
    del pk_in, pv_in
    i = pl.program_id(0) if tile is None else tile
    grp = grp_ref[...]
    qn = _group_norm64(zq_ref[...], qg_ref[0], grp) * (DF_DH ** -0.5)
    kn = _group_norm64(zk_ref[...], kg_ref[0], grp)
    zv = zv_ref[...]
    pos = lax.rem(i * tm, seq) + lax.broadcasted_iota(jnp.int32, (tm, BIAS_W), 0)
    lane = lax.broadcasted_iota(jnp.int32, (tm, BIAS_W), 1)
    hi_part = (pos >> CHUNK_SHIFT).astype(F32) * float(CHUNK)
    lo_part = (pos & (CHUNK - 1)).astype(F32)
    base = jnp.where(lane == 0, hi_part, jnp.where(lane == 1, lo_part, jnp.where(lane == 2, float(CHUNK), 0.0)))
    for h in range(DF_H):
        sl = slice(h * DF_DV, (h + 1) * DF_DV)
        slope = 2.0 ** (-8.0 * (h + 1) / DF_H)
        c0 = h * (DF_DV + BIAS_W)
        ke_ref[:, c0:c0 + DF_DV] = kn[:, sl].astype(BF16)
        ke_ref[:, c0 + DF_DV:c0 + DF_DV + BIAS_W] = (base * slope).astype(BF16)
        qt_ref[h, 0] = qn[:, sl].T.astype(BF16)
        vt_ref[h, 0] = zv[:, sl].T.astype(BF16)
        rows = pl.ds(h, tm, stride=DF_H)
        pk_ref[rows, :] = kn[:, sl]
        pv_ref[rows, :] = zv[:, sl]


def qk_prep_t(z, lidx, qg, kg, pk_buf, pv_buf, *, tm, seq):
    n_rows = z.shape[0]
    nt = n_rows // tm
    smap = lambda i, l: (l[0] * nt + i, 0)
    any_spec = pl.BlockSpec(memory_space=pl.ANY)
    lane = np.arange(DF_W)
    grp = jnp.asarray(((lane[:, None] // DF_DH) == (lane[None, :] // DF_DH)).astype(np.float32) / DF_DH, BF16)
    zmap = lambda blk: (lambda i, l: (i, blk))
    row = lambda i, l: (i, 0)
    tmap = lambda i, l: (0, i, 0, 0)
    kern = functools.partial(_qk_prep_t_kernel, tm=tm, seq=seq)
    return pl.pallas_call(
        kern,
        grid_spec=pltpu.PrefetchScalarGridSpec(
            num_scalar_prefetch=1,
            grid=(nt,),
            in_specs=[
                pl.BlockSpec((tm, DF_W), zmap(ZB_DFQ)),
                pl.BlockSpec((tm, DF_W), zmap(ZB_DFK)),
                pl.BlockSpec((tm, DF_W), zmap(ZB_DFV)),
                pl.BlockSpec((1, 1, DF_W), lambda i, l: (l[0], 0, 0)),
                pl.BlockSpec((1, 1, DF_W), lambda i, l: (l[0], 0, 0)),
                pl.BlockSpec((DF_W, DF_W), lambda i, l: (0, 0)),
                any_spec,
                any_spec,
            ],
            out_specs=[
                pl.BlockSpec((DF_H, 1, DF_DV, tm), tmap),
                pl.BlockSpec((tm, DF_H * (DF_DV + BIAS_W)), row),
                pl.BlockSpec((DF_H, 1, DF_DV, tm), tmap),
                pl.BlockSpec((tm * DF_H, DF_DV), smap),
                pl.BlockSpec((tm * DF_H, DF_DV), smap),
            ],
        ),
        out_shape=[
            jax.ShapeDtypeStruct((DF_H, nt, DF_DV, tm), BF16),
            jax.ShapeDtypeStruct((n_rows, DF_H * (DF_DV + BIAS_W)), BF16),
            jax.ShapeDtypeStruct((DF_H, nt, DF_DV, tm), BF16),
            jax.ShapeDtypeStruct(pk_buf.shape, F32),
            jax.ShapeDtypeStruct(pv_buf.shape, F32),
        ],
        input_output_aliases={7: 3, 8: 4},
        compiler_params=_cparams(("parallel",), 40 << 20),
        name="qk_prep_t",
    )(lidx, z, z, z, qg, kg, grp, pk_buf, pv_buf)


ONES_ROWS = 16


def _flash_t_kernel(l_ref, qt_ref, ke_ref, vt_ref, lam_ref, ogt_ref, y_ref, m_ref, acc_ref, sa_ref, sb_ref,
                    own_ref, *, tq):
    h = pl.program_id(1)
    qi = pl.program_id(2)
    tk = tq
    slope = jnp.exp2(jnp.full((1, 1), -8.0 / DF_H, F32) * (h + 1).astype(F32))
    qt = qt_ref[0, 0]
    row = lax.broadcasted_iota(jnp.int32, (DF_DV, tq), 0)
    zero = jnp.zeros_like(qt)
    q2t = jnp.concatenate([jnp.where(row < DF_DH, qt, zero), jnp.where(row >= DF_DH, qt, zero)], axis=1)
    brow = lax.broadcasted_iota(jnp.int32, (BIAS_W, 2 * tq), 0)
    a0 = ((qi * tq) >> CHUNK_SHIFT).astype(F32)
    extra = jnp.where(brow < 2, 1.0, jnp.where(brow == 2, -a0, 0.0)).astype(BF16)
    q2e = jnp.concatenate([q2t, extra], axis=0)
    ones = jnp.ones((ONES_ROWS, tk), BF16)

    m_ref[...] = jnp.full(m_ref.shape, NEG_BIG, F32)
    acc_ref[...] = jnp.zeros(acc_ref.shape, F32)

    def scores_into(dst_ref, kj):
        r0 = pl.multiple_of(kj * tk, tk)
        dst_ref[...] = _dot(ke_ref[pl.ds(r0, tk), :], q2e)

    def update(s, kj):
        m_old = m_ref[...]
        m_new = jnp.maximum(m_old, jnp.max(s, axis=0, keepdims=True))
        alpha = jnp.exp(m_old - m_new)
        p = jnp.exp(s - m_new).astype(BF16)
        vt_ext = jnp.concatenate([vt_ref[0, kj], ones], axis=0)
        acc_ref[...] = alpha * acc_ref[...] + _dot(vt_ext, p)
        m_ref[...] = m_new

    @pl.when(qi == 0)
    def _():
        rel_k = lax.broadcasted_iota(jnp.int32, (tq, 2 * tq), 0)
        rel_q = lax.broadcasted_iota(jnp.int32, (tq, 2 * tq), 1)
        rel_q = jnp.where(rel_q >= tq, rel_q - tq, rel_q)
        ahead = rel_k - rel_q
        fix = jnp.where(ahead > 0, (-2.0 * slope) * ahead.astype(F32), 0.0)
        visible = (rel_k >> CHUNK_SHIFT) <= (rel_q >> CHUNK_SHIFT)
        own_ref[...] = jnp.where(visible, fix, NEG_BIG)

    def own_block(s):
        update(s + own_ref[...], qi)

    scores_into(sa_ref, 0)

    def block_pair(t, carry):
        k0 = 2 * t
        scores_into(sb_ref, k0 + 1)
        update(sa_ref[...], k0)
        scores_into(sa_ref, k0 + 2)
        update(sb_ref[...], k0 + 1)
        return carry

    lax.fori_loop(0, qi >> 1, block_pair, 0)

    @pl.when((qi & 1) == 1)
    def _():
        scores_into(sb_ref, qi)
        update(sa_ref[...], qi - 1)
        own_block(sb_ref[...])

    @pl.when((qi & 1) == 0)
    def _():
        own_block(sa_ref[...])

    lam_p = lam_ref[0]
    lam_init = 0.8 - 0.6 * jnp.exp(jnp.full((1, 1), -0.3, F32) * l_ref[0].astype(F32))
    lam = (jnp.exp(jnp.sum(lam_p[0:1] * lam_p[1:2], axis=-1, keepdims=True))
           - jnp.exp(jnp.sum(lam_p[2:3] * lam_p[3:4], axis=-1, keepdims=True)) + lam_init)
    acc = acc_ref[...]
    o_all = acc[0:DF_DV] / acc[DF_DV:DF_DV + 1]
    o = o_all[:, 0:tq] - lam * o_all[:, tq:2 * tq]
    ms = jnp.mean(o * o, axis=0, keepdims=True)
    y = o * lax.rsqrt(ms + EPS) * ogt_ref[0] * (1.0 - lam_init)
    y_ref[...] = y.T.astype(BF16)


def diff_flash_t(qt, ke, vt, lidx, lam_p, out_g_t, *, nb, seq, tq):
    assert tq % CHUNK == 0
    nq = seq // tq
    n_rows = nb * seq
    kew = DF_DV + BIAS_W
    kern = functools.partial(_flash_t_kernel, tq=tq)
    vmem = 2 * seq * kew * 2 + 2 * seq * DF_DV * 2 + 10 * tq * 2 * tq * 4 + (8 << 20)
    return pl.pallas_call(
        kern,
        grid_spec=pltpu.PrefetchScalarGridSpec(
            num_scalar_prefetch=1,
            grid=(nb, DF_H, nq),
            in_specs=[
                pl.BlockSpec((1, 1, DF_DV, tq), lambda b, h, i, l: (h, b * nq + i, 0, 0)),
                pl.BlockSpec((seq, kew), lambda b, h, i, l: (b, h)),
                pl.BlockSpec((1, nq, DF_DV, tq), lambda b, h, i, l: (h, b, 0, 0)),
                pl.BlockSpec((1, 4, DF_DH), lambda b, h, i, l: (l[0], 0, 0)),
                pl.BlockSpec((1, DF_DV, 1), lambda b, h, i, l: (l[0], 0, 0)),
            ],
            out_specs=pl.BlockSpec((tq, DF_DV), lambda b, h, i, l: (b * nq + i, h)),
            scratch_shapes=[pltpu.VMEM((1, 2 * tq), F32), pltpu.VMEM((DF_DV + ONES_ROWS, 2 * tq), F32),
                            pltpu.VMEM((tq, 2 * tq), F32), pltpu.VMEM((tq, 2 * tq), F32),
                            pltpu.VMEM((tq, 2 * tq), F32)],
        ),
        out_shape=jax.ShapeDtypeStruct((n_rows, DF_W), BF16),
        compiler_params=_cparams(("parallel", "parallel", "arbitrary"), vmem),
        name="diff_flash_t",
    )(lidx, qt, ke, vt, lam_p, out_g_t)


def _flash_sample_kernel(l_ref, q_ref, km_ref, vm_ref, kx_ref, vx_ref, lam_ref, og_ref, y_ref, *, tq, tk, past):
    q0 = past
    lam_p = lam_ref[0]
    lam_init = 0.8 - 0.6 * jnp.exp(jnp.full((1, 1), -0.3, F32) * l_ref[0].astype(F32))
    lam = (jnp.exp(jnp.sum(lam_p[0:1] * lam_p[1:2], axis=-1, keepdims=True))
           - jnp.exp(jnp.sum(lam_p[2:3] * lam_p[3:4], axis=-1, keepdims=True)) + lam_init)
    lane = lax.broadcasted_iota(jnp.int32, (tq, DF_DV), 1)
    rel_k = lax.broadcasted_iota(jnp.int32, (2 * tq, tq), 1)
    rel_q = lax.broadcasted_iota(jnp.int32, (2 * tq, tq), 0)
    rel_q = jnp.where(rel_q >= tq, rel_q - tq, rel_q)
    own_bias = rel_q.astype(F32) - jnp.abs(rel_q - rel_k).astype(F32)
    visible = ((q0 + rel_k) >> CHUNK_SHIFT) <= ((q0 + rel_q) >> CHUNK_SHIFT)

    def update(state, s, v):
        m_old, l_old, acc_old = state
        m_new = jnp.maximum(m_old, jnp.max(s, axis=-1, keepdims=True))
        alpha = jnp.exp(m_old - m_new)
        p = jnp.exp(s - m_new)
        return (m_new, alpha * l_old + jnp.sum(p, axis=-1, keepdims=True),
                alpha * acc_old + _dot(p.astype(BF16), v))

    for h in range(DF_H):
        sl = slice(h * DF_DV, (h + 1) * DF_DV)
        slope = 2.0 ** (-8.0 * (h + 1) / DF_H)
        q = q_ref[:, sl]
        zero = jnp.zeros_like(q)
        q2 = jnp.concatenate([jnp.where(lane < DF_DH, q, zero), jnp.where(lane >= DF_DH, q, zero)], axis=0)
        state = (jnp.full((2 * tq, 1), NEG_BIG, F32), jnp.zeros((2 * tq, 1), F32), jnp.zeros((2 * tq, DF_DV), F32))
        for kj in range(past // tk):
            rows = pl.ds(kj * tk * DF_H + h, tk, stride=DF_H)
            kpos = kj * tk + lax.broadcasted_iota(jnp.int32, (1, tk), 1)
            s = _dot_nt(q2, km_ref[rows, :].astype(BF16)) + slope * (kpos - q0).astype(F32)
            state = update(state, s, vm_ref[rows, :].astype(BF16))
        s = _dot_nt(q2, kx_ref[:, sl].astype(BF16)) + slope * own_bias
        _, l_sum, acc = update(state, jnp.where(visible, s, NEG_BIG), vx_ref[:, sl].astype(BF16))
        o_all = acc / l_sum
        o = o_all[0:tq] - lam * o_all[tq:2 * tq]
        y_ref[:, sl] = (_rms(o, og_ref[0]) * (1.0 - lam_init)).astype(BF16)


def diff_flash_sample(qn, k_past, v_past, k_new, v_new, lidx, lam_p, out_g, *, nb, seq, past, tk):
    n_rows = nb * seq
    kern = functools.partial(_flash_sample_kernel, tq=seq, tk=tk, past=past)
    vmem = 4 * past * DF_H * DF_DV * 4 + (8 << 20)
    qmap = lambda b, l: (b, 0)
    cmap = lambda b, l: (l[0] * nb + b, 0)
    return pl.pallas_call(
        kern,
        grid_spec=pltpu.PrefetchScalarGridSpec(
            num_scalar_prefetch=1,
            grid=(nb,),
            in_specs=[
                pl.BlockSpec((seq, DF_W), qmap),
                pl.BlockSpec((past * DF_H, DF_DV), cmap),
                pl.BlockSpec((past * DF_H, DF_DV), cmap),
                pl.BlockSpec((seq, DF_W), qmap),
                pl.BlockSpec((seq, DF_W), qmap),
                pl.BlockSpec((1, 4, DF_DH), lambda b, l: (l[0], 0, 0)),
                pl.BlockSpec((1, 1, DF_DV), lambda b, l: (l[0], 0, 0)),
            ],
            out_specs=pl.BlockSpec((seq, DF_W), qmap),
        ),
        out_shape=jax.ShapeDtypeStruct((n_rows, DF_W), BF16),
        compiler_params=_cparams(("parallel",), vmem),
        name="diff_flash_sample",
    )(lidx, qn, k_past, v_past, k_new, v_new, lam_p, out_g)


def _trunk_layer(x, lidx, w, st, cfg):
    nb, seq, past = cfg["nb"], cfg["seq"], cfg["past"]
    z, gates = in_proj(x, lidx, w["norm_mix"], w["w_main"], w["w_gate"], tm=cfg["tm_in"], tn=cfg["tn_in"])
    prep = {"qg": w["df_qg"], "kg": w["df_kg"]}
    if cfg["prompt"]:
        assert cfg["tc"] == cfg["tq"]
        prep.update(pk_buf=st["pk_buf"], pv_buf=st["pv_buf"])
    y_ml, c_new, n_new, m_new, y_hg, s_new, *prep_out = recurrent_mixers(
        z, gates, lidx, st["hist0"], w["conv_w"], w["conv_b"], w["b_if"], w["ml_og"],
        st["c0"], st["n0"], st["m0"], w["hg_lbl"], w["hg_og"], st["s0"],
        nb=nb, seq=seq, tc=cfg["tc"], lc_ml=cfg["lc_ml"], lc_hg=cfg["lc_hg"], prep=prep)
    z3 = z.reshape(nb, seq, Z_W)
    new = {}
    if cfg["prompt"]:
        qt, ke, vt, new["pk_buf"], new["pv_buf"] = prep_out
        y_df = diff_flash_t(qt, ke, vt, lidx, w["df_lam"], w["df_og_t"], nb=nb, seq=seq, tq=cfg["tq"])
    else:
        qn, kn, kn_b, v_b = prep_out
        y_df = diff_flash_sample(qn, st["past_k"], st["past_v"], kn_b, v_b, lidx, w["df_lam"], w["df_og"],
                                 nb=nb, seq=seq, past=past, tk=cfg["tk_past"])
        new["attn_k"] = kn.reshape(nb, seq, DF_H, 2 * DF_DH)
        new["attn_v"] = z3[:, :, ZB_DFV * DF_W:(ZB_DFV + 1) * DF_W].reshape(nb, seq, DF_H, DF_DV)
    x = post_mix(x, y_ml, y_df, y_hg, lidx, w["w_out"], w["norm_cross"], w["wq"], w["ca_qg"],
                 st["mk"], st["mv"], w["wo"], tm=cfg["tm_post"], rows_per_batch=seq)
    x = ffn(x, lidx, w["norm_ffn"], w["w_gu"], w["w_down"], w["final_norm"], tm=cfg["tm_ffn"], th=cfg["th"],
            x_buffers=cfg["ffn_x_buffers"])
    new.update({
        "conv": z3[:, seq - (CONV_W - 1):, 0:2 * ML_W],
        "C": c_new,
        "n": n_new[:, :ML_H, :],
        "m": m_new[:, 0, :ML_H],
        "S": s_new,
    })
    return x, new


PROMPT_CFG = dict(prompt=True, tm_in=1024, tn_in=1536, tc=512, lc_ml=128, lc_hg=128, tq=512,
                  tm_post=512, tm_ffn=1024, th=512, ffn_x_buffers=2)
SAMPLE_CFG = dict(prompt=False, tm_in=128, tn_in=768, tc=16, lc_ml=16, lc_hg=16, tm_prep=128, tk_past=2048,
                  tm_post=128, tm_ffn=128, th=512, ffn_x_buffers=2)


def kernel(x_prompt, x_sample, mem_prompt, cache_attn_k, cache_attn_v, cache_mem_k, cache_mem_v, state_mlstm_conv, state_mlstm_C, state_mlstm_n, state_mlstm_m, state_hgrn_S, norm_mix, w_in, mlstm_conv_w, mlstm_conv_b, mlstm_b_i, mlstm_b_f, mlstm_out_norm, diff_q_norm, diff_k_norm, diff_lambda, diff_out_norm, hgrn_lb_logits, hgrn_out_norm, w_out, norm_cross, norm_mem, cross_wq, cross_wk, cross_wv, cross_q_norm, cross_k_norm, cross_wo, norm_ffn, ffn_w_gate_up, ffn_w_down, final_norm):
    bp, tp = x_prompt.shape[:2]
    bs, ts = x_sample.shape[:2]
    past = cache_attn_k.shape[2]
    depth = w_in.shape[0]
    assert depth == DEPTH and x_prompt.shape[2] == D_MODEL

    g_off = 4 * ML_W
    r3 = lambda a: a.reshape(depth, 1, a.shape[-1])
    w = {
        "norm_mix": r3(norm_mix),
        "w_main": jnp.concatenate([w_in[:, :, :g_off], w_in[:, :, g_off + 2 * ML_H:]], axis=-1).astype(BF16),
        "w_gate": jnp.pad(w_in[:, :, g_off:g_off + 2 * ML_H], ((0, 0), (0, 0), (0, GATE_W - 2 * ML_H))).astype(BF16),
        "conv_w": mlstm_conv_w,
        "conv_b": r3(mlstm_conv_b),
        "b_if": r3(jnp.pad(jnp.concatenate([mlstm_b_i, mlstm_b_f], axis=-1), ((0, 0), (0, GATE_W - 2 * ML_H)))),
        "ml_og": r3(mlstm_out_norm),
        "df_qg": r3(jnp.tile(diff_q_norm, (1, DF_W // DF_DH))),
        "df_kg": r3(jnp.tile(diff_k_norm, (1, DF_W // DF_DH))),
        "df_lam": diff_lambda,
        "df_og": r3(diff_out_norm),
        "df_og_t": diff_out_norm.reshape(depth, DF_DV, 1),
        "hg_lbl": hgrn_lb_logits,
        "hg_og": r3(hgrn_out_norm),
        "w_out": w_out.astype(BF16),
        "norm_cross": r3(norm_cross),
        "wq": cross_wq.astype(BF16),
        "ca_qg": r3(cross_q_norm),
        "wo": cross_wo.astype(BF16),
        "norm_ffn": r3(norm_ffn),
        "w_gu": ffn_w_gate_up.astype(BF16),
        "w_down": ffn_w_down.astype(BF16),
        "final_norm": final_norm.reshape(1, D_MODEL),
    }

    mk_p, mv_p = mem_kv(mem_prompt, r3(norm_mem), cross_wk.astype(BF16), cross_wv.astype(BF16), r3(cross_k_norm))

    def pad_hist(conv):
        pad = [(0, 0)] * (conv.ndim - 2) + [(HIST - (CONV_W - 1), 0), (0, 0)]
        return jnp.pad(conv, pad)

    st_p = {
        "hist0": jnp.zeros((bp, HIST, 2 * ML_W), F32),
        "c0": jnp.zeros((bp, ML_H, ML_DH, ML_DH), F32),
        "n0": jnp.zeros((bp, 8, ML_DH), F32),
        "m0": jnp.zeros((bp, 1, GATE_W), F32),
        "s0": jnp.zeros((bp, HG_H, HG_DK, HG_DV), F32),
        "mk": mk_p, "mv": mv_p,
    }
    hist_s = pad_hist(state_mlstm_conv)
    n_s = jnp.pad(state_mlstm_n, ((0, 0), (0, 0), (0, 8 - ML_H), (0, 0)))
    m_s = jnp.pad(state_mlstm_m, ((0, 0), (0, 0), (0, GATE_W - ML_H))).reshape(depth, bs, 1, GATE_W)
    past_k = cache_attn_k.reshape(depth * bs * past * DF_H, DF_DV)
    past_v = cache_attn_v.reshape(depth * bs * past * DF_H, DF_DV)
    mk_s = cache_mem_k.reshape(depth * bs, N_MEM, CA_W)
    mv_s = cache_mem_v.reshape(depth * bs, N_MEM, CA_W)

    cfg_p = dict(PROMPT_CFG, nb=bp, seq=tp, past=0)
    cfg_s = dict(SAMPLE_CFG, nb=bs, seq=ts, past=past)

    def layer(carry, xs):
        xp, xsm, pk_buf, pv_buf = carry
        l, hist_l, c_l, n_l, m_l, s_l = xs
        lidx = l.reshape(1).astype(jnp.int32)
        xp, new_p = _trunk_layer(xp, lidx, w, dict(st_p, pk_buf=pk_buf, pv_buf=pv_buf), cfg_p)
        pk_buf, pv_buf = new_p.pop("pk_buf"), new_p.pop("pv_buf")
        st_s = {"hist0": hist_l, "c0": c_l, "n0": n_l, "m0": m_l, "s0": s_l,
                "past_k": past_k, "past_v": past_v, "mk": mk_s, "mv": mv_s}
        xsm, new_s = _trunk_layer(xsm, lidx, w, st_s, cfg_s)
        return (xp, xsm, pk_buf, pv_buf), (new_p, new_s)

    xs = (jnp.arange(depth, dtype=jnp.int32), hist_s, state_mlstm_C, n_s, m_s, state_hgrn_S)
    kv_rows = depth * bp * tp * DF_H
    init = (x_prompt.reshape(bp * tp, D_MODEL), x_sample.reshape(bs * ts, D_MODEL),
            lax.empty((kv_rows, DF_DV), F32), lax.empty((kv_rows, DF_DV), F32))
    carry, first = layer(init, jax.tree.map(lambda a: a[0], xs))
    (xp, xsm, pk_buf, pv_buf), rest = lax.scan(layer, carry, jax.tree.map(lambda a: a[1:], xs))
    new_p, new_s = jax.tree.map(lambda a, b: jnp.concatenate([a[None], b], axis=0), first, rest)

    y_prompt = xp.reshape(bp, tp, D_MODEL)
    y_sample = xsm.reshape(bs, ts, D_MODEL)
    p_mem_k = mk_p.reshape(depth, bp, N_MEM, CA_H, CA_DH)
    p_mem_v = mv_p.reshape(depth, bp, N_MEM, CA_H, CA_DH)
    p_attn_k = pk_buf.reshape(depth, bp, tp, DF_H, 2 * DF_DH)
    p_attn_v = pv_buf.reshape(depth, bp, tp, DF_H, DF_DV)
    return (y_prompt, y_sample,
            p_attn_k, p_attn_v, p_mem_k, p_mem_v, new_p["conv"], new_p["C"], new_p["n"],
            new_p["m"], new_p["S"],
            new_s["attn_k"], new_s["attn_v"], new_s["conv"], new_s["C"], new_s["n"], new_s["m"], new_s["S"])
```
